```python
import jax, jax.numpy as jnp
from jax import lax
import numpy as np

D_MODEL = 1024
BATCH = 4
SEQ = 4096
DEPTH = 2
DEC_BATCH = 128
DEC_SEQ = 8
PAST_LEN = 16384
PAGE_SIZE = 128

N_MIXERS = 2
N_POOL_LAYERS = (DEPTH + 1) // 2
N_ATTN_LAYERS = DEPTH // 2
POOL_WINDOWS = (2, 4, 8, 16)
N_POOL_GROUPS = len(POOL_WINDOWS)
POOL_GROUP_DIM = D_MODEL // N_POOL_GROUPS
POOL_BUF = max(POOL_WINDOWS) - 1
HEAD_DIM = 64
N_HEADS = D_MODEL // HEAD_DIM
N_KV_HEADS = 4
GQA_GROUP = N_HEADS // N_KV_HEADS
QKV_DIM = (N_HEADS + 2 * N_KV_HEADS) * HEAD_DIM
WINDOW = 128
BLOCK = WINDOW
ROPE_THETA = 10000.0
N_EXPERT_GROUPS = 4
EXPERTS_PER_GROUP = 4
N_EXPERTS = N_EXPERT_GROUPS * EXPERTS_PER_GROUP
TOP_K = 2
D_EXPERT = 256
RMS_EPS = 1e-6

kernel_name = 'hybrid_pool_swa_hmoe_step'

F32 = jnp.float32


def _rmsnorm(x, g):
    x32 = x.astype(F32)
    y = x32 * lax.rsqrt(jnp.mean(x32 * x32, axis=-1, keepdims=True) + RMS_EPS) * g.astype(F32)
    return y.astype(x.dtype)


def _pool_mixer(h, buf, pos, w_grp, ls):
    T = h.shape[1]
    P = buf.shape[1]
    hc = jnp.concatenate([buf, h], axis=1)
    hc32 = hc.astype(F32)
    c0 = jnp.pad(jnp.cumsum(hc32, axis=1), ((0, 0), (1, 0), (0, 0)))
    outs = []
    for g, w in enumerate(POOL_WINDOWS):
        lo, hi = g * POOL_GROUP_DIM, (g + 1) * POOL_GROUP_DIM
        win_sum = c0[:, P + 1:P + 1 + T, lo:hi] - c0[:, P + 1 - w:P + 1 - w + T, lo:hi]
        cnt = jnp.minimum(w, pos + 1).astype(F32)[..., None]
        d = win_sum / cnt - hc32[:, P:, lo:hi]
        outs.append(jnp.einsum('btc,ce->bte', d, w_grp[g].astype(F32)))
    y = (jnp.concatenate(outs, axis=-1) * ls.astype(F32)).astype(h.dtype)
    new_buf = hc[:, hc.shape[1] - P:]
    return y, new_buf


def _rope(x, pos):
    inv = ROPE_THETA ** (-jnp.arange(0, HEAD_DIM, 2, dtype=F32) / HEAD_DIM)
    ang = pos.astype(F32)[..., None] * inv
    cos = jnp.cos(ang)[:, :, None, :]
    sin = jnp.sin(ang)[:, :, None, :]
    x1, x2 = jnp.split(x.astype(F32), 2, axis=-1)
    return jnp.concatenate([x1 * cos - x2 * sin, x2 * cos + x1 * sin], axis=-1).astype(x.dtype)


def _qkv(h, pos, w_qkv, b_qkv):
    B, T, _ = h.shape
    qkv = h @ w_qkv + b_qkv
    nq, nk = N_HEADS * HEAD_DIM, N_KV_HEADS * HEAD_DIM
    q = qkv[..., :nq].reshape(B, T, N_HEADS, HEAD_DIM)
    k = qkv[..., nq:nq + nk].reshape(B, T, N_KV_HEADS, HEAD_DIM)
    v = qkv[..., nq + nk:].reshape(B, T, N_KV_HEADS, HEAD_DIM)
    return _rope(q, pos), _rope(k, pos), v


def _sink_softmax(s, mask, sink):
    s = jnp.where(mask, s, -jnp.inf)
    m = jnp.maximum(jnp.max(s, axis=-1, keepdims=True), sink)
    p = jnp.exp(s - m)
    return p / (jnp.sum(p, axis=-1, keepdims=True) + jnp.exp(sink - m))


def _swa_prompt(q, k, v, sink):
    B, S = q.shape[:2]
    nb = S // BLOCK
    qb = q.reshape(B, nb, BLOCK, N_KV_HEADS, GQA_GROUP, HEAD_DIM)
    pad = ((0, 0), (BLOCK, 0), (0, 0), (0, 0))
    kb = jnp.pad(k, pad).reshape(B, nb + 1, BLOCK, N_KV_HEADS, HEAD_DIM)
    vb = jnp.pad(v, pad).reshape(B, nb + 1, BLOCK, N_KV_HEADS, HEAD_DIM)
    kw = jnp.concatenate([kb[:, :-1], kb[:, 1:]], axis=2)
    vw = jnp.concatenate([vb[:, :-1], vb[:, 1:]], axis=2)
    s = jnp.einsum('bnqkgd,bnskd->bnkgqs', qb, kw).astype(F32) * (HEAD_DIM ** -0.5)
    qi = jnp.arange(BLOCK)[:, None]
    kj = jnp.arange(2 * BLOCK)[None, :]
    rel = BLOCK + qi - kj
    key_pos = jnp.arange(nb)[:, None] * BLOCK - BLOCK + jnp.arange(2 * BLOCK)[None, :]
    mask = ((rel >= 0) & (rel < WINDOW))[None] & (key_pos >= 0)[:, None, :]
    p = _sink_softmax(s, mask[None, :, None, None], sink[None, None, :, :, None, None])
    o = jnp.einsum('bnkgqs,bnskd->bnqkgd', p.astype(vw.dtype), vw)
    return o.reshape(B, S, N_HEADS * HEAD_DIM)


def _swa_sample(q, k, v, k_buf, v_buf, pos, sink):
    Bd, T = q.shape[:2]
    Wb = k_buf.shape[1]
    kc = jnp.concatenate([k_buf, k], axis=1)
    vc = jnp.concatenate([v_buf, v], axis=1)
    qg = q.reshape(Bd, T, N_KV_HEADS, GQA_GROUP, HEAD_DIM)
    s = jnp.einsum('btkgd,bskd->bkgts', qg, kc).astype(F32) * (HEAD_DIM ** -0.5)
    kpos = pos[:, :1] - Wb + jnp.arange(Wb + T)[None, :]
    rel = pos[:, :, None] - kpos[:, None, :]
    mask = (rel >= 0) & (rel < WINDOW) & (kpos >= 0)[:, None, :]
    p = _sink_softmax(s, mask[:, None, None], sink[None, :, :, None, None])
    o = jnp.einsum('bkgts,bskd->btkgd', p.astype(vc.dtype), vc).reshape(Bd, T, N_HEADS * HEAD_DIM)
    return o, kc[:, kc.shape[1] - Wb:], vc[:, vc.shape[1] - Wb:]


def _moe(h, w_rg, b_rg, w_re, b_re, w_gate, w_up, w_down):
    B, T, D = h.shape
    t = h.reshape(B * T, D)
    gl = (t @ w_rg).astype(F32) + b_rg.astype(F32)
    gp = jax.nn.softmax(gl, axis=-1)
    g_oh = jax.nn.one_hot(jnp.argmax(gl, axis=-1), N_EXPERT_GROUPS, dtype=F32)
    gw = jnp.sum(gp * g_oh, axis=-1)
    el = ((t @ w_re).astype(F32) + b_re.astype(F32)).reshape(-1, N_EXPERT_GROUPS, EXPERTS_PER_GROUP)
    el_sel = jnp.einsum('nge,ng->ne', el, g_oh)
    tv, ti = lax.top_k(el_sel, TOP_K)
    ew = jax.nn.softmax(tv, axis=-1)
    within = jnp.einsum('nk,nke->ne', ew, jax.nn.one_hot(ti, EXPERTS_PER_GROUP, dtype=F32))
    combine = (gw[:, None, None] * g_oh[:, :, None] * within[:, None, :]).reshape(-1, N_EXPERTS)
    a = jax.nn.silu(jnp.einsum('nd,edf->nef', t, w_gate)) * jnp.einsum('nd,edf->nef', t, w_up)
    y = jnp.einsum('nef,efd,ne->nd', a, w_down, combine.astype(a.dtype))
    return y.reshape(B, T, D)


def _trunk(x, pos, pool_state, k_state, v_state, p, prompt):
    new_pool, new_k, new_v = [], [], []
    for layer in range(DEPTH):
        i = layer // N_MIXERS
        h = _rmsnorm(x, p['norm_mix'][layer])
        if layer % N_MIXERS == 0:
            buf = jnp.zeros((x.shape[0], POOL_BUF, D_MODEL), h.dtype) if prompt else pool_state[i]
            y, nbuf = _pool_mixer(h, buf, pos, p['w_pool'][i], p['ls_pool'][i])
            new_pool.append(nbuf)
        else:
            q, k, v = _qkv(h, pos, p['w_qkv'][i], p['b_qkv'][i])
            sink = p['sinks'][i].astype(F32).reshape(N_KV_HEADS, GQA_GROUP)
            if prompt:
                o = _swa_prompt(q, k, v, sink)
                keep = min(WINDOW, x.shape[1])
                nk, nv = k[:, x.shape[1] - keep:], v[:, x.shape[1] - keep:]
            else:
                o, nk, nv = _swa_sample(q, k, v, k_state[i], v_state[i], pos, sink)
            new_k.append(nk)
            new_v.append(nv)
            y = o @ p['w_o'][i] + p['b_o'][i]
        x = x + y
        h = _rmsnorm(x, p['norm_ffn'][layer])
        x = x + _moe(h, p['w_rg'][layer], p['b_rg'][layer], p['w_re'][layer], p['b_re'][layer],
                     p['w_gate'][layer], p['w_up'][layer], p['w_down'][layer])
    return _rmsnorm(x, p['norm_final']), jnp.stack(new_pool), jnp.stack(new_k), jnp.stack(new_v)


def setup_inputs(seed: int = 0) -> dict:
    key = jax.random.key(seed)
    ks = jax.random.split(key, 24)
    nrm = lambda k, s, sc: jax.random.normal(k, s, F32) * sc
    w_buf = min(WINDOW, PAST_LEN)
    return {
        'x_prompt': nrm(ks[0], (BATCH, SEQ, D_MODEL), 1.0),
        'x_sample': nrm(ks[1], (DEC_BATCH, DEC_SEQ, D_MODEL), 1.0),
        'state_pool': nrm(ks[2], (N_POOL_LAYERS, DEC_BATCH, POOL_BUF, D_MODEL), 1.0),
        'cache_k': nrm(ks[3], (N_ATTN_LAYERS, DEC_BATCH, w_buf, N_KV_HEADS, HEAD_DIM), 1.0),
        'cache_v': nrm(ks[4], (N_ATTN_LAYERS, DEC_BATCH, w_buf, N_KV_HEADS, HEAD_DIM), 1.0),
        'sample_start': jnp.full((DEC_BATCH,), PAST_LEN, jnp.int32),
        'norm_mix': 1.0 + nrm(ks[5], (DEPTH, D_MODEL), 0.05),
        'norm_ffn': 1.0 + nrm(ks[6], (DEPTH, D_MODEL), 0.05),
        'norm_final': 1.0 + nrm(ks[7], (D_MODEL,), 0.05),
        'w_pool': nrm(ks[8], (N_POOL_LAYERS, N_POOL_GROUPS, POOL_GROUP_DIM, POOL_GROUP_DIM), POOL_GROUP_DIM ** -0.5),
        'ls_pool': 1.0 + nrm(ks[9], (N_POOL_LAYERS, D_MODEL), 0.1),
        'w_qkv': nrm(ks[10], (N_ATTN_LAYERS, D_MODEL, QKV_DIM), D_MODEL ** -0.5),
        'b_qkv': nrm(ks[11], (N_ATTN_LAYERS, QKV_DIM), 0.02),
        'sinks': nrm(ks[12], (N_ATTN_LAYERS, N_HEADS), 0.5),
        'w_o': nrm(ks[13], (N_ATTN_LAYERS, N_HEADS * HEAD_DIM, D_MODEL), (N_HEADS * HEAD_DIM) ** -0.5),
        'b_o': nrm(ks[14], (N_ATTN_LAYERS, D_MODEL), 0.02),
        'w_rg': nrm(ks[15], (DEPTH, D_MODEL, N_EXPERT_GROUPS), D_MODEL ** -0.5),
        'b_rg': nrm(ks[16], (DEPTH, N_EXPERT_GROUPS), 0.01),
        'w_re': nrm(ks[17], (DEPTH, D_MODEL, N_EXPERTS), D_MODEL ** -0.5),
        'b_re': nrm(ks[18], (DEPTH, N_EXPERTS), 0.01),
        'w_gate': nrm(ks[19], (DEPTH, N_EXPERTS, D_MODEL, D_EXPERT), D_MODEL ** -0.5),
        'w_up': nrm(ks[20], (DEPTH, N_EXPERTS, D_MODEL, D_EXPERT), D_MODEL ** -0.5),
        'w_down': nrm(ks[21], (DEPTH, N_EXPERTS, D_EXPERT, D_MODEL), D_EXPERT ** -0.5),
    }


def reference(x_prompt, x_sample, state_pool, cache_k, cache_v, sample_start,
              norm_mix, norm_ffn, norm_final, w_pool, ls_pool, w_qkv, b_qkv, sinks, w_o, b_o,
              w_rg, b_rg, w_re, b_re, w_gate, w_up, w_down):
    p = {'norm_mix': norm_mix, 'norm_ffn': norm_ffn, 'norm_final': norm_final,
         'w_pool': w_pool, 'ls_pool': ls_pool, 'w_qkv': w_qkv, 'b_qkv': b_qkv, 'sinks': sinks,
         'w_o': w_o, 'b_o': b_o, 'w_rg': w_rg, 'b_rg': b_rg, 'w_re': w_re, 'b_re': b_re,
         'w_gate': w_gate, 'w_up': w_up, 'w_down': w_down}
    pos_p = jnp.arange(x_prompt.shape[1], dtype=jnp.int32)[None, :]
    pos_s = sample_start.astype(jnp.int32)[:, None] + jnp.arange(x_sample.shape[1], dtype=jnp.int32)[None, :]
    y_prompt, pool_p, k_p, v_p = _trunk(x_prompt, pos_p, None, None, None, p, True)
    y_sample, pool_s, k_s, v_s = _trunk(x_sample, pos_s, state_pool, cache_k, cache_v, p, False)
    return (y_prompt, y_sample, pool_p, k_p, v_p, pool_s, k_s, v_s)
```

```python
import functools

import jax
import jax.numpy as jnp
from jax import lax
from jax.experimental import pallas as pl
from jax.experimental.pallas import tpu as pltpu

F32 = jnp.float32
BF16 = jnp.bfloat16

D_MODEL = 1024
POOL_WINDOWS = (2, 4, 8, 16)
POOL_GROUP_DIM = 256
POOL_BUF = 15
HALO = 16
HEAD_DIM = 64
N_HEADS = 16
N_KV_HEADS = 4
WINDOW = 128
ROPE_THETA = 10000.0
N_EXPERT_GROUPS = 4
EXPERTS_PER_GROUP = 4
N_EXPERTS = 16
D_EXPERT = 256
RMS_EPS = 1e-6
LANES = 128
ROUTER_LANES = 128
VMEM_LIMIT = 56 * 1024 * 1024


def _rms(x, g):
    ms = jnp.mean(x * x, axis=-1, keepdims=True)
    return x * lax.rsqrt(ms + RMS_EPS) * g


def _pool_windows(hc, s2, s4, s8, rows):
    G = POOL_GROUP_DIM
    a = hc[pl.ds(8, rows), :]
    v2 = a + hc[pl.ds(7, rows), :]
    s2[pl.ds(8, rows), :] = v2
    v4 = v2[:, G:] + s2[pl.ds(6, rows), G:]
    s4[pl.ds(8, rows), G:] = v4
    v8 = v4[:, G:] + s4[pl.ds(4, rows), 2 * G:]
    s8[pl.ds(8, rows), 2 * G:] = v8
    v16 = v8[:, G:] + s8[pl.ds(0, rows), 3 * G:]
    return a, (v2[:, :G], v4[:, :G], v8[:, :G], v16)


def _pool_project(h, wins, pos, wp_ref, ls):
    G = POOL_GROUP_DIM
    outs = []
    for g, w in enumerate(POOL_WINDOWS):
        cnt = jnp.minimum(w, pos + 1).astype(F32)
        d = wins[g] / cnt - h[:, g * G:(g + 1) * G]
        outs.append(jnp.dot(d.astype(BF16), wp_ref[g], preferred_element_type=F32))
    return jnp.concatenate(outs, axis=-1) * ls


def _zero_pads(*refs):
    for r in refs:
        r[pl.ds(0, 8), :] = jnp.zeros((8, D_MODEL), F32)


def _pool_prompt_kernel(x_ref, buf_ref, g_ref, wp_ref, ls_ref, o_ref, nb_ref, hc, s2, s4, s8, *, tq):
    t = pl.program_id(1)

    @pl.when(t == 0)
    def _():
        _zero_pads(hc, s2, s4, s8)
        hc[pl.ds(8, HALO), :] = buf_ref[0]

    @pl.when(t > 0)
    def _():
        hc[pl.ds(8, HALO), :] = hc[pl.ds(8 + tq, HALO), :]

    x = x_ref[0]
    hc[pl.ds(8 + HALO, tq), :] = _rms(x, g_ref[...])
    a, wins = _pool_windows(hc, s2, s4, s8, HALO + tq)
    h = a[HALO:]
    wins = tuple(w[HALO:] for w in wins)
    pos = t * tq + lax.broadcasted_iota(jnp.int32, (tq, 1), 0)
    o_ref[0] = x + _pool_project(h, wins, pos, wp_ref, ls_ref[...])

    @pl.when(t == pl.num_programs(1) - 1)
    def _():
        nb_ref[0] = hc[pl.ds(8 + tq, HALO), :]


def _pool_prompt(x, buf16, g, wp, ls, tq=512):
    B, T, D = x.shape
    return pl.pallas_call(
        functools.partial(_pool_prompt_kernel, tq=tq),
        grid=(B, T // tq),
        in_specs=[
            pl.BlockSpec((1, tq, D), lambda b, t: (b, t, 0)),
            pl.BlockSpec((1, HALO, D), lambda b, t: (b, 0, 0)),
            pl.BlockSpec((1, D), lambda b, t: (0, 0)),
            pl.BlockSpec((4, POOL_GROUP_DIM, POOL_GROUP_DIM), lambda b, t: (0, 0, 0)),
            pl.BlockSpec((1, D), lambda b, t: (0, 0)),
        ],
        out_specs=[
            pl.BlockSpec((1, tq, D), lambda b, t: (b, t, 0)),
            pl.BlockSpec((1, HALO, D), lambda b, t: (b, 0, 0)),
        ],
        out_shape=[jax.ShapeDtypeStruct((B, T, D), F32), jax.ShapeDtypeStruct((B, HALO, D), F32)],
        scratch_shapes=[pltpu.VMEM((8 + HALO + tq, D), F32)] * 4,
        compiler_params=pltpu.CompilerParams(
            dimension_semantics=("arbitrary", "arbitrary"), vmem_limit_bytes=VMEM_LIMIT),
        name="pool_prompt",
    )(x, buf16, g, wp, ls)


def _pool_sample_kernel(xc_ref, pos_ref, g_ref, wp_ref, ls_ref, o_ref, hcat_ref, hc, s2, s4, s8, *, rows):
    @pl.when(pl.program_id(0) == 0)
    def _():
        _zero_pads(hc, s2, s4, s8)

    xc = xc_ref[...]
    pos = pos_ref[...]
    is_new = pos >= 0
    hrow = jnp.where(is_new, _rms(xc, g_ref[...]), xc)
    hc[pl.ds(8, rows), :] = hrow
    a, wins = _pool_windows(hc, s2, s4, s8, rows)
    o_ref[...] = xc + _pool_project(a, wins, jnp.maximum(pos, 0), wp_ref, ls_ref[...])
    hcat_ref[...] = a


def _pool_sample(xcat, posrow, g, wp, ls, rows=768):
    R, D = xcat.shape
    return pl.pallas_call(
        functools.partial(_pool_sample_kernel, rows=rows),
        grid=(R // rows,),
        in_specs=[
            pl.BlockSpec((rows, D), lambda i: (i, 0)),
            pl.BlockSpec((rows, 1), lambda i: (i, 0)),
            pl.BlockSpec((1, D), lambda i: (0, 0)),
            pl.BlockSpec((4, POOL_GROUP_DIM, POOL_GROUP_DIM), lambda i: (0, 0, 0)),
            pl.BlockSpec((1, D), lambda i: (0, 0)),
        ],
        out_specs=[pl.BlockSpec((rows, D), lambda i: (i, 0)), pl.BlockSpec((rows, D), lambda i: (i, 0))],
        out_shape=[jax.ShapeDtypeStruct((R, D), F32), jax.ShapeDtypeStruct((R, D), F32)],
        scratch_shapes=[pltpu.VMEM((8 + rows, D), F32)] * 4,
        compiler_params=pltpu.CompilerParams(
            dimension_semantics=("arbitrary",), vmem_limit_bytes=VMEM_LIMIT),
        name="pool_sample",
    )(xcat, posrow, g, wp, ls)


def _route(logits):
    lane = lax.broadcasted_iota(jnp.int32, logits.shape, 1).astype(F32)
    big = jnp.float32(1 << 20)
    neg = jnp.float32(-jnp.inf)
    is_g = lane < N_EXPERT_GROUPS
    gl = jnp.where(is_g, logits, neg)
    m = jnp.max(gl, axis=-1, keepdims=True)
    gidx = jnp.min(jnp.where(gl == m, lane, big), axis=-1, keepdims=True)
    z = jnp.sum(jnp.where(is_g, jnp.exp(gl - m), 0.0), axis=-1, keepdims=True)
    gw = 1.0 / z
    lo = N_EXPERT_GROUPS + gidx * EXPERTS_PER_GROUP
    in_grp = (lane >= lo) & (lane < lo + EXPERTS_PER_GROUP)
    el = jnp.where(in_grp, logits, neg)
    v1 = jnp.max(el, axis=-1, keepdims=True)
    i1 = jnp.min(jnp.where(el == v1, lane, big), axis=-1, keepdims=True)
    el2 = jnp.where(lane == i1, neg, el)
    v2 = jnp.max(el2, axis=-1, keepdims=True)
    i2 = jnp.min(jnp.where(el2 == v2, lane, big), axis=-1, keepdims=True)
    t = jnp.exp(v2 - v1)
    w1 = 1.0 / (1.0 + t)
    w2 = t * w1
    return gw * (jnp.where(lane == i1, w1, 0.0) + jnp.where(lane == i2, w2, 0.0))


def _moe_kernel(*refs, has_oproj, has_final, tm, rc):
    it = iter(refs)
    x_ref = next(it)
    if has_oproj:
        oin_ref, wo_ref, bo_ref = next(it), next(it), next(it)
    nf_ref, wr_ref, br_ref, wg_ref, wu_ref, wd_ref = (next(it) for _ in range(6))
    if has_final:
        fn_ref = next(it)
    out_ref = next(it)
    xres, hb, comb, acc = (next(it) for _ in range(4))

    g = pl.program_id(1)

    @pl.when(g == 0)
    def _():
        x = x_ref[...]
        if has_oproj:
            x = x + jnp.dot(oin_ref[...], wo_ref[...], preferred_element_type=F32) + bo_ref[...]
        xres[...] = x
        h = _rms(x, nf_ref[...])
        hb[...] = h.astype(BF16)
        logits = jnp.dot(h, wr_ref[...], preferred_element_type=F32,
                         precision=lax.Precision.HIGHEST) + br_ref[...]
        comb[...] = _route(logits)
        acc[...] = jnp.zeros_like(acc)

    wd = wd_ref[...].reshape(EXPERTS_PER_GROUP * D_EXPERT, D_MODEL)
    for c in range(tm // rc):
        rows = pl.ds(c * rc, rc)
        hbc = hb[rows, :]
        cw = comb[rows, :]
        lane = lax.broadcasted_iota(jnp.int32, cw.shape, 1)
        parts = []
        for e in range(EXPERTS_PER_GROUP):
            col = N_EXPERT_GROUPS + g * EXPERTS_PER_GROUP + e
            ce = jnp.sum(jnp.where(lane == col, cw, 0.0), axis=-1, keepdims=True)
            gt = jnp.dot(hbc, wg_ref[e], preferred_element_type=F32)
            up = jnp.dot(hbc, wu_ref[e], preferred_element_type=F32)
            a = gt / (1.0 + jnp.exp(-gt)) * up * ce
            parts.append(a.astype(BF16))
        a_all = jnp.concatenate(parts, axis=-1)
        acc[rows, :] += jnp.dot(a_all, wd, preferred_element_type=F32)

    @pl.when(g == N_EXPERT_GROUPS - 1)
    def _():
        y = xres[...] + acc[...]
        if has_final:
            y = _rms(y, fn_ref[...])
        out_ref[...] = y


def _moe(x, nf, wr, br, wg, wu, wd, oproj=None, final=None, tm=512, rc=256):
    N, D = x.shape
    has_oproj = oproj is not None
    has_final = final is not None
    const = lambda i, g: (0, 0)
    tile = lambda i, g: (i, 0)
    args = [x]
    in_specs = [pl.BlockSpec((tm, D), tile)]
    if has_oproj:
        o, wo, bo = oproj
        args += [o, wo, bo]
        in_specs += [pl.BlockSpec((tm, D), tile), pl.BlockSpec((D, D), const), pl.BlockSpec((1, D), const)]
    args += [nf, wr, br, wg, wu, wd]
    in_specs += [
        pl.BlockSpec((1, D), const),
        pl.BlockSpec((D, ROUTER_LANES), const),
        pl.BlockSpec((1, ROUTER_LANES), const),
        pl.BlockSpec((EXPERTS_PER_GROUP, D, D_EXPERT), lambda i, g: (g, 0, 0)),
        pl.BlockSpec((EXPERTS_PER_GROUP, D, D_EXPERT), lambda i, g: (g, 0, 0)),
        pl.BlockSpec((EXPERTS_PER_GROUP, D_EXPERT, D), lambda i, g: (g, 0, 0)),
    ]
    if has_final:
        args.append(final)
        in_specs.append(pl.BlockSpec((1, D), const))
    return pl.pallas_call(
        functools.partial(_moe_kernel, has_oproj=has_oproj, has_final=has_final, tm=tm, rc=rc),
        grid=(N // tm, N_EXPERT_GROUPS),
        in_specs=in_specs,
        out_specs=pl.BlockSpec((tm, D), tile),
        out_shape=jax.ShapeDtypeStruct((N, D), F32),
        scratch_shapes=[
            pltpu.VMEM((tm, D), F32),
            pltpu.VMEM((tm, D), BF16),
            pltpu.VMEM((tm, ROUTER_LANES), F32),
            pltpu.VMEM((tm, D), F32),
        ],
        compiler_params=pltpu.CompilerParams(
            dimension_semantics=("arbitrary", "arbitrary"), vmem_limit_bytes=VMEM_LIMIT),
        name="moe",
    )(*args)


def _swap_halves(x, low):
    return jnp.where(low, pltpu.roll(x, LANES - 32, axis=1), pltpu.roll(x, 32, axis=1))


def _qkv_kernel(x_ref, g_ref, w_ref, b_ref, cos_ref, sin_ref, q_ref, kd_ref, vd_ref, k_ref, v_ref):
    h = _rms(x_ref[...], g_ref[...]).astype(BF16)
    qkv = jnp.dot(h, w_ref[...], preferred_element_type=F32) + b_ref[...]
    cos = cos_ref[...]
    sin = sin_ref[...]
    lane = lax.broadcasted_iota(jnp.int32, cos.shape, 1)
    low32 = (lane % HEAD_DIM) < (HEAD_DIM // 2)
    low64 = lane < HEAD_DIM
    nq = N_HEADS * HEAD_DIM
    nk = N_KV_HEADS * HEAD_DIM

    def rope(c):
        return c * cos + _swap_halves(c, low32) * sin

    def dup(c):
        r = pltpu.roll(c, HEAD_DIM, axis=1)
        return jnp.where(low64, c, r), jnp.where(low64, r, c)

    for j in range(nq // LANES):
        c = rope(qkv[:, j * LANES:(j + 1) * LANES]) * (HEAD_DIM ** -0.5)
        q_ref[:, j * LANES:(j + 1) * LANES] = c.astype(BF16)
    for j in range(nk // LANES):
        c = rope(qkv[:, nq + j * LANES:nq + (j + 1) * LANES])
        k_ref[:, j * LANES:(j + 1) * LANES] = c
        d0, d1 = dup(c)
        kd_ref[:, 2 * j * LANES:(2 * j + 1) * LANES] = d0.astype(BF16)
        kd_ref[:, (2 * j + 1) * LANES:(2 * j + 2) * LANES] = d1.astype(BF16)
        c = qkv[:, nq + nk + j * LANES:nq + nk + (j + 1) * LANES]
        v_ref[:, j * LANES:(j + 1) * LANES] = c
        d0, d1 = dup(c)
        vd_ref[:, 2 * j * LANES:(2 * j + 1) * LANES] = d0.astype(BF16)
        vd_ref[:, (2 * j + 1) * LANES:(2 * j + 2) * LANES] = d1.astype(BF16)


def _qkv(x, g, w, b, cos, sin, tm, n_pos_tiles):
    N, D = x.shape
    nq, nk = N_HEADS * HEAD_DIM, N_KV_HEADS * HEAD_DIM
    const = lambda i: (0, 0)
    tile = lambda i: (i, 0)
    ptile = lambda i: (i % n_pos_tiles, 0)
    return pl.pallas_call(
        _qkv_kernel,
        grid=(N // tm,),
        in_specs=[
            pl.BlockSpec((tm, D), tile),
            pl.BlockSpec((1, D), const),
            pl.BlockSpec((D, nq + 2 * nk), const),
            pl.BlockSpec((1, nq + 2 * nk), const),
            pl.BlockSpec((tm, LANES), ptile),
            pl.BlockSpec((tm, LANES), ptile),
        ],
        out_specs=[
            pl.BlockSpec((tm, nq), tile),
            pl.BlockSpec((tm, 2 * nk), tile),
            pl.BlockSpec((tm, 2 * nk), tile),
            pl.BlockSpec((tm, nk), tile),
            pl.BlockSpec((tm, nk), tile),
        ],
        out_shape=[
            jax.ShapeDtypeStruct((N, nq), BF16),
            jax.ShapeDtypeStruct((N, 2 * nk), BF16),
            jax.ShapeDtypeStruct((N, 2 * nk), BF16),
            jax.ShapeDtypeStruct((N, nk), F32),
            jax.ShapeDtypeStruct((N, nk), F32),
        ],
        compiler_params=pltpu.CompilerParams(
            dimension_semantics=("arbitrary",), vmem_limit_bytes=VMEM_LIMIT),
        name="qkv",
    )(x, g, w, b, cos, sin)


def _attend(qc_list, kd, vd, mask, sinkcol):
    R = qc_list[0].shape[0]
    lane = lax.broadcasted_iota(jnp.int32, qc_list[0].shape, 1)
    low = lane < HEAD_DIM
    zero = jnp.zeros_like(qc_list[0])
    stack = []
    for qc in qc_list:
        stack.append(jnp.where(low, qc, zero))
        stack.append(jnp.where(low, zero, qc))
    qs = jnp.concatenate(stack, axis=0)
    s = lax.dot_general(qs, kd, (((1,), (1,)), ((), ())), preferred_element_type=F32)
    bias = jnp.where(mask, 0.0, -jnp.inf).astype(F32)
    s = s + jnp.concatenate([bias] * 4, axis=0)
    m = jnp.maximum(jnp.max(s, axis=-1, keepdims=True), sinkcol)
    p = jnp.exp(s - m)
    den = jnp.sum(p, axis=-1, keepdims=True) + jnp.exp(sinkcol - m)
    p = (p / den).astype(BF16)
    o = jnp.dot(p, vd, preferred_element_type=F32)
    lowf = lax.broadcasted_iota(jnp.int32, (R, LANES), 1) < HEAD_DIM
    return (jnp.where(lowf, o[0:R], o[R:2 * R]), jnp.where(lowf, o[2 * R:3 * R], o[3 * R:4 * R]))


def _attn_prompt_kernel(q_ref, kp_ref, kc_ref, vp_ref, vc_ref, sink_ref, o_ref, *, tq):
    n = pl.program_id(1)
    B = WINDOW
    kd = jnp.concatenate([kp_ref[...], kc_ref[...]], axis=0)
    vd = jnp.concatenate([vp_ref[...], vc_ref[...]], axis=0)
    qi = lax.broadcasted_iota(jnp.int32, (B, 2 * B), 0)
    kj = lax.broadcasted_iota(jnp.int32, (B, 2 * B), 1)
    rel = B + qi - kj
    band = (rel >= 0) & (rel < WINDOW)
    for j in range(tq // B):
        mask = band
        if j == 0:
            mask = band & ((kj >= B) | (n > 0))
        for g in range(N_KV_HEADS):
            qc = [q_ref[j * B:(j + 1) * B, (2 * g + i) * LANES:(2 * g + i + 1) * LANES] for i in range(2)]
            kg = kd[j * B:(j + 2) * B, g * LANES:(g + 1) * LANES]
            vg = vd[j * B:(j + 2) * B, g * LANES:(g + 1) * LANES]
            o0, o1 = _attend(qc, kg, vg, mask, sink_ref[g])
            o_ref[j * B:(j + 1) * B, 2 * g * LANES:(2 * g + 1) * LANES] = o0.astype(BF16)
            o_ref[j * B:(j + 1) * B, (2 * g + 1) * LANES:(2 * g + 2) * LANES] = o1.astype(BF16)


def _attn_prompt(q, kd, vd, sinkcol, batch, seq, tq=512):
    N = q.shape[0]
    nt = seq // tq
    r = tq // WINDOW
    cur = lambda b, n: (b * nt + n, 0)
    prev = lambda b, n: (jnp.maximum((b * nt + n) * r - 1, 0), 0)
    kvw = 2 * N_KV_HEADS * HEAD_DIM
    return pl.pallas_call(
        functools.partial(_attn_prompt_kernel, tq=tq),
        grid=(batch, nt),
        in_specs=[
            pl.BlockSpec((tq, N_HEADS * HEAD_DIM), cur),
            pl.BlockSpec((WINDOW, kvw), prev),
            pl.BlockSpec((tq, kvw), cur),
            pl.BlockSpec((WINDOW, kvw), prev),
            pl.BlockSpec((tq, kvw), cur),
            pl.BlockSpec((N_KV_HEADS, 4 * WINDOW, 1), lambda b, n: (0, 0, 0)),
        ],
        out_specs=pl.BlockSpec((tq, N_HEADS * HEAD_DIM), cur),
        out_shape=jax.ShapeDtypeStruct((N, N_HEADS * HEAD_DIM), BF16),
        compiler_params=pltpu.CompilerParams(
            dimension_semantics=("arbitrary", "arbitrary"), vmem_limit_bytes=VMEM_LIMIT),
        name="attn_prompt",
    )(q, kd, kd, vd, vd, sinkcol)


def _dup_cache(c, g):
    low = lax.broadcasted_iota(jnp.int32, c.shape, 1) < HEAD_DIM
    r = pltpu.roll(c, HEAD_DIM, axis=1)
    d = jnp.where(low, c, r) if g % 2 == 0 else jnp.where(low, r, c)
    return d.astype(BF16)


def _attn_sample_kernel(q_ref, kdn_ref, vdn_ref, kn_ref, vn_ref, ck_ref, cv_ref, mask_ref, kval_ref, sink_ref,
                        o_ref, nk_ref, nv_ref, *, bs, t):
    W = WINDOW
    ck = ck_ref[...]
    cv = cv_ref[...]
    nk_ref[:, 0:W - t, :] = ck[:, t:, :]
    nk_ref[:, W - t:, :] = kn_ref[...].reshape(bs, t, N_KV_HEADS * HEAD_DIM)
    nv_ref[:, 0:W - t, :] = cv[:, t:, :]
    nv_ref[:, W - t:, :] = vn_ref[...].reshape(bs, t, N_KV_HEADS * HEAD_DIM)

    ck2 = ck.reshape(bs * W, N_KV_HEADS * HEAD_DIM)
    cv2 = cv.reshape(bs * W, N_KV_HEADS * HEAD_DIM)
    mask = (mask_ref[...] > 0) & (kval_ref[0] > 0)
    for g in range(N_KV_HEADS):
        m = g // 2
        kg = jnp.concatenate([_dup_cache(ck2[:, m * LANES:(m + 1) * LANES], g),
                              kdn_ref[:, g * LANES:(g + 1) * LANES]], axis=0)
        vg = jnp.concatenate([_dup_cache(cv2[:, m * LANES:(m + 1) * LANES], g),
                              vdn_ref[:, g * LANES:(g + 1) * LANES]], axis=0)
        qc = [q_ref[:, (2 * g + i) * LANES:(2 * g + i + 1) * LANES] for i in range(2)]
        o0, o1 = _attend(qc, kg, vg, mask, sink_ref[g])
        o_ref[:, 2 * g * LANES:(2 * g + 1) * LANES] = o0.astype(BF16)
        o_ref[:, (2 * g + 1) * LANES:(2 * g + 2) * LANES] = o1.astype(BF16)


def _attn_sample(q, kdn, vdn, kn, vn, ck, cv, mask, kval, sinkcol, bs, t):
    Bd = ck.shape[0]
    rows = bs * t
    kvw = N_KV_HEADS * HEAD_DIM
    S = bs * WINDOW + rows
    tile = lambda i: (i, 0)
    tile3 = lambda i: (i, 0, 0)
    return pl.pallas_call(
        functools.partial(_attn_sample_kernel, bs=bs, t=t),
        grid=(Bd // bs,),
        in_specs=[
            pl.BlockSpec((rows, N_HEADS * HEAD_DIM), tile),
            pl.BlockSpec((rows, 2 * kvw), tile),
            pl.BlockSpec((rows, 2 * kvw), tile),
            pl.BlockSpec((rows, kvw), tile),
            pl.BlockSpec((rows, kvw), tile),
            pl.BlockSpec((bs, WINDOW, kvw), tile3),
            pl.BlockSpec((bs, WINDOW, kvw), tile3),
            pl.BlockSpec((rows, S), lambda i: (0, 0)),
            pl.BlockSpec((1, 1, S), tile3),
            pl.BlockSpec((N_KV_HEADS, 4 * rows, 1), lambda i: (0, 0, 0)),
        ],
        out_specs=[
            pl.BlockSpec((rows, N_HEADS * HEAD_DIM), tile),
            pl.BlockSpec((bs, WINDOW, kvw), tile3),
            pl.BlockSpec((bs, WINDOW, kvw), tile3),
        ],
        out_shape=[
            jax.ShapeDtypeStruct((Bd * t, N_HEADS * HEAD_DIM), BF16),
            jax.ShapeDtypeStruct((Bd, WINDOW, kvw), F32),
            jax.ShapeDtypeStruct((Bd, WINDOW, kvw), F32),
        ],
        compiler_params=pltpu.CompilerParams(
            dimension_semantics=("arbitrary",), vmem_limit_bytes=VMEM_LIMIT),
        name="attn_sample",
    )(q, kdn, vdn, kn, vn, ck, cv, mask, kval, sinkcol)


def _rope_tables(pos):
    inv = ROPE_THETA ** (-jnp.arange(0, HEAD_DIM, 2, dtype=F32) / HEAD_DIM)
    lane = jnp.arange(LANES)
    inv_l = inv[(lane % HEAD_DIM) % (HEAD_DIM // 2)]
    sign = jnp.where((lane % HEAD_DIM) < HEAD_DIM // 2, -1.0, 1.0).astype(F32)
    ang = pos.astype(F32)[:, None] * inv_l[None, :]
    return jnp.cos(ang), jnp.sin(ang) * sign[None, :]


def _sink_columns(sinks, rows):
    s = sinks.astype(F32).reshape(N_KV_HEADS, N_HEADS // N_KV_HEADS, 1)
    return jnp.broadcast_to(s[:, :, None, :], (N_KV_HEADS, 4, rows, 1)).reshape(N_KV_HEADS, 4 * rows, 1)


def kernel(x_prompt, x_sample, state_pool, cache_k, cache_v, sample_start, norm_mix, norm_ffn, norm_final,
           w_pool, ls_pool, w_qkv, b_qkv, sinks, w_o, b_o, w_rg, b_rg, w_re, b_re, w_gate, w_up, w_down):
    B, T, D = x_prompt.shape
    Bd, Td, _ = x_sample.shape
    kvw = N_KV_HEADS * HEAD_DIM
    row = lambda v: v.reshape(1, -1).astype(F32)

    wp = w_pool[0].astype(BF16)
    wqkv = w_qkv[0].astype(BF16)
    wo = w_o[0].astype(BF16)
    wg, wu, wd = w_gate.astype(BF16), w_up.astype(BF16), w_down.astype(BF16)
    pad = ROUTER_LANES - N_EXPERT_GROUPS - N_EXPERTS
    wr = [jnp.concatenate([w_rg[l], w_re[l], jnp.zeros((D, pad), F32)], axis=1) for l in range(2)]
    br = [jnp.concatenate([b_rg[l], b_re[l], jnp.zeros((pad,), F32)]).reshape(1, -1) for l in range(2)]

    start = sample_start.astype(jnp.int32)

    x1p, pool_p16 = _pool_prompt(x_prompt, jnp.zeros((B, HALO, D), F32), row(norm_mix[0]), wp, row(ls_pool[0]))
    seg = HALO + Td
    xcat = jnp.concatenate([jnp.zeros((Bd, 1, D), F32), state_pool[0], x_sample], axis=1).reshape(Bd * seg, D)
    r = jnp.arange(seg, dtype=jnp.int32)[None, :]
    posrow = jnp.where(r >= HALO, start[:, None] + r - HALO, -1).reshape(Bd * seg, 1)
    x1cat, hcat = _pool_sample(xcat, posrow, row(norm_mix[0]), wp, row(ls_pool[0]))
    x1s = x1cat.reshape(Bd, seg, D)[:, HALO:].reshape(Bd * Td, D)
    pool_s = hcat.reshape(Bd, seg, D)[:, seg - POOL_BUF:]
    pool_p = pool_p16[:, HALO - POOL_BUF:]

    moe0 = functools.partial(_moe, nf=row(norm_ffn[0]), wr=wr[0], br=br[0], wg=wg[0], wu=wu[0], wd=wd[0])
    x2p = moe0(x1p.reshape(B * T, D))
    x2s = moe0(x1s)

    cos_p, sin_p = _rope_tables(jnp.arange(T, dtype=jnp.int32))
    pos_s = (start[:, None] + jnp.arange(Td, dtype=jnp.int32)[None, :]).reshape(-1)
    cos_s, sin_s = _rope_tables(pos_s)
    g1 = row(norm_mix[1])
    tmq = 1024
    qp, kdp, vdp, kp, vp = _qkv(x2p, g1, wqkv, row(b_qkv[0]), cos_p, sin_p, tmq, T // tmq)
    qs, kds, vds, ks, vs = _qkv(x2s, g1, wqkv, row(b_qkv[0]), cos_s, sin_s, tmq, (Bd * Td) // tmq)

    op = _attn_prompt(qp, kdp, vdp, _sink_columns(sinks[0], WINDOW), B, T)

    bs = 8
    rows = bs * Td
    W = cache_k.shape[2]
    qrow = jnp.arange(rows)
    ccol = jnp.arange(bs * W)
    ncol = jnp.arange(rows)
    samp_q, t_q = qrow // Td, qrow % Td
    m_cache = (samp_q[:, None] == (ccol // W)[None, :]) & ((ccol % W)[None, :] > t_q[:, None])
    m_new = (samp_q[:, None] == (ncol // Td)[None, :]) & ((ncol % Td)[None, :] <= t_q[:, None])
    amask = jnp.concatenate([m_cache, m_new], axis=1).astype(F32)
    kv_cache = (jnp.arange(W, dtype=jnp.int32)[None, :] >= (W - start)[:, None]).reshape(Bd // bs, bs * W)
    kval = jnp.concatenate([kv_cache, jnp.ones((Bd // bs, rows), bool)], axis=1).astype(F32)
    kval = kval.reshape(Bd // bs, 1, bs * W + rows)
    osamp, nk_s, nv_s = _attn_sample(
        qs, kds, vds, ks, vs, cache_k[0].reshape(Bd, W, kvw), cache_v[0].reshape(Bd, W, kvw),
        amask, kval, _sink_columns(sinks[0], rows), bs, Td)

    moe1 = functools.partial(_moe, nf=row(norm_ffn[1]), wr=wr[1], br=br[1], wg=wg[1], wu=wu[1], wd=wd[1],
                             final=row(norm_final))
    yp = moe1(x2p, oproj=(op, wo, row(b_o[0])))
    ys = moe1(x2s, oproj=(osamp, wo, row(b_o[0])))

    keep = min(WINDOW, T)
    k_p = kp.reshape(B, T, N_KV_HEADS, HEAD_DIM)[:, T - keep:][None]
    v_p = vp.reshape(B, T, N_KV_HEADS, HEAD_DIM)[:, T - keep:][None]
    return (yp.reshape(B, T, D), ys.reshape(Bd, Td, D), pool_p[None], k_p, v_p, pool_s[None],
            nk_s.reshape(1, Bd, W, N_KV_HEADS, HEAD_DIM), nv_s.reshape(1, Bd, W, N_KV_HEADS, HEAD_DIM))
```

```python
import functools

import jax
import jax.numpy as jnp
from jax import lax
from jax.experimental import pallas as pl
from jax.experimental.pallas import tpu as pltpu

F32 = jnp.float32
BF16 = jnp.bfloat16

D_MODEL = 1024
POOL_WINDOWS = (2, 4, 8, 16)
POOL_GROUP_DIM = 256
POOL_BUF = 15
HALO = 16
HEAD_DIM = 64
N_HEADS = 16
N_KV_HEADS = 4
WINDOW = 128
ROPE_THETA = 10000.0
N_EXPERT_GROUPS = 4
EXPERTS_PER_GROUP = 4
N_EXPERTS = 16
D_EXPERT = 256
RMS_EPS = 1e-6
LANES = 128
ROUTER_LANES = 128
VMEM_LIMIT = 56 * 1024 * 1024


def _rms(x, g):
    ms = jnp.mean(x * x, axis=-1, keepdims=True)
    return x * lax.rsqrt(ms + RMS_EPS) * g


def _pool_windows(hc, s2, s4, s8, rows):
    G = POOL_GROUP_DIM
    a = hc[pl.ds(8, rows), :]
    v2 = a + hc[pl.ds(7, rows), :]
    s2[pl.ds(8, rows), :] = v2
    v4 = v2[:, G:] + s2[pl.ds(6, rows), G:]
    s4[pl.ds(8, rows), G:] = v4
    v8 = v4[:, G:] + s4[pl.ds(4, rows), 2 * G:]
    s8[pl.ds(8, rows), 2 * G:] = v8
    v16 = v8[:, G:] + s8[pl.ds(0, rows), 3 * G:]
    return a, (v2[:, :G], v4[:, :G], v8[:, :G], v16)


def _pool_project(h, wins, pos, wp_ref, ls):
    G = POOL_GROUP_DIM
    outs = []
    for g, w in enumerate(POOL_WINDOWS):
        cnt = jnp.minimum(w, pos + 1).astype(F32)
        d = wins[g] / cnt - h[:, g * G:(g + 1) * G]
        outs.append(jnp.dot(d.astype(BF16), wp_ref[g], preferred_element_type=F32))
    return jnp.concatenate(outs, axis=-1) * ls


def _zero_pads(*refs):
    for r in refs:
        r[pl.ds(0, 8), :] = jnp.zeros((8, D_MODEL), F32)


def _pool_prompt_kernel(x_ref, buf_ref, g_ref, wp_ref, ls_ref, o_ref, nb_ref, hc, s2, s4, s8, *, tq):
    t = pl.program_id(1)

    @pl.when(t == 0)
    def _():
        _zero_pads(hc, s2, s4, s8)
        hc[pl.ds(8, HALO), :] = buf_ref[0]

    @pl.when(t > 0)
    def _():
        hc[pl.ds(8, HALO), :] = hc[pl.ds(8 + tq, HALO), :]

    x = x_ref[0]
    hc[pl.ds(8 + HALO, tq), :] = _rms(x, g_ref[...])
    a, wins = _pool_windows(hc, s2, s4, s8, HALO + tq)
    h = a[HALO:]
    wins = tuple(w[HALO:] for w in wins)
    pos = t * tq + lax.broadcasted_iota(jnp.int32, (tq, 1), 0)
    o_ref[0] = x + _pool_project(h, wins, pos, wp_ref, ls_ref[...])

    @pl.when(t == pl.num_programs(1) - 1)
    def _():
        nb_ref[0] = hc[pl.ds(8 + tq, HALO), :]


def _pool_prompt(x, buf16, g, wp, ls, tq=512):
    B, T, D = x.shape
    return pl.pallas_call(
        functools.partial(_pool_prompt_kernel, tq=tq),
        grid=(B, T // tq),
        in_specs=[
            pl.BlockSpec((1, tq, D), lambda b, t: (b, t, 0)),
            pl.BlockSpec((1, HALO, D), lambda b, t: (b, 0, 0)),
            pl.BlockSpec((1, D), lambda b, t: (0, 0)),
            pl.BlockSpec((4, POOL_GROUP_DIM, POOL_GROUP_DIM), lambda b, t: (0, 0, 0)),
            pl.BlockSpec((1, D), lambda b, t: (0, 0)),
        ],
        out_specs=[
            pl.BlockSpec((1, tq, D), lambda b, t: (b, t, 0)),
            pl.BlockSpec((1, HALO, D), lambda b, t: (b, 0, 0)),
        ],
        out_shape=[jax.ShapeDtypeStruct((B, T, D), F32), jax.ShapeDtypeStruct((B, HALO, D), F32)],
        scratch_shapes=[pltpu.VMEM((8 + HALO + tq, D), F32)] * 4,
        compiler_params=pltpu.CompilerParams(
            dimension_semantics=("arbitrary", "arbitrary"), vmem_limit_bytes=VMEM_LIMIT),
        name="pool_prompt",
    )(x, buf16, g, wp, ls)


def _pool_sample_kernel(xc_ref, pos_ref, g_ref, wp_ref, ls_ref, o_ref, hcat_ref, hc, s2, s4, s8, *, rows):
    @pl.when(pl.program_id(0) == 0)
    def _():
        _zero_pads(hc, s2, s4, s8)

    xc = xc_ref[...]
    pos = pos_ref[...]
    is_new = pos >= 0
    hrow = jnp.where(is_new, _rms(xc, g_ref[...]), xc)
    hc[pl.ds(8, rows), :] = hrow
    a, wins = _pool_windows(hc, s2, s4, s8, rows)
    o_ref[...] = xc + _pool_project(a, wins, jnp.maximum(pos, 0), wp_ref, ls_ref[...])
    hcat_ref[...] = a


def _pool_sample(xcat, posrow, g, wp, ls, rows=768):
    R, D = xcat.shape
    return pl.pallas_call(
        functools.partial(_pool_sample_kernel, rows=rows),
        grid=(R // rows,),
        in_specs=[
            pl.BlockSpec((rows, D), lambda i: (i, 0)),
            pl.BlockSpec((rows, 1), lambda i: (i, 0)),
            pl.BlockSpec((1, D), lambda i: (0, 0)),
            pl.BlockSpec((4, POOL_GROUP_DIM, POOL_GROUP_DIM), lambda i: (0, 0, 0)),
            pl.BlockSpec((1, D), lambda i: (0, 0)),
        ],
        out_specs=[pl.BlockSpec((rows, D), lambda i: (i, 0)), pl.BlockSpec((rows, D), lambda i: (i, 0))],
        out_shape=[jax.ShapeDtypeStruct((R, D), F32), jax.ShapeDtypeStruct((R, D), F32)],
        scratch_shapes=[pltpu.VMEM((8 + rows, D), F32)] * 4,
        compiler_params=pltpu.CompilerParams(
            dimension_semantics=("arbitrary",), vmem_limit_bytes=VMEM_LIMIT),
        name="pool_sample",
    )(xcat, posrow, g, wp, ls)


def _route(logits):
    lane = lax.broadcasted_iota(jnp.int32, logits.shape, 1).astype(F32)
    big = jnp.float32(1 << 20)
    neg = jnp.float32(-jnp.inf)
    is_g = lane < N_EXPERT_GROUPS
    gl = jnp.where(is_g, logits, neg)
    m = jnp.max(gl, axis=-1, keepdims=True)
    gidx = jnp.min(jnp.where(gl == m, lane, big), axis=-1, keepdims=True)
    z = jnp.sum(jnp.where(is_g, jnp.exp(gl - m), 0.0), axis=-1, keepdims=True)
    gw = 1.0 / z
    lo = N_EXPERT_GROUPS + gidx * EXPERTS_PER_GROUP
    in_grp = (lane >= lo) & (lane < lo + EXPERTS_PER_GROUP)
    el = jnp.where(in_grp, logits, neg)
    v1 = jnp.max(el, axis=-1, keepdims=True)
    i1 = jnp.min(jnp.where(el == v1, lane, big), axis=-1, keepdims=True)
    el2 = jnp.where(lane == i1, neg, el)
    v2 = jnp.max(el2, axis=-1, keepdims=True)
    i2 = jnp.min(jnp.where(el2 == v2, lane, big), axis=-1, keepdims=True)
    t = jnp.exp(v2 - v1)
    w1 = 1.0 / (1.0 + t)
    w2 = t * w1
    cw = gw * (jnp.where(lane == i1 - lo, w1, 0.0) + jnp.where(lane == i2 - lo, w2, 0.0))
    return gidx, cw


def _moe_kernel(*refs, has_oproj, has_final, tm, bm):
    it = iter(refs)
    x_ref = next(it)
    if has_oproj:
        oin_ref, wo_ref, bo_ref = next(it), next(it), next(it)
    nf_ref, wr_ref, br_ref, wg_ref, wu_ref, wd_ref = (next(it) for _ in range(6))
    if has_final:
        fn_ref = next(it)
    out_ref = next(it)
    hs, cs, ys = (next(it) for _ in range(3))
    nblk = tm // bm + N_EXPERT_GROUPS
    cap = nblk * bm
    shift = bm.bit_length() - 1

    x = x_ref[...]
    if has_oproj:
        x = x + jnp.dot(oin_ref[...], wo_ref[...], preferred_element_type=F32) + bo_ref[...]
    out_ref[...] = x
    h = _rms(x, nf_ref[...])
    hb = h.astype(BF16)
    logits = jnp.dot(h, wr_ref[...], preferred_element_type=F32,
                     precision=lax.Precision.HIGHEST) + br_ref[...]
    gidx, cw = _route(logits)
    cw_hi = cw.astype(BF16)
    cw_lo = pltpu.roll(cw - cw_hi.astype(F32), EXPERTS_PER_GROUP, axis=1)
    cwb = cw_hi + cw_lo.astype(BF16)

    lane = lax.broadcasted_iota(jnp.int32, (tm, LANES), 1).astype(F32)
    oh = jnp.where(lane == gidx, 1.0, 0.0)
    ri = lax.broadcasted_iota(jnp.int32, (tm, tm), 0)
    ci = lax.broadcasted_iota(jnp.int32, (tm, tm), 1)
    before = jnp.where(ci < ri, 1.0, 0.0).astype(BF16)
    cnt_before = jnp.dot(before, oh.astype(BF16), preferred_element_type=F32)
    pos = jnp.sum(oh * cnt_before, axis=-1, keepdims=True)
    start_blk, n_blk = [], []
    off = jnp.int32(0)
    for g in range(N_EXPERT_GROUPS):
        in_g = gidx == g
        n_g = jnp.sum(jnp.where(in_g, 1.0, 0.0)).astype(jnp.int32)
        blocks = (n_g + (bm - 1)) >> shift
        start_blk.append(off)
        n_blk.append(blocks)
        pos = pos + jnp.where(in_g, (off << shift).astype(F32), 0.0)
        off = off + blocks
    pos_row = jnp.transpose(jnp.broadcast_to(pos, (tm, LANES)))[0:1, :]

    for b in range(nblk):
        sub = (b * bm + lax.broadcasted_iota(jnp.int32, (bm, tm), 0)).astype(F32)
        sel = jnp.where(pos_row == sub, 1.0, 0.0).astype(BF16)
        hs[b * bm:(b + 1) * bm, :] = jnp.dot(sel, hb, preferred_element_type=F32).astype(BF16)
        cs[b * bm:(b + 1) * bm, :] = jnp.dot(sel, cwb, preferred_element_type=F32)
    ys[...] = jnp.zeros_like(ys)

    for g in range(N_EXPERT_GROUPS):
        wd = wd_ref[g * EXPERTS_PER_GROUP:(g + 1) * EXPERTS_PER_GROUP].reshape(
            EXPERTS_PER_GROUP * D_EXPERT, D_MODEL)

        def block(b, carry, g=g, wd=wd):
            rows = pl.ds(pl.multiple_of(b * bm, bm), bm)
            hblk = hs[rows, :]
            cblk = cs[rows, :]
            ln = lax.broadcasted_iota(jnp.int32, cblk.shape, 1)
            parts = []
            for e in range(EXPERTS_PER_GROUP):
                ce = jnp.sum(jnp.where((ln == e) | (ln == e + EXPERTS_PER_GROUP), cblk, 0.0),
                             axis=-1, keepdims=True)
                gt = jnp.dot(hblk, wg_ref[g * EXPERTS_PER_GROUP + e], preferred_element_type=F32)
                up = jnp.dot(hblk, wu_ref[g * EXPERTS_PER_GROUP + e], preferred_element_type=F32)
                a = gt / (1.0 + jnp.exp(-gt)) * up * ce
                parts.append(a.astype(BF16))
            a_all = jnp.concatenate(parts, axis=-1)
            ys[rows, :] = jnp.dot(a_all, wd, preferred_element_type=F32).astype(BF16)
            return carry

        lax.fori_loop(start_blk[g], start_blk[g] + n_blk[g], block, 0)

    rc = 256
    for c in range(tm // rc):
        col = lax.broadcasted_iota(jnp.int32, (rc, cap), 1).astype(F32)
        sel = jnp.where(pos[c * rc:(c + 1) * rc] == col, 1.0, 0.0).astype(BF16)
        y = out_ref[c * rc:(c + 1) * rc, :] + jnp.dot(sel, ys[...], preferred_element_type=F32)
        if has_final:
            y = _rms(y, fn_ref[...])
        out_ref[c * rc:(c + 1) * rc, :] = y


def _moe(x, nf, wr, br, wg, wu, wd, oproj=None, final=None, tm=512, bm=128):
    N, D = x.shape
    has_oproj = oproj is not None
    has_final = final is not None
    const2 = lambda i: (0, 0)
    const3 = lambda i: (0, 0, 0)
    tile = lambda i: (i, 0)
    once = pl.Buffered(1)
    args = [x]
    in_specs = [pl.BlockSpec((tm, D), tile)]
    if has_oproj:
        o, wo, bo = oproj
        args += [o, wo, bo]
        in_specs += [pl.BlockSpec((tm, D), tile), pl.BlockSpec((D, D), const2, pipeline_mode=once),
                     pl.BlockSpec((1, D), const2)]
    args += [nf, wr, br, wg, wu, wd]
    in_specs += [
        pl.BlockSpec((1, D), const2),
        pl.BlockSpec((D, ROUTER_LANES), const2, pipeline_mode=once),
        pl.BlockSpec((1, ROUTER_LANES), const2),
        pl.BlockSpec((N_EXPERTS, D, D_EXPERT), const3, pipeline_mode=once),
        pl.BlockSpec((N_EXPERTS, D, D_EXPERT), const3, pipeline_mode=once),
        pl.BlockSpec((N_EXPERTS, D_EXPERT, D), const3, pipeline_mode=once),
    ]
    if has_final:
        args.append(final)
        in_specs.append(pl.BlockSpec((1, D), const2))
    cap = (tm // bm + N_EXPERT_GROUPS) * bm
    return pl.pallas_call(
        functools.partial(_moe_kernel, has_oproj=has_oproj, has_final=has_final, tm=tm, bm=bm),
        grid=(N // tm,),
        in_specs=in_specs,
        out_specs=pl.BlockSpec((tm, D), tile),
        out_shape=jax.ShapeDtypeStruct((N, D), F32),
        scratch_shapes=[
            pltpu.VMEM((cap, D), BF16),
            pltpu.VMEM((cap, LANES), F32),
            pltpu.VMEM((cap, D), BF16),
        ],
        compiler_params=pltpu.CompilerParams(
            dimension_semantics=("arbitrary",), vmem_limit_bytes=VMEM_LIMIT),
        name="moe",
    )(*args)


def _swap_halves(x, low):
    return jnp.where(low, pltpu.roll(x, LANES - 32, axis=1), pltpu.roll(x, 32, axis=1))


def _qkv_kernel(x_ref, g_ref, w_ref, b_ref, cos_ref, sin_ref, q_ref, kd_ref, vd_ref, k_ref, v_ref):
    h = _rms(x_ref[...], g_ref[...]).astype(BF16)
    qkv = jnp.dot(h, w_ref[...], preferred_element_type=F32) + b_ref[...]
    cos = cos_ref[...]
    sin = sin_ref[...]
    lane = lax.broadcasted_iota(jnp.int32, cos.shape, 1)
    low32 = (lane % HEAD_DIM) < (HEAD_DIM // 2)
    low64 = lane < HEAD_DIM
    nq = N_HEADS * HEAD_DIM
    nk = N_KV_HEADS * HEAD_DIM

    def rope(c):
        return c * cos + _swap_halves(c, low32) * sin

    def dup(c):
        r = pltpu.roll(c, HEAD_DIM, axis=1)
        return jnp.where(low64, c, r), jnp.where(low64, r, c)

    for j in range(nq // LANES):
        c = rope(qkv[:, j * LANES:(j + 1) * LANES]) * (HEAD_DIM ** -0.5)
        q_ref[:, j * LANES:(j + 1) * LANES] = c.astype(BF16)
    for j in range(nk // LANES):
        c = rope(qkv[:, nq + j * LANES:nq + (j + 1) * LANES])
        k_ref[:, j * LANES:(j + 1) * LANES] = c
        d0, d1 = dup(c)
        kd_ref[:, 2 * j * LANES:(2 * j + 1) * LANES] = d0.astype(BF16)
        kd_ref[:, (2 * j + 1) * LANES:(2 * j + 2) * LANES] = d1.astype(BF16)
        c = qkv[:, nq + nk + j * LANES:nq + nk + (j + 1) * LANES]
        v_ref[:, j * LANES:(j + 1) * LANES] = c
        d0, d1 = dup(c)
        vd_ref[:, 2 * j * LANES:(2 * j + 1) * LANES] = d0.astype(BF16)
        vd_ref[:, (2 * j + 1) * LANES:(2 * j + 2) * LANES] = d1.astype(BF16)


def _qkv(x, g, w, b, cos, sin, tm, n_pos_tiles):
    N, D = x.shape
    nq, nk = N_HEADS * HEAD_DIM, N_KV_HEADS * HEAD_DIM
    const = lambda i: (0, 0)
    tile = lambda i: (i, 0)
    ptile = lambda i: (i % n_pos_tiles, 0)
    return pl.pallas_call(
        _qkv_kernel,
        grid=(N // tm,),
        in_specs=[
            pl.BlockSpec((tm, D), tile),
            pl.BlockSpec((1, D), const),
            pl.BlockSpec((D, nq + 2 * nk), const),
            pl.BlockSpec((1, nq + 2 * nk), const),
            pl.BlockSpec((tm, LANES), ptile),
            pl.BlockSpec((tm, LANES), ptile),
        ],
        out_specs=[
            pl.BlockSpec((tm, nq), tile),
            pl.BlockSpec((tm, 2 * nk), tile),
            pl.BlockSpec((tm, 2 * nk), tile),
            pl.BlockSpec((tm, nk), tile),
            pl.BlockSpec((tm, nk), tile),
        ],
        out_shape=[
            jax.ShapeDtypeStruct((N, nq), BF16),
            jax.ShapeDtypeStruct((N, 2 * nk), BF16),
            jax.ShapeDtypeStruct((N, 2 * nk), BF16),
            jax.ShapeDtypeStruct((N, nk), F32),
            jax.ShapeDtypeStruct((N, nk), F32),
        ],
        compiler_params=pltpu.CompilerParams(
            dimension_semantics=("arbitrary",), vmem_limit_bytes=VMEM_LIMIT),
        name="qkv",
    )(x, g, w, b, cos, sin)


def _attend(qc_list, kd, vd, mask, sinkcol):
    R = qc_list[0].shape[0]
    lane = lax.broadcasted_iota(jnp.int32, qc_list[0].shape, 1)
    low = lane < HEAD_DIM
    zero = jnp.zeros_like(qc_list[0])
    stack = []
    for qc in qc_list:
        stack.append(jnp.where(low, qc, zero))
        stack.append(jnp.where(low, zero, qc))
    qs = jnp.concatenate(stack, axis=0)
    s = lax.dot_general(qs, kd, (((1,), (1,)), ((), ())), preferred_element_type=F32)
    bias = jnp.where(mask, 0.0, -jnp.inf).astype(F32)
    s = s + jnp.concatenate([bias] * 4, axis=0)
    m = jnp.maximum(jnp.max(s, axis=-1, keepdims=True), sinkcol)
    p = jnp.exp(s - m)
    den = jnp.sum(p, axis=-1, keepdims=True) + jnp.exp(sinkcol - m)
    p = (p / den).astype(BF16)
    o = jnp.dot(p, vd, preferred_element_type=F32)
    lowf = lax.broadcasted_iota(jnp.int32, (R, LANES), 1) < HEAD_DIM
    return (jnp.where(lowf, o[0:R], o[R:2 * R]), jnp.where(lowf, o[2 * R:3 * R], o[3 * R:4 * R]))


def _attn_prompt_kernel(q_ref, kp_ref, kc_ref, vp_ref, vc_ref, sink_ref, o_ref, *, tq):
    n = pl.program_id(1)
    B = WINDOW
    kd = jnp.concatenate([kp_ref[...], kc_ref[...]], axis=0)
    vd = jnp.concatenate([vp_ref[...], vc_ref[...]], axis=0)
    qi = lax.broadcasted_iota(jnp.int32, (B, 2 * B), 0)
    kj = lax.broadcasted_iota(jnp.int32, (B, 2 * B), 1)
    rel = B + qi - kj
    band = (rel >= 0) & (rel < WINDOW)
    for j in range(tq // B):
        mask = band
        if j == 0:
            mask = band & ((kj >= B) | (n > 0))
        for g in range(N_KV_HEADS):
            qc = [q_ref[j * B:(j + 1) * B, (2 * g + i) * LANES:(2 * g + i + 1) * LANES] for i in range(2)]
            kg = kd[j * B:(j + 2) * B, g * LANES:(g + 1) * LANES]
            vg = vd[j * B:(j + 2) * B, g * LANES:(g + 1) * LANES]
            o0, o1 = _attend(qc, kg, vg, mask, sink_ref[g])
            o_ref[j * B:(j + 1) * B, 2 * g * LANES:(2 * g + 1) * LANES] = o0.astype(BF16)
            o_ref[j * B:(j + 1) * B, (2 * g + 1) * LANES:(2 * g + 2) * LANES] = o1.astype(BF16)


def _attn_prompt(q, kd, vd, sinkcol, batch, seq, tq=512):
    N = q.shape[0]
    nt = seq // tq
    r = tq // WINDOW
    cur = lambda b, n: (b * nt + n, 0)
    prev = lambda b, n: (jnp.maximum((b * nt + n) * r - 1, 0), 0)
    kvw = 2 * N_KV_HEADS * HEAD_DIM
    return pl.pallas_call(
        functools.partial(_attn_prompt_kernel, tq=tq),
        grid=(batch, nt),
        in_specs=[
            pl.BlockSpec((tq, N_HEADS * HEAD_DIM), cur),
            pl.BlockSpec((WINDOW, kvw), prev),
            pl.BlockSpec((tq, kvw), cur),
            pl.BlockSpec((WINDOW, kvw), prev),
            pl.BlockSpec((tq, kvw), cur),
            pl.BlockSpec((N_KV_HEADS, 4 * WINDOW, 1), lambda b, n: (0, 0, 0)),
        ],
        out_specs=pl.BlockSpec((tq, N_HEADS * HEAD_DIM), cur),
        out_shape=jax.ShapeDtypeStruct((N, N_HEADS * HEAD_DIM), BF16),
        compiler_params=pltpu.CompilerParams(
            dimension_semantics=("arbitrary", "arbitrary"), vmem_limit_bytes=VMEM_LIMIT),
        name="attn_prompt",
    )(q, kd, kd, vd, vd, sinkcol)


def _dup_cache(c, g):
    low = lax.broadcasted_iota(jnp.int32, c.shape, 1) < HEAD_DIM
    r = pltpu.roll(c, HEAD_DIM, axis=1)
    d = jnp.where(low, c, r) if g % 2 == 0 else jnp.where(low, r, c)
    return d.astype(BF16)


def _attn_sample_kernel(q_ref, kdn_ref, vdn_ref, kn_ref, vn_ref, ck_ref, cv_ref, mask_ref, kval_ref, sink_ref,
                        o_ref, nk_ref, nv_ref, *, bs, t):
    W = WINDOW
    ck = ck_ref[...]
    cv = cv_ref[...]
    nk_ref[:, 0:W - t, :] = ck[:, t:, :]
    nk_ref[:, W - t:, :] = kn_ref[...].reshape(bs, t, N_KV_HEADS * HEAD_DIM)
    nv_ref[:, 0:W - t, :] = cv[:, t:, :]
    nv_ref[:, W - t:, :] = vn_ref[...].reshape(bs, t, N_KV_HEADS * HEAD_DIM)

    ck2 = ck.reshape(bs * W, N_KV_HEADS * HEAD_DIM)
    cv2 = cv.reshape(bs * W, N_KV_HEADS * HEAD_DIM)
    mask = (mask_ref[...] > 0) & (kval_ref[0] > 0)
    for g in range(N_KV_HEADS):
        m = g // 2
        kg = jnp.concatenate([_dup_cache(ck2[:, m * LANES:(m + 1) * LANES], g),
                              kdn_ref[:, g * LANES:(g + 1) * LANES]], axis=0)
        vg = jnp.concatenate([_dup_cache(cv2[:, m * LANES:(m + 1) * LANES], g),
                              vdn_ref[:, g * LANES:(g + 1) * LANES]], axis=0)
        qc = [q_ref[:, (2 * g + i) * LANES:(2 * g + i + 1) * LANES] for i in range(2)]
        o0, o1 = _attend(qc, kg, vg, mask, sink_ref[g])
        o_ref[:, 2 * g * LANES:(2 * g + 1) * LANES] = o0.astype(BF16)
        o_ref[:, (2 * g + 1) * LANES:(2 * g + 2) * LANES] = o1.astype(BF16)


def _attn_sample(q, kdn, vdn, kn, vn, ck, cv, mask, kval, sinkcol, bs, t):
    Bd = ck.shape[0]
    rows = bs * t
    kvw = N_KV_HEADS * HEAD_DIM
    S = bs * WINDOW + rows
    tile = lambda i: (i, 0)
    tile3 = lambda i: (i, 0, 0)
    return pl.pallas_call(
        functools.partial(_attn_sample_kernel, bs=bs, t=t),
        grid=(Bd // bs,),
        in_specs=[
            pl.BlockSpec((rows, N_HEADS * HEAD_DIM), tile),
            pl.BlockSpec((rows, 2 * kvw), tile),
            pl.BlockSpec((rows, 2 * kvw), tile),
            pl.BlockSpec((rows, kvw), tile),
            pl.BlockSpec((rows, kvw), tile),
            pl.BlockSpec((bs, WINDOW, kvw), tile3),
            pl.BlockSpec((bs, WINDOW, kvw), tile3),
            pl.BlockSpec((rows, S), lambda i: (0, 0)),
            pl.BlockSpec((1, 1, S), tile3),
            pl.BlockSpec((N_KV_HEADS, 4 * rows, 1), lambda i: (0, 0, 0)),
        ],
        out_specs=[
            pl.BlockSpec((rows, N_HEADS * HEAD_DIM), tile),
            pl.BlockSpec((bs, WINDOW, kvw), tile3),
            pl.BlockSpec((bs, WINDOW, kvw), tile3),
        ],
        out_shape=[
            jax.ShapeDtypeStruct((Bd * t, N_HEADS * HEAD_DIM), BF16),
            jax.ShapeDtypeStruct((Bd, WINDOW, kvw), F32),
            jax.ShapeDtypeStruct((Bd, WINDOW, kvw), F32),
        ],
        compiler_params=pltpu.CompilerParams(
            dimension_semantics=("arbitrary",), vmem_limit_bytes=VMEM_LIMIT),
        name="attn_sample",
    )(q, kdn, vdn, kn, vn, ck, cv, mask, kval, sinkcol)


def _rope_tables(pos):
    inv = ROPE_THETA ** (-jnp.arange(0, HEAD_DIM, 2, dtype=F32) / HEAD_DIM)
    lane = jnp.arange(LANES)
    inv_l = inv[(lane % HEAD_DIM) % (HEAD_DIM // 2)]
    sign = jnp.where((lane % HEAD_DIM) < HEAD_DIM // 2, -1.0, 1.0).astype(F32)
    ang = pos.astype(F32)[:, None] * inv_l[None, :]
    return jnp.cos(ang), jnp.sin(ang) * sign[None, :]


def _sink_columns(sinks, rows):
    s = sinks.astype(F32).reshape(N_KV_HEADS, N_HEADS // N_KV_HEADS, 1)
    return jnp.broadcast_to(s[:, :, None, :], (N_KV_HEADS, 4, rows, 1)).reshape(N_KV_HEADS, 4 * rows, 1)


def kernel(x_prompt, x_sample, state_pool, cache_k, cache_v, sample_start, norm_mix, norm_ffn, norm_final,
           w_pool, ls_pool, w_qkv, b_qkv, sinks, w_o, b_o, w_rg, b_rg, w_re, b_re, w_gate, w_up, w_down):
    B, T, D = x_prompt.shape
    Bd, Td, _ = x_sample.shape
    kvw = N_KV_HEADS * HEAD_DIM
    row = lambda v: v.reshape(1, -1).astype(F32)

    wp = w_pool[0].astype(BF16)
    wqkv = w_qkv[0].astype(BF16)
    wo = w_o[0].astype(BF16)
    wg, wu, wd = w_gate.astype(BF16), w_up.astype(BF16), w_down.astype(BF16)
    pad = ROUTER_LANES - N_EXPERT_GROUPS - N_EXPERTS
    wr = [jnp.concatenate([w_rg[l], w_re[l], jnp.zeros((D, pad), F32)], axis=1) for l in range(2)]
    br = [jnp.concatenate([b_rg[l], b_re[l], jnp.zeros((pad,), F32)]).reshape(1, -1) for l in range(2)]

    start = sample_start.astype(jnp.int32)

    x1p, pool_p16 = _pool_prompt(x_prompt, jnp.zeros((B, HALO, D), F32), row(norm_mix[0]), wp, row(ls_pool[0]))
    seg = HALO + Td
    xcat = jnp.concatenate([jnp.zeros((Bd, 1, D), F32), state_pool[0], x_sample], axis=1).reshape(Bd * seg, D)
    r = jnp.arange(seg, dtype=jnp.int32)[None, :]
    posrow = jnp.where(r >= HALO, start[:, None] + r - HALO, -1).reshape(Bd * seg, 1)
    x1cat, hcat = _pool_sample(xcat, posrow, row(norm_mix[0]), wp, row(ls_pool[0]))
    x1s = x1cat.reshape(Bd, seg, D)[:, HALO:].reshape(Bd * Td, D)
    pool_s = hcat.reshape(Bd, seg, D)[:, seg - POOL_BUF:]
    pool_p = pool_p16[:, HALO - POOL_BUF:]

    moe0 = functools.partial(_moe, nf=row(norm_ffn[0]), wr=wr[0], br=br[0], wg=wg[0], wu=wu[0], wd=wd[0])
    x2p = moe0(x1p.reshape(B * T, D))
    x2s = moe0(x1s)

    cos_p, sin_p = _rope_tables(jnp.arange(T, dtype=jnp.int32))
    pos_s = (start[:, None] + jnp.arange(Td, dtype=jnp.int32)[None, :]).reshape(-1)
    cos_s, sin_s = _rope_tables(pos_s)
    g1 = row(norm_mix[1])
    tmq = 1024
    qp, kdp, vdp, kp, vp = _qkv(x2p, g1, wqkv, row(b_qkv[0]), cos_p, sin_p, tmq, T // tmq)
    qs, kds, vds, ks, vs = _qkv(x2s, g1, wqkv, row(b_qkv[0]), cos_s, sin_s, tmq, (Bd * Td) // tmq)

    op = _attn_prompt(qp, kdp, vdp, _sink_columns(sinks[0], WINDOW), B, T)

    bs = 8
    rows = bs * Td
    W = cache_k.shape[2]
    qrow = jnp.arange(rows)
    ccol = jnp.arange(bs * W)
    ncol = jnp.arange(rows)
    samp_q, t_q = qrow // Td, qrow % Td
    m_cache = (samp_q[:, None] == (ccol // W)[None, :]) & ((ccol % W)[None, :] > t_q[:, None])
    m_new = (samp_q[:, None] == (ncol // Td)[None, :]) & ((ncol % Td)[None, :] <= t_q[:, None])
    amask = jnp.concatenate([m_cache, m_new], axis=1).astype(F32)
    kv_cache = (jnp.arange(W, dtype=jnp.int32)[None, :] >= (W - start)[:, None]).reshape(Bd // bs, bs * W)
    kval = jnp.concatenate([kv_cache, jnp.ones((Bd // bs, rows), bool)], axis=1).astype(F32)
    kval = kval.reshape(Bd // bs, 1, bs * W + rows)
    osamp, nk_s, nv_s = _attn_sample(
        qs, kds, vds, ks, vs, cache_k[0].reshape(Bd, W, kvw), cache_v[0].reshape(Bd, W, kvw),
        amask, kval, _sink_columns(sinks[0], rows), bs, Td)

    moe1 = functools.partial(_moe, nf=row(norm_ffn[1]), wr=wr[1], br=br[1], wg=wg[1], wu=wu[1], wd=wd[1],
                             final=row(norm_final))
    yp = moe1(x2p, oproj=(op, wo, row(b_o[0])))
    ys = moe1(x2s, oproj=(osamp, wo, row(b_o[0])))

    keep = min(WINDOW, T)
    k_p = kp.reshape(B, T, N_KV_HEADS, HEAD_DIM)[:, T - keep:][None]
    v_p = vp.reshape(B, T, N_KV_HEADS, HEAD_DIM)[:, T - keep:][None]
    return (yp.reshape(B, T, D), ys.reshape(Bd, Td, D), pool_p[None], k_p, v_p, pool_s[None],
            nk_s.reshape(1, Bd, W, N_KV_HEADS, HEAD_DIM), nv_s.reshape(1, Bd, W, N_KV_HEADS, HEAD_DIM))
```

```python
import functools

import jax
import jax.numpy as jnp
from jax import lax
from jax.experimental import pallas as pl
from jax.experimental.pallas import tpu as pltpu

F32 = jnp.float32
BF16 = jnp.bfloat16

D_MODEL = 1024
POOL_WINDOWS = (2, 4, 8, 16)
POOL_GROUP_DIM = 256
POOL_BUF = 15
HALO = 16
HEAD_DIM = 64
N_HEADS = 16
N_KV_HEADS = 4
WINDOW = 128
ROPE_THETA = 10000.0
N_EXPERT_GROUPS = 4
EXPERTS_PER_GROUP = 4
N_EXPERTS = 16
D_EXPERT = 256
RMS_EPS = 1e-6
LANES = 128
ROUTER_LANES = 128
VMEM_LIMIT = 56 * 1024 * 1024


def _rms(x, g):
    ms = jnp.mean(x * x, axis=-1, keepdims=True)
    return x * lax.rsqrt(ms + RMS_EPS) * g


def _pool_windows(hc, s2, s4, s8, rows):
    G = POOL_GROUP_DIM
    a = hc[pl.ds(8, rows), :]
    v2 = a + hc[pl.ds(7, rows), :]
    s2[pl.ds(8, rows), :] = v2
    v4 = v2[:, G:] + s2[pl.ds(6, rows), G:]
    s4[pl.ds(8, rows), G:] = v4
    v8 = v4[:, G:] + s4[pl.ds(4, rows), 2 * G:]
    s8[pl.ds(8, rows), 2 * G:] = v8
    v16 = v8[:, G:] + s8[pl.ds(0, rows), 3 * G:]
    return a, (v2[:, :G], v4[:, :G], v8[:, :G], v16)


def _pool_project(h, wins, pos, wp_ref, ls):
    G = POOL_GROUP_DIM
    outs = []
    for g, w in enumerate(POOL_WINDOWS):
        cnt = jnp.minimum(w, pos + 1).astype(F32)
        d = wins[g] / cnt - h[:, g * G:(g + 1) * G]
        outs.append(jnp.dot(d.astype(BF16), wp_ref[g], preferred_element_type=F32))
    return jnp.concatenate(outs, axis=-1) * ls


def _zero_pads(*refs):
    for r in refs:
        r[pl.ds(0, 8), :] = jnp.zeros((8, D_MODEL), F32)


def _pool_prompt_kernel(x_ref, buf_ref, g_ref, wp_ref, ls_ref, o_ref, nb_ref, hc, s2, s4, s8, *, tq):
    t = pl.program_id(1)

    @pl.when(t == 0)
    def _():
        _zero_pads(hc, s2, s4, s8)
        hc[pl.ds(8, HALO), :] = buf_ref[0]

    @pl.when(t > 0)
    def _():
        hc[pl.ds(8, HALO), :] = hc[pl.ds(8 + tq, HALO), :]

    x = x_ref[0]
    hc[pl.ds(8 + HALO, tq), :] = _rms(x, g_ref[...])
    a, wins = _pool_windows(hc, s2, s4, s8, HALO + tq)
    h = a[HALO:]
    wins = tuple(w[HALO:] for w in wins)
    pos = t * tq + lax.broadcasted_iota(jnp.int32, (tq, 1), 0)
    o_ref[0] = x + _pool_project(h, wins, pos, wp_ref, ls_ref[...])

    @pl.when(t == pl.num_programs(1) - 1)
    def _():
        nb_ref[0] = hc[pl.ds(8 + tq, HALO), :]


def _pool_prompt(x, buf16, g, wp, ls, tq=512):
    B, T, D = x.shape
    return pl.pallas_call(
        functools.partial(_pool_prompt_kernel, tq=tq),
        grid=(B, T // tq),
        in_specs=[
            pl.BlockSpec((1, tq, D), lambda b, t: (b, t, 0)),
            pl.BlockSpec((1, HALO, D), lambda b, t: (b, 0, 0)),
            pl.BlockSpec((1, D), lambda b, t: (0, 0)),
            pl.BlockSpec((4, POOL_GROUP_DIM, POOL_GROUP_DIM), lambda b, t: (0, 0, 0)),
            pl.BlockSpec((1, D), lambda b, t: (0, 0)),
        ],
        out_specs=[
            pl.BlockSpec((1, tq, D), lambda b, t: (b, t, 0)),
            pl.BlockSpec((1, HALO, D), lambda b, t: (b, 0, 0)),
        ],
        out_shape=[jax.ShapeDtypeStruct((B, T, D), F32), jax.ShapeDtypeStruct((B, HALO, D), F32)],
        scratch_shapes=[pltpu.VMEM((8 + HALO + tq, D), F32)] * 4,
        compiler_params=pltpu.CompilerParams(
            dimension_semantics=("arbitrary", "arbitrary"), vmem_limit_bytes=VMEM_LIMIT),
        name="pool_prompt",
    )(x, buf16, g, wp, ls)


def _pool_sample_kernel(xc_ref, pos_ref, g_ref, wp_ref, ls_ref, o_ref, hcat_ref, hc, s2, s4, s8, *, rows):
    @pl.when(pl.program_id(0) == 0)
    def _():
        _zero_pads(hc, s2, s4, s8)

    xc = xc_ref[...]
    pos = pos_ref[...]
    is_new = pos >= 0
    hrow = jnp.where(is_new, _rms(xc, g_ref[...]), xc)
    hc[pl.ds(8, rows), :] = hrow
    a, wins = _pool_windows(hc, s2, s4, s8, rows)
    o_ref[...] = xc + _pool_project(a, wins, jnp.maximum(pos, 0), wp_ref, ls_ref[...])
    hcat_ref[...] = a


def _pool_sample(xcat, posrow, g, wp, ls, rows=768):
    R, D = xcat.shape
    return pl.pallas_call(
        functools.partial(_pool_sample_kernel, rows=rows),
        grid=(R // rows,),
        in_specs=[
            pl.BlockSpec((rows, D), lambda i: (i, 0)),
            pl.BlockSpec((rows, 1), lambda i: (i, 0)),
            pl.BlockSpec((1, D), lambda i: (0, 0)),
            pl.BlockSpec((4, POOL_GROUP_DIM, POOL_GROUP_DIM), lambda i: (0, 0, 0)),
            pl.BlockSpec((1, D), lambda i: (0, 0)),
        ],
        out_specs=[pl.BlockSpec((rows, D), lambda i: (i, 0)), pl.BlockSpec((rows, D), lambda i: (i, 0))],
        out_shape=[jax.ShapeDtypeStruct((R, D), F32), jax.ShapeDtypeStruct((R, D), F32)],
        scratch_shapes=[pltpu.VMEM((8 + rows, D), F32)] * 4,
        compiler_params=pltpu.CompilerParams(
            dimension_semantics=("arbitrary",), vmem_limit_bytes=VMEM_LIMIT),
        name="pool_sample",
    )(xcat, posrow, g, wp, ls)


def _route(logits):
    lane = lax.broadcasted_iota(jnp.int32, logits.shape, 1).astype(F32)
    big = jnp.float32(1 << 20)
    neg = jnp.float32(-jnp.inf)
    is_g = lane < N_EXPERT_GROUPS
    gl = jnp.where(is_g, logits, neg)
    m = jnp.max(gl, axis=-1, keepdims=True)
    gidx = jnp.min(jnp.where(gl == m, lane, big), axis=-1, keepdims=True)
    z = jnp.sum(jnp.where(is_g, jnp.exp(gl - m), 0.0), axis=-1, keepdims=True)
    gw = 1.0 / z
    lo = N_EXPERT_GROUPS + gidx * EXPERTS_PER_GROUP
    in_grp = (lane >= lo) & (lane < lo + EXPERTS_PER_GROUP)
    el = jnp.where(in_grp, logits, neg)
    v1 = jnp.max(el, axis=-1, keepdims=True)
    i1 = jnp.min(jnp.where(el == v1, lane, big), axis=-1, keepdims=True)
    el2 = jnp.where(lane == i1, neg, el)
    v2 = jnp.max(el2, axis=-1, keepdims=True)
    i2 = jnp.min(jnp.where(el2 == v2, lane, big), axis=-1, keepdims=True)
    t = jnp.exp(v2 - v1)
    w1 = 1.0 / (1.0 + t)
    w2 = t * w1
    cw = gw * (jnp.where(lane == i1 - lo, w1, 0.0) + jnp.where(lane == i2 - lo, w2, 0.0))
    return gidx, cw


def _moe_kernel(*refs, has_oproj, has_final, tm, bm):
    it = iter(refs)
    x_ref = next(it)
    if has_oproj:
        oin_ref, wo_ref, bo_ref = next(it), next(it), next(it)
    nf_ref, wrh_ref, wrl_ref, br_ref, wg_ref, wu_ref, wd_ref = (next(it) for _ in range(7))
    if has_final:
        fn_ref = next(it)
    out_ref = next(it)
    hs, cs, ys = (next(it) for _ in range(3))
    nblk = tm // bm + N_EXPERT_GROUPS
    cap = nblk * bm
    shift = bm.bit_length() - 1
    qr = 2 * bm
    assert cap % qr == 0 and tm % qr == 0

    x = x_ref[...]
    if has_oproj:
        x = x + jnp.dot(oin_ref[...], wo_ref[...], preferred_element_type=F32) + bo_ref[...]
    out_ref[...] = x
    h = _rms(x, nf_ref[...])
    hb = h.astype(BF16)
    h_lo = (h - hb.astype(F32)).astype(BF16)
    logits = (jnp.dot(hb, wrh_ref[...], preferred_element_type=F32)
              + jnp.dot(h_lo, wrh_ref[...], preferred_element_type=F32)
              + jnp.dot(hb, wrl_ref[...], preferred_element_type=F32)) + br_ref[...]
    gidx, cw = _route(logits)
    cw_hi = cw.astype(BF16)
    cw_lo = pltpu.roll(cw - cw_hi.astype(F32), EXPERTS_PER_GROUP, axis=1)
    cwb = cw_hi + cw_lo.astype(BF16)

    lane = lax.broadcasted_iota(jnp.int32, (tm, LANES), 1).astype(F32)
    oh = jnp.where(lane == gidx, 1.0, 0.0)
    ri = lax.broadcasted_iota(jnp.int32, (tm, tm), 0)
    ci = lax.broadcasted_iota(jnp.int32, (tm, tm), 1)
    before = jnp.where(ci < ri, 1.0, 0.0).astype(BF16)
    cnt_before = jnp.dot(before, oh.astype(BF16), preferred_element_type=F32)
    pos = jnp.sum(oh * cnt_before, axis=-1, keepdims=True)
    start_blk, n_blk = [], []
    off = jnp.int32(0)
    for g in range(N_EXPERT_GROUPS):
        in_g = gidx == g
        n_g = jnp.sum(jnp.where(in_g, 1.0, 0.0)).astype(jnp.int32)
        blocks = (n_g + (bm - 1)) >> shift
        start_blk.append(off)
        n_blk.append(blocks)
        pos = pos + jnp.where(in_g, (off << shift).astype(F32), 0.0)
        off = off + blocks
    pos_row = jnp.transpose(jnp.broadcast_to(pos, (tm, LANES)))[0:1, :]

    total_blk = off
    min_q = tm // qr

    def sort_chunk(q):
        sub = (q * qr + lax.broadcasted_iota(jnp.int32, (qr, tm), 0)).astype(F32)
        sel = jnp.where(pos_row == sub, 1.0, 0.0).astype(BF16)
        hs[q * qr:(q + 1) * qr, :] = jnp.dot(sel, hb, preferred_element_type=F32).astype(BF16)
        cs[q * qr:(q + 1) * qr, :] = jnp.dot(sel, cwb, preferred_element_type=F32)

    for q in range(cap // qr):
        if q < min_q:
            sort_chunk(q)
        else:
            pl.when(total_blk > q * (qr // bm))(functools.partial(sort_chunk, q))
    ys[min_q * qr:, :] = jnp.zeros((cap - min_q * qr, D_MODEL), BF16)

    for g in range(N_EXPERT_GROUPS):
        wd = wd_ref[g * EXPERTS_PER_GROUP:(g + 1) * EXPERTS_PER_GROUP].reshape(
            EXPERTS_PER_GROUP * D_EXPERT, D_MODEL)

        def block(b, carry, g=g, wd=wd):
            rows = pl.ds(pl.multiple_of(b * bm, bm), bm)
            hblk = hs[rows, :]
            cblk = cs[rows, :]
            ln = lax.broadcasted_iota(jnp.int32, cblk.shape, 1)
            parts = []
            for e in range(EXPERTS_PER_GROUP):
                ce = jnp.sum(jnp.where((ln == e) | (ln == e + EXPERTS_PER_GROUP), cblk, 0.0),
                             axis=-1, keepdims=True)
                gt = jnp.dot(hblk, wg_ref[g * EXPERTS_PER_GROUP + e], preferred_element_type=F32)
                up = jnp.dot(hblk, wu_ref[g * EXPERTS_PER_GROUP + e], preferred_element_type=F32)
                a = gt / (1.0 + jnp.exp(-gt)) * up * ce
                parts.append(a.astype(BF16))
            a_all = jnp.concatenate(parts, axis=-1)
            ys[rows, :] = jnp.dot(a_all, wd, preferred_element_type=F32).astype(BF16)
            return carry

        lax.fori_loop(start_blk[g], start_blk[g] + n_blk[g], block, 0)

    def unsort_chunk(q):
        col = (q * qr + lax.broadcasted_iota(jnp.int32, (tm, qr), 1)).astype(F32)
        sel = jnp.where(pos == col, 1.0, 0.0).astype(BF16)
        out_ref[...] += jnp.dot(sel, ys[q * qr:(q + 1) * qr, :], preferred_element_type=F32)

    for q in range(cap // qr):
        if q < min_q:
            unsort_chunk(q)
        else:
            pl.when(total_blk > q * (qr // bm))(functools.partial(unsort_chunk, q))
    if has_final:
        out_ref[...] = _rms(out_ref[...], fn_ref[...])


def _moe(x, nf, wr, br, wg, wu, wd, oproj=None, final=None, tm=512, bm=128):
    N, D = x.shape
    has_oproj = oproj is not None
    has_final = final is not None
    const2 = lambda i: (0, 0)
    const3 = lambda i: (0, 0, 0)
    tile = lambda i: (i, 0)
    once = pl.Buffered(1)
    args = [x]
    in_specs = [pl.BlockSpec((tm, D), tile)]
    if has_oproj:
        o, wo, bo = oproj
        args += [o, wo, bo]
        in_specs += [pl.BlockSpec((tm, D), tile), pl.BlockSpec((D, D), const2, pipeline_mode=once),
                     pl.BlockSpec((1, D), const2)]
    wr_hi = wr.astype(BF16)
    wr_lo = (wr - wr_hi.astype(F32)).astype(BF16)
    args += [nf, wr_hi, wr_lo, br, wg, wu, wd]
    in_specs += [
        pl.BlockSpec((1, D), const2),
        pl.BlockSpec((D, ROUTER_LANES), const2, pipeline_mode=once),
        pl.BlockSpec((D, ROUTER_LANES), const2, pipeline_mode=once),
        pl.BlockSpec((1, ROUTER_LANES), const2),
        pl.BlockSpec((N_EXPERTS, D, D_EXPERT), const3, pipeline_mode=once),
        pl.BlockSpec((N_EXPERTS, D, D_EXPERT), const3, pipeline_mode=once),
        pl.BlockSpec((N_EXPERTS, D_EXPERT, D), const3, pipeline_mode=once),
    ]
    if has_final:
        args.append(final)
        in_specs.append(pl.BlockSpec((1, D), const2))
    cap = (tm // bm + N_EXPERT_GROUPS) * bm
    return pl.pallas_call(
        functools.partial(_moe_kernel, has_oproj=has_oproj, has_final=has_final, tm=tm, bm=bm),
        grid=(N // tm,),
        in_specs=in_specs,
        out_specs=pl.BlockSpec((tm, D), tile),
        out_shape=jax.ShapeDtypeStruct((N, D), F32),
        scratch_shapes=[
            pltpu.VMEM((cap, D), BF16),
            pltpu.VMEM((cap, LANES), F32),
            pltpu.VMEM((cap, D), BF16),
        ],
        compiler_params=pltpu.CompilerParams(
            dimension_semantics=("arbitrary",), vmem_limit_bytes=VMEM_LIMIT),
        name="moe",
    )(*args)


def _swap_halves(x, low):
    return jnp.where(low, pltpu.roll(x, LANES - 32, axis=1), pltpu.roll(x, 32, axis=1))


def _qkv_kernel(x_ref, g_ref, w_ref, b_ref, cos_ref, sin_ref, q_ref, kd_ref, vd_ref, k_ref, v_ref):
    h = _rms(x_ref[...], g_ref[...]).astype(BF16)
    qkv = jnp.dot(h, w_ref[...], preferred_element_type=F32) + b_ref[...]
    cos = cos_ref[...]
    sin = sin_ref[...]
    lane = lax.broadcasted_iota(jnp.int32, cos.shape, 1)
    low32 = (lane % HEAD_DIM) < (HEAD_DIM // 2)
    low64 = lane < HEAD_DIM
    nq = N_HEADS * HEAD_DIM
    nk = N_KV_HEADS * HEAD_DIM

    def rope(c):
        return c * cos + _swap_halves(c, low32) * sin

    def dup(c):
        r = pltpu.roll(c, HEAD_DIM, axis=1)
        return jnp.where(low64, c, r), jnp.where(low64, r, c)

    for j in range(nq // LANES):
        c = rope(qkv[:, j * LANES:(j + 1) * LANES]) * (HEAD_DIM ** -0.5)
        q_ref[:, j * LANES:(j + 1) * LANES] = c.astype(BF16)
    for j in range(nk // LANES):
        c = rope(qkv[:, nq + j * LANES:nq + (j + 1) * LANES])
        k_ref[:, j * LANES:(j + 1) * LANES] = c
        d0, d1 = dup(c)
        kd_ref[:, 2 * j * LANES:(2 * j + 1) * LANES] = d0.astype(BF16)
        kd_ref[:, (2 * j + 1) * LANES:(2 * j + 2) * LANES] = d1.astype(BF16)
        c = qkv[:, nq + nk + j * LANES:nq + nk + (j + 1) * LANES]
        v_ref[:, j * LANES:(j + 1) * LANES] = c
        d0, d1 = dup(c)
        vd_ref[:, 2 * j * LANES:(2 * j + 1) * LANES] = d0.astype(BF16)
        vd_ref[:, (2 * j + 1) * LANES:(2 * j + 2) * LANES] = d1.astype(BF16)


def _qkv(x, g, w, b, cos, sin, tm, n_pos_tiles):
    N, D = x.shape
    nq, nk = N_HEADS * HEAD_DIM, N_KV_HEADS * HEAD_DIM
    const = lambda i: (0, 0)
    tile = lambda i: (i, 0)
    ptile = lambda i: (i % n_pos_tiles, 0)
    return pl.pallas_call(
        _qkv_kernel,
        grid=(N // tm,),
        in_specs=[
            pl.BlockSpec((tm, D), tile),
            pl.BlockSpec((1, D), const),
            pl.BlockSpec((D, nq + 2 * nk), const),
            pl.BlockSpec((1, nq + 2 * nk), const),
            pl.BlockSpec((tm, LANES), ptile),
            pl.BlockSpec((tm, LANES), ptile),
        ],
        out_specs=[
            pl.BlockSpec((tm, nq), tile),
            pl.BlockSpec((tm, 2 * nk), tile),
            pl.BlockSpec((tm, 2 * nk), tile),
            pl.BlockSpec((tm, nk), tile),
            pl.BlockSpec((tm, nk), tile),
        ],
        out_shape=[
            jax.ShapeDtypeStruct((N, nq), BF16),
            jax.ShapeDtypeStruct((N, 2 * nk), BF16),
            jax.ShapeDtypeStruct((N, 2 * nk), BF16),
            jax.ShapeDtypeStruct((N, nk), F32),
            jax.ShapeDtypeStruct((N, nk), F32),
        ],
        compiler_params=pltpu.CompilerParams(
            dimension_semantics=("arbitrary",), vmem_limit_bytes=VMEM_LIMIT),
        name="qkv",
    )(x, g, w, b, cos, sin)


def _attend(qc_list, kd, vd, mask, sinkcol):
    R = qc_list[0].shape[0]
    lane = lax.broadcasted_iota(jnp.int32, qc_list[0].shape, 1)
    low = lane < HEAD_DIM
    zero = jnp.zeros_like(qc_list[0])
    stack = []
    for qc in qc_list:
        stack.append(jnp.where(low, qc, zero))
        stack.append(jnp.where(low, zero, qc))
    qs = jnp.concatenate(stack, axis=0)
    s = lax.dot_general(qs, kd, (((1,), (1,)), ((), ())), preferred_element_type=F32)
    bias = jnp.where(mask, 0.0, -jnp.inf).astype(F32)
    s = s + jnp.concatenate([bias] * 4, axis=0)
    m = jnp.maximum(jnp.max(s, axis=-1, keepdims=True), sinkcol)
    p = jnp.exp(s - m)
    den = jnp.sum(p, axis=-1, keepdims=True) + jnp.exp(sinkcol - m)
    p = (p / den).astype(BF16)
    o = jnp.dot(p, vd, preferred_element_type=F32)
    lowf = lax.broadcasted_iota(jnp.int32, (R, LANES), 1) < HEAD_DIM
    return (jnp.where(lowf, o[0:R], o[R:2 * R]), jnp.where(lowf, o[2 * R:3 * R], o[3 * R:4 * R]))


def _attn_prompt_kernel(q_ref, kp_ref, kc_ref, vp_ref, vc_ref, sink_ref, o_ref, *, tq):
    n = pl.program_id(1)
    B = WINDOW
    kd = jnp.concatenate([kp_ref[...], kc_ref[...]], axis=0)
    vd = jnp.concatenate([vp_ref[...], vc_ref[...]], axis=0)
    qi = lax.broadcasted_iota(jnp.int32, (B, 2 * B), 0)
    kj = lax.broadcasted_iota(jnp.int32, (B, 2 * B), 1)
    rel = B + qi - kj
    band = (rel >= 0) & (rel < WINDOW)
    for j in range(tq // B):
        mask = band
        if j == 0:
            mask = band & ((kj >= B) | (n > 0))
        for g in range(N_KV_HEADS):
            qc = [q_ref[j * B:(j + 1) * B, (2 * g + i) * LANES:(2 * g + i + 1) * LANES] for i in range(2)]
            kg = kd[j * B:(j + 2) * B, g * LANES:(g + 1) * LANES]
            vg = vd[j * B:(j + 2) * B, g * LANES:(g + 1) * LANES]
            o0, o1 = _attend(qc, kg, vg, mask, sink_ref[g])
            o_ref[j * B:(j + 1) * B, 2 * g * LANES:(2 * g + 1) * LANES] = o0.astype(BF16)
            o_ref[j * B:(j + 1) * B, (2 * g + 1) * LANES:(2 * g + 2) * LANES] = o1.astype(BF16)


def _attn_prompt(q, kd, vd, sinkcol, batch, seq, tq=512):
    N = q.shape[0]
    nt = seq // tq
    r = tq // WINDOW
    cur = lambda b, n: (b * nt + n, 0)
    prev = lambda b, n: (jnp.maximum((b * nt + n) * r - 1, 0), 0)
    kvw = 2 * N_KV_HEADS * HEAD_DIM
    return pl.pallas_call(
        functools.partial(_attn_prompt_kernel, tq=tq),
        grid=(batch, nt),
        in_specs=[
            pl.BlockSpec((tq, N_HEADS * HEAD_DIM), cur),
            pl.BlockSpec((WINDOW, kvw), prev),
            pl.BlockSpec((tq, kvw), cur),
            pl.BlockSpec((WINDOW, kvw), prev),
            pl.BlockSpec((tq, kvw), cur),
            pl.BlockSpec((N_KV_HEADS, 4 * WINDOW, 1), lambda b, n: (0, 0, 0)),
        ],
        out_specs=pl.BlockSpec((tq, N_HEADS * HEAD_DIM), cur),
        out_shape=jax.ShapeDtypeStruct((N, N_HEADS * HEAD_DIM), BF16),
        compiler_params=pltpu.CompilerParams(
            dimension_semantics=("arbitrary", "arbitrary"), vmem_limit_bytes=VMEM_LIMIT),
        name="attn_prompt",
    )(q, kd, kd, vd, vd, sinkcol)


def _dup_cache(c, g):
    low = lax.broadcasted_iota(jnp.int32, c.shape, 1) < HEAD_DIM
    r = pltpu.roll(c, HEAD_DIM, axis=1)
    d = jnp.where(low, c, r) if g % 2 == 0 else jnp.where(low, r, c)
    return d.astype(BF16)


def _attn_sample_kernel(q_ref, kdn_ref, vdn_ref, kn_ref, vn_ref, ck_ref, cv_ref, mask_ref, kval_ref, sink_ref,
                        o_ref, nk_ref, nv_ref, *, bs, t):
    W = WINDOW
    ck = ck_ref[...]
    cv = cv_ref[...]
    nk_ref[:, 0:W - t, :] = ck[:, t:, :]
    nk_ref[:, W - t:, :] = kn_ref[...].reshape(bs, t, N_KV_HEADS * HEAD_DIM)
    nv_ref[:, 0:W - t, :] = cv[:, t:, :]
    nv_ref[:, W - t:, :] = vn_ref[...].reshape(bs, t, N_KV_HEADS * HEAD_DIM)

    ck2 = ck.reshape(bs * W, N_KV_HEADS * HEAD_DIM)
    cv2 = cv.reshape(bs * W, N_KV_HEADS * HEAD_DIM)
    mask = (mask_ref[...] > 0) & (kval_ref[0] > 0)
    for g in range(N_KV_HEADS):
        m = g // 2
        kg = jnp.concatenate([_dup_cache(ck2[:, m * LANES:(m + 1) * LANES], g),
                              kdn_ref[:, g * LANES:(g + 1) * LANES]], axis=0)
        vg = jnp.concatenate([_dup_cache(cv2[:, m * LANES:(m + 1) * LANES], g),
                              vdn_ref[:, g * LANES:(g + 1) * LANES]], axis=0)
        qc = [q_ref[:, (2 * g + i) * LANES:(2 * g + i + 1) * LANES] for i in range(2)]
        o0, o1 = _attend(qc, kg, vg, mask, sink_ref[g])
        o_ref[:, 2 * g * LANES:(2 * g + 1) * LANES] = o0.astype(BF16)
        o_ref[:, (2 * g + 1) * LANES:(2 * g + 2) * LANES] = o1.astype(BF16)


def _attn_sample(q, kdn, vdn, kn, vn, ck, cv, mask, kval, sinkcol, bs, t):
    Bd = ck.shape[0]
    rows = bs * t
    kvw = N_KV_HEADS * HEAD_DIM
    S = bs * WINDOW + rows
    tile = lambda i: (i, 0)
    tile3 = lambda i: (i, 0, 0)
    return pl.pallas_call(
        functools.partial(_attn_sample_kernel, bs=bs, t=t),
        grid=(Bd // bs,),
        in_specs=[
            pl.BlockSpec((rows, N_HEADS * HEAD_DIM), tile),
            pl.BlockSpec((rows, 2 * kvw), tile),
            pl.BlockSpec((rows, 2 * kvw), tile),
            pl.BlockSpec((rows, kvw), tile),
            pl.BlockSpec((rows, kvw), tile),
            pl.BlockSpec((bs, WINDOW, kvw), tile3),
            pl.BlockSpec((bs, WINDOW, kvw), tile3),
            pl.BlockSpec((rows, S), lambda i: (0, 0)),
            pl.BlockSpec((1, 1, S), tile3),
            pl.BlockSpec((N_KV_HEADS, 4 * rows, 1), lambda i: (0, 0, 0)),
        ],
        out_specs=[
            pl.BlockSpec((rows, N_HEADS * HEAD_DIM), tile),
            pl.BlockSpec((bs, WINDOW, kvw), tile3),
            pl.BlockSpec((bs, WINDOW, kvw), tile3),
        ],
        out_shape=[
            jax.ShapeDtypeStruct((Bd * t, N_HEADS * HEAD_DIM), BF16),
            jax.ShapeDtypeStruct((Bd, WINDOW, kvw), F32),
            jax.ShapeDtypeStruct((Bd, WINDOW, kvw), F32),
        ],
        compiler_params=pltpu.CompilerParams(
            dimension_semantics=("arbitrary",), vmem_limit_bytes=VMEM_LIMIT),
        name="attn_sample",
    )(q, kdn, vdn, kn, vn, ck, cv, mask, kval, sinkcol)


def _rope_tables(pos):
    inv = ROPE_THETA ** (-jnp.arange(0, HEAD_DIM, 2, dtype=F32) / HEAD_DIM)
    lane = jnp.arange(LANES)
    inv_l = inv[(lane % HEAD_DIM) % (HEAD_DIM // 2)]
    sign = jnp.where((lane % HEAD_DIM) < HEAD_DIM // 2, -1.0, 1.0).astype(F32)
    ang = pos.astype(F32)[:, None] * inv_l[None, :]
    return jnp.cos(ang), jnp.sin(ang) * sign[None, :]


def _sink_columns(sinks, rows):
    s = sinks.astype(F32).reshape(N_KV_HEADS, N_HEADS // N_KV_HEADS, 1)
    return jnp.broadcast_to(s[:, :, None, :], (N_KV_HEADS, 4, rows, 1)).reshape(N_KV_HEADS, 4 * rows, 1)


def kernel(x_prompt, x_sample, state_pool, cache_k, cache_v, sample_start, norm_mix, norm_ffn, norm_final,
           w_pool, ls_pool, w_qkv, b_qkv, sinks, w_o, b_o, w_rg, b_rg, w_re, b_re, w_gate, w_up, w_down):
    B, T, D = x_prompt.shape
    Bd, Td, _ = x_sample.shape
    kvw = N_KV_HEADS * HEAD_DIM
    row = lambda v: v.reshape(1, -1).astype(F32)

    wp = w_pool[0].astype(BF16)
    wqkv = w_qkv[0].astype(BF16)
    wo = w_o[0].astype(BF16)
    wg, wu, wd = w_gate.astype(BF16), w_up.astype(BF16), w_down.astype(BF16)
    pad = ROUTER_LANES - N_EXPERT_GROUPS - N_EXPERTS
    wr = [jnp.concatenate([w_rg[l], w_re[l], jnp.zeros((D, pad), F32)], axis=1) for l in range(2)]
    br = [jnp.concatenate([b_rg[l], b_re[l], jnp.zeros((pad,), F32)]).reshape(1, -1) for l in range(2)]

    start = sample_start.astype(jnp.int32)

    x1p, pool_p16 = _pool_prompt(x_prompt, jnp.zeros((B, HALO, D), F32), row(norm_mix[0]), wp, row(ls_pool[0]))
    seg = HALO + Td
    xcat = jnp.concatenate([jnp.zeros((Bd, 1, D), F32), state_pool[0], x_sample], axis=1).reshape(Bd * seg, D)
    r = jnp.arange(seg, dtype=jnp.int32)[None, :]
    posrow = jnp.where(r >= HALO, start[:, None] + r - HALO, -1).reshape(Bd * seg, 1)
    x1cat, hcat = _pool_sample(xcat, posrow, row(norm_mix[0]), wp, row(ls_pool[0]))
    x1s = x1cat.reshape(Bd, seg, D)[:, HALO:].reshape(Bd * Td, D)
    pool_s = hcat.reshape(Bd, seg, D)[:, seg - POOL_BUF:]
    pool_p = pool_p16[:, HALO - POOL_BUF:]

    moe0 = functools.partial(_moe, nf=row(norm_ffn[0]), wr=wr[0], br=br[0], wg=wg[0], wu=wu[0], wd=wd[0])
    x2p = moe0(x1p.reshape(B * T, D))
    x2s = moe0(x1s)

    cos_p, sin_p = _rope_tables(jnp.arange(T, dtype=jnp.int32))
    pos_s = (start[:, None] + jnp.arange(Td, dtype=jnp.int32)[None, :]).reshape(-1)
    cos_s, sin_s = _rope_tables(pos_s)
    g1 = row(norm_mix[1])
    tmq = 1024
    qp, kdp, vdp, kp, vp = _qkv(x2p, g1, wqkv, row(b_qkv[0]), cos_p, sin_p, tmq, T // tmq)
    qs, kds, vds, ks, vs = _qkv(x2s, g1, wqkv, row(b_qkv[0]), cos_s, sin_s, tmq, (Bd * Td) // tmq)

    op = _attn_prompt(qp, kdp, vdp, _sink_columns(sinks[0], WINDOW), B, T)

    bs = 8
    rows = bs * Td
    W = cache_k.shape[2]
    qrow = jnp.arange(rows)
    ccol = jnp.arange(bs * W)
    ncol = jnp.arange(rows)
    samp_q, t_q = qrow // Td, qrow % Td
    m_cache = (samp_q[:, None] == (ccol // W)[None, :]) & ((ccol % W)[None, :] > t_q[:, None])
    m_new = (samp_q[:, None] == (ncol // Td)[None, :]) & ((ncol % Td)[None, :] <= t_q[:, None])
    amask = jnp.concatenate([m_cache, m_new], axis=1).astype(F32)
    kv_cache = (jnp.arange(W, dtype=jnp.int32)[None, :] >= (W - start)[:, None]).reshape(Bd // bs, bs * W)
    kval = jnp.concatenate([kv_cache, jnp.ones((Bd // bs, rows), bool)], axis=1).astype(F32)
    kval = kval.reshape(Bd // bs, 1, bs * W + rows)
    osamp, nk_s, nv_s = _attn_sample(
        qs, kds, vds, ks, vs, cache_k[0].reshape(Bd, W, kvw), cache_v[0].reshape(Bd, W, kvw),
        amask, kval, _sink_columns(sinks[0], rows), bs, Td)

    moe1 = functools.partial(_moe, nf=row(norm_ffn[1]), wr=wr[1], br=br[1], wg=wg[1], wu=wu[1], wd=wd[1],
                             final=row(norm_final))
    yp = moe1(x2p, oproj=(op, wo, row(b_o[0])))
    ys = moe1(x2s, oproj=(osamp, wo, row(b_o[0])))

    keep = min(WINDOW, T)
    k_p = kp.reshape(B, T, N_KV_HEADS, HEAD_DIM)[:, T - keep:][None]
    v_p = vp.reshape(B, T, N_KV_HEADS, HEAD_DIM)[:, T - keep:][None]
    return (yp.reshape(B, T, D), ys.reshape(Bd, Td, D), pool_p[None], k_p, v_p, pool_s[None],
            nk_s.reshape(1, Bd, W, N_KV_HEADS, HEAD_DIM), nv_s.reshape(1, Bd, W, N_KV_HEADS, HEAD_DIM))
```

```python
import functools

import jax
import jax.numpy as jnp
import numpy as np
from jax import lax
from jax.experimental import pallas as pl
from jax.experimental.pallas import tpu as pltpu

F32 = jnp.float32
BF16 = jnp.bfloat16

D_MODEL = 1024
POOL_WINDOWS = (2, 4, 8, 16)
POOL_GROUP_DIM = 256
POOL_BUF = 15
HALO = 16
HEAD_DIM = 64
N_HEADS = 16
N_KV_HEADS = 4
WINDOW = 128
ROPE_THETA = 10000.0
N_EXPERT_GROUPS = 4
EXPERTS_PER_GROUP = 4
N_EXPERTS = 16
D_EXPERT = 256
RMS_EPS = 1e-6
LANES = 128
ROUTER_LANES = 128
VMEM_LIMIT = 56 * 1024 * 1024
LOG2E = 1.4426950408889634
Q_SCALE = LOG2E * HEAD_DIM ** -0.5


def _rms(x, g):
    ms = jnp.mean(x * x, axis=-1, keepdims=True)
    return x * lax.rsqrt(ms + RMS_EPS) * g


def _pool_windows(hc, s2, s4, s8, rows):
    G = POOL_GROUP_DIM
    a = hc[pl.ds(8, rows), :]
    v2 = a + hc[pl.ds(7, rows), :]
    s2[pl.ds(8, rows), :] = v2
    v4 = v2[:, G:] + s2[pl.ds(6, rows), G:]
    s4[pl.ds(8, rows), G:] = v4
    v8 = v4[:, G:] + s4[pl.ds(4, rows), 2 * G:]
    s8[pl.ds(8, rows), 2 * G:] = v8
    v16 = v8[:, G:] + s8[pl.ds(0, rows), 3 * G:]
    return a, (v2[:, :G], v4[:, :G], v8[:, :G], v16)


def _pool_project(h, wins, pos, wp_ref, ls):
    G = POOL_GROUP_DIM
    outs = []
    for g, w in enumerate(POOL_WINDOWS):
        cnt = jnp.minimum(w, pos + 1).astype(F32)
        d = wins[g] / cnt - h[:, g * G:(g + 1) * G]
        outs.append(jnp.dot(d.astype(BF16), wp_ref[g], preferred_element_type=F32))
    return jnp.concatenate(outs, axis=-1) * ls


def _zero_pads(*refs):
    for r in refs:
        r[pl.ds(0, 8), :] = jnp.zeros((8, D_MODEL), F32)


def _pool_prompt_kernel(x_ref, buf_ref, g_ref, wp_ref, ls_ref, o_ref, nb_ref, hc, s2, s4, s8, *, tq):
    t = pl.program_id(1)

    @pl.when(t == 0)
    def _():
        _zero_pads(hc, s2, s4, s8)
        hc[pl.ds(8, HALO), :] = buf_ref[0]

    @pl.when(t > 0)
    def _():
        hc[pl.ds(8, HALO), :] = hc[pl.ds(8 + tq, HALO), :]

    x = x_ref[0]
    hc[pl.ds(8 + HALO, tq), :] = _rms(x, g_ref[...])
    a, wins = _pool_windows(hc, s2, s4, s8, HALO + tq)
    h = a[HALO:]
    wins = tuple(w[HALO:] for w in wins)
    pos = t * tq + lax.broadcasted_iota(jnp.int32, (tq, 1), 0)
    o_ref[0] = x + _pool_project(h, wins, pos, wp_ref, ls_ref[...])

    @pl.when(t == pl.num_programs(1) - 1)
    def _():
        nb_ref[0] = hc[pl.ds(8 + tq, HALO), :]


def _pool_prompt(x, buf16, g, wp, ls, tq=512):
    B, T, D = x.shape
    return pl.pallas_call(
        functools.partial(_pool_prompt_kernel, tq=tq),
        grid=(B, T // tq),
        in_specs=[
            pl.BlockSpec((1, tq, D), lambda b, t: (b, t, 0)),
            pl.BlockSpec((1, HALO, D), lambda b, t: (b, 0, 0)),
            pl.BlockSpec((1, D), lambda b, t: (0, 0)),
            pl.BlockSpec((4, POOL_GROUP_DIM, POOL_GROUP_DIM), lambda b, t: (0, 0, 0)),
            pl.BlockSpec((1, D), lambda b, t: (0, 0)),
        ],
        out_specs=[
            pl.BlockSpec((1, tq, D), lambda b, t: (b, t, 0)),
            pl.BlockSpec((1, HALO, D), lambda b, t: (b, 0, 0)),
        ],
        out_shape=[jax.ShapeDtypeStruct((B, T, D), F32), jax.ShapeDtypeStruct((B, HALO, D), F32)],
        scratch_shapes=[pltpu.VMEM((8 + HALO + tq, D), F32)] * 4,
        compiler_params=pltpu.CompilerParams(
            dimension_semantics=("arbitrary", "arbitrary"), vmem_limit_bytes=VMEM_LIMIT),
        name="pool_prompt",
    )(x, buf16, g, wp, ls)


def _pool_sample_kernel(x_ref, st_ref, pos_ref, g_ref, wp_ref, ls_ref, o_ref, ns_ref, hc, s2, s4, s8, *, bs, t):
    seg = HALO + t

    @pl.when(pl.program_id(0) == 0)
    def _():
        _zero_pads(hc, s2, s4, s8)

    x = x_ref[...].reshape(bs * t, D_MODEL)
    hn = _rms(x, g_ref[...])
    for s in range(bs):
        hc[pl.ds(8 + seg * s, HALO), :] = st_ref[s]
        hc[pl.ds(8 + seg * s + HALO, t), :] = hn[s * t:(s + 1) * t]
    _, wins = _pool_windows(hc, s2, s4, s8, bs * seg)

    def new_rows(v):
        return v.reshape(bs, seg, v.shape[-1])[:, HALO:, :].reshape(bs * t, v.shape[-1])

    y = _pool_project(hn, tuple(new_rows(w) for w in wins), pos_ref[...], wp_ref, ls_ref[...])
    o_ref[...] = (x + y).reshape(bs, t, D_MODEL)
    ns_ref[:, 0:HALO - t, :] = st_ref[:, t:, :]
    ns_ref[:, HALO - t:, :] = hn.reshape(bs, t, D_MODEL)


def _pool_sample(x, st16, pos, g, wp, ls, bs=32):
    Bd, t, D = x.shape
    assert HALO % t == 0
    rows = bs * (HALO + t)
    return pl.pallas_call(
        functools.partial(_pool_sample_kernel, bs=bs, t=t),
        grid=(Bd // bs,),
        in_specs=[
            pl.BlockSpec((bs, t, D), lambda i: (i, 0, 0)),
            pl.BlockSpec((bs, HALO, D), lambda i: (i, 0, 0)),
            pl.BlockSpec((bs * t, 1), lambda i: (i, 0)),
            pl.BlockSpec((1, D), lambda i: (0, 0)),
            pl.BlockSpec((4, POOL_GROUP_DIM, POOL_GROUP_DIM), lambda i: (0, 0, 0)),
            pl.BlockSpec((1, D), lambda i: (0, 0)),
        ],
        out_specs=[pl.BlockSpec((bs, t, D), lambda i: (i, 0, 0)), pl.BlockSpec((bs, HALO, D), lambda i: (i, 0, 0))],
        out_shape=[jax.ShapeDtypeStruct((Bd, t, D), F32), jax.ShapeDtypeStruct((Bd, HALO, D), F32)],
        scratch_shapes=[pltpu.VMEM((8 + rows, D), F32)] * 4,
        compiler_params=pltpu.CompilerParams(
            dimension_semantics=("arbitrary",), vmem_limit_bytes=VMEM_LIMIT),
        name="pool_sample",
    )(x, st16, pos, g, wp, ls)


def _route(logits):
    lane = lax.broadcasted_iota(jnp.int32, logits.shape, 1).astype(F32)
    big = jnp.float32(1 << 20)
    neg = jnp.float32(-jnp.inf)
    is_g = lane < N_EXPERT_GROUPS
    gl = jnp.where(is_g, logits, neg)
    m = jnp.max(gl, axis=-1, keepdims=True)
    gidx = jnp.min(jnp.where(gl == m, lane, big), axis=-1, keepdims=True)
    z = jnp.sum(jnp.where(is_g, jnp.exp(gl - m), 0.0), axis=-1, keepdims=True)
    gw = 1.0 / z
    lo = N_EXPERT_GROUPS + gidx * EXPERTS_PER_GROUP
    in_grp = (lane >= lo) & (lane < lo + EXPERTS_PER_GROUP)
    el = jnp.where(in_grp, logits, neg)
    v1 = jnp.max(el, axis=-1, keepdims=True)
    i1 = jnp.min(jnp.where(el == v1, lane, big), axis=-1, keepdims=True)
    el2 = jnp.where(lane == i1, neg, el)
    v2 = jnp.max(el2, axis=-1, keepdims=True)
    i2 = jnp.min(jnp.where(el2 == v2, lane, big), axis=-1, keepdims=True)
    t = jnp.exp(v2 - v1)
    w1 = 1.0 / (1.0 + t)
    w2 = t * w1
    cw = gw * (jnp.where(lane == i1 - lo, w1, 0.0) + jnp.where(lane == i2 - lo, w2, 0.0))
    return gidx, cw


def _moe_kernel(*refs, has_oproj, has_final, tm, bm):
    it = iter(refs)
    x_ref = next(it)
    if has_oproj:
        oin_ref, wo_ref, bo_ref = next(it), next(it), next(it)
    nf_ref, wrh_ref, wrl_ref, br_ref, wg_ref, wu_ref, wd_ref = (next(it) for _ in range(7))
    if has_final:
        fn_ref = next(it)
    out_ref = next(it)
    hs, cs, ys = (next(it) for _ in range(3))
    nblk = tm // bm + N_EXPERT_GROUPS
    cap = nblk * bm
    shift = bm.bit_length() - 1
    qr = 2 * bm
    assert cap % qr == 0 and tm % qr == 0

    x = x_ref[...]
    if has_oproj:
        x = x + jnp.dot(oin_ref[...], wo_ref[...], preferred_element_type=F32) + bo_ref[...]
    out_ref[...] = x
    h = _rms(x, nf_ref[...])
    hb = h.astype(BF16)
    h_lo = (h - hb.astype(F32)).astype(BF16)
    logits = (jnp.dot(hb, wrh_ref[...], preferred_element_type=F32)
              + jnp.dot(h_lo, wrh_ref[...], preferred_element_type=F32)
              + jnp.dot(hb, wrl_ref[...], preferred_element_type=F32)) + br_ref[...]
    gidx, cw = _route(logits)
    cw_hi = cw.astype(BF16)
    cw_lo = pltpu.roll(cw - cw_hi.astype(F32), EXPERTS_PER_GROUP, axis=1)
    cwb = cw_hi + cw_lo.astype(BF16)

    lane = lax.broadcasted_iota(jnp.int32, (tm, LANES), 1).astype(F32)
    oh = jnp.where(lane == gidx, 1.0, 0.0)
    ri = lax.broadcasted_iota(jnp.int32, (tm, tm), 0)
    ci = lax.broadcasted_iota(jnp.int32, (tm, tm), 1)
    before = jnp.where(ci < ri, 1.0, 0.0).astype(BF16)
    cnt_before = jnp.dot(before, oh.astype(BF16), preferred_element_type=F32)
    pos = jnp.sum(oh * cnt_before, axis=-1, keepdims=True)
    start_blk, n_blk = [], []
    off = jnp.int32(0)
    for g in range(N_EXPERT_GROUPS):
        in_g = gidx == g
        n_g = jnp.sum(jnp.where(in_g, 1.0, 0.0)).astype(jnp.int32)
        blocks = (n_g + (bm - 1)) >> shift
        start_blk.append(off)
        n_blk.append(blocks)
        pos = pos + jnp.where(in_g, (off << shift).astype(F32), 0.0)
        off = off + blocks
    pos_row = jnp.transpose(jnp.broadcast_to(pos, (tm, LANES)))[0:1, :]

    total_blk = off
    min_q = tm // qr

    def sort_chunk(q):
        sub = (q * qr + lax.broadcasted_iota(jnp.int32, (qr, tm), 0)).astype(F32)
        sel = jnp.where(pos_row == sub, 1.0, 0.0).astype(BF16)
        hs[q * qr:(q + 1) * qr, :] = jnp.dot(sel, hb, preferred_element_type=F32).astype(BF16)
        cs[q * qr:(q + 1) * qr, :] = jnp.dot(sel, cwb, preferred_element_type=F32)

    for q in range(cap // qr):
        if q < min_q:
            sort_chunk(q)
        else:
            pl.when(total_blk > q * (qr // bm))(functools.partial(sort_chunk, q))
    ys[min_q * qr:, :] = jnp.zeros((cap - min_q * qr, D_MODEL), BF16)

    for g in range(N_EXPERT_GROUPS):
        wd = wd_ref[g * EXPERTS_PER_GROUP:(g + 1) * EXPERTS_PER_GROUP].reshape(
            EXPERTS_PER_GROUP * D_EXPERT, D_MODEL)

        def block(b, carry, g=g, wd=wd):
            rows = pl.ds(pl.multiple_of(b * bm, bm), bm)
            hblk = hs[rows, :]
            cblk = cs[rows, :]
            ln = lax.broadcasted_iota(jnp.int32, cblk.shape, 1)
            parts = []
            for e in range(EXPERTS_PER_GROUP):
                ce = jnp.sum(jnp.where((ln == e) | (ln == e + EXPERTS_PER_GROUP), cblk, 0.0),
                             axis=-1, keepdims=True)
                gt = jnp.dot(hblk, wg_ref[g * EXPERTS_PER_GROUP + e], preferred_element_type=F32)
                up = jnp.dot(hblk, wu_ref[g * EXPERTS_PER_GROUP + e], preferred_element_type=F32)
                a = gt / (1.0 + jnp.exp(-gt)) * up * ce
                parts.append(a.astype(BF16))
            a_all = jnp.concatenate(parts, axis=-1)
            ys[rows, :] = jnp.dot(a_all, wd, preferred_element_type=F32).astype(BF16)
            return carry

        lax.fori_loop(start_blk[g], start_blk[g] + n_blk[g], block, 0)

    def unsort_chunk(q):
        col = (q * qr + lax.broadcasted_iota(jnp.int32, (tm, qr), 1)).astype(F32)
        sel = jnp.where(pos == col, 1.0, 0.0).astype(BF16)
        out_ref[...] += jnp.dot(sel, ys[q * qr:(q + 1) * qr, :], preferred_element_type=F32)

    for q in range(cap // qr):
        if q < min_q:
            unsort_chunk(q)
        else:
            pl.when(total_blk > q * (qr // bm))(functools.partial(unsort_chunk, q))
    if has_final:
        out_ref[...] = _rms(out_ref[...], fn_ref[...])


def _moe(x, nf, wr, br, wg, wu, wd, oproj=None, final=None, tm=512, bm=128):
    N, D = x.shape
    has_oproj = oproj is not None
    has_final = final is not None
    const2 = lambda i: (0, 0)
    const3 = lambda i: (0, 0, 0)
    tile = lambda i: (i, 0)
    once = pl.Buffered(1)
    args = [x]
    in_specs = [pl.BlockSpec((tm, D), tile)]
    if has_oproj:
        o, wo, bo = oproj
        args += [o, wo, bo]
        in_specs += [pl.BlockSpec((tm, D), tile), pl.BlockSpec((D, D), const2, pipeline_mode=once),
                     pl.BlockSpec((1, D), const2)]
    wr_hi = wr.astype(BF16)
    wr_lo = (wr - wr_hi.astype(F32)).astype(BF16)
    args += [nf, wr_hi, wr_lo, br, wg, wu, wd]
    in_specs += [
        pl.BlockSpec((1, D), const2),
        pl.BlockSpec((D, ROUTER_LANES), const2, pipeline_mode=once),
        pl.BlockSpec((D, ROUTER_LANES), const2, pipeline_mode=once),
        pl.BlockSpec((1, ROUTER_LANES), const2),
        pl.BlockSpec((N_EXPERTS, D, D_EXPERT), const3, pipeline_mode=once),
        pl.BlockSpec((N_EXPERTS, D, D_EXPERT), const3, pipeline_mode=once),
        pl.BlockSpec((N_EXPERTS, D_EXPERT, D), const3, pipeline_mode=once),
    ]
    if has_final:
        args.append(final)
        in_specs.append(pl.BlockSpec((1, D), const2))
    cap = (tm // bm + N_EXPERT_GROUPS) * bm
    return pl.pallas_call(
        functools.partial(_moe_kernel, has_oproj=has_oproj, has_final=has_final, tm=tm, bm=bm),
        grid=(N // tm,),
        in_specs=in_specs,
        out_specs=pl.BlockSpec((tm, D), tile),
        out_shape=jax.ShapeDtypeStruct((N, D), F32),
        scratch_shapes=[
            pltpu.VMEM((cap, D), BF16),
            pltpu.VMEM((cap, LANES), F32),
            pltpu.VMEM((cap, D), BF16),
        ],
        compiler_params=pltpu.CompilerParams(
            dimension_semantics=("arbitrary",), vmem_limit_bytes=VMEM_LIMIT),
        name="moe",
    )(*args)


def _swap_halves(x, low):
    return jnp.where(low, pltpu.roll(x, LANES - 32, axis=1), pltpu.roll(x, 32, axis=1))


def _qkv_kernel(x_ref, g_ref, w_ref, b_ref, cos_ref, sin_ref, q_ref, kd_ref, vd_ref, k_ref, v_ref):
    h = _rms(x_ref[...], g_ref[...]).astype(BF16)
    qkv = jnp.dot(h, w_ref[...], preferred_element_type=F32) + b_ref[...]
    cos = cos_ref[...]
    sin = sin_ref[...]
    lane = lax.broadcasted_iota(jnp.int32, cos.shape, 1)
    low32 = (lane % HEAD_DIM) < (HEAD_DIM // 2)
    low64 = lane < HEAD_DIM
    nq = N_HEADS * HEAD_DIM
    nk = N_KV_HEADS * HEAD_DIM

    def rope(c):
        return c * cos + _swap_halves(c, low32) * sin

    def dup(c):
        r = pltpu.roll(c, HEAD_DIM, axis=1)
        return jnp.where(low64, c, r), jnp.where(low64, r, c)

    for j in range(nq // LANES):
        c = rope(qkv[:, j * LANES:(j + 1) * LANES]) * Q_SCALE
        q_ref[:, j * LANES:(j + 1) * LANES] = c.astype(BF16)
    for j in range(nk // LANES):
        c = rope(qkv[:, nq + j * LANES:nq + (j + 1) * LANES])
        k_ref[:, j * LANES:(j + 1) * LANES] = c
        d0, d1 = dup(c)
        kd_ref[:, 2 * j * LANES:(2 * j + 1) * LANES] = d0.astype(BF16)
        kd_ref[:, (2 * j + 1) * LANES:(2 * j + 2) * LANES] = d1.astype(BF16)
        c = qkv[:, nq + nk + j * LANES:nq + nk + (j + 1) * LANES]
        v_ref[:, j * LANES:(j + 1) * LANES] = c
        d0, d1 = dup(c)
        vd_ref[:, 2 * j * LANES:(2 * j + 1) * LANES] = d0.astype(BF16)
        vd_ref[:, (2 * j + 1) * LANES:(2 * j + 2) * LANES] = d1.astype(BF16)


def _qkv(x, g, w, b, cos, sin, tm, n_pos_tiles):
    N, D = x.shape
    nq, nk = N_HEADS * HEAD_DIM, N_KV_HEADS * HEAD_DIM
    const = lambda i: (0, 0)
    tile = lambda i: (i, 0)
    ptile = lambda i: (i % n_pos_tiles, 0)
    return pl.pallas_call(
        _qkv_kernel,
        grid=(N // tm,),
        in_specs=[
            pl.BlockSpec((tm, D), tile),
            pl.BlockSpec((1, D), const),
            pl.BlockSpec((D, nq + 2 * nk), const),
            pl.BlockSpec((1, nq + 2 * nk), const),
            pl.BlockSpec((tm, LANES), ptile),
            pl.BlockSpec((tm, LANES), ptile),
        ],
        out_specs=[
            pl.BlockSpec((tm, nq), tile),
            pl.BlockSpec((tm, 2 * nk), tile),
            pl.BlockSpec((tm, 2 * nk), tile),
            pl.BlockSpec((tm, nk), tile),
            pl.BlockSpec((tm, nk), tile),
        ],
        out_shape=[
            jax.ShapeDtypeStruct((N, nq), BF16),
            jax.ShapeDtypeStruct((N, 2 * nk), BF16),
            jax.ShapeDtypeStruct((N, 2 * nk), BF16),
            jax.ShapeDtypeStruct((N, nk), F32),
            jax.ShapeDtypeStruct((N, nk), F32),
        ],
        compiler_params=pltpu.CompilerParams(
            dimension_semantics=("arbitrary",), vmem_limit_bytes=VMEM_LIMIT),
        name="qkv",
    )(x, g, w, b, cos, sin)


def _attend(qc_list, kd, vd, bias, sinkrow):
    R = qc_list[0].shape[0]
    lane = lax.broadcasted_iota(jnp.int32, qc_list[0].shape, 1)
    low = lane < HEAD_DIM
    zero = jnp.zeros_like(qc_list[0])
    stack = []
    for qc in qc_list:
        stack.append(jnp.where(low, qc, zero))
        stack.append(jnp.where(low, zero, qc))
    qs = jnp.concatenate(stack, axis=0)
    s = lax.dot_general(kd, qs, (((1,), (1,)), ((), ())), preferred_element_type=F32) + bias
    m = jnp.maximum(jnp.max(s, axis=0, keepdims=True), sinkrow)
    p = jnp.exp2(s - m)
    den = jnp.sum(p, axis=0, keepdims=True) + jnp.exp2(sinkrow - m)
    p = (p * (1.0 / den)).astype(BF16)
    o = lax.dot_general(p, vd, (((0,), (0,)), ((), ())), preferred_element_type=F32)
    lowf = lax.broadcasted_iota(jnp.int32, (R, LANES), 1) < HEAD_DIM
    return (jnp.where(lowf, o[0:R], o[R:2 * R]), jnp.where(lowf, o[2 * R:3 * R], o[3 * R:4 * R]))


def _attn_prompt_kernel(q_ref, kp_ref, kc_ref, vp_ref, vc_ref, sink_ref, o_ref, *, tq):
    n = pl.program_id(1)
    B = WINDOW
    kd = jnp.concatenate([kp_ref[...], kc_ref[...]], axis=0)
    vd = jnp.concatenate([vp_ref[...], vc_ref[...]], axis=0)
    kj = lax.broadcasted_iota(jnp.int32, (2 * B, B), 0)
    qi = lax.broadcasted_iota(jnp.int32, (2 * B, B), 1)
    rel = B + qi - kj
    band = (rel >= 0) & (rel < WINDOW)
    neg = jnp.float32(-jnp.inf)
    band_bias = jnp.concatenate([jnp.where(band, 0.0, neg)] * 4, axis=1)
    first_bias = jnp.concatenate([jnp.where(band & ((kj >= B) | (n > 0)), 0.0, neg)] * 4, axis=1)
    for j in range(tq // B):
        bias = first_bias if j == 0 else band_bias
        for g in range(N_KV_HEADS):
            qc = [q_ref[j * B:(j + 1) * B, (2 * g + i) * LANES:(2 * g + i + 1) * LANES] for i in range(2)]
            kg = kd[j * B:(j + 2) * B, g * LANES:(g + 1) * LANES]
            vg = vd[j * B:(j + 2) * B, g * LANES:(g + 1) * LANES]
            o0, o1 = _attend(qc, kg, vg, bias, sink_ref[g])
            o_ref[j * B:(j + 1) * B, 2 * g * LANES:(2 * g + 1) * LANES] = o0.astype(BF16)
            o_ref[j * B:(j + 1) * B, (2 * g + 1) * LANES:(2 * g + 2) * LANES] = o1.astype(BF16)


def _attn_prompt(q, kd, vd, sinkrow, batch, seq, tq=512):
    N = q.shape[0]
    nt = seq // tq
    r = tq // WINDOW
    cur = lambda b, n: (b * nt + n, 0)
    prev = lambda b, n: (jnp.maximum((b * nt + n) * r - 1, 0), 0)
    kvw = 2 * N_KV_HEADS * HEAD_DIM
    return pl.pallas_call(
        functools.partial(_attn_prompt_kernel, tq=tq),
        grid=(batch, nt),
        in_specs=[
            pl.BlockSpec((tq, N_HEADS * HEAD_DIM), cur),
            pl.BlockSpec((WINDOW, kvw), prev),
            pl.BlockSpec((tq, kvw), cur),
            pl.BlockSpec((WINDOW, kvw), prev),
            pl.BlockSpec((tq, kvw), cur),
            pl.BlockSpec((N_KV_HEADS, 1, 4 * WINDOW), lambda b, n: (0, 0, 0)),
        ],
        out_specs=pl.BlockSpec((tq, N_HEADS * HEAD_DIM), cur),
        out_shape=jax.ShapeDtypeStruct((N, N_HEADS * HEAD_DIM), BF16),
        compiler_params=pltpu.CompilerParams(
            dimension_semantics=("arbitrary", "arbitrary"), vmem_limit_bytes=VMEM_LIMIT),
        name="attn_prompt",
    )(q, kd, kd, vd, vd, sinkrow)


def _dup_cache(c, g):
    low = lax.broadcasted_iota(jnp.int32, c.shape, 1) < HEAD_DIM
    r = pltpu.roll(c, HEAD_DIM, axis=1)
    d = jnp.where(low, c, r) if g % 2 == 0 else jnp.where(low, r, c)
    return d.astype(BF16)


def _attn_sample_kernel(q_ref, kdn_ref, vdn_ref, kn_ref, vn_ref, ck_ref, cv_ref, mask_ref, kval_ref, sink_ref,
                        o_ref, nk_ref, nv_ref, *, bs, t):
    W = WINDOW
    ck = ck_ref[...]
    cv = cv_ref[...]
    nk_ref[:, 0:W - t, :] = ck[:, t:, :]
    nk_ref[:, W - t:, :] = kn_ref[...].reshape(bs, t, N_KV_HEADS * HEAD_DIM)
    nv_ref[:, 0:W - t, :] = cv[:, t:, :]
    nv_ref[:, W - t:, :] = vn_ref[...].reshape(bs, t, N_KV_HEADS * HEAD_DIM)

    ck2 = ck.reshape(bs * W, N_KV_HEADS * HEAD_DIM)
    cv2 = cv.reshape(bs * W, N_KV_HEADS * HEAD_DIM)
    bias = jnp.where((mask_ref[...] > 0) & (kval_ref[0] > 0), 0.0, -jnp.inf).astype(F32)
    for g in range(N_KV_HEADS):
        m = g // 2
        kg = jnp.concatenate([_dup_cache(ck2[:, m * LANES:(m + 1) * LANES], g),
                              kdn_ref[:, g * LANES:(g + 1) * LANES]], axis=0)
        vg = jnp.concatenate([_dup_cache(cv2[:, m * LANES:(m + 1) * LANES], g),
                              vdn_ref[:, g * LANES:(g + 1) * LANES]], axis=0)
        qc = [q_ref[:, (2 * g + i) * LANES:(2 * g + i + 1) * LANES] for i in range(2)]
        o0, o1 = _attend(qc, kg, vg, bias, sink_ref[g])
        o_ref[:, 2 * g * LANES:(2 * g + 1) * LANES] = o0.astype(BF16)
        o_ref[:, (2 * g + 1) * LANES:(2 * g + 2) * LANES] = o1.astype(BF16)


def _attn_sample(q, kdn, vdn, kn, vn, ck, cv, mask, kval, sinkrow, bs, t):
    Bd = ck.shape[0]
    rows = bs * t
    kvw = N_KV_HEADS * HEAD_DIM
    S = bs * WINDOW + rows
    tile = lambda i: (i, 0)
    tile3 = lambda i: (i, 0, 0)
    return pl.pallas_call(
        functools.partial(_attn_sample_kernel, bs=bs, t=t),
        grid=(Bd // bs,),
        in_specs=[
            pl.BlockSpec((rows, N_HEADS * HEAD_DIM), tile),
            pl.BlockSpec((rows, 2 * kvw), tile),
            pl.BlockSpec((rows, 2 * kvw), tile),
            pl.BlockSpec((rows, kvw), tile),
            pl.BlockSpec((rows, kvw), tile),
            pl.BlockSpec((bs, WINDOW, kvw), tile3),
            pl.BlockSpec((bs, WINDOW, kvw), tile3),
            pl.BlockSpec((S, 4 * rows), lambda i: (0, 0)),
            pl.BlockSpec((1, S, 1), tile3),
            pl.BlockSpec((N_KV_HEADS, 1, 4 * rows), lambda i: (0, 0, 0)),
        ],
        out_specs=[
            pl.BlockSpec((rows, N_HEADS * HEAD_DIM), tile),
            pl.BlockSpec((bs, WINDOW, kvw), tile3),
            pl.BlockSpec((bs, WINDOW, kvw), tile3),
        ],
        out_shape=[
            jax.ShapeDtypeStruct((Bd * t, N_HEADS * HEAD_DIM), BF16),
            jax.ShapeDtypeStruct((Bd, WINDOW, kvw), F32),
            jax.ShapeDtypeStruct((Bd, WINDOW, kvw), F32),
        ],
        compiler_params=pltpu.CompilerParams(
            dimension_semantics=("arbitrary",), vmem_limit_bytes=VMEM_LIMIT),
        name="attn_sample",
    )(q, kdn, vdn, kn, vn, ck, cv, mask, kval, sinkrow)


def _rope_tables(pos):
    inv = ROPE_THETA ** (-jnp.arange(0, HEAD_DIM, 2, dtype=F32) / HEAD_DIM)
    inv_l = jnp.tile(inv, LANES // (HEAD_DIM // 2))
    sign = jnp.asarray(np.where((np.arange(LANES) % HEAD_DIM) < HEAD_DIM // 2, -1.0, 1.0), F32)
    ang = pos.astype(F32)[:, None] * inv_l[None, :]
    return jnp.cos(ang), jnp.sin(ang) * sign[None, :]


def _sink_rows(sinks, rows):
    s = (sinks.astype(F32) * LOG2E).reshape(N_KV_HEADS, 1, N_HEADS // N_KV_HEADS, 1)
    return jnp.broadcast_to(s, (N_KV_HEADS, 1, 4, rows)).reshape(N_KV_HEADS, 1, 4 * rows)


def kernel(x_prompt, x_sample, state_pool, cache_k, cache_v, sample_start, norm_mix, norm_ffn, norm_final,
           w_pool, ls_pool, w_qkv, b_qkv, sinks, w_o, b_o, w_rg, b_rg, w_re, b_re, w_gate, w_up, w_down):
    B, T, D = x_prompt.shape
    Bd, Td, _ = x_sample.shape
    kvw = N_KV_HEADS * HEAD_DIM
    row = lambda v: v.reshape(1, -1).astype(F32)

    wp = w_pool[0].astype(BF16)
    wqkv = w_qkv[0].astype(BF16)
    wo = w_o[0].astype(BF16)
    wg, wu, wd = w_gate.astype(BF16), w_up.astype(BF16), w_down.astype(BF16)
    pad = ROUTER_LANES - N_EXPERT_GROUPS - N_EXPERTS
    wr = [jnp.concatenate([w_rg[l], w_re[l], jnp.zeros((D, pad), F32)], axis=1) for l in range(2)]
    br = [jnp.concatenate([b_rg[l], b_re[l], jnp.zeros((pad,), F32)]).reshape(1, -1) for l in range(2)]

    start = sample_start.astype(jnp.int32)

    x1p, pool_p16 = _pool_prompt(x_prompt, jnp.zeros((B, HALO, D), F32), row(norm_mix[0]), wp, row(ls_pool[0]))
    pos_s = (start[:, None] + jnp.arange(Td, dtype=jnp.int32)[None, :]).reshape(-1)
    st16 = jnp.pad(state_pool[0], ((0, 0), (HALO - POOL_BUF, 0), (0, 0)))
    x1s, pool_s16 = _pool_sample(x_sample, st16, pos_s.reshape(-1, 1), row(norm_mix[0]), wp, row(ls_pool[0]))
    x1s = x1s.reshape(Bd * Td, D)
    pool_s = pool_s16[:, HALO - POOL_BUF:]
    pool_p = pool_p16[:, HALO - POOL_BUF:]

    moe0 = functools.partial(_moe, nf=row(norm_ffn[0]), wr=wr[0], br=br[0], wg=wg[0], wu=wu[0], wd=wd[0])
    x2p = moe0(x1p.reshape(B * T, D))
    x2s = moe0(x1s)

    cos_p, sin_p = _rope_tables(jnp.arange(T, dtype=jnp.int32))
    cos_s, sin_s = _rope_tables(pos_s)
    g1 = row(norm_mix[1])
    tmq = 1024
    qp, kdp, vdp, kp, vp = _qkv(x2p, g1, wqkv, row(b_qkv[0]), cos_p, sin_p, tmq, T // tmq)
    qs, kds, vds, ks, vs = _qkv(x2s, g1, wqkv, row(b_qkv[0]), cos_s, sin_s, tmq, (Bd * Td) // tmq)

    op = _attn_prompt(qp, kdp, vdp, _sink_rows(sinks[0], WINDOW), B, T)

    bs = 8
    rows = bs * Td
    W = cache_k.shape[2]
    qrow = np.arange(rows)
    ccol = np.arange(bs * W)
    ncol = np.arange(rows)
    samp_q, t_q = qrow // Td, qrow % Td
    m_cache = (samp_q[:, None] == (ccol // W)[None, :]) & ((ccol % W)[None, :] > t_q[:, None])
    m_new = (samp_q[:, None] == (ncol // Td)[None, :]) & ((ncol % Td)[None, :] <= t_q[:, None])
    amask = jnp.asarray(np.tile(np.concatenate([m_cache, m_new], axis=1).T, (1, 4)), F32)
    kv_cache = (jnp.arange(W, dtype=jnp.int32)[None, :] >= (W - start)[:, None]).reshape(Bd // bs, bs * W)
    kval = jnp.concatenate([kv_cache, jnp.ones((Bd // bs, rows), bool)], axis=1).astype(F32)
    kval = kval.reshape(Bd // bs, bs * W + rows, 1)
    osamp, nk_s, nv_s = _attn_sample(
        qs, kds, vds, ks, vs, cache_k[0].reshape(Bd, W, kvw), cache_v[0].reshape(Bd, W, kvw),
        amask, kval, _sink_rows(sinks[0], rows), bs, Td)

    moe1 = functools.partial(_moe, nf=row(norm_ffn[1]), wr=wr[1], br=br[1], wg=wg[1], wu=wu[1], wd=wd[1],
                             final=row(norm_final))
    yp = moe1(x2p, oproj=(op, wo, row(b_o[0])))
    ys = moe1(x2s, oproj=(osamp, wo, row(b_o[0])))

    keep = min(WINDOW, T)
    k_p = kp.reshape(B, T, N_KV_HEADS, HEAD_DIM)[:, T - keep:][None]
    v_p = vp.reshape(B, T, N_KV_HEADS, HEAD_DIM)[:, T - keep:][None]
    return (yp.reshape(B, T, D), ys.reshape(Bd, Td, D), pool_p[None], k_p, v_p, pool_s[None],
            nk_s.reshape(1, Bd, W, N_KV_HEADS, HEAD_DIM), nv_s.reshape(1, Bd, W, N_KV_HEADS, HEAD_DIM))
```

```python
import functools

import jax
import jax.numpy as jnp
import numpy as np
from jax import lax
from jax.experimental import pallas as pl
from jax.experimental.pallas import tpu as pltpu

F32 = jnp.float32
BF16 = jnp.bfloat16

D_MODEL = 1024
POOL_WINDOWS = (2, 4, 8, 16)
POOL_GROUP_DIM = 256
POOL_BUF = 15
HALO = 16
HEAD_DIM = 64
N_HEADS = 16
N_KV_HEADS = 4
WINDOW = 128
ROPE_THETA = 10000.0
N_EXPERT_GROUPS = 4
EXPERTS_PER_GROUP = 4
N_EXPERTS = 16
D_EXPERT = 256
RMS_EPS = 1e-6
LANES = 128
ROUTER_ROWS = 32
EXPERT_ROW0 = 8
VMEM_LIMIT = 56 * 1024 * 1024
LOG2E = 1.4426950408889634
Q_SCALE = LOG2E * HEAD_DIM ** -0.5


def _rms(x, g):
    ms = jnp.mean(x * x, axis=-1, keepdims=True)
    return x * lax.rsqrt(ms + RMS_EPS) * g


def _pool_windows(hc, s2, s4, s8, rows):
    G = POOL_GROUP_DIM
    a = hc[pl.ds(8, rows), :]
    v2 = a + hc[pl.ds(7, rows), :]
    s2[pl.ds(8, rows), :] = v2
    v4 = v2[:, G:] + s2[pl.ds(6, rows), G:]
    s4[pl.ds(8, rows), G:] = v4
    v8 = v4[:, G:] + s4[pl.ds(4, rows), 2 * G:]
    s8[pl.ds(8, rows), 2 * G:] = v8
    v16 = v8[:, G:] + s8[pl.ds(0, rows), 3 * G:]
    return a, (v2[:, :G], v4[:, :G], v8[:, :G], v16)


def _pool_project(h, wins, pos, wp_ref, ls):
    G = POOL_GROUP_DIM
    outs = []
    for g, w in enumerate(POOL_WINDOWS):
        cnt = jnp.minimum(w, pos + 1).astype(F32)
        d = wins[g] / cnt - h[:, g * G:(g + 1) * G]
        outs.append(jnp.dot(d.astype(BF16), wp_ref[g], preferred_element_type=F32))
    return jnp.concatenate(outs, axis=-1) * ls


def _zero_pads(*refs):
    for r in refs:
        r[pl.ds(0, 8), :] = jnp.zeros((8, D_MODEL), F32)


def _pool_prompt_kernel(x_ref, buf_ref, g_ref, wp_ref, ls_ref, o_ref, nb_ref, hc, s2, s4, s8, *, tq):
    t = pl.program_id(1)

    @pl.when(t == 0)
    def _():
        _zero_pads(hc, s2, s4, s8)
        hc[pl.ds(8, HALO), :] = buf_ref[0]

    @pl.when(t > 0)
    def _():
        hc[pl.ds(8, HALO), :] = hc[pl.ds(8 + tq, HALO), :]

    x = x_ref[0]
    hc[pl.ds(8 + HALO, tq), :] = _rms(x, g_ref[...])
    a, wins = _pool_windows(hc, s2, s4, s8, HALO + tq)
    h = a[HALO:]
    wins = tuple(w[HALO:] for w in wins)
    pos = t * tq + lax.broadcasted_iota(jnp.int32, (tq, 1), 0)
    o_ref[0] = x + _pool_project(h, wins, pos, wp_ref, ls_ref[...])

    @pl.when(t == pl.num_programs(1) - 1)
    def _():
        nb_ref[0] = hc[pl.ds(8 + tq, HALO), :]


def _pool_prompt(x, buf16, g, wp, ls, tq=512):
    B, T, D = x.shape
    return pl.pallas_call(
        functools.partial(_pool_prompt_kernel, tq=tq),
        grid=(B, T // tq),
        in_specs=[
            pl.BlockSpec((1, tq, D), lambda b, t: (b, t, 0)),
            pl.BlockSpec((1, HALO, D), lambda b, t: (b, 0, 0)),
            pl.BlockSpec((1, D), lambda b, t: (0, 0)),
            pl.BlockSpec((4, POOL_GROUP_DIM, POOL_GROUP_DIM), lambda b, t: (0, 0, 0)),
            pl.BlockSpec((1, D), lambda b, t: (0, 0)),
        ],
        out_specs=[
            pl.BlockSpec((1, tq, D), lambda b, t: (b, t, 0)),
            pl.BlockSpec((1, HALO, D), lambda b, t: (b, 0, 0)),
        ],
        out_shape=[jax.ShapeDtypeStruct((B, T, D), F32), jax.ShapeDtypeStruct((B, HALO, D), F32)],
        scratch_shapes=[pltpu.VMEM((8 + HALO + tq, D), F32)] * 4,
        compiler_params=pltpu.CompilerParams(
            dimension_semantics=("arbitrary", "arbitrary"), vmem_limit_bytes=VMEM_LIMIT),
        name="pool_prompt",
    )(x, buf16, g, wp, ls)


def _pool_sample_kernel(x_ref, st_ref, pos_ref, g_ref, wp_ref, ls_ref, o_ref, ns_ref, hc, s2, s4, s8, *, bs, t):
    seg = HALO + t

    @pl.when(pl.program_id(0) == 0)
    def _():
        _zero_pads(hc, s2, s4, s8)

    x = x_ref[...].reshape(bs * t, D_MODEL)
    hn = _rms(x, g_ref[...])
    for s in range(bs):
        hc[pl.ds(8 + seg * s, HALO), :] = st_ref[s]
        hc[pl.ds(8 + seg * s + HALO, t), :] = hn[s * t:(s + 1) * t]
    _, wins = _pool_windows(hc, s2, s4, s8, bs * seg)

    def new_rows(v):
        return v.reshape(bs, seg, v.shape[-1])[:, HALO:, :].reshape(bs * t, v.shape[-1])

    y = _pool_project(hn, tuple(new_rows(w) for w in wins), pos_ref[...], wp_ref, ls_ref[...])
    o_ref[...] = (x + y).reshape(bs, t, D_MODEL)
    ns_ref[:, 0:HALO - t, :] = st_ref[:, t:, :]
    ns_ref[:, HALO - t:, :] = hn.reshape(bs, t, D_MODEL)


def _pool_sample(x, st16, pos, g, wp, ls, bs=32):
    Bd, t, D = x.shape
    assert HALO % t == 0
    rows = bs * (HALO + t)
    return pl.pallas_call(
        functools.partial(_pool_sample_kernel, bs=bs, t=t),
        grid=(Bd // bs,),
        in_specs=[
            pl.BlockSpec((bs, t, D), lambda i: (i, 0, 0)),
            pl.BlockSpec((bs, HALO, D), lambda i: (i, 0, 0)),
            pl.BlockSpec((bs * t, 1), lambda i: (i, 0)),
            pl.BlockSpec((1, D), lambda i: (0, 0)),
            pl.BlockSpec((4, POOL_GROUP_DIM, POOL_GROUP_DIM), lambda i: (0, 0, 0)),
            pl.BlockSpec((1, D), lambda i: (0, 0)),
        ],
        out_specs=[pl.BlockSpec((bs, t, D), lambda i: (i, 0, 0)), pl.BlockSpec((bs, HALO, D), lambda i: (i, 0, 0))],
        out_shape=[jax.ShapeDtypeStruct((Bd, t, D), F32), jax.ShapeDtypeStruct((Bd, HALO, D), F32)],
        scratch_shapes=[pltpu.VMEM((8 + rows, D), F32)] * 4,
        compiler_params=pltpu.CompilerParams(
            dimension_semantics=("arbitrary",), vmem_limit_bytes=VMEM_LIMIT),
        name="pool_sample",
    )(x, st16, pos, g, wp, ls)


def _route(lg):
    R = lg.shape[1]
    big = jnp.float32(1 << 20)
    neg = jnp.float32(-jnp.inf)
    r8 = lax.broadcasted_iota(jnp.int32, (8, R), 0).astype(F32)
    r16 = lax.broadcasted_iota(jnp.int32, (N_EXPERTS, R), 0).astype(F32)
    is_g = r8 < N_EXPERT_GROUPS
    gl = jnp.where(is_g, lg[0:8], neg)
    m = jnp.max(gl, axis=0, keepdims=True)
    gidx = jnp.min(jnp.where(gl == m, r8, big), axis=0, keepdims=True)
    z = jnp.sum(jnp.where(is_g, jnp.exp(gl - m), 0.0), axis=0, keepdims=True)
    gw = 1.0 / z
    lo = gidx * EXPERTS_PER_GROUP
    in_grp = (r16 >= lo) & (r16 < lo + EXPERTS_PER_GROUP)
    el = jnp.where(in_grp, lg[EXPERT_ROW0:EXPERT_ROW0 + N_EXPERTS], neg)
    v1 = jnp.max(el, axis=0, keepdims=True)
    i1 = jnp.min(jnp.where(el == v1, r16, big), axis=0, keepdims=True)
    el2 = jnp.where(r16 == i1, neg, el)
    v2 = jnp.max(el2, axis=0, keepdims=True)
    i2 = jnp.min(jnp.where(el2 == v2, r16, big), axis=0, keepdims=True)
    t = jnp.exp(v2 - v1)
    w1 = 1.0 / (1.0 + t)
    w2 = t * w1
    e8 = jnp.where(is_g, r8, r8 - EXPERTS_PER_GROUP)
    cw = gw * (jnp.where(e8 == i1 - lo, w1, 0.0) + jnp.where(e8 == i2 - lo, w2, 0.0))
    return gidx, cw


def _moe_kernel(*refs, has_oproj, has_final, tm, bm):
    it = iter(refs)
    x_ref = next(it)
    if has_oproj:
        oin_ref, wo_ref, bo_ref = next(it), next(it), next(it)
    nf_ref, wr_ref, br_ref, wg_ref, wu_ref, wd_ref = (next(it) for _ in range(6))
    if has_final:
        fn_ref = next(it)
    out_ref = next(it)
    hs, cs, ys, sel_s, earlier = (next(it) for _ in range(5))
    nblk = tm // bm + N_EXPERT_GROUPS
    cap = nblk * bm
    shift = bm.bit_length() - 1
    qr = 2 * bm
    assert cap % qr == 0 and tm % qr == 0
    nt = (((1,), (1,)), ((), ()))

    @pl.when(pl.program_id(0) == 0)
    def _():
        ri = lax.broadcasted_iota(jnp.int32, (tm, tm), 0)
        ci = lax.broadcasted_iota(jnp.int32, (tm, tm), 1)
        earlier[...] = jnp.where(ri < ci, 1.0, 0.0).astype(BF16)

    x = x_ref[...]
    if has_oproj:
        x = x + jnp.dot(oin_ref[...], wo_ref[...], preferred_element_type=F32) + bo_ref[...]
    out_ref[...] = x
    h = _rms(x, nf_ref[...])
    hb = h.astype(BF16)
    h_lo = (h - hb.astype(F32)).astype(BF16)
    wr = wr_ref[...]
    both = lax.dot_general(wr, hb, nt, preferred_element_type=F32)
    lg = (both[0:ROUTER_ROWS] + both[ROUTER_ROWS:]
          + lax.dot_general(wr[0:ROUTER_ROWS], h_lo, nt, preferred_element_type=F32)) + br_ref[...]
    gidx, cw = _route(lg)
    r8 = lax.broadcasted_iota(jnp.int32, (8, tm), 0)
    cw_hi = cw.astype(BF16).astype(F32)
    cwt = jnp.concatenate([jnp.where(r8 < EXPERTS_PER_GROUP, cw_hi, cw - cw_hi),
                           jnp.zeros((LANES - 8, tm), F32)], axis=0).astype(BF16)

    oh = jnp.where(r8.astype(F32) == gidx, 1.0, 0.0)
    cnt_before = jnp.dot(oh.astype(BF16), earlier[...], preferred_element_type=F32)
    pos_row = jnp.sum(oh * cnt_before, axis=0, keepdims=True)
    start_blk, n_blk = [], []
    off = jnp.int32(0)
    for g in range(N_EXPERT_GROUPS):
        in_g = gidx == g
        n_g = jnp.sum(jnp.where(in_g, 1.0, 0.0)).astype(jnp.int32)
        blocks = (n_g + (bm - 1)) >> shift
        start_blk.append(off)
        n_blk.append(blocks)
        pos_row = pos_row + jnp.where(in_g, (off << shift).astype(F32), 0.0)
        off = off + blocks

    total_blk = off
    min_q = tm // qr

    def sort_chunk(q):
        sub = (q * qr + lax.broadcasted_iota(jnp.int32, (qr, tm), 0)).astype(F32)
        sel = jnp.where(pos_row == sub, 1.0, 0.0).astype(BF16)
        sel_s[q * qr:(q + 1) * qr, :] = sel
        hs[q * qr:(q + 1) * qr, :] = jnp.dot(sel, hb, preferred_element_type=F32).astype(BF16)
        cs[q * qr:(q + 1) * qr, :] = lax.dot_general(sel, cwt, nt, preferred_element_type=F32)

    for q in range(cap // qr):
        if q < min_q:
            sort_chunk(q)
        else:
            pl.when(total_blk > q * (qr // bm))(functools.partial(sort_chunk, q))
    ys[min_q * qr:, :] = jnp.zeros((cap - min_q * qr, D_MODEL), BF16)

    for g in range(N_EXPERT_GROUPS):
        wd = wd_ref[g * EXPERTS_PER_GROUP:(g + 1) * EXPERTS_PER_GROUP].reshape(
            EXPERTS_PER_GROUP * D_EXPERT, D_MODEL)

        def block(b, carry, g=g, wd=wd):
            rows = pl.ds(pl.multiple_of(b * bm, bm), bm)
            hblk = hs[rows, :]
            cblk = cs[rows, :]
            ln = lax.broadcasted_iota(jnp.int32, cblk.shape, 1)
            parts = []
            for e in range(EXPERTS_PER_GROUP):
                ce = jnp.sum(jnp.where((ln == e) | (ln == e + EXPERTS_PER_GROUP), cblk, 0.0),
                             axis=-1, keepdims=True)
                gt = jnp.dot(hblk, wg_ref[g * EXPERTS_PER_GROUP + e], preferred_element_type=F32)
                up = jnp.dot(hblk, wu_ref[g * EXPERTS_PER_GROUP + e], preferred_element_type=F32)
                a = gt / (1.0 + jnp.exp(-gt)) * up * ce
                parts.append(a.astype(BF16))
            a_all = jnp.concatenate(parts, axis=-1)
            ys[rows, :] = jnp.dot(a_all, wd, preferred_element_type=F32).astype(BF16)
            return carry

        lax.fori_loop(start_blk[g], start_blk[g] + n_blk[g], block, 0)

    def unsort_chunk(q):
        out_ref[...] += lax.dot_general(sel_s[q * qr:(q + 1) * qr, :], ys[q * qr:(q + 1) * qr, :],
                                        (((0,), (0,)), ((), ())), preferred_element_type=F32)

    for q in range(cap // qr):
        if q < min_q:
            unsort_chunk(q)
        else:
            pl.when(total_blk > q * (qr // bm))(functools.partial(unsort_chunk, q))
    if has_final:
        out_ref[...] = _rms(out_ref[...], fn_ref[...])


def _moe(x, nf, wr, br, wg, wu, wd, oproj=None, final=None, tm=512, bm=128):
    N, D = x.shape
    has_oproj = oproj is not None
    has_final = final is not None
    const2 = lambda i: (0, 0)
    const3 = lambda i: (0, 0, 0)
    tile = lambda i: (i, 0)
    once = pl.Buffered(1)
    args = [x]
    in_specs = [pl.BlockSpec((tm, D), tile)]
    if has_oproj:
        o, wo, bo = oproj
        args += [o, wo, bo]
        in_specs += [pl.BlockSpec((tm, D), tile), pl.BlockSpec((D, D), const2, pipeline_mode=once),
                     pl.BlockSpec((1, D), const2)]
    wr_hi = wr.astype(BF16)
    wr_lo = (wr - wr_hi.astype(F32)).astype(BF16)
    args += [nf, jnp.concatenate([wr_hi, wr_lo], axis=0), br, wg, wu, wd]
    in_specs += [
        pl.BlockSpec((1, D), const2),
        pl.BlockSpec((2 * ROUTER_ROWS, D), const2, pipeline_mode=once),
        pl.BlockSpec((ROUTER_ROWS, 1), const2),
        pl.BlockSpec((N_EXPERTS, D, D_EXPERT), const3, pipeline_mode=once),
        pl.BlockSpec((N_EXPERTS, D, D_EXPERT), const3, pipeline_mode=once),
        pl.BlockSpec((N_EXPERTS, D_EXPERT, D), const3, pipeline_mode=once),
    ]
    if has_final:
        args.append(final)
        in_specs.append(pl.BlockSpec((1, D), const2))
    cap = (tm // bm + N_EXPERT_GROUPS) * bm
    return pl.pallas_call(
        functools.partial(_moe_kernel, has_oproj=has_oproj, has_final=has_final, tm=tm, bm=bm),
        grid=(N // tm,),
        in_specs=in_specs,
        out_specs=pl.BlockSpec((tm, D), tile),
        out_shape=jax.ShapeDtypeStruct((N, D), F32),
        scratch_shapes=[
            pltpu.VMEM((cap, D), BF16),
            pltpu.VMEM((cap, LANES), F32),
            pltpu.VMEM((cap, D), BF16),
            pltpu.VMEM((cap, tm), BF16),
            pltpu.VMEM((tm, tm), BF16),
        ],
        compiler_params=pltpu.CompilerParams(
            dimension_semantics=("arbitrary",), vmem_limit_bytes=VMEM_LIMIT),
        name="moe",
    )(*args)


def _swap_halves(x, low):
    return jnp.where(low, pltpu.roll(x, LANES - 32, axis=1), pltpu.roll(x, 32, axis=1))


def _qkv_kernel(x_ref, g_ref, w_ref, b_ref, cos_ref, sin_ref, q_ref, kd_ref, vd_ref, k_ref, v_ref):
    h = _rms(x_ref[...], g_ref[...]).astype(BF16)
    qkv = jnp.dot(h, w_ref[...], preferred_element_type=F32) + b_ref[...]
    cos = cos_ref[...]
    sin = sin_ref[...]
    lane = lax.broadcasted_iota(jnp.int32, cos.shape, 1)
    low32 = (lane % HEAD_DIM) < (HEAD_DIM // 2)
    low64 = lane < HEAD_DIM
    nq = N_HEADS * HEAD_DIM
    nk = N_KV_HEADS * HEAD_DIM

    def rope(c):
        return c * cos + _swap_halves(c, low32) * sin

    def dup(c):
        r = pltpu.roll(c, HEAD_DIM, axis=1)
        return jnp.where(low64, c, r), jnp.where(low64, r, c)

    for j in range(nq // LANES):
        c = rope(qkv[:, j * LANES:(j + 1) * LANES]) * Q_SCALE
        q_ref[:, j * LANES:(j + 1) * LANES] = c.astype(BF16)
    for j in range(nk // LANES):
        c = rope(qkv[:, nq + j * LANES:nq + (j + 1) * LANES])
        k_ref[:, j * LANES:(j + 1) * LANES] = c
        d0, d1 = dup(c)
        kd_ref[:, 2 * j * LANES:(2 * j + 1) * LANES] = d0.astype(BF16)
        kd_ref[:, (2 * j + 1) * LANES:(2 * j + 2) * LANES] = d1.astype(BF16)
        c = qkv[:, nq + nk + j * LANES:nq + nk + (j + 1) * LANES]
        v_ref[:, j * LANES:(j + 1) * LANES] = c
        d0, d1 = dup(c)
        vd_ref[:, 2 * j * LANES:(2 * j + 1) * LANES] = d0.astype(BF16)
        vd_ref[:, (2 * j + 1) * LANES:(2 * j + 2) * LANES] = d1.astype(BF16)


def _qkv(x, g, w, b, cos, sin, tm, n_pos_tiles):
    N, D = x.shape
    nq, nk = N_HEADS * HEAD_DIM, N_KV_HEADS * HEAD_DIM
    const = lambda i: (0, 0)
    tile = lambda i: (i, 0)
    ptile = lambda i: (i % n_pos_tiles, 0)
    return pl.pallas_call(
        _qkv_kernel,
        grid=(N // tm,),
        in_specs=[
            pl.BlockSpec((tm, D), tile),
            pl.BlockSpec((1, D), const),
            pl.BlockSpec((D, nq + 2 * nk), const),
            pl.BlockSpec((1, nq + 2 * nk), const),
            pl.BlockSpec((tm, LANES), ptile),
            pl.BlockSpec((tm, LANES), ptile),
        ],
        out_specs=[
            pl.BlockSpec((tm, nq), tile),
            pl.BlockSpec((tm, 2 * nk), tile),
            pl.BlockSpec((tm, 2 * nk), tile),
            pl.BlockSpec((tm, nk), tile),
            pl.BlockSpec((tm, nk), tile),
        ],
        out_shape=[
            jax.ShapeDtypeStruct((N, nq), BF16),
            jax.ShapeDtypeStruct((N, 2 * nk), BF16),
            jax.ShapeDtypeStruct((N, 2 * nk), BF16),
            jax.ShapeDtypeStruct((N, nk), F32),
            jax.ShapeDtypeStruct((N, nk), F32),
        ],
        compiler_params=pltpu.CompilerParams(
            dimension_semantics=("arbitrary",), vmem_limit_bytes=VMEM_LIMIT),
        name="qkv",
    )(x, g, w, b, cos, sin)


def _attend(qc_list, kd, vd, bias, sinkrow):
    R = qc_list[0].shape[0]
    lane = lax.broadcasted_iota(jnp.int32, qc_list[0].shape, 1)
    low = lane < HEAD_DIM
    zero = jnp.zeros_like(qc_list[0])
    stack = []
    for qc in qc_list:
        stack.append(jnp.where(low, qc, zero))
        stack.append(jnp.where(low, zero, qc))
    qs = jnp.concatenate(stack, axis=0)
    s = lax.dot_general(kd, qs, (((1,), (1,)), ((), ())), preferred_element_type=F32) + bias
    m = jnp.maximum(jnp.max(s, axis=0, keepdims=True), sinkrow)
    p = jnp.exp2(s - m)
    den = jnp.sum(p, axis=0, keepdims=True) + jnp.exp2(sinkrow - m)
    p = (p * (1.0 / den)).astype(BF16)
    o = lax.dot_general(p, vd, (((0,), (0,)), ((), ())), preferred_element_type=F32)
    lowf = lax.broadcasted_iota(jnp.int32, (R, LANES), 1) < HEAD_DIM
    return (jnp.where(lowf, o[0:R], o[R:2 * R]), jnp.where(lowf, o[2 * R:3 * R], o[3 * R:4 * R]))


def _attn_prompt_kernel(q_ref, kp_ref, kc_ref, vp_ref, vc_ref, sink_ref, o_ref, *, tq):
    n = pl.program_id(1)
    B = WINDOW
    kd = jnp.concatenate([kp_ref[...], kc_ref[...]], axis=0)
    vd = jnp.concatenate([vp_ref[...], vc_ref[...]], axis=0)
    kj = lax.broadcasted_iota(jnp.int32, (2 * B, B), 0)
    qi = lax.broadcasted_iota(jnp.int32, (2 * B, B), 1)
    rel = B + qi - kj
    band = (rel >= 0) & (rel < WINDOW)
    neg = jnp.float32(-jnp.inf)
    band_bias = jnp.concatenate([jnp.where(band, 0.0, neg)] * 4, axis=1)
    first_bias = jnp.concatenate([jnp.where(band & ((kj >= B) | (n > 0)), 0.0, neg)] * 4, axis=1)
    for j in range(tq // B):
        bias = first_bias if j == 0 else band_bias
        for g in range(N_KV_HEADS):
            qc = [q_ref[j * B:(j + 1) * B, (2 * g + i) * LANES:(2 * g + i + 1) * LANES] for i in range(2)]
            kg = kd[j * B:(j + 2) * B, g * LANES:(g + 1) * LANES]
            vg = vd[j * B:(j + 2) * B, g * LANES:(g + 1) * LANES]
            o0, o1 = _attend(qc, kg, vg, bias, sink_ref[g])
            o_ref[j * B:(j + 1) * B, 2 * g * LANES:(2 * g + 1) * LANES] = o0.astype(BF16)
            o_ref[j * B:(j + 1) * B, (2 * g + 1) * LANES:(2 * g + 2) * LANES] = o1.astype(BF16)


def _attn_prompt(q, kd, vd, sinkrow, batch, seq, tq=512):
    N = q.shape[0]
    nt = seq // tq
    r = tq // WINDOW
    cur = lambda b, n: (b * nt + n, 0)
    prev = lambda b, n: (jnp.maximum((b * nt + n) * r - 1, 0), 0)
    kvw = 2 * N_KV_HEADS * HEAD_DIM
    return pl.pallas_call(
        functools.partial(_attn_prompt_kernel, tq=tq),
        grid=(batch, nt),
        in_specs=[
            pl.BlockSpec((tq, N_HEADS * HEAD_DIM), cur),
            pl.BlockSpec((WINDOW, kvw), prev),
            pl.BlockSpec((tq, kvw), cur),
            pl.BlockSpec((WINDOW, kvw), prev),
            pl.BlockSpec((tq, kvw), cur),
            pl.BlockSpec((N_KV_HEADS, 1, 4 * WINDOW), lambda b, n: (0, 0, 0)),
        ],
        out_specs=pl.BlockSpec((tq, N_HEADS * HEAD_DIM), cur),
        out_shape=jax.ShapeDtypeStruct((N, N_HEADS * HEAD_DIM), BF16),
        compiler_params=pltpu.CompilerParams(
            dimension_semantics=("arbitrary", "arbitrary"), vmem_limit_bytes=VMEM_LIMIT),
        name="attn_prompt",
    )(q, kd, kd, vd, vd, sinkrow)


def _dup_cache(c, g):
    low = lax.broadcasted_iota(jnp.int32, c.shape, 1) < HEAD_DIM
    r = pltpu.roll(c, HEAD_DIM, axis=1)
    d = jnp.where(low, c, r) if g % 2 == 0 else jnp.where(low, r, c)
    return d.astype(BF16)


def _attn_sample_kernel(q_ref, kdn_ref, vdn_ref, kn_ref, vn_ref, ck_ref, cv_ref, mask_ref, kval_ref, sink_ref,
                        o_ref, nk_ref, nv_ref, *, bs, t):
    W = WINDOW
    ck = ck_ref[...]
    cv = cv_ref[...]
    nk_ref[:, 0:W - t, :] = ck[:, t:, :]
    nk_ref[:, W - t:, :] = kn_ref[...].reshape(bs, t, N_KV_HEADS * HEAD_DIM)
    nv_ref[:, 0:W - t, :] = cv[:, t:, :]
    nv_ref[:, W - t:, :] = vn_ref[...].reshape(bs, t, N_KV_HEADS * HEAD_DIM)

    ck2 = ck.reshape(bs * W, N_KV_HEADS * HEAD_DIM)
    cv2 = cv.reshape(bs * W, N_KV_HEADS * HEAD_DIM)
    bias = jnp.where((mask_ref[...] > 0) & (kval_ref[0] > 0), 0.0, -jnp.inf).astype(F32)
    for g in range(N_KV_HEADS):
        m = g // 2
        kg = jnp.concatenate([_dup_cache(ck2[:, m * LANES:(m + 1) * LANES], g),
                              kdn_ref[:, g * LANES:(g + 1) * LANES]], axis=0)
        vg = jnp.concatenate([_dup_cache(cv2[:, m * LANES:(m + 1) * LANES], g),
                              vdn_ref[:, g * LANES:(g + 1) * LANES]], axis=0)
        qc = [q_ref[:, (2 * g + i) * LANES:(2 * g + i + 1) * LANES] for i in range(2)]
        o0, o1 = _attend(qc, kg, vg, bias, sink_ref[g])
        o_ref[:, 2 * g * LANES:(2 * g + 1) * LANES] = o0.astype(BF16)
        o_ref[:, (2 * g + 1) * LANES:(2 * g + 2) * LANES] = o1.astype(BF16)


def _attn_sample(q, kdn, vdn, kn, vn, ck, cv, mask, kval, sinkrow, bs, t):
    Bd = ck.shape[0]
    rows = bs * t
    kvw = N_KV_HEADS * HEAD_DIM
    S = bs * WINDOW + rows
    tile = lambda i: (i, 0)
    tile3 = lambda i: (i, 0, 0)
    return pl.pallas_call(
        functools.partial(_attn_sample_kernel, bs=bs, t=t),
        grid=(Bd // bs,),
        in_specs=[
            pl.BlockSpec((rows, N_HEADS * HEAD_DIM), tile),
            pl.BlockSpec((rows, 2 * kvw), tile),
            pl.BlockSpec((rows, 2 * kvw), tile),
            pl.BlockSpec((rows, kvw), tile),
            pl.BlockSpec((rows, kvw), tile),
            pl.BlockSpec((bs, WINDOW, kvw), tile3),
            pl.BlockSpec((bs, WINDOW, kvw), tile3),
            pl.BlockSpec((S, 4 * rows), lambda i: (0, 0)),
            pl.BlockSpec((1, S, 1), tile3),
            pl.BlockSpec((N_KV_HEADS, 1, 4 * rows), lambda i: (0, 0, 0)),
        ],
        out_specs=[
            pl.BlockSpec((rows, N_HEADS * HEAD_DIM), tile),
            pl.BlockSpec((bs, WINDOW, kvw), tile3),
            pl.BlockSpec((bs, WINDOW, kvw), tile3),
        ],
        out_shape=[
            jax.ShapeDtypeStruct((Bd * t, N_HEADS * HEAD_DIM), BF16),
            jax.ShapeDtypeStruct((Bd, WINDOW, kvw), F32),
            jax.ShapeDtypeStruct((Bd, WINDOW, kvw), F32),
        ],
        compiler_params=pltpu.CompilerParams(
            dimension_semantics=("arbitrary",), vmem_limit_bytes=VMEM_LIMIT),
        name="attn_sample",
    )(q, kdn, vdn, kn, vn, ck, cv, mask, kval, sinkrow)


def _rope_tables(pos):
    inv = ROPE_THETA ** (-jnp.arange(0, HEAD_DIM, 2, dtype=F32) / HEAD_DIM)
    inv_l = jnp.tile(inv, LANES // (HEAD_DIM // 2))
    sign = jnp.asarray(np.where((np.arange(LANES) % HEAD_DIM) < HEAD_DIM // 2, -1.0, 1.0), F32)
    ang = pos.astype(F32)[:, None] * inv_l[None, :]
    return jnp.cos(ang), jnp.sin(ang) * sign[None, :]


def _sink_rows(sinks, rows):
    s = (sinks.astype(F32) * LOG2E).reshape(N_KV_HEADS, 1, N_HEADS // N_KV_HEADS, 1)
    return jnp.broadcast_to(s, (N_KV_HEADS, 1, 4, rows)).reshape(N_KV_HEADS, 1, 4 * rows)


def kernel(x_prompt, x_sample, state_pool, cache_k, cache_v, sample_start, norm_mix, norm_ffn, norm_final,
           w_pool, ls_pool, w_qkv, b_qkv, sinks, w_o, b_o, w_rg, b_rg, w_re, b_re, w_gate, w_up, w_down):
    B, T, D = x_prompt.shape
    Bd, Td, _ = x_sample.shape
    kvw = N_KV_HEADS * HEAD_DIM
    row = lambda v: v.reshape(1, -1).astype(F32)

    wp = w_pool[0].astype(BF16)
    wqkv = w_qkv[0].astype(BF16)
    wo = w_o[0].astype(BF16)
    wg, wu, wd = w_gate.astype(BF16), w_up.astype(BF16), w_down.astype(BF16)
    def router_rows(g_part, e_part):
        z = lambda n: jnp.zeros((n,) + g_part.shape[1:], F32)
        return jnp.concatenate([g_part, z(EXPERT_ROW0 - N_EXPERT_GROUPS), e_part,
                                z(ROUTER_ROWS - EXPERT_ROW0 - N_EXPERTS)], axis=0)

    wr = [router_rows(w_rg[l].T, w_re[l].T) for l in range(2)]
    br = [router_rows(b_rg[l][:, None], b_re[l][:, None]) for l in range(2)]

    start = sample_start.astype(jnp.int32)

    x1p, pool_p16 = _pool_prompt(x_prompt, jnp.zeros((B, HALO, D), F32), row(norm_mix[0]), wp, row(ls_pool[0]))
    pos_s = (start[:, None] + jnp.arange(Td, dtype=jnp.int32)[None, :]).reshape(-1)
    st16 = jnp.pad(state_pool[0], ((0, 0), (HALO - POOL_BUF, 0), (0, 0)))
    x1s, pool_s16 = _pool_sample(x_sample, st16, pos_s.reshape(-1, 1), row(norm_mix[0]), wp, row(ls_pool[0]))
    x1s = x1s.reshape(Bd * Td, D)
    pool_s = pool_s16[:, HALO - POOL_BUF:]
    pool_p = pool_p16[:, HALO - POOL_BUF:]

    moe0 = functools.partial(_moe, nf=row(norm_ffn[0]), wr=wr[0], br=br[0], wg=wg[0], wu=wu[0], wd=wd[0])
    x2p = moe0(x1p.reshape(B * T, D))
    x2s = moe0(x1s)

    cos_p, sin_p = _rope_tables(jnp.arange(T, dtype=jnp.int32))
    cos_s, sin_s = _rope_tables(pos_s)
    g1 = row(norm_mix[1])
    tmq = 1024
    qp, kdp, vdp, kp, vp = _qkv(x2p, g1, wqkv, row(b_qkv[0]), cos_p, sin_p, tmq, T // tmq)
    qs, kds, vds, ks, vs = _qkv(x2s, g1, wqkv, row(b_qkv[0]), cos_s, sin_s, tmq, (Bd * Td) // tmq)

    op = _attn_prompt(qp, kdp, vdp, _sink_rows(sinks[0], WINDOW), B, T)

    bs = 8
    rows = bs * Td
    W = cache_k.shape[2]
    qrow = np.arange(rows)
    ccol = np.arange(bs * W)
    ncol = np.arange(rows)
    samp_q, t_q = qrow // Td, qrow % Td
    m_cache = (samp_q[:, None] == (ccol // W)[None, :]) & ((ccol % W)[None, :] > t_q[:, None])
    m_new = (samp_q[:, None] == (ncol // Td)[None, :]) & ((ncol % Td)[None, :] <= t_q[:, None])
    amask = jnp.asarray(np.tile(np.concatenate([m_cache, m_new], axis=1).T, (1, 4)), F32)
    kv_cache = (jnp.arange(W, dtype=jnp.int32)[None, :] >= (W - start)[:, None]).reshape(Bd // bs, bs * W)
    kval = jnp.concatenate([kv_cache, jnp.ones((Bd // bs, rows), bool)], axis=1).astype(F32)
    kval = kval.reshape(Bd // bs, bs * W + rows, 1)
    osamp, nk_s, nv_s = _attn_sample(
        qs, kds, vds, ks, vs, cache_k[0].reshape(Bd, W, kvw), cache_v[0].reshape(Bd, W, kvw),
        amask, kval, _sink_rows(sinks[0], rows), bs, Td)

    moe1 = functools.partial(_moe, nf=row(norm_ffn[1]), wr=wr[1], br=br[1], wg=wg[1], wu=wu[1], wd=wd[1],
                             final=row(norm_final))
    yp = moe1(x2p, oproj=(op, wo, row(b_o[0])))
    ys = moe1(x2s, oproj=(osamp, wo, row(b_o[0])))

    keep = min(WINDOW, T)
    k_p = kp.reshape(B, T, N_KV_HEADS, HEAD_DIM)[:, T - keep:][None]
    v_p = vp.reshape(B, T, N_KV_HEADS, HEAD_DIM)[:, T - keep:][None]
    return (yp.reshape(B, T, D), ys.reshape(Bd, Td, D), pool_p[None], k_p, v_p, pool_s[None],
            nk_s.reshape(1, Bd, W, N_KV_HEADS, HEAD_DIM), nv_s.reshape(1, Bd, W, N_KV_HEADS, HEAD_DIM))
```

```python
import functools

import jax
import jax.numpy as jnp
import numpy as np
from jax import lax
from jax.experimental import pallas as pl
from jax.experimental.pallas import tpu as pltpu

F32 = jnp.float32
BF16 = jnp.bfloat16

D_MODEL = 1024
POOL_WINDOWS = (2, 4, 8, 16)
POOL_GROUP_DIM = 256
POOL_BUF = 15
HALO = 16
HEAD_DIM = 64
N_HEADS = 16
N_KV_HEADS = 4
WINDOW = 128
ROPE_THETA = 10000.0
N_EXPERT_GROUPS = 4
EXPERTS_PER_GROUP = 4
N_EXPERTS = 16
D_EXPERT = 256
RMS_EPS = 1e-6
LANES = 128
ROUTER_ROWS = 32
EXPERT_ROW0 = 8
VMEM_LIMIT = 56 * 1024 * 1024
LOG2E = 1.4426950408889634
Q_SCALE = LOG2E * HEAD_DIM ** -0.5


def _rms(x, g):
    ms = jnp.mean(x * x, axis=-1, keepdims=True)
    return x * lax.rsqrt(ms + RMS_EPS) * g


def _pool_windows(hc, s2, s4, s8, rows):
    G = POOL_GROUP_DIM
    a = hc[pl.ds(8, rows), :]
    v2 = a + hc[pl.ds(7, rows), :]
    s2[pl.ds(8, rows), :] = v2
    v4 = v2[:, G:] + s2[pl.ds(6, rows), G:]
    s4[pl.ds(8, rows), G:] = v4
    v8 = v4[:, G:] + s4[pl.ds(4, rows), 2 * G:]
    s8[pl.ds(8, rows), 2 * G:] = v8
    v16 = v8[:, G:] + s8[pl.ds(0, rows), 3 * G:]
    return a, (v2[:, :G], v4[:, :G], v8[:, :G], v16)


def _pool_project(h, wins, pos, wp_ref, ls):
    G = POOL_GROUP_DIM
    outs = []
    for g, w in enumerate(POOL_WINDOWS):
        cnt = jnp.minimum(w, pos + 1).astype(F32)
        d = wins[g] / cnt - h[:, g * G:(g + 1) * G]
        outs.append(jnp.dot(d.astype(BF16), wp_ref[g], preferred_element_type=F32))
    return jnp.concatenate(outs, axis=-1) * ls


def _zero_pads(*refs):
    for r in refs:
        r[pl.ds(0, 8), :] = jnp.zeros((8, D_MODEL), F32)


def _pool_prompt_kernel(x_ref, buf_ref, g_ref, wp_ref, ls_ref, o_ref, nb_ref, hc, s2, s4, s8, *, tq):
    t = pl.program_id(1)

    @pl.when(t == 0)
    def _():
        _zero_pads(hc, s2, s4, s8)
        hc[pl.ds(8, HALO), :] = buf_ref[0]

    @pl.when(t > 0)
    def _():
        hc[pl.ds(8, HALO), :] = hc[pl.ds(8 + tq, HALO), :]

    x = x_ref[0]
    hc[pl.ds(8 + HALO, tq), :] = _rms(x, g_ref[...])
    a, wins = _pool_windows(hc, s2, s4, s8, HALO + tq)
    h = a[HALO:]
    wins = tuple(w[HALO:] for w in wins)
    pos = t * tq + lax.broadcasted_iota(jnp.int32, (tq, 1), 0)
    o_ref[0] = x + _pool_project(h, wins, pos, wp_ref, ls_ref[...])

    @pl.when(t == pl.num_programs(1) - 1)
    def _():
        nb_ref[0] = hc[pl.ds(8 + tq, HALO), :]


def _pool_prompt(x, buf16, g, wp, ls, tq=512):
    B, T, D = x.shape
    return pl.pallas_call(
        functools.partial(_pool_prompt_kernel, tq=tq),
        grid=(B, T // tq),
        in_specs=[
            pl.BlockSpec((1, tq, D), lambda b, t: (b, t, 0)),
            pl.BlockSpec((1, HALO, D), lambda b, t: (b, 0, 0)),
            pl.BlockSpec((1, D), lambda b, t: (0, 0)),
            pl.BlockSpec((4, POOL_GROUP_DIM, POOL_GROUP_DIM), lambda b, t: (0, 0, 0)),
            pl.BlockSpec((1, D), lambda b, t: (0, 0)),
        ],
        out_specs=[
            pl.BlockSpec((1, tq, D), lambda b, t: (b, t, 0)),
            pl.BlockSpec((1, HALO, D), lambda b, t: (b, 0, 0)),
        ],
        out_shape=[jax.ShapeDtypeStruct((B, T, D), F32), jax.ShapeDtypeStruct((B, HALO, D), F32)],
        scratch_shapes=[pltpu.VMEM((8 + HALO + tq, D), F32)] * 4,
        compiler_params=pltpu.CompilerParams(
            dimension_semantics=("arbitrary", "arbitrary"), vmem_limit_bytes=VMEM_LIMIT),
        name="pool_prompt",
    )(x, buf16, g, wp, ls)


def _pool_sample_kernel(x_ref, st_ref, start_ref, g_ref, wp_ref, ls_ref, o_ref, ns_ref, *, bs, t):
    G = POOL_GROUP_DIM
    xs = [x_ref[:, i, :] for i in range(t)]
    hn = [_rms(x, g_ref[...]) for x in xs]
    hist = [st_ref[r] for r in range(POOL_BUF)] + hn

    def doubled(prev, lag, lo):
        out = [None] * len(prev)
        for i in range(len(prev)):
            if i >= lag and prev[i] is not None and prev[i - lag] is not None:
                out[i] = prev[i][:, lo:] + prev[i - lag][:, lo:]
        return out

    s2 = doubled(hist, 1, 0)
    s4 = doubled(s2, 2, G)
    s8 = doubled(s4, 4, G)
    s16 = doubled(s8, 8, G)
    wins = (s2, s4, s8, s16)
    start = start_ref[...]
    ds = [[] for _ in POOL_WINDOWS]
    for i in range(t):
        for g, w in enumerate(POOL_WINDOWS):
            cnt = jnp.minimum(w, start + (i + 1)).astype(F32)
            ds[g].append(wins[g][POOL_BUF + i][:, :G] / cnt - hn[i][:, g * G:(g + 1) * G])
    ys = [jnp.dot(jnp.concatenate(ds[g], axis=0).astype(BF16), wp_ref[g], preferred_element_type=F32)
          for g in range(len(POOL_WINDOWS))]
    y = jnp.concatenate(ys, axis=-1) * ls_ref[...]
    for i in range(t):
        o_ref[:, i, :] = xs[i] + y[i * bs:(i + 1) * bs]
    for r in range(POOL_BUF):
        ns_ref[r] = hist[r + t]


def _pool_sample(x, st, start, g, wp, ls, bs=32):
    Bd, t, D = x.shape
    return pl.pallas_call(
        functools.partial(_pool_sample_kernel, bs=bs, t=t),
        grid=(Bd // bs,),
        in_specs=[
            pl.BlockSpec((bs, t, D), lambda i: (i, 0, 0)),
            pl.BlockSpec((POOL_BUF, bs, D), lambda i: (0, i, 0)),
            pl.BlockSpec((bs, 1), lambda i: (i, 0)),
            pl.BlockSpec((1, D), lambda i: (0, 0)),
            pl.BlockSpec((4, POOL_GROUP_DIM, POOL_GROUP_DIM), lambda i: (0, 0, 0)),
            pl.BlockSpec((1, D), lambda i: (0, 0)),
        ],
        out_specs=[pl.BlockSpec((bs, t, D), lambda i: (i, 0, 0)),
                   pl.BlockSpec((POOL_BUF, bs, D), lambda i: (0, i, 0))],
        out_shape=[jax.ShapeDtypeStruct((Bd, t, D), F32), jax.ShapeDtypeStruct((POOL_BUF, Bd, D), F32)],
        compiler_params=pltpu.CompilerParams(
            dimension_semantics=("arbitrary",), vmem_limit_bytes=VMEM_LIMIT),
        name="pool_sample",
    )(x, st, start, g, wp, ls)


def _route(lg):
    R = lg.shape[1]
    big = jnp.float32(1 << 20)
    neg = jnp.float32(-jnp.inf)
    r8 = lax.broadcasted_iota(jnp.int32, (8, R), 0).astype(F32)
    r16 = lax.broadcasted_iota(jnp.int32, (N_EXPERTS, R), 0).astype(F32)
    is_g = r8 < N_EXPERT_GROUPS
    gl = jnp.where(is_g, lg[0:8], neg)
    m = jnp.max(gl, axis=0, keepdims=True)
    gidx = jnp.min(jnp.where(gl == m, r8, big), axis=0, keepdims=True)
    z = jnp.sum(jnp.where(is_g, jnp.exp(gl - m), 0.0), axis=0, keepdims=True)
    gw = 1.0 / z
    lo = gidx * EXPERTS_PER_GROUP
    in_grp = (r16 >= lo) & (r16 < lo + EXPERTS_PER_GROUP)
    el = jnp.where(in_grp, lg[EXPERT_ROW0:EXPERT_ROW0 + N_EXPERTS], neg)
    v1 = jnp.max(el, axis=0, keepdims=True)
    i1 = jnp.min(jnp.where(el == v1, r16, big), axis=0, keepdims=True)
    el2 = jnp.where(r16 == i1, neg, el)
    v2 = jnp.max(el2, axis=0, keepdims=True)
    i2 = jnp.min(jnp.where(el2 == v2, r16, big), axis=0, keepdims=True)
    t = jnp.exp(v2 - v1)
    w1 = 1.0 / (1.0 + t)
    w2 = t * w1
    e8 = jnp.where(is_g, r8, r8 - EXPERTS_PER_GROUP)
    cw = gw * (jnp.where(e8 == i1 - lo, w1, 0.0) + jnp.where(e8 == i2 - lo, w2, 0.0))
    return gidx, cw


def _moe_kernel(*refs, has_oproj, has_final, tm, bm):
    it = iter(refs)
    x_ref = next(it)
    if has_oproj:
        oin_ref, wo_ref, bo_ref = next(it), next(it), next(it)
    nf_ref, wr_ref, br_ref, wg_ref, wu_ref, wd_ref = (next(it) for _ in range(6))
    if has_final:
        fn_ref = next(it)
    out_ref = next(it)
    hs, cs, ys, sel_s, earlier = (next(it) for _ in range(5))
    nblk = tm // bm + N_EXPERT_GROUPS
    cap = nblk * bm
    shift = bm.bit_length() - 1
    qr = 2 * bm
    assert cap % qr == 0 and tm % qr == 0
    nt = (((1,), (1,)), ((), ()))

    @pl.when(pl.program_id(0) == 0)
    def _():
        ri = lax.broadcasted_iota(jnp.int32, (tm, tm), 0)
        ci = lax.broadcasted_iota(jnp.int32, (tm, tm), 1)
        earlier[...] = jnp.where(ri < ci, 1.0, 0.0).astype(BF16)

    x = x_ref[...]
    if has_oproj:
        x = x + jnp.dot(oin_ref[...], wo_ref[...], preferred_element_type=F32) + bo_ref[...]
    out_ref[...] = x
    h = _rms(x, nf_ref[...])
    hb = h.astype(BF16)
    h_lo = (h - hb.astype(F32)).astype(BF16)
    wr = wr_ref[...]
    both = lax.dot_general(wr, hb, nt, preferred_element_type=F32)
    lg = (both[0:ROUTER_ROWS] + both[ROUTER_ROWS:]
          + lax.dot_general(wr[0:ROUTER_ROWS], h_lo, nt, preferred_element_type=F32)) + br_ref[...]
    gidx, cw = _route(lg)
    r8 = lax.broadcasted_iota(jnp.int32, (8, tm), 0)
    cw_hi = cw.astype(BF16).astype(F32)
    cwt = jnp.concatenate([jnp.where(r8 < EXPERTS_PER_GROUP, cw_hi, cw - cw_hi),
                           jnp.zeros((LANES - 8, tm), F32)], axis=0).astype(BF16)

    oh = jnp.where(r8.astype(F32) == gidx, 1.0, 0.0)
    cnt_before = jnp.dot(oh.astype(BF16), earlier[...], preferred_element_type=F32)
    pos_row = jnp.sum(oh * cnt_before, axis=0, keepdims=True)
    start_blk, n_blk = [], []
    off = jnp.int32(0)
    for g in range(N_EXPERT_GROUPS):
        in_g = gidx == g
        n_g = jnp.sum(jnp.where(in_g, 1.0, 0.0)).astype(jnp.int32)
        blocks = (n_g + (bm - 1)) >> shift
        start_blk.append(off)
        n_blk.append(blocks)
        pos_row = pos_row + jnp.where(in_g, (off << shift).astype(F32), 0.0)
        off = off + blocks

    total_blk = off
    min_q = tm // qr

    def sort_chunk(q):
        sub = (q * qr + lax.broadcasted_iota(jnp.int32, (qr, tm), 0)).astype(F32)
        sel = jnp.where(pos_row == sub, 1.0, 0.0).astype(BF16)
        sel_s[q * qr:(q + 1) * qr, :] = sel
        hs[q * qr:(q + 1) * qr, :] = jnp.dot(sel, hb, preferred_element_type=F32).astype(BF16)
        cs[q * qr:(q + 1) * qr, :] = lax.dot_general(sel, cwt, nt, preferred_element_type=F32)

    for q in range(cap // qr):
        if q < min_q:
            sort_chunk(q)
        else:
            pl.when(total_blk > q * (qr // bm))(functools.partial(sort_chunk, q))
    ys[min_q * qr:, :] = jnp.zeros((cap - min_q * qr, D_MODEL), BF16)

    for g in range(N_EXPERT_GROUPS):
        wd = wd_ref[g * EXPERTS_PER_GROUP:(g + 1) * EXPERTS_PER_GROUP].reshape(
            EXPERTS_PER_GROUP * D_EXPERT, D_MODEL)

        def block(b, carry, g=g, wd=wd):
            rows = pl.ds(pl.multiple_of(b * bm, bm), bm)
            hblk = hs[rows, :]
            cblk = cs[rows, :]
            ln = lax.broadcasted_iota(jnp.int32, cblk.shape, 1)
            parts = []
            for e in range(EXPERTS_PER_GROUP):
                ce = jnp.sum(jnp.where((ln == e) | (ln == e + EXPERTS_PER_GROUP), cblk, 0.0),
                             axis=-1, keepdims=True)
                gt = jnp.dot(hblk, wg_ref[g * EXPERTS_PER_GROUP + e], preferred_element_type=F32)
                up = jnp.dot(hblk, wu_ref[g * EXPERTS_PER_GROUP + e], preferred_element_type=F32)
                a = gt / (1.0 + jnp.exp(-gt)) * up * ce
                parts.append(a.astype(BF16))
            a_all = jnp.concatenate(parts, axis=-1)
            ys[rows, :] = jnp.dot(a_all, wd, preferred_element_type=F32).astype(BF16)
            return carry

        lax.fori_loop(start_blk[g], start_blk[g] + n_blk[g], block, 0)

    def unsort_chunk(q):
        out_ref[...] += lax.dot_general(sel_s[q * qr:(q + 1) * qr, :], ys[q * qr:(q + 1) * qr, :],
                                        (((0,), (0,)), ((), ())), preferred_element_type=F32)

    for q in range(cap // qr):
        if q < min_q:
            unsort_chunk(q)
        else:
            pl.when(total_blk > q * (qr // bm))(functools.partial(unsort_chunk, q))
    if has_final:
        out_ref[...] = _rms(out_ref[...], fn_ref[...])


def _moe(x, nf, wr, br, wg, wu, wd, oproj=None, final=None, tm=512, bm=128):
    N, D = x.shape
    has_oproj = oproj is not None
    has_final = final is not None
    const2 = lambda i: (0, 0)
    const3 = lambda i: (0, 0, 0)
    tile = lambda i: (i, 0)
    once = pl.Buffered(1)
    args = [x]
    in_specs = [pl.BlockSpec((tm, D), tile)]
    if has_oproj:
        o, wo, bo = oproj
        args += [o, wo, bo]
        in_specs += [pl.BlockSpec((tm, D), tile), pl.BlockSpec((D, D), const2, pipeline_mode=once),
                     pl.BlockSpec((1, D), const2)]
    wr_hi = wr.astype(BF16)
    wr_lo = (wr - wr_hi.astype(F32)).astype(BF16)
    args += [nf, jnp.concatenate([wr_hi, wr_lo], axis=0), br, wg, wu, wd]
    in_specs += [
        pl.BlockSpec((1, D), const2),
        pl.BlockSpec((2 * ROUTER_ROWS, D), const2, pipeline_mode=once),
        pl.BlockSpec((ROUTER_ROWS, 1), const2),
        pl.BlockSpec((N_EXPERTS, D, D_EXPERT), const3, pipeline_mode=once),
        pl.BlockSpec((N_EXPERTS, D, D_EXPERT), const3, pipeline_mode=once),
        pl.BlockSpec((N_EXPERTS, D_EXPERT, D), const3, pipeline_mode=once),
    ]
    if has_final:
        args.append(final)
        in_specs.append(pl.BlockSpec((1, D), const2))
    cap = (tm // bm + N_EXPERT_GROUPS) * bm
    return pl.pallas_call(
        functools.partial(_moe_kernel, has_oproj=has_oproj, has_final=has_final, tm=tm, bm=bm),
        grid=(N // tm,),
        in_specs=in_specs,
        out_specs=pl.BlockSpec((tm, D), tile),
        out_shape=jax.ShapeDtypeStruct((N, D), F32),
        scratch_shapes=[
            pltpu.VMEM((cap, D), BF16),
            pltpu.VMEM((cap, LANES), F32),
            pltpu.VMEM((cap, D), BF16),
            pltpu.VMEM((cap, tm), BF16),
            pltpu.VMEM((tm, tm), BF16),
        ],
        compiler_params=pltpu.CompilerParams(
            dimension_semantics=("arbitrary",), vmem_limit_bytes=VMEM_LIMIT),
        name="moe",
    )(*args)


def _swap_halves(x, low):
    return jnp.where(low, pltpu.roll(x, LANES - 32, axis=1), pltpu.roll(x, 32, axis=1))


def _qkv_kernel(x_ref, g_ref, w_ref, b_ref, cos_ref, sin_ref, q_ref, kd_ref, vd_ref, k_ref, v_ref, *, keep, tiles):
    tm = x_ref.shape[0]
    emit = pl.program_id(0) % tiles == tiles - 1
    h = _rms(x_ref[...], g_ref[...]).astype(BF16)
    qkv = jnp.dot(h, w_ref[...], preferred_element_type=F32) + b_ref[...]
    cos = cos_ref[...]
    sin = sin_ref[...]
    lane = lax.broadcasted_iota(jnp.int32, cos.shape, 1)
    low32 = (lane % HEAD_DIM) < (HEAD_DIM // 2)
    low64 = lane < HEAD_DIM
    nq = N_HEADS * HEAD_DIM
    nk = N_KV_HEADS * HEAD_DIM

    def rope(c):
        return c * cos + _swap_halves(c, low32) * sin

    def dup(c):
        r = pltpu.roll(c, HEAD_DIM, axis=1)
        return jnp.where(low64, c, r), jnp.where(low64, r, c)

    for j in range(nq // LANES):
        c = rope(qkv[:, j * LANES:(j + 1) * LANES]) * Q_SCALE
        q_ref[:, j * LANES:(j + 1) * LANES] = c.astype(BF16)
    for j in range(nk // LANES):
        c = rope(qkv[:, nq + j * LANES:nq + (j + 1) * LANES])

        @pl.when(emit)
        def _(c=c, j=j):
            k_ref[:, j * LANES:(j + 1) * LANES] = c[tm - keep:]

        d0, d1 = dup(c)
        kd_ref[:, 2 * j * LANES:(2 * j + 1) * LANES] = d0.astype(BF16)
        kd_ref[:, (2 * j + 1) * LANES:(2 * j + 2) * LANES] = d1.astype(BF16)
        c = qkv[:, nq + nk + j * LANES:nq + nk + (j + 1) * LANES]

        @pl.when(emit)
        def _(c=c, j=j):
            v_ref[:, j * LANES:(j + 1) * LANES] = c[tm - keep:]

        d0, d1 = dup(c)
        vd_ref[:, 2 * j * LANES:(2 * j + 1) * LANES] = d0.astype(BF16)
        vd_ref[:, (2 * j + 1) * LANES:(2 * j + 2) * LANES] = d1.astype(BF16)


def _qkv(x, g, w, b, cos, sin, tm, n_pos_tiles, keep):
    N, D = x.shape
    nq, nk = N_HEADS * HEAD_DIM, N_KV_HEADS * HEAD_DIM
    const = lambda i: (0, 0)
    tile = lambda i: (i, 0)
    ptile = lambda i: (i % n_pos_tiles, 0)
    seq = lambda i: (i // n_pos_tiles, 0)
    n_seq = N // (tm * n_pos_tiles)
    return pl.pallas_call(
        functools.partial(_qkv_kernel, keep=keep, tiles=n_pos_tiles),
        grid=(N // tm,),
        in_specs=[
            pl.BlockSpec((tm, D), tile),
            pl.BlockSpec((1, D), const),
            pl.BlockSpec((D, nq + 2 * nk), const),
            pl.BlockSpec((1, nq + 2 * nk), const),
            pl.BlockSpec((tm, LANES), ptile),
            pl.BlockSpec((tm, LANES), ptile),
        ],
        out_specs=[
            pl.BlockSpec((tm, nq), tile),
            pl.BlockSpec((tm, 2 * nk), tile),
            pl.BlockSpec((tm, 2 * nk), tile),
            pl.BlockSpec((keep, nk), seq),
            pl.BlockSpec((keep, nk), seq),
        ],
        out_shape=[
            jax.ShapeDtypeStruct((N, nq), BF16),
            jax.ShapeDtypeStruct((N, 2 * nk), BF16),
            jax.ShapeDtypeStruct((N, 2 * nk), BF16),
            jax.ShapeDtypeStruct((n_seq * keep, nk), F32),
            jax.ShapeDtypeStruct((n_seq * keep, nk), F32),
        ],
        compiler_params=pltpu.CompilerParams(
            dimension_semantics=("arbitrary",), vmem_limit_bytes=VMEM_LIMIT),
        name="qkv",
    )(x, g, w, b, cos, sin)


def _stack_heads(qc_list):
    lane = lax.broadcasted_iota(jnp.int32, qc_list[0].shape, 1)
    low = lane < HEAD_DIM
    zero = jnp.zeros_like(qc_list[0])
    stack = []
    for qc in qc_list:
        stack.append(jnp.where(low, qc, zero))
        stack.append(jnp.where(low, zero, qc))
    return jnp.concatenate(stack, axis=0)


def _softmax_keys_on_rows(s, sinkrow):
    m = jnp.maximum(jnp.max(s, axis=0, keepdims=True), sinkrow)
    p = jnp.exp2(s - m)
    den = jnp.sum(p, axis=0, keepdims=True) + jnp.exp2(sinkrow - m)
    return (p * (1.0 / den)).astype(BF16)


def _unstack_heads(o, R):
    lowf = lax.broadcasted_iota(jnp.int32, (R, LANES), 1) < HEAD_DIM
    return (jnp.where(lowf, o[0:R], o[R:2 * R]), jnp.where(lowf, o[2 * R:3 * R], o[3 * R:4 * R]))


NT_DIMS = (((1,), (1,)), ((), ()))
TN_DIMS = (((0,), (0,)), ((), ()))
TT_DIMS = (((0,), (1,)), ((), ()))


def _attend(qc_list, kd, vd, bias, sinkrow):
    qs = _stack_heads(qc_list)
    s = lax.dot_general(kd, qs, NT_DIMS, preferred_element_type=F32) + bias
    p = _softmax_keys_on_rows(s, sinkrow)
    o = lax.dot_general(p, vd, TN_DIMS, preferred_element_type=F32)
    return _unstack_heads(o, qc_list[0].shape[0])


def _attn_prompt_kernel(q_ref, kp_ref, kc_ref, vp_ref, vc_ref, sink_ref, o_ref, *, tq):
    n = pl.program_id(1)
    B = WINDOW
    kd = jnp.concatenate([kp_ref[...], kc_ref[...]], axis=0)
    vd = jnp.concatenate([vp_ref[...], vc_ref[...]], axis=0)
    kj = lax.broadcasted_iota(jnp.int32, (2 * B, B), 0)
    qi = lax.broadcasted_iota(jnp.int32, (2 * B, B), 1)
    rel = B + qi - kj
    band = (rel >= 0) & (rel < WINDOW)
    neg = jnp.float32(-jnp.inf)
    band_bias = jnp.concatenate([jnp.where(band, 0.0, neg)] * 4, axis=1)
    first_bias = jnp.concatenate([jnp.where(band & ((kj >= B) | (n > 0)), 0.0, neg)] * 4, axis=1)
    for j in range(tq // B):
        bias = first_bias if j == 0 else band_bias
        for g in range(N_KV_HEADS):
            qc = [q_ref[j * B:(j + 1) * B, (2 * g + i) * LANES:(2 * g + i + 1) * LANES] for i in range(2)]
            kg = kd[j * B:(j + 2) * B, g * LANES:(g + 1) * LANES]
            vg = vd[j * B:(j + 2) * B, g * LANES:(g + 1) * LANES]
            o0, o1 = _attend(qc, kg, vg, bias, sink_ref[g])
            o_ref[j * B:(j + 1) * B, 2 * g * LANES:(2 * g + 1) * LANES] = o0.astype(BF16)
            o_ref[j * B:(j + 1) * B, (2 * g + 1) * LANES:(2 * g + 2) * LANES] = o1.astype(BF16)


def _attn_prompt(q, kd, vd, sinkrow, batch, seq, tq=512):
    N = q.shape[0]
    nt = seq // tq
    r = tq // WINDOW
    cur = lambda b, n: (b * nt + n, 0)
    prev = lambda b, n: (jnp.maximum((b * nt + n) * r - 1, 0), 0)
    kvw = 2 * N_KV_HEADS * HEAD_DIM
    return pl.pallas_call(
        functools.partial(_attn_prompt_kernel, tq=tq),
        grid=(batch, nt),
        in_specs=[
            pl.BlockSpec((tq, N_HEADS * HEAD_DIM), cur),
            pl.BlockSpec((WINDOW, kvw), prev),
            pl.BlockSpec((tq, kvw), cur),
            pl.BlockSpec((WINDOW, kvw), prev),
            pl.BlockSpec((tq, kvw), cur),
            pl.BlockSpec((N_KV_HEADS, 1, 4 * WINDOW), lambda b, n: (0, 0, 0)),
        ],
        out_specs=pl.BlockSpec((tq, N_HEADS * HEAD_DIM), cur),
        out_shape=jax.ShapeDtypeStruct((N, N_HEADS * HEAD_DIM), BF16),
        compiler_params=pltpu.CompilerParams(
            dimension_semantics=("arbitrary", "arbitrary"), vmem_limit_bytes=VMEM_LIMIT),
        name="attn_prompt",
    )(q, kd, kd, vd, vd, sinkrow)


def _attn_sample_kernel(q_ref, kdn_ref, vdn_ref, kn_ref, vn_ref, ck_ref, cv_ref, mask_ref, kval_ref, sink_ref,
                        o_ref, nk_ref, nv_ref, *, bs, t):
    W = WINDOW
    rows = bs * t
    bias = jnp.where((mask_ref[...] > 0) & (kval_ref[0] > 0), 0.0, -jnp.inf).astype(F32)
    lane = lax.broadcasted_iota(jnp.int32, (HEAD_DIM, W), 1)
    fresh = lane >= W - t

    def new_cols(n_ref):
        n = jnp.concatenate([n_ref[...], jnp.zeros((LANES - rows, N_KV_HEADS * HEAD_DIM), F32)], axis=0)
        nt_ = jnp.transpose(n)
        return [pltpu.roll(nt_, (W - t - s * t) % LANES, axis=1) for s in range(bs)]

    k_cols, v_cols = new_cols(kn_ref), new_cols(vn_ref)
    for g in range(N_KV_HEADS):
        hd = slice(g * HEAD_DIM, (g + 1) * HEAD_DIM)
        kt = jnp.concatenate([ck_ref[s, g] for s in range(bs)], axis=1).astype(BF16)
        vt = jnp.concatenate([cv_ref[s, g] for s in range(bs)], axis=1).astype(BF16)
        kt2 = jnp.concatenate([kt, kt], axis=0)
        vt2 = jnp.concatenate([vt, vt], axis=0)
        qs = _stack_heads([q_ref[:, (2 * g + i) * LANES:(2 * g + i + 1) * LANES] for i in range(2)])
        s_old = lax.dot_general(kt2, qs, TT_DIMS, preferred_element_type=F32)
        s_new = lax.dot_general(kdn_ref[:, g * LANES:(g + 1) * LANES], qs, NT_DIMS, preferred_element_type=F32)
        p = _softmax_keys_on_rows(jnp.concatenate([s_old, s_new], axis=0) + bias, sink_ref[g])
        o = (lax.dot_general(p[:bs * W], vt2, TT_DIMS, preferred_element_type=F32)
             + lax.dot_general(p[bs * W:], vdn_ref[:, g * LANES:(g + 1) * LANES], TN_DIMS,
                               preferred_element_type=F32))
        o0, o1 = _unstack_heads(o, rows)
        o_ref[:, 2 * g * LANES:(2 * g + 1) * LANES] = o0.astype(BF16)
        o_ref[:, (2 * g + 1) * LANES:(2 * g + 2) * LANES] = o1.astype(BF16)
        for s in range(bs):
            nk_ref[s, g] = jnp.where(fresh, k_cols[s][hd], pltpu.roll(ck_ref[s, g], W - t, axis=1))
            nv_ref[s, g] = jnp.where(fresh, v_cols[s][hd], pltpu.roll(cv_ref[s, g], W - t, axis=1))


def _attn_sample(q, kdn, vdn, kn, vn, ck, cv, mask, kval, sinkrow, bs, t):
    Bd = ck.shape[0]
    rows = bs * t
    kvw = N_KV_HEADS * HEAD_DIM
    S = bs * WINDOW + rows
    tile = lambda i: (i, 0)
    tile3 = lambda i: (i, 0, 0)
    tile4 = lambda i: (i, 0, 0, 0)
    cblk = (bs, N_KV_HEADS, HEAD_DIM, WINDOW)
    return pl.pallas_call(
        functools.partial(_attn_sample_kernel, bs=bs, t=t),
        grid=(Bd // bs,),
        in_specs=[
            pl.BlockSpec((rows, N_HEADS * HEAD_DIM), tile),
            pl.BlockSpec((rows, 2 * kvw), tile),
            pl.BlockSpec((rows, 2 * kvw), tile),
            pl.BlockSpec((rows, kvw), tile),
            pl.BlockSpec((rows, kvw), tile),
            pl.BlockSpec(cblk, tile4),
            pl.BlockSpec(cblk, tile4),
            pl.BlockSpec((S, 4 * rows), lambda i: (0, 0)),
            pl.BlockSpec((1, S, 1), tile3),
            pl.BlockSpec((N_KV_HEADS, 1, 4 * rows), lambda i: (0, 0, 0)),
        ],
        out_specs=[
            pl.BlockSpec((rows, N_HEADS * HEAD_DIM), tile),
            pl.BlockSpec(cblk, tile4),
            pl.BlockSpec(cblk, tile4),
        ],
        out_shape=[
            jax.ShapeDtypeStruct((Bd * t, N_HEADS * HEAD_DIM), BF16),
            jax.ShapeDtypeStruct((Bd,) + cblk[1:], F32),
            jax.ShapeDtypeStruct((Bd,) + cblk[1:], F32),
        ],
        compiler_params=pltpu.CompilerParams(
            dimension_semantics=("arbitrary",), vmem_limit_bytes=VMEM_LIMIT),
        name="attn_sample",
    )(q, kdn, vdn, kn, vn, ck, cv, mask, kval, sinkrow)


def _rope_tables(pos):
    inv = ROPE_THETA ** (-jnp.arange(0, HEAD_DIM, 2, dtype=F32) / HEAD_DIM)
    inv_l = jnp.tile(inv, LANES // (HEAD_DIM // 2))
    sign = jnp.asarray(np.where((np.arange(LANES) % HEAD_DIM) < HEAD_DIM // 2, -1.0, 1.0), F32)
    ang = pos.astype(F32)[:, None] * inv_l[None, :]
    return jnp.cos(ang), jnp.sin(ang) * sign[None, :]


def _sink_rows(sinks, rows):
    s = (sinks.astype(F32) * LOG2E).reshape(N_KV_HEADS, 1, N_HEADS // N_KV_HEADS, 1)
    return jnp.broadcast_to(s, (N_KV_HEADS, 1, 4, rows)).reshape(N_KV_HEADS, 1, 4 * rows)


def kernel(x_prompt, x_sample, state_pool, cache_k, cache_v, sample_start, norm_mix, norm_ffn, norm_final,
           w_pool, ls_pool, w_qkv, b_qkv, sinks, w_o, b_o, w_rg, b_rg, w_re, b_re, w_gate, w_up, w_down):
    B, T, D = x_prompt.shape
    Bd, Td, _ = x_sample.shape
    kvw = N_KV_HEADS * HEAD_DIM
    row = lambda v: v.reshape(1, -1).astype(F32)

    wp = w_pool[0].astype(BF16)
    wqkv = w_qkv[0].astype(BF16)
    wo = w_o[0].astype(BF16)
    wg, wu, wd = w_gate.astype(BF16), w_up.astype(BF16), w_down.astype(BF16)
    def router_rows(g_part, e_part):
        z = lambda n: jnp.zeros((n,) + g_part.shape[1:], F32)
        return jnp.concatenate([g_part, z(EXPERT_ROW0 - N_EXPERT_GROUPS), e_part,
                                z(ROUTER_ROWS - EXPERT_ROW0 - N_EXPERTS)], axis=0)

    wr = [router_rows(w_rg[l].T, w_re[l].T) for l in range(2)]
    br = [router_rows(b_rg[l][:, None], b_re[l][:, None]) for l in range(2)]

    start = sample_start.astype(jnp.int32)

    x1p, pool_p16 = _pool_prompt(x_prompt, jnp.zeros((B, HALO, D), F32), row(norm_mix[0]), wp, row(ls_pool[0]))
    pos_s = (start[:, None] + jnp.arange(Td, dtype=jnp.int32)[None, :]).reshape(-1)
    x1s, pool_s_t = _pool_sample(x_sample, jnp.transpose(state_pool[0], (1, 0, 2)), start[:, None],
                                 row(norm_mix[0]), wp, row(ls_pool[0]))
    x1s = x1s.reshape(Bd * Td, D)
    pool_s = jnp.transpose(pool_s_t, (1, 0, 2))
    pool_p = pool_p16[:, HALO - POOL_BUF:]

    moe0 = functools.partial(_moe, nf=row(norm_ffn[0]), wr=wr[0], br=br[0], wg=wg[0], wu=wu[0], wd=wd[0])
    x2p = moe0(x1p.reshape(B * T, D))
    x2s = moe0(x1s)

    cos_p, sin_p = _rope_tables(jnp.arange(T, dtype=jnp.int32))
    cos_s, sin_s = _rope_tables(pos_s)
    g1 = row(norm_mix[1])
    tmq = 1024
    keep = min(WINDOW, T)
    qp, kdp, vdp, kp, vp = _qkv(x2p, g1, wqkv, row(b_qkv[0]), cos_p, sin_p, tmq, T // tmq, keep)
    qs, kds, vds, ks, vs = _qkv(x2s, g1, wqkv, row(b_qkv[0]), cos_s, sin_s, tmq, 1, tmq)

    op = _attn_prompt(qp, kdp, vdp, _sink_rows(sinks[0], WINDOW), B, T)

    bs = 8
    rows = bs * Td
    W = cache_k.shape[2]
    qrow = np.arange(rows)
    ccol = np.arange(bs * W)
    ncol = np.arange(rows)
    samp_q, t_q = qrow // Td, qrow % Td
    m_cache = (samp_q[:, None] == (ccol // W)[None, :]) & ((ccol % W)[None, :] > t_q[:, None])
    m_new = (samp_q[:, None] == (ncol // Td)[None, :]) & ((ncol % Td)[None, :] <= t_q[:, None])
    amask = jnp.asarray(np.tile(np.concatenate([m_cache, m_new], axis=1).T, (1, 4)), F32)
    kv_cache = (jnp.arange(W, dtype=jnp.int32)[None, :] >= (W - start)[:, None]).reshape(Bd // bs, bs * W)
    kval = jnp.concatenate([kv_cache, jnp.ones((Bd // bs, rows), bool)], axis=1).astype(F32)
    kval = kval.reshape(Bd // bs, bs * W + rows, 1)
    to_stored = lambda c: jnp.transpose(c[0], (0, 2, 3, 1))
    from_stored = lambda c: jnp.transpose(c, (0, 3, 1, 2))[None]
    osamp, nk_s, nv_s = _attn_sample(
        qs, kds, vds, ks, vs, to_stored(cache_k), to_stored(cache_v),
        amask, kval, _sink_rows(sinks[0], rows), bs, Td)

    moe1 = functools.partial(_moe, nf=row(norm_ffn[1]), wr=wr[1], br=br[1], wg=wg[1], wu=wu[1], wd=wd[1],
                             final=row(norm_final))
    yp = moe1(x2p, oproj=(op, wo, row(b_o[0])))
    ys = moe1(x2s, oproj=(osamp, wo, row(b_o[0])))

    k_p = kp.reshape(1, B, keep, N_KV_HEADS, HEAD_DIM)
    v_p = vp.reshape(1, B, keep, N_KV_HEADS, HEAD_DIM)
    return (yp.reshape(B, T, D), ys.reshape(Bd, Td, D), pool_p[None], k_p, v_p, pool_s[None],
            from_stored(nk_s), from_stored(nv_s))
```

```python
import functools

import jax
import jax.numpy as jnp
import numpy as np
from jax import lax
from jax.experimental import pallas as pl
from jax.experimental.pallas import tpu as pltpu

F32 = jnp.float32
BF16 = jnp.bfloat16

D_MODEL = 1024
POOL_WINDOWS = (2, 4, 8, 16)
POOL_GROUP_DIM = 256
POOL_BUF = 15
HALO = 16
HEAD_DIM = 64
N_HEADS = 16
N_KV_HEADS = 4
WINDOW = 128
ROPE_THETA = 10000.0
N_EXPERT_GROUPS = 4
EXPERTS_PER_GROUP = 4
N_EXPERTS = 16
D_EXPERT = 256
RMS_EPS = 1e-6
LANES = 128
ROUTER_ROWS = 32
EXPERT_ROW0 = 8
VMEM_LIMIT = 56 * 1024 * 1024
LOG2E = 1.4426950408889634
Q_SCALE = LOG2E * HEAD_DIM ** -0.5


def _rms(x, g):
    ms = jnp.mean(x * x, axis=-1, keepdims=True)
    return x * lax.rsqrt(ms + RMS_EPS) * g


def _pool_windows(hc, s2, s4, s8, rows):
    G = POOL_GROUP_DIM
    a = hc[pl.ds(8, rows), :]
    v2 = a + hc[pl.ds(7, rows), :]
    s2[pl.ds(8, rows), :] = v2
    v4 = v2[:, G:] + s2[pl.ds(6, rows), G:]
    s4[pl.ds(8, rows), G:] = v4
    v8 = v4[:, G:] + s4[pl.ds(4, rows), 2 * G:]
    s8[pl.ds(8, rows), 2 * G:] = v8
    v16 = v8[:, G:] + s8[pl.ds(0, rows), 3 * G:]
    return a, (v2[:, :G], v4[:, :G], v8[:, :G], v16)


def _pool_project(h, wins, pos, wp_ref, ls):
    G = POOL_GROUP_DIM
    outs = []
    for g, w in enumerate(POOL_WINDOWS):
        cnt = jnp.minimum(w, pos + 1).astype(F32)
        d = wins[g] / cnt - h[:, g * G:(g + 1) * G]
        outs.append(jnp.dot(d.astype(BF16), wp_ref[g], preferred_element_type=F32))
    return jnp.concatenate(outs, axis=-1) * ls


def _zero_pads(*refs):
    for r in refs:
        r[pl.ds(0, 8), :] = jnp.zeros((8, D_MODEL), F32)


def _pool_prompt_kernel(x_ref, buf_ref, g_ref, wp_ref, ls_ref, o_ref, nb_ref, hc, s2, s4, s8, *, tq):
    t = pl.program_id(1)

    @pl.when(t == 0)
    def _():
        _zero_pads(hc, s2, s4, s8)
        hc[pl.ds(8, HALO), :] = buf_ref[0]

    @pl.when(t > 0)
    def _():
        hc[pl.ds(8, HALO), :] = hc[pl.ds(8 + tq, HALO), :]

    x = x_ref[0]
    hc[pl.ds(8 + HALO, tq), :] = _rms(x, g_ref[...])
    a, wins = _pool_windows(hc, s2, s4, s8, HALO + tq)
    h = a[HALO:]
    wins = tuple(w[HALO:] for w in wins)
    pos = t * tq + lax.broadcasted_iota(jnp.int32, (tq, 1), 0)
    o_ref[0] = x + _pool_project(h, wins, pos, wp_ref, ls_ref[...])

    @pl.when(t == pl.num_programs(1) - 1)
    def _():
        nb_ref[0] = hc[pl.ds(8 + tq, HALO), :]


def _pool_prompt(x, buf16, g, wp, ls, tq=512):
    B, T, D = x.shape
    return pl.pallas_call(
        functools.partial(_pool_prompt_kernel, tq=tq),
        grid=(B, T // tq),
        in_specs=[
            pl.BlockSpec((1, tq, D), lambda b, t: (b, t, 0)),
            pl.BlockSpec((1, HALO, D), lambda b, t: (b, 0, 0)),
            pl.BlockSpec((1, D), lambda b, t: (0, 0)),
            pl.BlockSpec((4, POOL_GROUP_DIM, POOL_GROUP_DIM), lambda b, t: (0, 0, 0)),
            pl.BlockSpec((1, D), lambda b, t: (0, 0)),
        ],
        out_specs=[
            pl.BlockSpec((1, tq, D), lambda b, t: (b, t, 0)),
            pl.BlockSpec((1, HALO, D), lambda b, t: (b, 0, 0)),
        ],
        out_shape=[jax.ShapeDtypeStruct((B, T, D), F32), jax.ShapeDtypeStruct((B, HALO, D), F32)],
        scratch_shapes=[pltpu.VMEM((8 + HALO + tq, D), F32)] * 4,
        compiler_params=pltpu.CompilerParams(
            dimension_semantics=("arbitrary", "arbitrary"), vmem_limit_bytes=VMEM_LIMIT),
        name="pool_prompt",
    )(x, buf16, g, wp, ls)


def _pool_sample_kernel(x_ref, st_ref, start_ref, g_ref, wp_ref, ls_ref, o_ref, ns_ref, *, bs, t):
    G = POOL_GROUP_DIM
    xs = [x_ref[:, i, :] for i in range(t)]
    hn = [_rms(x, g_ref[...]) for x in xs]
    hist = [st_ref[r] for r in range(POOL_BUF)] + hn

    def doubled(prev, lag, lo):
        out = [None] * len(prev)
        for i in range(len(prev)):
            if i >= lag and prev[i] is not None and prev[i - lag] is not None:
                out[i] = prev[i][:, lo:] + prev[i - lag][:, lo:]
        return out

    s2 = doubled(hist, 1, 0)
    s4 = doubled(s2, 2, G)
    s8 = doubled(s4, 4, G)
    s16 = doubled(s8, 8, G)
    wins = (s2, s4, s8, s16)
    start = start_ref[...]
    ds = [[] for _ in POOL_WINDOWS]
    for i in range(t):
        for g, w in enumerate(POOL_WINDOWS):
            cnt = jnp.minimum(w, start + (i + 1)).astype(F32)
            ds[g].append(wins[g][POOL_BUF + i][:, :G] / cnt - hn[i][:, g * G:(g + 1) * G])
    ys = [jnp.dot(jnp.concatenate(ds[g], axis=0).astype(BF16), wp_ref[g], preferred_element_type=F32)
          for g in range(len(POOL_WINDOWS))]
    y = jnp.concatenate(ys, axis=-1) * ls_ref[...]
    for i in range(t):
        o_ref[:, i, :] = xs[i] + y[i * bs:(i + 1) * bs]
    for r in range(POOL_BUF):
        ns_ref[r] = hist[r + t]


def _pool_sample(x, st, start, g, wp, ls, bs=32):
    Bd, t, D = x.shape
    return pl.pallas_call(
        functools.partial(_pool_sample_kernel, bs=bs, t=t),
        grid=(Bd // bs,),
        in_specs=[
            pl.BlockSpec((bs, t, D), lambda i: (i, 0, 0)),
            pl.BlockSpec((POOL_BUF, bs, D), lambda i: (0, i, 0)),
            pl.BlockSpec((bs, 1), lambda i: (i, 0)),
            pl.BlockSpec((1, D), lambda i: (0, 0)),
            pl.BlockSpec((4, POOL_GROUP_DIM, POOL_GROUP_DIM), lambda i: (0, 0, 0)),
            pl.BlockSpec((1, D), lambda i: (0, 0)),
        ],
        out_specs=[pl.BlockSpec((bs, t, D), lambda i: (i, 0, 0)),
                   pl.BlockSpec((POOL_BUF, bs, D), lambda i: (0, i, 0))],
        out_shape=[jax.ShapeDtypeStruct((Bd, t, D), F32), jax.ShapeDtypeStruct((POOL_BUF, Bd, D), F32)],
        compiler_params=pltpu.CompilerParams(
            dimension_semantics=("arbitrary",), vmem_limit_bytes=VMEM_LIMIT),
        name="pool_sample",
    )(x, st, start, g, wp, ls)


def _route(lg):
    R = lg.shape[1]
    big = jnp.float32(1 << 20)
    neg = jnp.float32(-jnp.inf)
    r8 = lax.broadcasted_iota(jnp.int32, (8, R), 0).astype(F32)
    r16 = lax.broadcasted_iota(jnp.int32, (N_EXPERTS, R), 0).astype(F32)
    is_g = r8 < N_EXPERT_GROUPS
    gl = jnp.where(is_g, lg[0:8], neg)
    m = jnp.max(gl, axis=0, keepdims=True)
    gidx = jnp.min(jnp.where(gl == m, r8, big), axis=0, keepdims=True)
    z = jnp.sum(jnp.where(is_g, jnp.exp(gl - m), 0.0), axis=0, keepdims=True)
    gw = 1.0 / z
    lo = gidx * EXPERTS_PER_GROUP
    in_grp = (r16 >= lo) & (r16 < lo + EXPERTS_PER_GROUP)
    el = jnp.where(in_grp, lg[EXPERT_ROW0:EXPERT_ROW0 + N_EXPERTS], neg)
    v1 = jnp.max(el, axis=0, keepdims=True)
    i1 = jnp.min(jnp.where(el == v1, r16, big), axis=0, keepdims=True)
    el2 = jnp.where(r16 == i1, neg, el)
    v2 = jnp.max(el2, axis=0, keepdims=True)
    i2 = jnp.min(jnp.where(el2 == v2, r16, big), axis=0, keepdims=True)
    t = jnp.exp(v2 - v1)
    w1 = 1.0 / (1.0 + t)
    w2 = t * w1
    e8 = jnp.where(is_g, r8, r8 - EXPERTS_PER_GROUP)
    cw = gw * (jnp.where(e8 == i1 - lo, w1, 0.0) + jnp.where(e8 == i2 - lo, w2, 0.0))
    return gidx, cw


def _moe_kernel(*refs, has_oproj, has_final, tm, bm):
    it = iter(refs)
    x_ref = next(it)
    if has_oproj:
        oin_ref, wo_ref, bo_ref = next(it), next(it), next(it)
    nf_ref, wr_ref, br_ref, wg_ref, wu_ref, wd_ref = (next(it) for _ in range(6))
    if has_final:
        fn_ref = next(it)
    out_ref = next(it)
    hs, cs, ys, sel_s, earlier = (next(it) for _ in range(5))
    nblk = tm // bm + N_EXPERT_GROUPS
    cap = nblk * bm
    shift = bm.bit_length() - 1
    qr = 2 * bm
    assert cap % qr == 0 and tm % qr == 0
    nt = (((1,), (1,)), ((), ()))

    @pl.when(pl.program_id(0) == 0)
    def _():
        ri = lax.broadcasted_iota(jnp.int32, (tm, tm), 0)
        ci = lax.broadcasted_iota(jnp.int32, (tm, tm), 1)
        earlier[...] = jnp.where(ri < ci, 1.0, 0.0).astype(BF16)

    x = x_ref[...]
    if has_oproj:
        x = x + jnp.dot(oin_ref[...], wo_ref[...], preferred_element_type=F32) + bo_ref[...]
    out_ref[...] = x
    h = _rms(x, nf_ref[...])
    hb = h.astype(BF16)
    h_lo = (h - hb.astype(F32)).astype(BF16)
    wr = wr_ref[...]
    both = lax.dot_general(wr, hb, nt, preferred_element_type=F32)
    lg = (both[0:ROUTER_ROWS] + both[ROUTER_ROWS:]
          + lax.dot_general(wr[0:ROUTER_ROWS], h_lo, nt, preferred_element_type=F32)) + br_ref[...]
    gidx, cw = _route(lg)
    r8 = lax.broadcasted_iota(jnp.int32, (8, tm), 0)
    cw_hi = cw.astype(BF16).astype(F32)
    cwt = jnp.concatenate([jnp.where(r8 < EXPERTS_PER_GROUP, cw_hi, cw - cw_hi),
                           jnp.zeros((LANES - 8, tm), F32)], axis=0).astype(BF16)

    oh = jnp.where(r8.astype(F32) == gidx, 1.0, 0.0)
    cnt_before = jnp.dot(oh.astype(BF16), earlier[...], preferred_element_type=F32)
    pos_row = jnp.sum(oh * cnt_before, axis=0, keepdims=True)
    start_blk, n_blk = [], []
    off = jnp.int32(0)
    for g in range(N_EXPERT_GROUPS):
        in_g = gidx == g
        n_g = jnp.sum(jnp.where(in_g, 1.0, 0.0)).astype(jnp.int32)
        blocks = (n_g + (bm - 1)) >> shift
        start_blk.append(off)
        n_blk.append(blocks)
        pos_row = pos_row + jnp.where(in_g, (off << shift).astype(F32), 0.0)
        off = off + blocks

    total_blk = off
    main = tm + qr
    rare = total_blk > main // bm

    def sort_rows(r0, n):
        sub = (r0 + lax.broadcasted_iota(jnp.int32, (n, tm), 0)).astype(F32)
        sel = jnp.where(pos_row == sub, 1.0, 0.0).astype(BF16)
        sel_s[r0:r0 + n, :] = sel
        hs[r0:r0 + n, :] = jnp.dot(sel, hb, preferred_element_type=F32).astype(BF16)
        cs[r0:r0 + n, :] = lax.dot_general(sel, cwt, nt, preferred_element_type=F32)

    sort_rows(0, main)
    pl.when(rare)(functools.partial(sort_rows, main, cap - main))
    min_q = tm // qr
    ys[min_q * qr:, :] = jnp.zeros((cap - min_q * qr, D_MODEL), BF16)

    for g in range(N_EXPERT_GROUPS):
        wd = wd_ref[g * EXPERTS_PER_GROUP:(g + 1) * EXPERTS_PER_GROUP].reshape(
            EXPERTS_PER_GROUP * D_EXPERT, D_MODEL)

        def block(b, carry, g=g, wd=wd):
            rows = pl.ds(pl.multiple_of(b * bm, bm), bm)
            hblk = hs[rows, :]
            cblk = cs[rows, :]
            ln = lax.broadcasted_iota(jnp.int32, cblk.shape, 1)
            parts = []
            for e in range(EXPERTS_PER_GROUP):
                ce = jnp.sum(jnp.where((ln == e) | (ln == e + EXPERTS_PER_GROUP), cblk, 0.0),
                             axis=-1, keepdims=True)
                gt = jnp.dot(hblk, wg_ref[g * EXPERTS_PER_GROUP + e], preferred_element_type=F32)
                up = jnp.dot(hblk, wu_ref[g * EXPERTS_PER_GROUP + e], preferred_element_type=F32)
                a = gt / (1.0 + jnp.exp(-gt)) * up * ce
                parts.append(a.astype(BF16))
            a_all = jnp.concatenate(parts, axis=-1)
            ys[rows, :] = jnp.dot(a_all, wd, preferred_element_type=F32).astype(BF16)
            return carry

        lax.fori_loop(start_blk[g], start_blk[g] + n_blk[g], block, 0)

    def unsort_rows(r0, n):
        out_ref[...] += lax.dot_general(sel_s[r0:r0 + n, :], ys[r0:r0 + n, :],
                                        (((0,), (0,)), ((), ())), preferred_element_type=F32)

    unsort_rows(0, main)
    pl.when(rare)(functools.partial(unsort_rows, main, cap - main))
    if has_final:
        out_ref[...] = _rms(out_ref[...], fn_ref[...])


def _moe(x, nf, wr, br, wg, wu, wd, oproj=None, final=None, tm=512, bm=128):
    N, D = x.shape
    has_oproj = oproj is not None
    has_final = final is not None
    const2 = lambda i: (0, 0)
    const3 = lambda i: (0, 0, 0)
    tile = lambda i: (i, 0)
    once = pl.Buffered(1)
    args = [x]
    in_specs = [pl.BlockSpec((tm, D), tile)]
    if has_oproj:
        o, wo, bo = oproj
        args += [o, wo, bo]
        in_specs += [pl.BlockSpec((tm, D), tile), pl.BlockSpec((D, D), const2, pipeline_mode=once),
                     pl.BlockSpec((1, D), const2)]
    wr_hi = wr.astype(BF16)
    wr_lo = (wr - wr_hi.astype(F32)).astype(BF16)
    args += [nf, jnp.concatenate([wr_hi, wr_lo], axis=0), br, wg, wu, wd]
    in_specs += [
        pl.BlockSpec((1, D), const2),
        pl.BlockSpec((2 * ROUTER_ROWS, D), const2, pipeline_mode=once),
        pl.BlockSpec((ROUTER_ROWS, 1), const2),
        pl.BlockSpec((N_EXPERTS, D, D_EXPERT), const3, pipeline_mode=once),
        pl.BlockSpec((N_EXPERTS, D, D_EXPERT), const3, pipeline_mode=once),
        pl.BlockSpec((N_EXPERTS, D_EXPERT, D), const3, pipeline_mode=once),
    ]
    if has_final:
        args.append(final)
        in_specs.append(pl.BlockSpec((1, D), const2))
    cap = (tm // bm + N_EXPERT_GROUPS) * bm
    return pl.pallas_call(
        functools.partial(_moe_kernel, has_oproj=has_oproj, has_final=has_final, tm=tm, bm=bm),
        grid=(N // tm,),
        in_specs=in_specs,
        out_specs=pl.BlockSpec((tm, D), tile),
        out_shape=jax.ShapeDtypeStruct((N, D), F32),
        scratch_shapes=[
            pltpu.VMEM((cap, D), BF16),
            pltpu.VMEM((cap, LANES), F32),
            pltpu.VMEM((cap, D), BF16),
            pltpu.VMEM((cap, tm), BF16),
            pltpu.VMEM((tm, tm), BF16),
        ],
        compiler_params=pltpu.CompilerParams(
            dimension_semantics=("arbitrary",), vmem_limit_bytes=VMEM_LIMIT),
        name="moe",
    )(*args)


def _swap_halves(x, low):
    return jnp.where(low, pltpu.roll(x, LANES - 32, axis=1), pltpu.roll(x, 32, axis=1))


def _qkv_kernel(x_ref, g_ref, w_ref, b_ref, cos_ref, sin_ref, q_ref, kd_ref, vd_ref, k_ref, v_ref, *, keep, tiles):
    tm = x_ref.shape[0]
    emit = pl.program_id(0) % tiles == tiles - 1
    h = _rms(x_ref[...], g_ref[...]).astype(BF16)
    qkv = jnp.dot(h, w_ref[...], preferred_element_type=F32) + b_ref[...]
    cos = cos_ref[...]
    sin = sin_ref[...]
    lane = lax.broadcasted_iota(jnp.int32, cos.shape, 1)
    low32 = (lane % HEAD_DIM) < (HEAD_DIM // 2)
    low64 = lane < HEAD_DIM
    nq = N_HEADS * HEAD_DIM
    nk = N_KV_HEADS * HEAD_DIM

    def rope(c):
        return c * cos + _swap_halves(c, low32) * sin

    def dup(c):
        r = pltpu.roll(c, HEAD_DIM, axis=1)
        return jnp.where(low64, c, r), jnp.where(low64, r, c)

    for j in range(nq // LANES):
        c = rope(qkv[:, j * LANES:(j + 1) * LANES]) * Q_SCALE
        q_ref[:, j * LANES:(j + 1) * LANES] = c.astype(BF16)
    for j in range(nk // LANES):
        c = rope(qkv[:, nq + j * LANES:nq + (j + 1) * LANES])

        @pl.when(emit)
        def _(c=c, j=j):
            k_ref[:, j * LANES:(j + 1) * LANES] = c[tm - keep:]

        d0, d1 = dup(c)
        kd_ref[:, 2 * j * LANES:(2 * j + 1) * LANES] = d0.astype(BF16)
        kd_ref[:, (2 * j + 1) * LANES:(2 * j + 2) * LANES] = d1.astype(BF16)
        c = qkv[:, nq + nk + j * LANES:nq + nk + (j + 1) * LANES]

        @pl.when(emit)
        def _(c=c, j=j):
            v_ref[:, j * LANES:(j + 1) * LANES] = c[tm - keep:]

        d0, d1 = dup(c)
        vd_ref[:, 2 * j * LANES:(2 * j + 1) * LANES] = d0.astype(BF16)
        vd_ref[:, (2 * j + 1) * LANES:(2 * j + 2) * LANES] = d1.astype(BF16)


def _qkv(x, g, w, b, cos, sin, tm, n_pos_tiles, keep):
    N, D = x.shape
    nq, nk = N_HEADS * HEAD_DIM, N_KV_HEADS * HEAD_DIM
    const = lambda i: (0, 0)
    tile = lambda i: (i, 0)
    ptile = lambda i: (i % n_pos_tiles, 0)
    seq = lambda i: (i // n_pos_tiles, 0)
    n_seq = N // (tm * n_pos_tiles)
    return pl.pallas_call(
        functools.partial(_qkv_kernel, keep=keep, tiles=n_pos_tiles),
        grid=(N // tm,),
        in_specs=[
            pl.BlockSpec((tm, D), tile),
            pl.BlockSpec((1, D), const),
            pl.BlockSpec((D, nq + 2 * nk), const),
            pl.BlockSpec((1, nq + 2 * nk), const),
            pl.BlockSpec((tm, LANES), ptile),
            pl.BlockSpec((tm, LANES), ptile),
        ],
        out_specs=[
            pl.BlockSpec((tm, nq), tile),
            pl.BlockSpec((tm, 2 * nk), tile),
            pl.BlockSpec((tm, 2 * nk), tile),
            pl.BlockSpec((keep, nk), seq),
            pl.BlockSpec((keep, nk), seq),
        ],
        out_shape=[
            jax.ShapeDtypeStruct((N, nq), BF16),
            jax.ShapeDtypeStruct((N, 2 * nk), BF16),
            jax.ShapeDtypeStruct((N, 2 * nk), BF16),
            jax.ShapeDtypeStruct((n_seq * keep, nk), F32),
            jax.ShapeDtypeStruct((n_seq * keep, nk), F32),
        ],
        compiler_params=pltpu.CompilerParams(
            dimension_semantics=("arbitrary",), vmem_limit_bytes=VMEM_LIMIT),
        name="qkv",
    )(x, g, w, b, cos, sin)


def _stack_heads(qc_list):
    lane = lax.broadcasted_iota(jnp.int32, qc_list[0].shape, 1)
    low = lane < HEAD_DIM
    zero = jnp.zeros_like(qc_list[0])
    stack = []
    for qc in qc_list:
        stack.append(jnp.where(low, qc, zero))
        stack.append(jnp.where(low, zero, qc))
    return jnp.concatenate(stack, axis=0)


def _softmax_keys_on_rows(s, sinkrow):
    m = jnp.maximum(jnp.max(s, axis=0, keepdims=True), sinkrow)
    p = jnp.exp2(s - m)
    den = jnp.sum(p, axis=0, keepdims=True) + jnp.exp2(sinkrow - m)
    return (p * (1.0 / den)).astype(BF16)


def _unstack_heads(o, R):
    lowf = lax.broadcasted_iota(jnp.int32, (R, LANES), 1) < HEAD_DIM
    return (jnp.where(lowf, o[0:R], o[R:2 * R]), jnp.where(lowf, o[2 * R:3 * R], o[3 * R:4 * R]))


NT_DIMS = (((1,), (1,)), ((), ()))
TN_DIMS = (((0,), (0,)), ((), ()))
TT_DIMS = (((0,), (1,)), ((), ()))


def _attend(qc_list, kd, vd, bias, sinkrow):
    qs = _stack_heads(qc_list)
    s = lax.dot_general(kd, qs, NT_DIMS, preferred_element_type=F32) + bias
    p = _softmax_keys_on_rows(s, sinkrow)
    o = lax.dot_general(p, vd, TN_DIMS, preferred_element_type=F32)
    return _unstack_heads(o, qc_list[0].shape[0])


def _attn_prompt_kernel(q_ref, kp_ref, kc_ref, vp_ref, vc_ref, sink_ref, o_ref, *, tq):
    n = pl.program_id(1)
    B = WINDOW
    kd = jnp.concatenate([kp_ref[...], kc_ref[...]], axis=0)
    vd = jnp.concatenate([vp_ref[...], vc_ref[...]], axis=0)
    kj = lax.broadcasted_iota(jnp.int32, (2 * B, B), 0)
    qi = lax.broadcasted_iota(jnp.int32, (2 * B, B), 1)
    rel = B + qi - kj
    band = (rel >= 0) & (rel < WINDOW)
    neg = jnp.float32(-jnp.inf)
    band_bias = jnp.concatenate([jnp.where(band, 0.0, neg)] * 4, axis=1)
    first_bias = jnp.concatenate([jnp.where(band & ((kj >= B) | (n > 0)), 0.0, neg)] * 4, axis=1)
    for j in range(tq // B):
        bias = first_bias if j == 0 else band_bias
        for g in range(N_KV_HEADS):
            qc = [q_ref[j * B:(j + 1) * B, (2 * g + i) * LANES:(2 * g + i + 1) * LANES] for i in range(2)]
            kg = kd[j * B:(j + 2) * B, g * LANES:(g + 1) * LANES]
            vg = vd[j * B:(j + 2) * B, g * LANES:(g + 1) * LANES]
            o0, o1 = _attend(qc, kg, vg, bias, sink_ref[g])
            o_ref[j * B:(j + 1) * B, 2 * g * LANES:(2 * g + 1) * LANES] = o0.astype(BF16)
            o_ref[j * B:(j + 1) * B, (2 * g + 1) * LANES:(2 * g + 2) * LANES] = o1.astype(BF16)


def _attn_prompt(q, kd, vd, sinkrow, batch, seq, tq=512):
    N = q.shape[0]
    nt = seq // tq
    r = tq // WINDOW
    cur = lambda b, n: (b * nt + n, 0)
    prev = lambda b, n: (jnp.maximum((b * nt + n) * r - 1, 0), 0)
    kvw = 2 * N_KV_HEADS * HEAD_DIM
    return pl.pallas_call(
        functools.partial(_attn_prompt_kernel, tq=tq),
        grid=(batch, nt),
        in_specs=[
            pl.BlockSpec((tq, N_HEADS * HEAD_DIM), cur),
            pl.BlockSpec((WINDOW, kvw), prev),
            pl.BlockSpec((tq, kvw), cur),
            pl.BlockSpec((WINDOW, kvw), prev),
            pl.BlockSpec((tq, kvw), cur),
            pl.BlockSpec((N_KV_HEADS, 1, 4 * WINDOW), lambda b, n: (0, 0, 0)),
        ],
        out_specs=pl.BlockSpec((tq, N_HEADS * HEAD_DIM), cur),
        out_shape=jax.ShapeDtypeStruct((N, N_HEADS * HEAD_DIM), BF16),
        compiler_params=pltpu.CompilerParams(
            dimension_semantics=("arbitrary", "arbitrary"), vmem_limit_bytes=VMEM_LIMIT),
        name="attn_prompt",
    )(q, kd, kd, vd, vd, sinkrow)


def _attn_sample_kernel(q_ref, kdn_ref, vdn_ref, kn_ref, vn_ref, ck_ref, cv_ref, mask_ref, kval_ref, sink_ref,
                        o_ref, nk_ref, nv_ref, *, bs, t):
    W = WINDOW
    rows = bs * t
    bias = jnp.where((mask_ref[...] > 0) & (kval_ref[0] > 0), 0.0, -jnp.inf).astype(F32)
    lane = lax.broadcasted_iota(jnp.int32, (HEAD_DIM, W), 1)
    fresh = lane >= W - t

    def new_cols(n_ref):
        n = jnp.concatenate([n_ref[...], jnp.zeros((LANES - rows, N_KV_HEADS * HEAD_DIM), F32)], axis=0)
        nt_ = jnp.transpose(n)
        return [pltpu.roll(nt_, (W - t - s * t) % LANES, axis=1) for s in range(bs)]

    k_cols, v_cols = new_cols(kn_ref), new_cols(vn_ref)
    for g in range(N_KV_HEADS):
        hd = slice(g * HEAD_DIM, (g + 1) * HEAD_DIM)
        kt = jnp.concatenate([ck_ref[s, g] for s in range(bs)], axis=1).astype(BF16)
        vt = jnp.concatenate([cv_ref[s, g] for s in range(bs)], axis=1).astype(BF16)
        kt2 = jnp.concatenate([kt, kt], axis=0)
        vt2 = jnp.concatenate([vt, vt], axis=0)
        qs = _stack_heads([q_ref[:, (2 * g + i) * LANES:(2 * g + i + 1) * LANES] for i in range(2)])
        s_old = lax.dot_general(kt2, qs, TT_DIMS, preferred_element_type=F32)
        s_new = lax.dot_general(kdn_ref[:, g * LANES:(g + 1) * LANES], qs, NT_DIMS, preferred_element_type=F32)
        p = _softmax_keys_on_rows(jnp.concatenate([s_old, s_new], axis=0) + bias, sink_ref[g])
        o = (lax.dot_general(p[:bs * W], vt2, TT_DIMS, preferred_element_type=F32)
             + lax.dot_general(p[bs * W:], vdn_ref[:, g * LANES:(g + 1) * LANES], TN_DIMS,
                               preferred_element_type=F32))
        o0, o1 = _unstack_heads(o, rows)
        o_ref[:, 2 * g * LANES:(2 * g + 1) * LANES] = o0.astype(BF16)
        o_ref[:, (2 * g + 1) * LANES:(2 * g + 2) * LANES] = o1.astype(BF16)
        for s in range(bs):
            nk_ref[s, g] = jnp.where(fresh, k_cols[s][hd], pltpu.roll(ck_ref[s, g], W - t, axis=1))
            nv_ref[s, g] = jnp.where(fresh, v_cols[s][hd], pltpu.roll(cv_ref[s, g], W - t, axis=1))


def _attn_sample(q, kdn, vdn, kn, vn, ck, cv, mask, kval, sinkrow, bs, t):
    Bd = ck.shape[0]
    rows = bs * t
    kvw = N_KV_HEADS * HEAD_DIM
    S = bs * WINDOW + rows
    tile = lambda i: (i, 0)
    tile3 = lambda i: (i, 0, 0)
    tile4 = lambda i: (i, 0, 0, 0)
    cblk = (bs, N_KV_HEADS, HEAD_DIM, WINDOW)
    return pl.pallas_call(
        functools.partial(_attn_sample_kernel, bs=bs, t=t),
        grid=(Bd // bs,),
        in_specs=[
            pl.BlockSpec((rows, N_HEADS * HEAD_DIM), tile),
            pl.BlockSpec((rows, 2 * kvw), tile),
            pl.BlockSpec((rows, 2 * kvw), tile),
            pl.BlockSpec((rows, kvw), tile),
            pl.BlockSpec((rows, kvw), tile),
            pl.BlockSpec(cblk, tile4),
            pl.BlockSpec(cblk, tile4),
            pl.BlockSpec((S, 4 * rows), lambda i: (0, 0)),
            pl.BlockSpec((1, S, 1), tile3),
            pl.BlockSpec((N_KV_HEADS, 1, 4 * rows), lambda i: (0, 0, 0)),
        ],
        out_specs=[
            pl.BlockSpec((rows, N_HEADS * HEAD_DIM), tile),
            pl.BlockSpec(cblk, tile4),
            pl.BlockSpec(cblk, tile4),
        ],
        out_shape=[
            jax.ShapeDtypeStruct((Bd * t, N_HEADS * HEAD_DIM), BF16),
            jax.ShapeDtypeStruct((Bd,) + cblk[1:], F32),
            jax.ShapeDtypeStruct((Bd,) + cblk[1:], F32),
        ],
        compiler_params=pltpu.CompilerParams(
            dimension_semantics=("arbitrary",), vmem_limit_bytes=VMEM_LIMIT),
        name="attn_sample",
    )(q, kdn, vdn, kn, vn, ck, cv, mask, kval, sinkrow)


def _rope_tables(pos):
    inv = ROPE_THETA ** (-jnp.arange(0, HEAD_DIM, 2, dtype=F32) / HEAD_DIM)
    inv_l = jnp.tile(inv, LANES // (HEAD_DIM // 2))
    sign = jnp.asarray(np.where((np.arange(LANES) % HEAD_DIM) < HEAD_DIM // 2, -1.0, 1.0), F32)
    ang = pos.astype(F32)[:, None] * inv_l[None, :]
    return jnp.cos(ang), jnp.sin(ang) * sign[None, :]


def _sink_rows(sinks, rows):
    s = (sinks.astype(F32) * LOG2E).reshape(N_KV_HEADS, 1, N_HEADS // N_KV_HEADS, 1)
    return jnp.broadcast_to(s, (N_KV_HEADS, 1, 4, rows)).reshape(N_KV_HEADS, 1, 4 * rows)


def kernel(x_prompt, x_sample, state_pool, cache_k, cache_v, sample_start, norm_mix, norm_ffn, norm_final,
           w_pool, ls_pool, w_qkv, b_qkv, sinks, w_o, b_o, w_rg, b_rg, w_re, b_re, w_gate, w_up, w_down):
    B, T, D = x_prompt.shape
    Bd, Td, _ = x_sample.shape
    kvw = N_KV_HEADS * HEAD_DIM
    row = lambda v: v.reshape(1, -1).astype(F32)

    wp = w_pool[0].astype(BF16)
    wqkv = w_qkv[0].astype(BF16)
    wo = w_o[0].astype(BF16)
    wg, wu, wd = ([w[l].astype(BF16) for l in range(2)] for w in (w_gate, w_up, w_down))
    def router_rows(g_part, e_part):
        z = lambda n: jnp.zeros((n,) + g_part.shape[1:], F32)
        return jnp.concatenate([g_part, z(EXPERT_ROW0 - N_EXPERT_GROUPS), e_part,
                                z(ROUTER_ROWS - EXPERT_ROW0 - N_EXPERTS)], axis=0)

    wr = [router_rows(w_rg[l].T, w_re[l].T) for l in range(2)]
    br = [router_rows(b_rg[l][:, None], b_re[l][:, None]) for l in range(2)]

    start = sample_start.astype(jnp.int32)

    x1p, pool_p16 = _pool_prompt(x_prompt, jnp.zeros((B, HALO, D), F32), row(norm_mix[0]), wp, row(ls_pool[0]))
    pos_s = (start[:, None] + jnp.arange(Td, dtype=jnp.int32)[None, :]).reshape(-1)
    x1s, pool_s_t = _pool_sample(x_sample, jnp.transpose(state_pool[0], (1, 0, 2)), start[:, None],
                                 row(norm_mix[0]), wp, row(ls_pool[0]))
    x1s = x1s.reshape(Bd * Td, D)
    pool_s = jnp.transpose(pool_s_t, (1, 0, 2))
    pool_p = pool_p16[:, HALO - POOL_BUF:]

    moe0 = functools.partial(_moe, nf=row(norm_ffn[0]), wr=wr[0], br=br[0], wg=wg[0], wu=wu[0], wd=wd[0])
    x2p = moe0(x1p.reshape(B * T, D))
    x2s = moe0(x1s)

    cos_p, sin_p = _rope_tables(jnp.arange(T, dtype=jnp.int32))
    cos_s, sin_s = _rope_tables(pos_s)
    g1 = row(norm_mix[1])
    tmq = 1024
    keep = min(WINDOW, T)
    qp, kdp, vdp, kp, vp = _qkv(x2p, g1, wqkv, row(b_qkv[0]), cos_p, sin_p, tmq, T // tmq, keep)
    qs, kds, vds, ks, vs = _qkv(x2s, g1, wqkv, row(b_qkv[0]), cos_s, sin_s, tmq, 1, tmq)

    op = _attn_prompt(qp, kdp, vdp, _sink_rows(sinks[0], WINDOW), B, T)

    bs = 8
    rows = bs * Td
    W = cache_k.shape[2]
    qrow = np.arange(rows)
    ccol = np.arange(bs * W)
    ncol = np.arange(rows)
    samp_q, t_q = qrow // Td, qrow % Td
    m_cache = (samp_q[:, None] == (ccol // W)[None, :]) & ((ccol % W)[None, :] > t_q[:, None])
    m_new = (samp_q[:, None] == (ncol // Td)[None, :]) & ((ncol % Td)[None, :] <= t_q[:, None])
    amask = jnp.asarray(np.tile(np.concatenate([m_cache, m_new], axis=1).T, (1, 4)), F32)
    kv_cache = (jnp.arange(W, dtype=jnp.int32)[None, :] >= (W - start)[:, None]).reshape(Bd // bs, bs * W)
    kval = jnp.concatenate([kv_cache, jnp.ones((Bd // bs, rows), bool)], axis=1).astype(F32)
    kval = kval.reshape(Bd // bs, bs * W + rows, 1)
    to_stored = lambda c: jnp.transpose(c[0], (0, 2, 3, 1))
    from_stored = lambda c: jnp.transpose(c, (0, 3, 1, 2))[None]
    osamp, nk_s, nv_s = _attn_sample(
        qs, kds, vds, ks, vs, to_stored(cache_k), to_stored(cache_v),
        amask, kval, _sink_rows(sinks[0], rows), bs, Td)

    moe1 = functools.partial(_moe, nf=row(norm_ffn[1]), wr=wr[1], br=br[1], wg=wg[1], wu=wu[1], wd=wd[1],
                             final=row(norm_final))
    yp = moe1(x2p, oproj=(op, wo, row(b_o[0])))
    ys = moe1(x2s, oproj=(osamp, wo, row(b_o[0])))

    k_p = kp.reshape(1, B, keep, N_KV_HEADS, HEAD_DIM)
    v_p = vp.reshape(1, B, keep, N_KV_HEADS, HEAD_DIM)
    return (yp.reshape(B, T, D), ys.reshape(Bd, Td, D), pool_p[None], k_p, v_p, pool_s[None],
            from_stored(nk_s), from_stored(nv_s))
```

```python
import functools

import jax
import jax.numpy as jnp
import numpy as np
from jax import lax
from jax.experimental import pallas as pl
from jax.experimental.pallas import tpu as pltpu

F32 = jnp.float32
BF16 = jnp.bfloat16

D_MODEL = 1024
POOL_WINDOWS = (2, 4, 8, 16)
POOL_GROUP_DIM = 256
POOL_BUF = 15
HALO = 16
HEAD_DIM = 64
N_HEADS = 16
N_KV_HEADS = 4
WINDOW = 128
ROPE_THETA = 10000.0
N_EXPERT_GROUPS = 4
EXPERTS_PER_GROUP = 4
N_EXPERTS = 16
D_EXPERT = 256
RMS_EPS = 1e-6
LANES = 128
ROUTER_ROWS = 32
EXPERT_ROW0 = 8
VMEM_LIMIT = 56 * 1024 * 1024
LOG2E = 1.4426950408889634
Q_SCALE = LOG2E * HEAD_DIM ** -0.5


def _rms(x, g):
    ms = jnp.mean(x * x, axis=-1, keepdims=True)
    return x * lax.rsqrt(ms + RMS_EPS) * g


def _pool_project(h, wins, pos, wp_ref, ls):
    G = POOL_GROUP_DIM
    outs = []
    for g, w in enumerate(POOL_WINDOWS):
        cnt = jnp.minimum(w, pos + 1).astype(F32)
        d = wins[g] / cnt - h[:, g * G:(g + 1) * G]
        outs.append(jnp.dot(d.astype(BF16), wp_ref[g], preferred_element_type=F32))
    return jnp.concatenate(outs, axis=-1) * ls


POOL_BLOCK = 128


def _pool_band():
    t = np.arange(POOL_BLOCK)[:, None] + HALO
    k = np.arange(POOL_BLOCK + HALO)[None, :]
    return jnp.asarray(np.stack([(k <= t) & (k > t - w) for w in POOL_WINDOWS]), BF16)


def _pool_prompt_kernel(x_ref, buf_ref, g_ref, wp_ref, ls_ref, band_ref, o_ref, nb_ref, hc, *, tq):
    t = pl.program_id(1)
    G = POOL_GROUP_DIM

    @pl.when(t == 0)
    def _():
        hc[pl.ds(0, HALO), :] = buf_ref[0]

    @pl.when(t > 0)
    def _():
        hc[pl.ds(0, HALO), :] = hc[pl.ds(tq, HALO), :]

    x = x_ref[0]
    h = _rms(x, g_ref[...])
    hc[pl.ds(HALO, tq), :] = h
    hb = hc[...].astype(BF16)
    wins = []
    for g in range(len(POOL_WINDOWS)):
        blocks = [jnp.dot(band_ref[g], hb[b * POOL_BLOCK:(b + 1) * POOL_BLOCK + HALO, g * G:(g + 1) * G],
                          preferred_element_type=F32) for b in range(tq // POOL_BLOCK)]
        wins.append(jnp.concatenate(blocks, axis=0))
    pos = t * tq + lax.broadcasted_iota(jnp.int32, (tq, 1), 0)
    o_ref[0] = x + _pool_project(h, wins, pos, wp_ref, ls_ref[...])

    @pl.when(t == pl.num_programs(1) - 1)
    def _():
        nb_ref[0] = hc[pl.ds(tq, HALO), :]


def _pool_prompt(x, buf16, g, wp, ls, tq=512):
    B, T, D = x.shape
    return pl.pallas_call(
        functools.partial(_pool_prompt_kernel, tq=tq),
        grid=(B, T // tq),
        in_specs=[
            pl.BlockSpec((1, tq, D), lambda b, t: (b, t, 0)),
            pl.BlockSpec((1, HALO, D), lambda b, t: (b, 0, 0)),
            pl.BlockSpec((1, D), lambda b, t: (0, 0)),
            pl.BlockSpec((4, POOL_GROUP_DIM, POOL_GROUP_DIM), lambda b, t: (0, 0, 0)),
            pl.BlockSpec((1, D), lambda b, t: (0, 0)),
            pl.BlockSpec((4, POOL_BLOCK, POOL_BLOCK + HALO), lambda b, t: (0, 0, 0)),
        ],
        out_specs=[
            pl.BlockSpec((1, tq, D), lambda b, t: (b, t, 0)),
            pl.BlockSpec((1, HALO, D), lambda b, t: (b, 0, 0)),
        ],
        out_shape=[jax.ShapeDtypeStruct((B, T, D), F32), jax.ShapeDtypeStruct((B, HALO, D), F32)],
        scratch_shapes=[pltpu.VMEM((HALO + tq, D), F32)],
        compiler_params=pltpu.CompilerParams(
            dimension_semantics=("arbitrary", "arbitrary"), vmem_limit_bytes=VMEM_LIMIT),
        name="pool_prompt",
    )(x, buf16, g, wp, ls, _pool_band())


def _pool_sample_kernel(x_ref, st_ref, start_ref, g_ref, wp_ref, ls_ref, o_ref, ns_ref, *, bs, t):
    G = POOL_GROUP_DIM
    xs = [x_ref[:, i, :] for i in range(t)]
    hn = [_rms(x, g_ref[...]) for x in xs]
    hist = [st_ref[r] for r in range(POOL_BUF)] + hn

    def doubled(prev, lag, lo):
        out = [None] * len(prev)
        for i in range(len(prev)):
            if i >= lag and prev[i] is not None and prev[i - lag] is not None:
                out[i] = prev[i][:, lo:] + prev[i - lag][:, lo:]
        return out

    s2 = doubled(hist, 1, 0)
    s4 = doubled(s2, 2, G)
    s8 = doubled(s4, 4, G)
    s16 = doubled(s8, 8, G)
    wins = (s2, s4, s8, s16)
    start = start_ref[...]
    ds = [[] for _ in POOL_WINDOWS]
    for i in range(t):
        for g, w in enumerate(POOL_WINDOWS):
            cnt = jnp.minimum(w, start + (i + 1)).astype(F32)
            ds[g].append(wins[g][POOL_BUF + i][:, :G] / cnt - hn[i][:, g * G:(g + 1) * G])
    ys = [jnp.dot(jnp.concatenate(ds[g], axis=0).astype(BF16), wp_ref[g], preferred_element_type=F32)
          for g in range(len(POOL_WINDOWS))]
    y = jnp.concatenate(ys, axis=-1) * ls_ref[...]
    for i in range(t):
        o_ref[:, i, :] = xs[i] + y[i * bs:(i + 1) * bs]
    for r in range(POOL_BUF):
        ns_ref[r] = hist[r + t]


def _pool_sample(x, st, start, g, wp, ls, bs=32):
    Bd, t, D = x.shape
    return pl.pallas_call(
        functools.partial(_pool_sample_kernel, bs=bs, t=t),
        grid=(Bd // bs,),
        in_specs=[
            pl.BlockSpec((bs, t, D), lambda i: (i, 0, 0)),
            pl.BlockSpec((POOL_BUF, bs, D), lambda i: (0, i, 0)),
            pl.BlockSpec((bs, 1), lambda i: (i, 0)),
            pl.BlockSpec((1, D), lambda i: (0, 0)),
            pl.BlockSpec((4, POOL_GROUP_DIM, POOL_GROUP_DIM), lambda i: (0, 0, 0)),
            pl.BlockSpec((1, D), lambda i: (0, 0)),
        ],
        out_specs=[pl.BlockSpec((bs, t, D), lambda i: (i, 0, 0)),
                   pl.BlockSpec((POOL_BUF, bs, D), lambda i: (0, i, 0))],
        out_shape=[jax.ShapeDtypeStruct((Bd, t, D), F32), jax.ShapeDtypeStruct((POOL_BUF, Bd, D), F32)],
        compiler_params=pltpu.CompilerParams(
            dimension_semantics=("arbitrary",), vmem_limit_bytes=VMEM_LIMIT),
        name="pool_sample",
    )(x, st, start, g, wp, ls)


def _route(lg):
    R = lg.shape[1]
    big = jnp.float32(1 << 20)
    neg = jnp.float32(-jnp.inf)
    r8 = lax.broadcasted_iota(jnp.int32, (8, R), 0).astype(F32)
    r16 = lax.broadcasted_iota(jnp.int32, (N_EXPERTS, R), 0).astype(F32)
    is_g = r8 < N_EXPERT_GROUPS
    gl = jnp.where(is_g, lg[0:8], neg)
    m = jnp.max(gl, axis=0, keepdims=True)
    gidx = jnp.min(jnp.where(gl == m, r8, big), axis=0, keepdims=True)
    z = jnp.sum(jnp.where(is_g, jnp.exp(gl - m), 0.0), axis=0, keepdims=True)
    gw = 1.0 / z
    lo = gidx * EXPERTS_PER_GROUP
    in_grp = (r16 >= lo) & (r16 < lo + EXPERTS_PER_GROUP)
    el = jnp.where(in_grp, lg[EXPERT_ROW0:EXPERT_ROW0 + N_EXPERTS], neg)
    v1 = jnp.max(el, axis=0, keepdims=True)
    i1 = jnp.min(jnp.where(el == v1, r16, big), axis=0, keepdims=True)
    el2 = jnp.where(r16 == i1, neg, el)
    v2 = jnp.max(el2, axis=0, keepdims=True)
    i2 = jnp.min(jnp.where(el2 == v2, r16, big), axis=0, keepdims=True)
    t = jnp.exp(v2 - v1)
    w1 = 1.0 / (1.0 + t)
    w2 = t * w1
    e8 = jnp.where(is_g, r8, r8 - EXPERTS_PER_GROUP)
    cw = gw * (jnp.where(e8 == i1 - lo, w1, 0.0) + jnp.where(e8 == i2 - lo, w2, 0.0))
    return gidx, cw


def _moe_kernel(*refs, has_oproj, has_final, tm, bm):
    it = iter(refs)
    x_ref = next(it)
    if has_oproj:
        oin_ref, wo_ref, bo_ref = next(it), next(it), next(it)
    nf_ref, wr_ref, br_ref, wg_ref, wu_ref, wd_ref = (next(it) for _ in range(6))
    if has_final:
        fn_ref = next(it)
    out_ref = next(it)
    hs, cs, ys, sel_s, earlier = (next(it) for _ in range(5))
    nblk = tm // bm + N_EXPERT_GROUPS
    cap = nblk * bm
    shift = bm.bit_length() - 1
    qr = 2 * bm
    assert cap % qr == 0 and tm % qr == 0
    nt = (((1,), (1,)), ((), ()))

    @pl.when(pl.program_id(0) == 0)
    def _():
        ri = lax.broadcasted_iota(jnp.int32, (tm, tm), 0)
        ci = lax.broadcasted_iota(jnp.int32, (tm, tm), 1)
        earlier[...] = jnp.where(ri < ci, 1.0, 0.0).astype(BF16)

    x = x_ref[...]
    if has_oproj:
        x = x + jnp.dot(oin_ref[...], wo_ref[...], preferred_element_type=F32) + bo_ref[...]
    out_ref[...] = x
    h = _rms(x, nf_ref[...])
    hb = h.astype(BF16)
    h_lo = (h - hb.astype(F32)).astype(BF16)
    wr = wr_ref[...]
    both = lax.dot_general(wr, hb, nt, preferred_element_type=F32)
    lg = (both[0:ROUTER_ROWS] + both[ROUTER_ROWS:]
          + lax.dot_general(wr[0:ROUTER_ROWS], h_lo, nt, preferred_element_type=F32)) + br_ref[...]
    gidx, cw = _route(lg)
    r8 = lax.broadcasted_iota(jnp.int32, (8, tm), 0)
    cw_hi = cw.astype(BF16).astype(F32)
    cwt = jnp.concatenate([jnp.where(r8 < EXPERTS_PER_GROUP, cw_hi, cw - cw_hi),
                           jnp.zeros((LANES - 8, tm), F32)], axis=0).astype(BF16)

    oh = jnp.where(r8.astype(F32) == gidx, 1.0, 0.0)
    cnt_before = jnp.dot(oh.astype(BF16), earlier[...], preferred_element_type=F32)
    pos_row = jnp.sum(oh * cnt_before, axis=0, keepdims=True)
    start_blk, n_blk = [], []
    off = jnp.int32(0)
    for g in range(N_EXPERT_GROUPS):
        in_g = gidx == g
        n_g = jnp.sum(jnp.where(in_g, 1.0, 0.0)).astype(jnp.int32)
        blocks = (n_g + (bm - 1)) >> shift
        start_blk.append(off)
        n_blk.append(blocks)
        pos_row = pos_row + jnp.where(in_g, (off << shift).astype(F32), 0.0)
        off = off + blocks

    total_blk = off
    main = tm + qr
    rare = total_blk > main // bm

    def sort_rows(r0, n):
        sub = (r0 + lax.broadcasted_iota(jnp.int32, (n, tm), 0)).astype(F32)
        sel = jnp.where(pos_row == sub, 1.0, 0.0).astype(BF16)
        sel_s[r0:r0 + n, :] = sel
        hs[r0:r0 + n, :] = jnp.dot(sel, hb, preferred_element_type=F32).astype(BF16)
        cs[r0:r0 + n, :] = lax.dot_general(sel, cwt, nt, preferred_element_type=F32)

    sort_rows(0, main)
    pl.when(rare)(functools.partial(sort_rows, main, cap - main))
    min_q = tm // qr
    ys[min_q * qr:, :] = jnp.zeros((cap - min_q * qr, D_MODEL), BF16)

    for g in range(N_EXPERT_GROUPS):
        wd = wd_ref[g * EXPERTS_PER_GROUP:(g + 1) * EXPERTS_PER_GROUP].reshape(
            EXPERTS_PER_GROUP * D_EXPERT, D_MODEL)

        def block(b, carry, g=g, wd=wd):
            rows = pl.ds(pl.multiple_of(b * bm, bm), bm)
            hblk = hs[rows, :]
            cblk = cs[rows, :]
            ln = lax.broadcasted_iota(jnp.int32, cblk.shape, 1)
            parts = []
            for e in range(EXPERTS_PER_GROUP):
                ce = jnp.sum(jnp.where((ln == e) | (ln == e + EXPERTS_PER_GROUP), cblk, 0.0),
                             axis=-1, keepdims=True)
                gt = jnp.dot(hblk, wg_ref[g * EXPERTS_PER_GROUP + e], preferred_element_type=F32)
                up = jnp.dot(hblk, wu_ref[g * EXPERTS_PER_GROUP + e], preferred_element_type=F32)
                a = gt / (1.0 + jnp.exp(-gt)) * up * ce
                parts.append(a.astype(BF16))
            a_all = jnp.concatenate(parts, axis=-1)
            ys[rows, :] = jnp.dot(a_all, wd, preferred_element_type=F32).astype(BF16)
            return carry

        lax.fori_loop(start_blk[g], start_blk[g] + n_blk[g], block, 0)

    def unsort_rows(r0, n):
        out_ref[...] += lax.dot_general(sel_s[r0:r0 + n, :], ys[r0:r0 + n, :],
                                        (((0,), (0,)), ((), ())), preferred_element_type=F32)

    unsort_rows(0, main)
    pl.when(rare)(functools.partial(unsort_rows, main, cap - main))
    if has_final:
        out_ref[...] = _rms(out_ref[...], fn_ref[...])


def _moe(x, nf, wr, br, wg, wu, wd, layer, oproj=None, final=None, tm=512, bm=128):
    N, D = x.shape
    has_oproj = oproj is not None
    has_final = final is not None
    const2 = lambda i: (0, 0)
    of_layer = lambda i: (layer, 0, 0, 0)
    tile = lambda i: (i, 0)
    once = pl.Buffered(1)
    args = [x]
    in_specs = [pl.BlockSpec((tm, D), tile)]
    if has_oproj:
        o, wo, bo = oproj
        args += [o, wo, bo]
        in_specs += [pl.BlockSpec((tm, D), tile), pl.BlockSpec((D, D), const2, pipeline_mode=once),
                     pl.BlockSpec((1, D), const2)]
    wr_hi = wr.astype(BF16)
    wr_lo = (wr - wr_hi.astype(F32)).astype(BF16)
    args += [nf, jnp.concatenate([wr_hi, wr_lo], axis=0), br, wg, wu, wd]
    in_specs += [
        pl.BlockSpec((1, D), const2),
        pl.BlockSpec((2 * ROUTER_ROWS, D), const2, pipeline_mode=once),
        pl.BlockSpec((ROUTER_ROWS, 1), const2),
        pl.BlockSpec((None, N_EXPERTS, D, D_EXPERT), of_layer, pipeline_mode=once),
        pl.BlockSpec((None, N_EXPERTS, D, D_EXPERT), of_layer, pipeline_mode=once),
        pl.BlockSpec((None, N_EXPERTS, D_EXPERT, D), of_layer, pipeline_mode=once),
    ]
    if has_final:
        args.append(final)
        in_specs.append(pl.BlockSpec((1, D), const2))
    cap = (tm // bm + N_EXPERT_GROUPS) * bm
    return pl.pallas_call(
        functools.partial(_moe_kernel, has_oproj=has_oproj, has_final=has_final, tm=tm, bm=bm),
        grid=(N // tm,),
        in_specs=in_specs,
        out_specs=pl.BlockSpec((tm, D), tile),
        out_shape=jax.ShapeDtypeStruct((N, D), F32),
        scratch_shapes=[
            pltpu.VMEM((cap, D), BF16),
            pltpu.VMEM((cap, LANES), F32),
            pltpu.VMEM((cap, D), BF16),
            pltpu.VMEM((cap, tm), BF16),
            pltpu.VMEM((tm, tm), BF16),
        ],
        compiler_params=pltpu.CompilerParams(
            dimension_semantics=("arbitrary",), vmem_limit_bytes=VMEM_LIMIT),
        name="moe",
    )(*args)


def _swap_halves(x, low):
    return jnp.where(low, pltpu.roll(x, LANES - 32, axis=1), pltpu.roll(x, 32, axis=1))


def _qkv_kernel(x_ref, g_ref, w_ref, b_ref, cos_ref, sin_ref, q_ref, kd_ref, vd_ref, k_ref, v_ref, *, keep, tiles):
    tm = x_ref.shape[0]
    emit = pl.program_id(0) % tiles == tiles - 1
    h = _rms(x_ref[...], g_ref[...]).astype(BF16)
    qkv = jnp.dot(h, w_ref[...], preferred_element_type=F32) + b_ref[...]
    cos = cos_ref[...]
    sin = sin_ref[...]
    lane = lax.broadcasted_iota(jnp.int32, cos.shape, 1)
    low32 = (lane % HEAD_DIM) < (HEAD_DIM // 2)
    low64 = lane < HEAD_DIM
    nq = N_HEADS * HEAD_DIM
    nk = N_KV_HEADS * HEAD_DIM

    def rope(c):
        return c * cos + _swap_halves(c, low32) * sin

    def dup(c):
        r = pltpu.roll(c, HEAD_DIM, axis=1)
        return jnp.where(low64, c, r), jnp.where(low64, r, c)

    for j in range(nq // LANES):
        c = rope(qkv[:, j * LANES:(j + 1) * LANES]) * Q_SCALE
        q_ref[:, j * LANES:(j + 1) * LANES] = c.astype(BF16)
    for j in range(nk // LANES):
        c = rope(qkv[:, nq + j * LANES:nq + (j + 1) * LANES])

        @pl.when(emit)
        def _(c=c, j=j):
            k_ref[:, j * LANES:(j + 1) * LANES] = c[tm - keep:]

        d0, d1 = dup(c)
        kd_ref[:, 2 * j * LANES:(2 * j + 1) * LANES] = d0.astype(BF16)
        kd_ref[:, (2 * j + 1) * LANES:(2 * j + 2) * LANES] = d1.astype(BF16)
        c = qkv[:, nq + nk + j * LANES:nq + nk + (j + 1) * LANES]

        @pl.when(emit)
        def _(c=c, j=j):
            v_ref[:, j * LANES:(j + 1) * LANES] = c[tm - keep:]

        d0, d1 = dup(c)
        vd_ref[:, 2 * j * LANES:(2 * j + 1) * LANES] = d0.astype(BF16)
        vd_ref[:, (2 * j + 1) * LANES:(2 * j + 2) * LANES] = d1.astype(BF16)


def _qkv(x, g, w, b, cos, sin, tm, n_pos_tiles, keep):
    N, D = x.shape
    nq, nk = N_HEADS * HEAD_DIM, N_KV_HEADS * HEAD_DIM
    const = lambda i: (0, 0)
    tile = lambda i: (i, 0)
    ptile = lambda i: (i % n_pos_tiles, 0)
    seq = lambda i: (i // n_pos_tiles, 0)
    n_seq = N // (tm * n_pos_tiles)
    return pl.pallas_call(
        functools.partial(_qkv_kernel, keep=keep, tiles=n_pos_tiles),
        grid=(N // tm,),
        in_specs=[
            pl.BlockSpec((tm, D), tile),
            pl.BlockSpec((1, D), const),
            pl.BlockSpec((D, nq + 2 * nk), const),
            pl.BlockSpec((1, nq + 2 * nk), const),
            pl.BlockSpec((tm, LANES), ptile),
            pl.BlockSpec((tm, LANES), ptile),
        ],
        out_specs=[
            pl.BlockSpec((tm, nq), tile),
            pl.BlockSpec((tm, 2 * nk), tile),
            pl.BlockSpec((tm, 2 * nk), tile),
            pl.BlockSpec((keep, nk), seq),
            pl.BlockSpec((keep, nk), seq),
        ],
        out_shape=[
            jax.ShapeDtypeStruct((N, nq), BF16),
            jax.ShapeDtypeStruct((N, 2 * nk), BF16),
            jax.ShapeDtypeStruct((N, 2 * nk), BF16),
            jax.ShapeDtypeStruct((n_seq * keep, nk), F32),
            jax.ShapeDtypeStruct((n_seq * keep, nk), F32),
        ],
        compiler_params=pltpu.CompilerParams(
            dimension_semantics=("arbitrary",), vmem_limit_bytes=VMEM_LIMIT),
        name="qkv",
    )(x, g, w, b, cos, sin)


def _stack_heads(qc_list):
    lane = lax.broadcasted_iota(jnp.int32, qc_list[0].shape, 1)
    low = lane < HEAD_DIM
    zero = jnp.zeros_like(qc_list[0])
    stack = []
    for qc in qc_list:
        stack.append(jnp.where(low, qc, zero))
        stack.append(jnp.where(low, zero, qc))
    return jnp.concatenate(stack, axis=0)


def _softmax_keys_on_rows(s, sinkrow):
    m = jnp.maximum(jnp.max(s, axis=0, keepdims=True), sinkrow)
    p = jnp.exp2(s - m)
    den = jnp.sum(p, axis=0, keepdims=True) + jnp.exp2(sinkrow - m)
    return (p * (1.0 / den)).astype(BF16)


def _unstack_heads(o, R):
    lowf = lax.broadcasted_iota(jnp.int32, (R, LANES), 1) < HEAD_DIM
    return (jnp.where(lowf, o[0:R], o[R:2 * R]), jnp.where(lowf, o[2 * R:3 * R], o[3 * R:4 * R]))


NT_DIMS = (((1,), (1,)), ((), ()))
TN_DIMS = (((0,), (0,)), ((), ()))
TT_DIMS = (((0,), (1,)), ((), ()))


def _attend(qc_list, kd, vd, bias, sinkrow):
    qs = _stack_heads(qc_list)
    s = lax.dot_general(kd, qs, NT_DIMS, preferred_element_type=F32) + bias
    p = _softmax_keys_on_rows(s, sinkrow)
    o = lax.dot_general(p, vd, TN_DIMS, preferred_element_type=F32)
    return _unstack_heads(o, qc_list[0].shape[0])


def _attn_prompt_kernel(q_ref, kp_ref, kc_ref, vp_ref, vc_ref, sink_ref, o_ref, *, tq):
    n = pl.program_id(1)
    B = WINDOW
    kd = jnp.concatenate([kp_ref[...], kc_ref[...]], axis=0)
    vd = jnp.concatenate([vp_ref[...], vc_ref[...]], axis=0)
    kj = lax.broadcasted_iota(jnp.int32, (2 * B, B), 0)
    qi = lax.broadcasted_iota(jnp.int32, (2 * B, B), 1)
    rel = B + qi - kj
    band = (rel >= 0) & (rel < WINDOW)
    neg = jnp.float32(-jnp.inf)
    band_bias = jnp.concatenate([jnp.where(band, 0.0, neg)] * 4, axis=1)
    first_bias = jnp.concatenate([jnp.where(band & ((kj >= B) | (n > 0)), 0.0, neg)] * 4, axis=1)
    for j in range(tq // B):
        bias = first_bias if j == 0 else band_bias
        for g in range(N_KV_HEADS):
            qc = [q_ref[j * B:(j + 1) * B, (2 * g + i) * LANES:(2 * g + i + 1) * LANES] for i in range(2)]
            kg = kd[j * B:(j + 2) * B, g * LANES:(g + 1) * LANES]
            vg = vd[j * B:(j + 2) * B, g * LANES:(g + 1) * LANES]
            o0, o1 = _attend(qc, kg, vg, bias, sink_ref[g])
            o_ref[j * B:(j + 1) * B, 2 * g * LANES:(2 * g + 1) * LANES] = o0.astype(BF16)
            o_ref[j * B:(j + 1) * B, (2 * g + 1) * LANES:(2 * g + 2) * LANES] = o1.astype(BF16)


def _attn_prompt(q, kd, vd, sinkrow, batch, seq, tq=512):
    N = q.shape[0]
    nt = seq // tq
    r = tq // WINDOW
    cur = lambda b, n: (b * nt + n, 0)
    prev = lambda b, n: (jnp.maximum((b * nt + n) * r - 1, 0), 0)
    kvw = 2 * N_KV_HEADS * HEAD_DIM
    return pl.pallas_call(
        functools.partial(_attn_prompt_kernel, tq=tq),
        grid=(batch, nt),
        in_specs=[
            pl.BlockSpec((tq, N_HEADS * HEAD_DIM), cur),
            pl.BlockSpec((WINDOW, kvw), prev),
            pl.BlockSpec((tq, kvw), cur),
            pl.BlockSpec((WINDOW, kvw), prev),
            pl.BlockSpec((tq, kvw), cur),
            pl.BlockSpec((N_KV_HEADS, 1, 4 * WINDOW), lambda b, n: (0, 0, 0)),
        ],
        out_specs=pl.BlockSpec((tq, N_HEADS * HEAD_DIM), cur),
        out_shape=jax.ShapeDtypeStruct((N, N_HEADS * HEAD_DIM), BF16),
        compiler_params=pltpu.CompilerParams(
            dimension_semantics=("arbitrary", "arbitrary"), vmem_limit_bytes=VMEM_LIMIT),
        name="attn_prompt",
    )(q, kd, kd, vd, vd, sinkrow)


def _attn_sample_kernel(q_ref, kdn_ref, vdn_ref, kn_ref, vn_ref, ck_ref, cv_ref, mask_ref, kval_ref, sink_ref,
                        o_ref, nk_ref, nv_ref, *, bs, t):
    W = WINDOW
    rows = bs * t
    bias = jnp.where((mask_ref[...] > 0) & (kval_ref[0] > 0), 0.0, -jnp.inf).astype(F32)
    lane = lax.broadcasted_iota(jnp.int32, (HEAD_DIM, W), 1)
    fresh = lane >= W - t

    def new_cols(n_ref):
        n = jnp.concatenate([n_ref[...], jnp.zeros((LANES - rows, N_KV_HEADS * HEAD_DIM), F32)], axis=0)
        nt_ = jnp.transpose(n)
        return [pltpu.roll(nt_, (W - t - s * t) % LANES, axis=1) for s in range(bs)]

    k_cols, v_cols = new_cols(kn_ref), new_cols(vn_ref)
    for g in range(N_KV_HEADS):
        hd = slice(g * HEAD_DIM, (g + 1) * HEAD_DIM)
        kt = jnp.concatenate([ck_ref[s, g] for s in range(bs)], axis=1).astype(BF16)
        vt = jnp.concatenate([cv_ref[s, g] for s in range(bs)], axis=1).astype(BF16)
        kt2 = jnp.concatenate([kt, kt], axis=0)
        vt2 = jnp.concatenate([vt, vt], axis=0)
        qs = _stack_heads([q_ref[:, (2 * g + i) * LANES:(2 * g + i + 1) * LANES] for i in range(2)])
        s_old = lax.dot_general(kt2, qs, TT_DIMS, preferred_element_type=F32)
        s_new = lax.dot_general(kdn_ref[:, g * LANES:(g + 1) * LANES], qs, NT_DIMS, preferred_element_type=F32)
        p = _softmax_keys_on_rows(jnp.concatenate([s_old, s_new], axis=0) + bias, sink_ref[g])
        o = (lax.dot_general(p[:bs * W], vt2, TT_DIMS, preferred_element_type=F32)
             + lax.dot_general(p[bs * W:], vdn_ref[:, g * LANES:(g + 1) * LANES], TN_DIMS,
                               preferred_element_type=F32))
        o0, o1 = _unstack_heads(o, rows)
        o_ref[:, 2 * g * LANES:(2 * g + 1) * LANES] = o0.astype(BF16)
        o_ref[:, (2 * g + 1) * LANES:(2 * g + 2) * LANES] = o1.astype(BF16)
        for s in range(bs):
            nk_ref[s, g] = jnp.where(fresh, k_cols[s][hd], pltpu.roll(ck_ref[s, g], W - t, axis=1))
            nv_ref[s, g] = jnp.where(fresh, v_cols[s][hd], pltpu.roll(cv_ref[s, g], W - t, axis=1))


def _attn_sample(q, kdn, vdn, kn, vn, ck, cv, mask, kval, sinkrow, bs, t):
    Bd = ck.shape[0]
    rows = bs * t
    kvw = N_KV_HEADS * HEAD_DIM
    S = bs * WINDOW + rows
    tile = lambda i: (i, 0)
    tile3 = lambda i: (i, 0, 0)
    tile4 = lambda i: (i, 0, 0, 0)
    cblk = (bs, N_KV_HEADS, HEAD_DIM, WINDOW)
    return pl.pallas_call(
        functools.partial(_attn_sample_kernel, bs=bs, t=t),
        grid=(Bd // bs,),
        in_specs=[
            pl.BlockSpec((rows, N_HEADS * HEAD_DIM), tile),
            pl.BlockSpec((rows, 2 * kvw), tile),
            pl.BlockSpec((rows, 2 * kvw), tile),
            pl.BlockSpec((rows, kvw), tile),
            pl.BlockSpec((rows, kvw), tile),
            pl.BlockSpec(cblk, tile4),
            pl.BlockSpec(cblk, tile4),
            pl.BlockSpec((S, 4 * rows), lambda i: (0, 0)),
            pl.BlockSpec((1, S, 1), tile3),
            pl.BlockSpec((N_KV_HEADS, 1, 4 * rows), lambda i: (0, 0, 0)),
        ],
        out_specs=[
            pl.BlockSpec((rows, N_HEADS * HEAD_DIM), tile),
            pl.BlockSpec(cblk, tile4),
            pl.BlockSpec(cblk, tile4),
        ],
        out_shape=[
            jax.ShapeDtypeStruct((Bd * t, N_HEADS * HEAD_DIM), BF16),
            jax.ShapeDtypeStruct((Bd,) + cblk[1:], F32),
            jax.ShapeDtypeStruct((Bd,) + cblk[1:], F32),
        ],
        compiler_params=pltpu.CompilerParams(
            dimension_semantics=("arbitrary",), vmem_limit_bytes=VMEM_LIMIT),
        name="attn_sample",
    )(q, kdn, vdn, kn, vn, ck, cv, mask, kval, sinkrow)


def _rope_tables(pos):
    inv = ROPE_THETA ** (-jnp.arange(0, HEAD_DIM, 2, dtype=F32) / HEAD_DIM)
    inv_l = jnp.tile(inv, LANES // (HEAD_DIM // 2))
    sign = jnp.asarray(np.where((np.arange(LANES) % HEAD_DIM) < HEAD_DIM // 2, -1.0, 1.0), F32)
    ang = pos.astype(F32)[:, None] * inv_l[None, :]
    return jnp.cos(ang), jnp.sin(ang) * sign[None, :]


def _sink_rows(sinks, rows):
    s = (sinks.astype(F32) * LOG2E).reshape(N_KV_HEADS, 1, N_HEADS // N_KV_HEADS, 1)
    return jnp.broadcast_to(s, (N_KV_HEADS, 1, 4, rows)).reshape(N_KV_HEADS, 1, 4 * rows)


def kernel(x_prompt, x_sample, state_pool, cache_k, cache_v, sample_start, norm_mix, norm_ffn, norm_final,
           w_pool, ls_pool, w_qkv, b_qkv, sinks, w_o, b_o, w_rg, b_rg, w_re, b_re, w_gate, w_up, w_down):
    B, T, D = x_prompt.shape
    Bd, Td, _ = x_sample.shape
    kvw = N_KV_HEADS * HEAD_DIM
    row = lambda v: v.reshape(1, -1).astype(F32)

    wp = w_pool[0].astype(BF16)
    wqkv = w_qkv[0].astype(BF16)
    wo = w_o[0].astype(BF16)
    wg, wu, wd = w_gate.astype(BF16), w_up.astype(BF16), w_down.astype(BF16)
    def router_rows(g_part, e_part):
        z = lambda n: jnp.zeros((n,) + g_part.shape[1:], F32)
        return jnp.concatenate([g_part, z(EXPERT_ROW0 - N_EXPERT_GROUPS), e_part,
                                z(ROUTER_ROWS - EXPERT_ROW0 - N_EXPERTS)], axis=0)

    wr = [router_rows(w_rg[l].T, w_re[l].T) for l in range(2)]
    br = [router_rows(b_rg[l][:, None], b_re[l][:, None]) for l in range(2)]

    start = sample_start.astype(jnp.int32)

    x1p, pool_p16 = _pool_prompt(x_prompt, jnp.zeros((B, HALO, D), F32), row(norm_mix[0]), wp, row(ls_pool[0]))
    pos_s = (start[:, None] + jnp.arange(Td, dtype=jnp.int32)[None, :]).reshape(-1)
    x1s, pool_s_t = _pool_sample(x_sample, jnp.transpose(state_pool[0], (1, 0, 2)), start[:, None],
                                 row(norm_mix[0]), wp, row(ls_pool[0]))
    x1s = x1s.reshape(Bd * Td, D)
    pool_s = jnp.transpose(pool_s_t, (1, 0, 2))
    pool_p = pool_p16[:, HALO - POOL_BUF:]

    moe0 = functools.partial(_moe, nf=row(norm_ffn[0]), wr=wr[0], br=br[0], wg=wg, wu=wu, wd=wd, layer=0)
    x2p = moe0(x1p.reshape(B * T, D))
    x2s = moe0(x1s)

    cos_p, sin_p = _rope_tables(jnp.arange(T, dtype=jnp.int32))
    cos_s, sin_s = _rope_tables(pos_s)
    g1 = row(norm_mix[1])
    tmq = 1024
    keep = min(WINDOW, T)
    qp, kdp, vdp, kp, vp = _qkv(x2p, g1, wqkv, row(b_qkv[0]), cos_p, sin_p, tmq, T // tmq, keep)
    qs, kds, vds, ks, vs = _qkv(x2s, g1, wqkv, row(b_qkv[0]), cos_s, sin_s, tmq, 1, tmq)

    op = _attn_prompt(qp, kdp, vdp, _sink_rows(sinks[0], WINDOW), B, T)

    bs = 8
    rows = bs * Td
    W = cache_k.shape[2]
    qrow = np.arange(rows)
    ccol = np.arange(bs * W)
    ncol = np.arange(rows)
    samp_q, t_q = qrow // Td, qrow % Td
    m_cache = (samp_q[:, None] == (ccol // W)[None, :]) & ((ccol % W)[None, :] > t_q[:, None])
    m_new = (samp_q[:, None] == (ncol // Td)[None, :]) & ((ncol % Td)[None, :] <= t_q[:, None])
    amask = jnp.asarray(np.tile(np.concatenate([m_cache, m_new], axis=1).T, (1, 4)), F32)
    kv_cache = (jnp.arange(W, dtype=jnp.int32)[None, :] >= (W - start)[:, None]).reshape(Bd // bs, bs * W)
    kval = jnp.concatenate([kv_cache, jnp.ones((Bd // bs, rows), bool)], axis=1).astype(F32)
    kval = kval.reshape(Bd // bs, bs * W + rows, 1)
    to_stored = lambda c: jnp.transpose(c[0], (0, 2, 3, 1))
    from_stored = lambda c: jnp.transpose(c, (0, 3, 1, 2))[None]
    osamp, nk_s, nv_s = _attn_sample(
        qs, kds, vds, ks, vs, to_stored(cache_k), to_stored(cache_v),
        amask, kval, _sink_rows(sinks[0], rows), bs, Td)

    moe1 = functools.partial(_moe, nf=row(norm_ffn[1]), wr=wr[1], br=br[1], wg=wg, wu=wu, wd=wd, layer=1,
                             final=row(norm_final))
    yp = moe1(x2p, oproj=(op, wo, row(b_o[0])))
    ys = moe1(x2s, oproj=(osamp, wo, row(b_o[0])))

    k_p = kp.reshape(1, B, keep, N_KV_HEADS, HEAD_DIM)
    v_p = vp.reshape(1, B, keep, N_KV_HEADS, HEAD_DIM)
    return (yp.reshape(B, T, D), ys.reshape(Bd, Td, D), pool_p[None], k_p, v_p, pool_s[None],
            from_stored(nk_s), from_stored(nv_s))
```

```python
import functools

import jax
import jax.numpy as jnp
import numpy as np
from jax import lax
from jax.experimental import pallas as pl
from jax.experimental.pallas import tpu as pltpu

F32 = jnp.float32
BF16 = jnp.bfloat16

D_MODEL = 1024
POOL_WINDOWS = (2, 4, 8, 16)
POOL_GROUP_DIM = 256
POOL_BUF = 15
HALO = 16
HEAD_DIM = 64
N_HEADS = 16
N_KV_HEADS = 4
WINDOW = 128
ROPE_THETA = 10000.0
N_EXPERT_GROUPS = 4
EXPERTS_PER_GROUP = 4
N_EXPERTS = 16
D_EXPERT = 256
RMS_EPS = 1e-6
LANES = 128
ROUTER_ROWS = 32
EXPERT_ROW0 = 8
VMEM_LIMIT = 56 * 1024 * 1024
LOG2E = 1.4426950408889634
Q_SCALE = LOG2E * HEAD_DIM ** -0.5


def _rms(x, g):
    ms = jnp.mean(x * x, axis=-1, keepdims=True)
    return x * lax.rsqrt(ms + RMS_EPS) * g


def _pool_project(h, wins, pos, wp_ref, ls):
    G = POOL_GROUP_DIM
    outs = []
    for g, w in enumerate(POOL_WINDOWS):
        cnt = jnp.minimum(w, pos + 1).astype(F32)
        d = wins[g] / cnt - h[:, g * G:(g + 1) * G]
        outs.append(jnp.dot(d.astype(BF16), wp_ref[g], preferred_element_type=F32))
    return jnp.concatenate(outs, axis=-1) * ls


POOL_BLOCK = 128


def _pool_band():
    t = np.arange(POOL_BLOCK)[:, None] + HALO
    k = np.arange(POOL_BLOCK + HALO)[None, :]
    return jnp.asarray(np.stack([(k <= t) & (k > t - w) for w in POOL_WINDOWS]), BF16)


def _pool_prompt_kernel(x_ref, buf_ref, g_ref, wp_ref, ls_ref, band_ref, o_ref, nb_ref, hc, *, tq):
    t = pl.program_id(1)
    G = POOL_GROUP_DIM

    @pl.when(t == 0)
    def _():
        hc[pl.ds(0, HALO), :] = buf_ref[0]

    @pl.when(t > 0)
    def _():
        hc[pl.ds(0, HALO), :] = hc[pl.ds(tq, HALO), :]

    x = x_ref[0]
    h = _rms(x, g_ref[...])
    hc[pl.ds(HALO, tq), :] = h
    hb = hc[...].astype(BF16)
    wins = []
    for g in range(len(POOL_WINDOWS)):
        blocks = [jnp.dot(band_ref[g], hb[b * POOL_BLOCK:(b + 1) * POOL_BLOCK + HALO, g * G:(g + 1) * G],
                          preferred_element_type=F32) for b in range(tq // POOL_BLOCK)]
        wins.append(jnp.concatenate(blocks, axis=0))
    pos = t * tq + lax.broadcasted_iota(jnp.int32, (tq, 1), 0)
    o_ref[0] = x + _pool_project(h, wins, pos, wp_ref, ls_ref[...])

    @pl.when(t == pl.num_programs(1) - 1)
    def _():
        nb_ref[0] = hc[pl.ds(tq, HALO), :]


def _pool_prompt(x, buf16, g, wp, ls, tq=1024):
    B, T, D = x.shape
    return pl.pallas_call(
        functools.partial(_pool_prompt_kernel, tq=tq),
        grid=(B, T // tq),
        in_specs=[
            pl.BlockSpec((1, tq, D), lambda b, t: (b, t, 0)),
            pl.BlockSpec((1, HALO, D), lambda b, t: (b, 0, 0)),
            pl.BlockSpec((1, D), lambda b, t: (0, 0)),
            pl.BlockSpec((4, POOL_GROUP_DIM, POOL_GROUP_DIM), lambda b, t: (0, 0, 0)),
            pl.BlockSpec((1, D), lambda b, t: (0, 0)),
            pl.BlockSpec((4, POOL_BLOCK, POOL_BLOCK + HALO), lambda b, t: (0, 0, 0)),
        ],
        out_specs=[
            pl.BlockSpec((1, tq, D), lambda b, t: (b, t, 0)),
            pl.BlockSpec((1, HALO, D), lambda b, t: (b, 0, 0)),
        ],
        out_shape=[jax.ShapeDtypeStruct((B, T, D), F32), jax.ShapeDtypeStruct((B, HALO, D), F32)],
        scratch_shapes=[pltpu.VMEM((HALO + tq, D), F32)],
        compiler_params=pltpu.CompilerParams(
            dimension_semantics=("arbitrary", "arbitrary"), vmem_limit_bytes=VMEM_LIMIT),
        name="pool_prompt",
    )(x, buf16, g, wp, ls, _pool_band())


def _pool_sample_kernel(x_ref, st_ref, start_ref, g_ref, wp_ref, ls_ref, o_ref, ns_ref, *, bs, t):
    G = POOL_GROUP_DIM
    xs = [x_ref[:, i, :] for i in range(t)]
    hn = [_rms(x, g_ref[...]) for x in xs]
    hist = [st_ref[r] for r in range(POOL_BUF)] + hn

    def doubled(prev, lag, lo):
        out = [None] * len(prev)
        for i in range(len(prev)):
            if i >= lag and prev[i] is not None and prev[i - lag] is not None:
                out[i] = prev[i][:, lo:] + prev[i - lag][:, lo:]
        return out

    s2 = doubled(hist, 1, 0)
    s4 = doubled(s2, 2, G)
    s8 = doubled(s4, 4, G)
    s16 = doubled(s8, 8, G)
    wins = (s2, s4, s8, s16)
    start = start_ref[...]
    ds = [[] for _ in POOL_WINDOWS]
    for i in range(t):
        for g, w in enumerate(POOL_WINDOWS):
            cnt = jnp.minimum(w, start + (i + 1)).astype(F32)
            ds[g].append(wins[g][POOL_BUF + i][:, :G] / cnt - hn[i][:, g * G:(g + 1) * G])
    ys = [jnp.dot(jnp.concatenate(ds[g], axis=0).astype(BF16), wp_ref[g], preferred_element_type=F32)
          for g in range(len(POOL_WINDOWS))]
    y = jnp.concatenate(ys, axis=-1) * ls_ref[...]
    for i in range(t):
        o_ref[:, i, :] = xs[i] + y[i * bs:(i + 1) * bs]
    for r in range(POOL_BUF):
        ns_ref[r] = hist[r + t]


def _pool_sample(x, st, start, g, wp, ls, bs=32):
    Bd, t, D = x.shape
    return pl.pallas_call(
        functools.partial(_pool_sample_kernel, bs=bs, t=t),
        grid=(Bd // bs,),
        in_specs=[
            pl.BlockSpec((bs, t, D), lambda i: (i, 0, 0)),
            pl.BlockSpec((POOL_BUF, bs, D), lambda i: (0, i, 0)),
            pl.BlockSpec((bs, 1), lambda i: (i, 0)),
            pl.BlockSpec((1, D), lambda i: (0, 0)),
            pl.BlockSpec((4, POOL_GROUP_DIM, POOL_GROUP_DIM), lambda i: (0, 0, 0)),
            pl.BlockSpec((1, D), lambda i: (0, 0)),
        ],
        out_specs=[pl.BlockSpec((bs, t, D), lambda i: (i, 0, 0)),
                   pl.BlockSpec((POOL_BUF, bs, D), lambda i: (0, i, 0))],
        out_shape=[jax.ShapeDtypeStruct((Bd, t, D), F32), jax.ShapeDtypeStruct((POOL_BUF, Bd, D), F32)],
        compiler_params=pltpu.CompilerParams(
            dimension_semantics=("arbitrary",), vmem_limit_bytes=VMEM_LIMIT),
        name="pool_sample",
    )(x, st, start, g, wp, ls)


def _route(lg):
    R = lg.shape[1]
    big = jnp.float32(1 << 20)
    neg = jnp.float32(-jnp.inf)
    r8 = lax.broadcasted_iota(jnp.int32, (8, R), 0).astype(F32)
    r16 = lax.broadcasted_iota(jnp.int32, (N_EXPERTS, R), 0).astype(F32)
    is_g = r8 < N_EXPERT_GROUPS
    gl = jnp.where(is_g, lg[0:8], neg)
    m = jnp.max(gl, axis=0, keepdims=True)
    gidx = jnp.min(jnp.where(gl == m, r8, big), axis=0, keepdims=True)
    z = jnp.sum(jnp.where(is_g, jnp.exp(gl - m), 0.0), axis=0, keepdims=True)
    gw = 1.0 / z
    lo = gidx * EXPERTS_PER_GROUP
    in_grp = (r16 >= lo) & (r16 < lo + EXPERTS_PER_GROUP)
    el = jnp.where(in_grp, lg[EXPERT_ROW0:EXPERT_ROW0 + N_EXPERTS], neg)
    v1 = jnp.max(el, axis=0, keepdims=True)
    i1 = jnp.min(jnp.where(el == v1, r16, big), axis=0, keepdims=True)
    el2 = jnp.where(r16 == i1, neg, el)
    v2 = jnp.max(el2, axis=0, keepdims=True)
    i2 = jnp.min(jnp.where(el2 == v2, r16, big), axis=0, keepdims=True)
    t = jnp.exp(v2 - v1)
    w1 = 1.0 / (1.0 + t)
    w2 = t * w1
    e8 = jnp.where(is_g, r8, r8 - EXPERTS_PER_GROUP)
    cw = gw * (jnp.where(e8 == i1 - lo, w1, 0.0) + jnp.where(e8 == i2 - lo, w2, 0.0))
    return gidx, cw


def _moe_kernel(*refs, has_oproj, has_final, tm, bm):
    it = iter(refs)
    x_ref = next(it)
    if has_oproj:
        oin_ref, wo_ref, bo_ref = next(it), next(it), next(it)
    nf_ref, wr_ref, br_ref, wg_ref, wu_ref, wd_ref = (next(it) for _ in range(6))
    if has_final:
        fn_ref = next(it)
    out_ref = next(it)
    hs, cs, ys, sel_s, earlier, act = (next(it) for _ in range(6))
    nblk = tm // bm + N_EXPERT_GROUPS
    cap = nblk * bm
    shift = bm.bit_length() - 1
    qr = 2 * bm
    assert cap % qr == 0 and tm % qr == 0
    nt = (((1,), (1,)), ((), ()))

    @pl.when(pl.program_id(0) == 0)
    def _():
        ri = lax.broadcasted_iota(jnp.int32, (tm, tm), 0)
        ci = lax.broadcasted_iota(jnp.int32, (tm, tm), 1)
        earlier[...] = jnp.where(ri < ci, 1.0, 0.0).astype(BF16)

    x = x_ref[...]
    if has_oproj:
        x = x + jnp.dot(oin_ref[...], wo_ref[...], preferred_element_type=F32) + bo_ref[...]
    out_ref[...] = x
    h = _rms(x, nf_ref[...])
    hb = h.astype(BF16)
    h_lo = (h - hb.astype(F32)).astype(BF16)
    wr = wr_ref[...]
    both = lax.dot_general(wr, hb, nt, preferred_element_type=F32)
    lg = (both[0:ROUTER_ROWS] + both[ROUTER_ROWS:]
          + lax.dot_general(wr[0:ROUTER_ROWS], h_lo, nt, preferred_element_type=F32)) + br_ref[...]
    gidx, cw = _route(lg)
    r8 = lax.broadcasted_iota(jnp.int32, (8, tm), 0)
    cw_hi = cw.astype(BF16).astype(F32)
    cwt = jnp.concatenate([jnp.where(r8 < EXPERTS_PER_GROUP, cw_hi, cw - cw_hi),
                           jnp.zeros((LANES - 8, tm), F32)], axis=0).astype(BF16)

    oh = jnp.where(r8.astype(F32) == gidx, 1.0, 0.0)
    cnt_before = jnp.dot(oh.astype(BF16), earlier[...], preferred_element_type=F32)
    pos_row = jnp.sum(oh * cnt_before, axis=0, keepdims=True)
    start_blk, n_blk = [], []
    off = jnp.int32(0)
    for g in range(N_EXPERT_GROUPS):
        in_g = gidx == g
        n_g = jnp.sum(jnp.where(in_g, 1.0, 0.0)).astype(jnp.int32)
        blocks = (n_g + (bm - 1)) >> shift
        start_blk.append(off)
        n_blk.append(blocks)
        pos_row = pos_row + jnp.where(in_g, (off << shift).astype(F32), 0.0)
        off = off + blocks

    total_blk = off
    main = tm + qr
    rare = total_blk > main // bm

    def sort_rows(r0, n):
        sub = (r0 + lax.broadcasted_iota(jnp.int32, (n, tm), 0)).astype(F32)
        sel = jnp.where(pos_row == sub, 1.0, 0.0).astype(BF16)
        sel_s[r0:r0 + n, :] = sel
        hs[r0:r0 + n, :] = jnp.dot(sel, hb, preferred_element_type=F32).astype(BF16)
        cs[r0:r0 + n, :] = lax.dot_general(sel, cwt, nt, preferred_element_type=F32)

    sort_rows(0, main)
    pl.when(rare)(functools.partial(sort_rows, main, cap - main))
    min_q = tm // qr
    ys[min_q * qr:, :] = jnp.zeros((cap - min_q * qr, D_MODEL), BF16)

    def block_rows(b):
        return pl.ds(pl.multiple_of(b * bm, bm), bm)

    def first_expert(b):
        g = sum(jnp.where(b >= start_blk[i], 1, 0) for i in range(1, N_EXPERT_GROUPS))
        return g * EXPERTS_PER_GROUP

    def gate_up(b):
        e0 = first_expert(b)
        hblk = hs[block_rows(b), :]
        cblk = cs[block_rows(b), :]
        ln = lax.broadcasted_iota(jnp.int32, cblk.shape, 1)
        parts = []
        for e in range(EXPERTS_PER_GROUP):
            ce = jnp.sum(jnp.where((ln == e) | (ln == e + EXPERTS_PER_GROUP), cblk, 0.0), axis=-1, keepdims=True)
            gt = jnp.dot(hblk, wg_ref[e0 + e], preferred_element_type=F32)
            up = jnp.dot(hblk, wu_ref[e0 + e], preferred_element_type=F32)
            parts.append((gt / (1.0 + jnp.exp(-gt)) * up * ce).astype(BF16))
        return jnp.concatenate(parts, axis=-1)

    def down(b, a):
        wd = wd_ref[pl.ds(first_expert(b), EXPERTS_PER_GROUP)].reshape(EXPERTS_PER_GROUP * D_EXPERT, D_MODEL)
        ys[block_rows(b), :] = jnp.dot(a, wd, preferred_element_type=F32).astype(BF16)

    act[...] = gate_up(0)

    def step(b, carry):
        a_prev = act[...]
        act[...] = gate_up(b)
        down(b - 1, a_prev)
        return carry

    lax.fori_loop(1, total_blk, step, 0)
    down(total_blk - 1, act[...])

    def unsort_rows(r0, n):
        out_ref[...] += lax.dot_general(sel_s[r0:r0 + n, :], ys[r0:r0 + n, :],
                                        (((0,), (0,)), ((), ())), preferred_element_type=F32)

    unsort_rows(0, main)
    pl.when(rare)(functools.partial(unsort_rows, main, cap - main))
    if has_final:
        out_ref[...] = _rms(out_ref[...], fn_ref[...])


def _moe(x, nf, wr, br, wg, wu, wd, layer, oproj=None, final=None, tm=512, bm=128):
    N, D = x.shape
    has_oproj = oproj is not None
    has_final = final is not None
    const2 = lambda i: (0, 0)
    of_layer = lambda i: (layer, 0, 0, 0)
    tile = lambda i: (i, 0)
    once = pl.Buffered(1)
    args = [x]
    in_specs = [pl.BlockSpec((tm, D), tile)]
    if has_oproj:
        o, wo, bo = oproj
        args += [o, wo, bo]
        in_specs += [pl.BlockSpec((tm, D), tile), pl.BlockSpec((D, D), const2, pipeline_mode=once),
                     pl.BlockSpec((1, D), const2)]
    wr_hi = wr.astype(BF16)
    wr_lo = (wr - wr_hi.astype(F32)).astype(BF16)
    args += [nf, jnp.concatenate([wr_hi, wr_lo], axis=0), br, wg, wu, wd]
    in_specs += [
        pl.BlockSpec((1, D), const2),
        pl.BlockSpec((2 * ROUTER_ROWS, D), const2, pipeline_mode=once),
        pl.BlockSpec((ROUTER_ROWS, 1), const2),
        pl.BlockSpec((None, N_EXPERTS, D, D_EXPERT), of_layer, pipeline_mode=once),
        pl.BlockSpec((None, N_EXPERTS, D, D_EXPERT), of_layer, pipeline_mode=once),
        pl.BlockSpec((None, N_EXPERTS, D_EXPERT, D), of_layer, pipeline_mode=once),
    ]
    if has_final:
        args.append(final)
        in_specs.append(pl.BlockSpec((1, D), const2))
    cap = (tm // bm + N_EXPERT_GROUPS) * bm
    return pl.pallas_call(
        functools.partial(_moe_kernel, has_oproj=has_oproj, has_final=has_final, tm=tm, bm=bm),
        grid=(N // tm,),
        in_specs=in_specs,
        out_specs=pl.BlockSpec((tm, D), tile),
        out_shape=jax.ShapeDtypeStruct((N, D), F32),
        scratch_shapes=[
            pltpu.VMEM((cap, D), BF16),
            pltpu.VMEM((cap, LANES), F32),
            pltpu.VMEM((cap, D), BF16),
            pltpu.VMEM((cap, tm), BF16),
            pltpu.VMEM((tm, tm), BF16),
            pltpu.VMEM((bm, EXPERTS_PER_GROUP * D_EXPERT), BF16),
        ],
        compiler_params=pltpu.CompilerParams(
            dimension_semantics=("arbitrary",), vmem_limit_bytes=VMEM_LIMIT),
        name="moe",
    )(*args)


def _swap_halves(x, low):
    return jnp.where(low, pltpu.roll(x, LANES - 32, axis=1), pltpu.roll(x, 32, axis=1))


def _qkv_kernel(x_ref, g_ref, w_ref, b_ref, cos_ref, sin_ref, q_ref, kd_ref, vd_ref, k_ref, v_ref, *, keep):
    tm = x_ref.shape[0]
    h = _rms(x_ref[...], g_ref[...]).astype(BF16)
    cos = cos_ref[...]
    sin = sin_ref[...]
    lane = lax.broadcasted_iota(jnp.int32, cos.shape, 1)
    low32 = (lane % HEAD_DIM) < (HEAD_DIM // 2)
    low64 = lane < HEAD_DIM
    nq = N_HEADS * HEAD_DIM
    nk = N_KV_HEADS * HEAD_DIM
    cw = 2 * LANES

    def project(c0):
        y = jnp.dot(h, w_ref[:, c0:c0 + cw], preferred_element_type=F32) + b_ref[:, c0:c0 + cw]
        return [y[:, i * LANES:(i + 1) * LANES] for i in range(cw // LANES)]

    def rope(c):
        return c * cos + _swap_halves(c, low32) * sin

    def dup(c):
        r = pltpu.roll(c, HEAD_DIM, axis=1)
        return jnp.where(low64, c, r), jnp.where(low64, r, c)

    for c0 in range(0, nq, cw):
        for i, c in enumerate(project(c0)):
            lanes = slice(c0 + i * LANES, c0 + (i + 1) * LANES)
            q_ref[:, lanes] = (rope(c) * Q_SCALE).astype(BF16)
    for c0 in range(0, nk, cw):
        for (src, rot, f_ref, d_ref) in ((nq, True, k_ref, kd_ref), (nq + nk, False, v_ref, vd_ref)):
            for i, c in enumerate(project(src + c0)):
                j = c0 // LANES + i
                c = rope(c) if rot else c
                f_ref[:, j * LANES:(j + 1) * LANES] = c[tm - keep:]
                d0, d1 = dup(c)
                d_ref[:, 2 * j * LANES:(2 * j + 1) * LANES] = d0.astype(BF16)
                d_ref[:, (2 * j + 1) * LANES:(2 * j + 2) * LANES] = d1.astype(BF16)


def _qkv(x, g, w, b, cos, sin, tm, n_pos_tiles, keep):
    N, D = x.shape
    nq, nk = N_HEADS * HEAD_DIM, N_KV_HEADS * HEAD_DIM
    const = lambda i: (0, 0)
    tile = lambda i: (i, 0)
    ptile = lambda i: (i % n_pos_tiles, 0)
    seq = lambda i: (i // n_pos_tiles, 0)
    n_seq = N // (tm * n_pos_tiles)
    return pl.pallas_call(
        functools.partial(_qkv_kernel, keep=keep),
        grid=(N // tm,),
        in_specs=[
            pl.BlockSpec((tm, D), tile),
            pl.BlockSpec((1, D), const),
            pl.BlockSpec((D, nq + 2 * nk), const),
            pl.BlockSpec((1, nq + 2 * nk), const),
            pl.BlockSpec((tm, LANES), ptile),
            pl.BlockSpec((tm, LANES), ptile),
        ],
        out_specs=[
            pl.BlockSpec((tm, nq), tile),
            pl.BlockSpec((tm, 2 * nk), tile),
            pl.BlockSpec((tm, 2 * nk), tile),
            pl.BlockSpec((keep, nk), seq),
            pl.BlockSpec((keep, nk), seq),
        ],
        out_shape=[
            jax.ShapeDtypeStruct((N, nq), BF16),
            jax.ShapeDtypeStruct((N, 2 * nk), BF16),
            jax.ShapeDtypeStruct((N, 2 * nk), BF16),
            jax.ShapeDtypeStruct((n_seq * keep, nk), F32),
            jax.ShapeDtypeStruct((n_seq * keep, nk), F32),
        ],
        compiler_params=pltpu.CompilerParams(
            dimension_semantics=("arbitrary",), vmem_limit_bytes=VMEM_LIMIT),
        name="qkv",
    )(x, g, w, b, cos, sin)


def _stack_heads(qc_list):
    lane = lax.broadcasted_iota(jnp.int32, qc_list[0].shape, 1)
    low = lane < HEAD_DIM
    zero = jnp.zeros_like(qc_list[0])
    stack = []
    for qc in qc_list:
        stack.append(jnp.where(low, qc, zero))
        stack.append(jnp.where(low, zero, qc))
    return jnp.concatenate(stack, axis=0)


def _softmax_keys_on_rows(s, sinkrow):
    m = jnp.maximum(jnp.max(s, axis=0, keepdims=True), sinkrow)
    p = jnp.exp2(s - m)
    den = jnp.sum(p, axis=0, keepdims=True) + jnp.exp2(sinkrow - m)
    return (p * (1.0 / den)).astype(BF16)


def _unstack_heads(o, R):
    lowf = lax.broadcasted_iota(jnp.int32, (R, LANES), 1) < HEAD_DIM
    return (jnp.where(lowf, o[0:R], o[R:2 * R]), jnp.where(lowf, o[2 * R:3 * R], o[3 * R:4 * R]))


NT_DIMS = (((1,), (1,)), ((), ()))
TN_DIMS = (((0,), (0,)), ((), ()))
TT_DIMS = (((0,), (1,)), ((), ()))


def _attend(qc_list, kd, vd, bias, sinkrow):
    qs = _stack_heads(qc_list)
    s = lax.dot_general(kd, qs, NT_DIMS, preferred_element_type=F32) + bias
    p = _softmax_keys_on_rows(s, sinkrow)
    o = lax.dot_general(p, vd, TN_DIMS, preferred_element_type=F32)
    return _unstack_heads(o, qc_list[0].shape[0])


def _attn_prompt_kernel(q_ref, kp_ref, kc_ref, vp_ref, vc_ref, sink_ref, o_ref, *, tq):
    n = pl.program_id(1)
    B = WINDOW
    kd = jnp.concatenate([kp_ref[...], kc_ref[...]], axis=0)
    vd = jnp.concatenate([vp_ref[...], vc_ref[...]], axis=0)
    kj = lax.broadcasted_iota(jnp.int32, (2 * B, B), 0)
    qi = lax.broadcasted_iota(jnp.int32, (2 * B, B), 1)
    rel = B + qi - kj
    band = (rel >= 0) & (rel < WINDOW)
    neg = jnp.float32(-jnp.inf)
    band_bias = jnp.concatenate([jnp.where(band, 0.0, neg)] * 4, axis=1)
    first_bias = jnp.concatenate([jnp.where(band & ((kj >= B) | (n > 0)), 0.0, neg)] * 4, axis=1)
    for j in range(tq // B):
        bias = first_bias if j == 0 else band_bias
        for g in range(N_KV_HEADS):
            qc = [q_ref[j * B:(j + 1) * B, (2 * g + i) * LANES:(2 * g + i + 1) * LANES] for i in range(2)]
            kg = kd[j * B:(j + 2) * B, g * LANES:(g + 1) * LANES]
            vg = vd[j * B:(j + 2) * B, g * LANES:(g + 1) * LANES]
            o0, o1 = _attend(qc, kg, vg, bias, sink_ref[g])
            o_ref[j * B:(j + 1) * B, 2 * g * LANES:(2 * g + 1) * LANES] = o0.astype(BF16)
            o_ref[j * B:(j + 1) * B, (2 * g + 1) * LANES:(2 * g + 2) * LANES] = o1.astype(BF16)


def _attn_prompt(q, kd, vd, sinkrow, batch, seq, tq=512):
    N = q.shape[0]
    nt = seq // tq
    r = tq // WINDOW
    cur = lambda b, n: (b * nt + n, 0)
    prev = lambda b, n: (jnp.maximum((b * nt + n) * r - 1, 0), 0)
    kvw = 2 * N_KV_HEADS * HEAD_DIM
    return pl.pallas_call(
        functools.partial(_attn_prompt_kernel, tq=tq),
        grid=(batch, nt),
        in_specs=[
            pl.BlockSpec((tq, N_HEADS * HEAD_DIM), cur),
            pl.BlockSpec((WINDOW, kvw), prev),
            pl.BlockSpec((tq, kvw), cur),
            pl.BlockSpec((WINDOW, kvw), prev),
            pl.BlockSpec((tq, kvw), cur),
            pl.BlockSpec((N_KV_HEADS, 1, 4 * WINDOW), lambda b, n: (0, 0, 0)),
        ],
        out_specs=pl.BlockSpec((tq, N_HEADS * HEAD_DIM), cur),
        out_shape=jax.ShapeDtypeStruct((N, N_HEADS * HEAD_DIM), BF16),
        compiler_params=pltpu.CompilerParams(
            dimension_semantics=("arbitrary", "arbitrary"), vmem_limit_bytes=VMEM_LIMIT),
        name="attn_prompt",
    )(q, kd, kd, vd, vd, sinkrow)


def _attn_sample_kernel(q_ref, kdn_ref, vdn_ref, kn_ref, vn_ref, ck_ref, cv_ref, mask_ref, kval_ref, sink_ref,
                        o_ref, nk_ref, nv_ref, *, bs, t):
    W = WINDOW
    rows = bs * t
    bias = jnp.where((mask_ref[...] > 0) & (kval_ref[0] > 0), 0.0, -jnp.inf).astype(F32)
    lane = lax.broadcasted_iota(jnp.int32, (HEAD_DIM, W), 1)
    fresh = lane >= W - t

    def new_cols(n_ref):
        n = jnp.concatenate([n_ref[...], jnp.zeros((LANES - rows, N_KV_HEADS * HEAD_DIM), F32)], axis=0)
        nt_ = jnp.transpose(n)
        return [pltpu.roll(nt_, (W - t - s * t) % LANES, axis=1) for s in range(bs)]

    k_cols, v_cols = new_cols(kn_ref), new_cols(vn_ref)
    for g in range(N_KV_HEADS):
        hd = slice(g * HEAD_DIM, (g + 1) * HEAD_DIM)
        kt = jnp.concatenate([ck_ref[s, g] for s in range(bs)], axis=1).astype(BF16)
        vt = jnp.concatenate([cv_ref[s, g] for s in range(bs)], axis=1).astype(BF16)
        kt2 = jnp.concatenate([kt, kt], axis=0)
        vt2 = jnp.concatenate([vt, vt], axis=0)
        qs = _stack_heads([q_ref[:, (2 * g + i) * LANES:(2 * g + i + 1) * LANES] for i in range(2)])
        s_old = lax.dot_general(kt2, qs, TT_DIMS, preferred_element_type=F32)
        s_new = lax.dot_general(kdn_ref[:, g * LANES:(g + 1) * LANES], qs, NT_DIMS, preferred_element_type=F32)
        p = _softmax_keys_on_rows(jnp.concatenate([s_old, s_new], axis=0) + bias, sink_ref[g])
        o = (lax.dot_general(p[:bs * W], vt2, TT_DIMS, preferred_element_type=F32)
             + lax.dot_general(p[bs * W:], vdn_ref[:, g * LANES:(g + 1) * LANES], TN_DIMS,
                               preferred_element_type=F32))
        o0, o1 = _unstack_heads(o, rows)
        o_ref[:, 2 * g * LANES:(2 * g + 1) * LANES] = o0.astype(BF16)
        o_ref[:, (2 * g + 1) * LANES:(2 * g + 2) * LANES] = o1.astype(BF16)
        for s in range(bs):
            nk_ref[s, g] = jnp.where(fresh, k_cols[s][hd], pltpu.roll(ck_ref[s, g], W - t, axis=1))
            nv_ref[s, g] = jnp.where(fresh, v_cols[s][hd], pltpu.roll(cv_ref[s, g], W - t, axis=1))


def _attn_sample(q, kdn, vdn, kn, vn, ck, cv, mask, kval, sinkrow, bs, t):
    Bd = ck.shape[0]
    rows = bs * t
    kvw = N_KV_HEADS * HEAD_DIM
    S = bs * WINDOW + rows
    tile = lambda i: (i, 0)
    tile3 = lambda i: (i, 0, 0)
    tile4 = lambda i: (i, 0, 0, 0)
    cblk = (bs, N_KV_HEADS, HEAD_DIM, WINDOW)
    return pl.pallas_call(
        functools.partial(_attn_sample_kernel, bs=bs, t=t),
        grid=(Bd // bs,),
        in_specs=[
            pl.BlockSpec((rows, N_HEADS * HEAD_DIM), tile),
            pl.BlockSpec((rows, 2 * kvw), tile),
            pl.BlockSpec((rows, 2 * kvw), tile),
            pl.BlockSpec((rows, kvw), tile),
            pl.BlockSpec((rows, kvw), tile),
            pl.BlockSpec(cblk, tile4),
            pl.BlockSpec(cblk, tile4),
            pl.BlockSpec((S, 4 * rows), lambda i: (0, 0)),
            pl.BlockSpec((1, S, 1), tile3),
            pl.BlockSpec((N_KV_HEADS, 1, 4 * rows), lambda i: (0, 0, 0)),
        ],
        out_specs=[
            pl.BlockSpec((rows, N_HEADS * HEAD_DIM), tile),
            pl.BlockSpec(cblk, tile4),
            pl.BlockSpec(cblk, tile4),
        ],
        out_shape=[
            jax.ShapeDtypeStruct((Bd * t, N_HEADS * HEAD_DIM), BF16),
            jax.ShapeDtypeStruct((Bd,) + cblk[1:], F32),
            jax.ShapeDtypeStruct((Bd,) + cblk[1:], F32),
        ],
        compiler_params=pltpu.CompilerParams(
            dimension_semantics=("arbitrary",), vmem_limit_bytes=VMEM_LIMIT),
        name="attn_sample",
    )(q, kdn, vdn, kn, vn, ck, cv, mask, kval, sinkrow)


def _rope_tables(pos):
    inv = ROPE_THETA ** (-jnp.arange(0, HEAD_DIM, 2, dtype=F32) / HEAD_DIM)
    inv_l = jnp.tile(inv, LANES // (HEAD_DIM // 2))
    sign = jnp.asarray(np.where((np.arange(LANES) % HEAD_DIM) < HEAD_DIM // 2, -1.0, 1.0), F32)
    ang = pos.astype(F32)[:, None] * inv_l[None, :]
    return jnp.cos(ang), jnp.sin(ang) * sign[None, :]


def _sink_rows(sinks, rows):
    s = (sinks.astype(F32) * LOG2E).reshape(N_KV_HEADS, 1, N_HEADS // N_KV_HEADS, 1)
    return jnp.broadcast_to(s, (N_KV_HEADS, 1, 4, rows)).reshape(N_KV_HEADS, 1, 4 * rows)


def kernel(x_prompt, x_sample, state_pool, cache_k, cache_v, sample_start, norm_mix, norm_ffn, norm_final,
           w_pool, ls_pool, w_qkv, b_qkv, sinks, w_o, b_o, w_rg, b_rg, w_re, b_re, w_gate, w_up, w_down):
    B, T, D = x_prompt.shape
    Bd, Td, _ = x_sample.shape
    kvw = N_KV_HEADS * HEAD_DIM
    row = lambda v: v.reshape(1, -1).astype(F32)

    wp = w_pool[0].astype(BF16)
    wqkv = w_qkv[0].astype(BF16)
    wo = w_o[0].astype(BF16)
    wg, wu, wd = w_gate.astype(BF16), w_up.astype(BF16), w_down.astype(BF16)
    def router_rows(g_part, e_part):
        z = lambda n: jnp.zeros((n,) + g_part.shape[1:], F32)
        return jnp.concatenate([g_part, z(EXPERT_ROW0 - N_EXPERT_GROUPS), e_part,
                                z(ROUTER_ROWS - EXPERT_ROW0 - N_EXPERTS)], axis=0)

    wr = [router_rows(w_rg[l].T, w_re[l].T) for l in range(2)]
    br = [router_rows(b_rg[l][:, None], b_re[l][:, None]) for l in range(2)]

    start = sample_start.astype(jnp.int32)

    x1p, pool_p16 = _pool_prompt(x_prompt, jnp.zeros((B, HALO, D), F32), row(norm_mix[0]), wp, row(ls_pool[0]))
    pos_s = (start[:, None] + jnp.arange(Td, dtype=jnp.int32)[None, :]).reshape(-1)
    x1s, pool_s_t = _pool_sample(x_sample, jnp.transpose(state_pool[0], (1, 0, 2)), start[:, None],
                                 row(norm_mix[0]), wp, row(ls_pool[0]))
    x1s = x1s.reshape(Bd * Td, D)
    pool_s = jnp.transpose(pool_s_t, (1, 0, 2))
    pool_p = pool_p16[:, HALO - POOL_BUF:]

    moe0 = functools.partial(_moe, nf=row(norm_ffn[0]), wr=wr[0], br=br[0], wg=wg, wu=wu, wd=wd, layer=0)
    x2p = moe0(x1p.reshape(B * T, D))
    x2s = moe0(x1s)

    cos_p, sin_p = _rope_tables(jnp.arange(T, dtype=jnp.int32))
    cos_s, sin_s = _rope_tables(pos_s)
    g1 = row(norm_mix[1])
    tmq = 1024
    keep = min(WINDOW, T)
    qp, kdp, vdp, kp, vp = _qkv(x2p, g1, wqkv, row(b_qkv[0]), cos_p, sin_p, tmq, T // tmq, keep)
    qs, kds, vds, ks, vs = _qkv(x2s, g1, wqkv, row(b_qkv[0]), cos_s, sin_s, tmq, 1, tmq)

    op = _attn_prompt(qp, kdp, vdp, _sink_rows(sinks[0], WINDOW), B, T)

    bs = 8
    rows = bs * Td
    W = cache_k.shape[2]
    qrow = np.arange(rows)
    ccol = np.arange(bs * W)
    ncol = np.arange(rows)
    samp_q, t_q = qrow // Td, qrow % Td
    m_cache = (samp_q[:, None] == (ccol // W)[None, :]) & ((ccol % W)[None, :] > t_q[:, None])
    m_new = (samp_q[:, None] == (ncol // Td)[None, :]) & ((ncol % Td)[None, :] <= t_q[:, None])
    amask = jnp.asarray(np.tile(np.concatenate([m_cache, m_new], axis=1).T, (1, 4)), F32)
    kv_cache = (jnp.arange(W, dtype=jnp.int32)[None, :] >= (W - start)[:, None]).reshape(Bd // bs, bs * W)
    kval = jnp.concatenate([kv_cache, jnp.ones((Bd // bs, rows), bool)], axis=1).astype(F32)
    kval = kval.reshape(Bd // bs, bs * W + rows, 1)
    to_stored = lambda c: jnp.transpose(c[0], (0, 2, 3, 1))
    from_stored = lambda c: jnp.transpose(c, (0, 3, 1, 2))[None]
    osamp, nk_s, nv_s = _attn_sample(
        qs, kds, vds, ks, vs, to_stored(cache_k), to_stored(cache_v),
        amask, kval, _sink_rows(sinks[0], rows), bs, Td)

    moe1 = functools.partial(_moe, nf=row(norm_ffn[1]), wr=wr[1], br=br[1], wg=wg, wu=wu, wd=wd, layer=1,
                             final=row(norm_final))
    yp = moe1(x2p, oproj=(op, wo, row(b_o[0])))
    ys = moe1(x2s, oproj=(osamp, wo, row(b_o[0])))

    k_p = kp.reshape(1, B, keep, N_KV_HEADS, HEAD_DIM)
    v_p = vp.reshape(1, B, keep, N_KV_HEADS, HEAD_DIM)
    return (yp.reshape(B, T, D), ys.reshape(Bd, Td, D), pool_p[None], k_p, v_p, pool_s[None],
            from_stored(nk_s), from_stored(nv_s))
```

```python
import functools

import jax
import jax.numpy as jnp
import numpy as np
from jax import lax
from jax.experimental import pallas as pl
from jax.experimental.pallas import tpu as pltpu

F32 = jnp.float32
BF16 = jnp.bfloat16

D_MODEL = 1024
POOL_WINDOWS = (2, 4, 8, 16)
POOL_GROUP_DIM = 256
POOL_BUF = 15
HALO = 16
HEAD_DIM = 64
N_HEADS = 16
N_KV_HEADS = 4
WINDOW = 128
ROPE_THETA = 10000.0
N_EXPERT_GROUPS = 4
EXPERTS_PER_GROUP = 4
N_EXPERTS = 16
D_EXPERT = 256
RMS_EPS = 1e-6
LANES = 128
ROUTER_ROWS = 32
EXPERT_ROW0 = 8
VMEM_LIMIT = 56 * 1024 * 1024
LOG2E = 1.4426950408889634
Q_SCALE = LOG2E * HEAD_DIM ** -0.5


def _rms(x, g):
    ms = jnp.mean(x * x, axis=-1, keepdims=True)
    return x * lax.rsqrt(ms + RMS_EPS) * g


def _pool_project(h, wins, pos, wp_ref, ls):
    G = POOL_GROUP_DIM
    outs = []
    for g, w in enumerate(POOL_WINDOWS):
        cnt = jnp.minimum(w, pos + 1).astype(F32)
        d = wins[g] / cnt - h[:, g * G:(g + 1) * G]
        outs.append(jnp.dot(d.astype(BF16), wp_ref[g], preferred_element_type=F32))
    return jnp.concatenate(outs, axis=-1) * ls


POOL_BLOCK = 128


def _pool_band():
    t = np.arange(POOL_BLOCK)[:, None] + HALO
    k = np.arange(POOL_BLOCK + HALO)[None, :]
    return jnp.asarray(np.stack([(k <= t) & (k > t - w) for w in POOL_WINDOWS]), BF16)


def _pool_prompt_kernel(x_ref, buf_ref, g_ref, wp_ref, ls_ref, band_ref, o_ref, nb_ref, hc, *, tq):
    t = pl.program_id(1)
    G = POOL_GROUP_DIM

    @pl.when(t == 0)
    def _():
        hc[pl.ds(0, HALO), :] = buf_ref[0]

    @pl.when(t > 0)
    def _():
        hc[pl.ds(0, HALO), :] = hc[pl.ds(tq, HALO), :]

    x = x_ref[0]
    h = _rms(x, g_ref[...])
    hc[pl.ds(HALO, tq), :] = h
    hb = hc[...].astype(BF16)
    wins = []
    for g in range(len(POOL_WINDOWS)):
        blocks = [jnp.dot(band_ref[g], hb[b * POOL_BLOCK:(b + 1) * POOL_BLOCK + HALO, g * G:(g + 1) * G],
                          preferred_element_type=F32) for b in range(tq // POOL_BLOCK)]
        wins.append(jnp.concatenate(blocks, axis=0))
    pos = t * tq + lax.broadcasted_iota(jnp.int32, (tq, 1), 0)
    o_ref[0] = x + _pool_project(h, wins, pos, wp_ref, ls_ref[...])

    @pl.when(t == pl.num_programs(1) - 1)
    def _():
        nb_ref[0] = hc[pl.ds(tq, HALO), :]


def _pool_prompt(x, buf16, g, wp, ls, tq=1024):
    B, T, D = x.shape
    return pl.pallas_call(
        functools.partial(_pool_prompt_kernel, tq=tq),
        grid=(B, T // tq),
        in_specs=[
            pl.BlockSpec((1, tq, D), lambda b, t: (b, t, 0)),
            pl.BlockSpec((1, HALO, D), lambda b, t: (b, 0, 0)),
            pl.BlockSpec((1, D), lambda b, t: (0, 0)),
            pl.BlockSpec((4, POOL_GROUP_DIM, POOL_GROUP_DIM), lambda b, t: (0, 0, 0)),
            pl.BlockSpec((1, D), lambda b, t: (0, 0)),
            pl.BlockSpec((4, POOL_BLOCK, POOL_BLOCK + HALO), lambda b, t: (0, 0, 0)),
        ],
        out_specs=[
            pl.BlockSpec((1, tq, D), lambda b, t: (b, t, 0)),
            pl.BlockSpec((1, HALO, D), lambda b, t: (b, 0, 0)),
        ],
        out_shape=[jax.ShapeDtypeStruct((B, T, D), F32), jax.ShapeDtypeStruct((B, HALO, D), F32)],
        scratch_shapes=[pltpu.VMEM((HALO + tq, D), F32)],
        compiler_params=pltpu.CompilerParams(
            dimension_semantics=("arbitrary", "arbitrary"), vmem_limit_bytes=VMEM_LIMIT),
        name="pool_prompt",
    )(x, buf16, g, wp, ls, _pool_band())


def _pool_sample_kernel(x_ref, st_ref, start_ref, g_ref, wp_ref, ls_ref, o_ref, ns_ref, *, bs, t):
    G = POOL_GROUP_DIM
    xs = [x_ref[:, i, :] for i in range(t)]
    hn = [_rms(x, g_ref[...]) for x in xs]
    hist = [st_ref[r] for r in range(POOL_BUF)] + hn

    def doubled(prev, lag, lo):
        out = [None] * len(prev)
        for i in range(len(prev)):
            if i >= lag and prev[i] is not None and prev[i - lag] is not None:
                out[i] = prev[i][:, lo:] + prev[i - lag][:, lo:]
        return out

    s2 = doubled(hist, 1, 0)
    s4 = doubled(s2, 2, G)
    s8 = doubled(s4, 4, G)
    s16 = doubled(s8, 8, G)
    wins = (s2, s4, s8, s16)
    start = start_ref[...]
    ds = [[] for _ in POOL_WINDOWS]
    for i in range(t):
        for g, w in enumerate(POOL_WINDOWS):
            cnt = jnp.minimum(w, start + (i + 1)).astype(F32)
            ds[g].append(wins[g][POOL_BUF + i][:, :G] / cnt - hn[i][:, g * G:(g + 1) * G])
    ys = [jnp.dot(jnp.concatenate(ds[g], axis=0).astype(BF16), wp_ref[g], preferred_element_type=F32)
          for g in range(len(POOL_WINDOWS))]
    y = jnp.concatenate(ys, axis=-1) * ls_ref[...]
    for i in range(t):
        o_ref[:, i, :] = xs[i] + y[i * bs:(i + 1) * bs]
    for r in range(POOL_BUF):
        ns_ref[r] = hist[r + t]


def _pool_sample(x, st, start, g, wp, ls, bs=32):
    Bd, t, D = x.shape
    return pl.pallas_call(
        functools.partial(_pool_sample_kernel, bs=bs, t=t),
        grid=(Bd // bs,),
        in_specs=[
            pl.BlockSpec((bs, t, D), lambda i: (i, 0, 0)),
            pl.BlockSpec((POOL_BUF, bs, D), lambda i: (0, i, 0)),
            pl.BlockSpec((bs, 1), lambda i: (i, 0)),
            pl.BlockSpec((1, D), lambda i: (0, 0)),
            pl.BlockSpec((4, POOL_GROUP_DIM, POOL_GROUP_DIM), lambda i: (0, 0, 0)),
            pl.BlockSpec((1, D), lambda i: (0, 0)),
        ],
        out_specs=[pl.BlockSpec((bs, t, D), lambda i: (i, 0, 0)),
                   pl.BlockSpec((POOL_BUF, bs, D), lambda i: (0, i, 0))],
        out_shape=[jax.ShapeDtypeStruct((Bd, t, D), F32), jax.ShapeDtypeStruct((POOL_BUF, Bd, D), F32)],
        compiler_params=pltpu.CompilerParams(
            dimension_semantics=("arbitrary",), vmem_limit_bytes=VMEM_LIMIT),
        name="pool_sample",
    )(x, st, start, g, wp, ls)


def _route(lg):
    R = lg.shape[1]
    big = jnp.float32(1 << 20)
    neg = jnp.float32(-jnp.inf)
    r8 = lax.broadcasted_iota(jnp.int32, (8, R), 0).astype(F32)
    r16 = lax.broadcasted_iota(jnp.int32, (N_EXPERTS, R), 0).astype(F32)
    is_g = r8 < N_EXPERT_GROUPS
    gl = jnp.where(is_g, lg[0:8], neg)
    m = jnp.max(gl, axis=0, keepdims=True)
    gidx = jnp.min(jnp.where(gl == m, r8, big), axis=0, keepdims=True)
    z = jnp.sum(jnp.where(is_g, jnp.exp(gl - m), 0.0), axis=0, keepdims=True)
    gw = 1.0 / z
    lo = gidx * EXPERTS_PER_GROUP
    in_grp = (r16 >= lo) & (r16 < lo + EXPERTS_PER_GROUP)
    el = jnp.where(in_grp, lg[EXPERT_ROW0:EXPERT_ROW0 + N_EXPERTS], neg)
    v1 = jnp.max(el, axis=0, keepdims=True)
    i1 = jnp.min(jnp.where(el == v1, r16, big), axis=0, keepdims=True)
    el2 = jnp.where(r16 == i1, neg, el)
    v2 = jnp.max(el2, axis=0, keepdims=True)
    i2 = jnp.min(jnp.where(el2 == v2, r16, big), axis=0, keepdims=True)
    t = jnp.exp(v2 - v1)
    w1 = 1.0 / (1.0 + t)
    w2 = t * w1
    e8 = jnp.where(is_g, r8, r8 - EXPERTS_PER_GROUP)
    cw = gw * (jnp.where(e8 == i1 - lo, w1, 0.0) + jnp.where(e8 == i2 - lo, w2, 0.0))
    return gidx, cw


def _moe_kernel(*refs, has_oproj, has_final, tm, bm):
    it = iter(refs)
    x_ref = next(it)
    if has_oproj:
        oin_ref, wo_ref, bo_ref = next(it), next(it), next(it)
    nf_ref, wr_ref, br_ref, wg_ref, wu_ref, wd_ref = (next(it) for _ in range(6))
    if has_final:
        fn_ref = next(it)
    out_ref = next(it)
    hs, cs, ys, sel_s, earlier = (next(it) for _ in range(5))
    min_blk = -(-tm // bm)
    nblk = tm // bm + N_EXPERT_GROUPS
    cap = nblk * bm
    assert bm % 16 == 0
    nt = (((1,), (1,)), ((), ()))

    @pl.when(pl.program_id(0) == 0)
    def _():
        ri = lax.broadcasted_iota(jnp.int32, (tm, tm), 0)
        ci = lax.broadcasted_iota(jnp.int32, (tm, tm), 1)
        earlier[...] = jnp.where(ri < ci, 1.0, 0.0).astype(BF16)

    x = x_ref[...]
    if has_oproj:
        x = x + jnp.dot(oin_ref[...], wo_ref[...], preferred_element_type=F32) + bo_ref[...]
    out_ref[...] = x
    h = _rms(x, nf_ref[...])
    hb = h.astype(BF16)
    h_lo = (h - hb.astype(F32)).astype(BF16)
    wr = wr_ref[...]
    both = lax.dot_general(wr, hb, nt, preferred_element_type=F32)
    lg = (both[0:ROUTER_ROWS] + both[ROUTER_ROWS:]
          + lax.dot_general(wr[0:ROUTER_ROWS], h_lo, nt, preferred_element_type=F32)) + br_ref[...]
    gidx, cw = _route(lg)
    r8 = lax.broadcasted_iota(jnp.int32, (8, tm), 0)
    cw_hi = cw.astype(BF16).astype(F32)
    cwt = jnp.concatenate([jnp.where(r8 < EXPERTS_PER_GROUP, cw_hi, cw - cw_hi),
                           jnp.zeros((LANES - 8, tm), F32)], axis=0).astype(BF16)

    oh = jnp.where(r8.astype(F32) == gidx, 1.0, 0.0)
    cnt_before = jnp.dot(oh.astype(BF16), earlier[...], preferred_element_type=F32)
    pos_row = jnp.sum(oh * cnt_before, axis=0, keepdims=True)
    start_blk, n_blk = [], []
    off = jnp.int32(0)
    for g in range(N_EXPERT_GROUPS):
        in_g = gidx == g
        n_g = jnp.sum(jnp.where(in_g, 1.0, 0.0)).astype(jnp.int32)
        blocks = sum(jnp.where(n_g > k * bm, 1, 0) for k in range(min_blk))
        start_blk.append(off)
        n_blk.append(blocks)
        pos_row = pos_row + jnp.where(in_g, (off * bm).astype(F32), 0.0)
        off = off + blocks

    total_blk = off
    main = (min_blk + 1) * bm
    rare = total_blk > min_blk + 1

    def sort_rows(r0, n):
        sub = (r0 + lax.broadcasted_iota(jnp.int32, (n, tm), 0)).astype(F32)
        sel = jnp.where(pos_row == sub, 1.0, 0.0).astype(BF16)
        sel_s[r0:r0 + n, :] = sel
        hs[r0:r0 + n, :] = jnp.dot(sel, hb, preferred_element_type=F32).astype(BF16)
        cs[r0:r0 + n, :] = lax.dot_general(sel, cwt, nt, preferred_element_type=F32)

    sort_rows(0, main)
    pl.when(rare)(functools.partial(sort_rows, main, cap - main))
    ys[min_blk * bm:, :] = jnp.zeros((cap - min_blk * bm, D_MODEL), BF16)

    for g in range(N_EXPERT_GROUPS):
        wd = wd_ref[g * EXPERTS_PER_GROUP:(g + 1) * EXPERTS_PER_GROUP].reshape(
            EXPERTS_PER_GROUP * D_EXPERT, D_MODEL)

        def block(b, carry, g=g, wd=wd):
            rows = pl.ds(pl.multiple_of(b * bm, 16), bm)
            hblk = hs[rows, :]
            cblk = cs[rows, :]
            ln = lax.broadcasted_iota(jnp.int32, cblk.shape, 1)
            parts = []
            for e in range(EXPERTS_PER_GROUP):
                ce = jnp.sum(jnp.where((ln == e) | (ln == e + EXPERTS_PER_GROUP), cblk, 0.0),
                             axis=-1, keepdims=True)
                gt = jnp.dot(hblk, wg_ref[g * EXPERTS_PER_GROUP + e], preferred_element_type=F32)
                up = jnp.dot(hblk, wu_ref[g * EXPERTS_PER_GROUP + e], preferred_element_type=F32)
                a = gt / (1.0 + jnp.exp(-gt)) * up * ce
                parts.append(a.astype(BF16))
            a_all = jnp.concatenate(parts, axis=-1)
            ys[rows, :] = jnp.dot(a_all, wd, preferred_element_type=F32).astype(BF16)
            return carry

        lax.fori_loop(start_blk[g], start_blk[g] + n_blk[g], block, 0)

    def unsort_rows(r0, n):
        out_ref[...] += lax.dot_general(sel_s[r0:r0 + n, :], ys[r0:r0 + n, :],
                                        (((0,), (0,)), ((), ())), preferred_element_type=F32)

    unsort_rows(0, main)
    pl.when(rare)(functools.partial(unsort_rows, main, cap - main))
    if has_final:
        out_ref[...] = _rms(out_ref[...], fn_ref[...])


def _moe(x, nf, wr, br, wg, wu, wd, layer, oproj=None, final=None, tm=512, bm=144):
    N, D = x.shape
    has_oproj = oproj is not None
    has_final = final is not None
    const2 = lambda i: (0, 0)
    of_layer = lambda i: (layer, 0, 0, 0)
    tile = lambda i: (i, 0)
    once = pl.Buffered(1)
    args = [x]
    in_specs = [pl.BlockSpec((tm, D), tile)]
    if has_oproj:
        o, wo, bo = oproj
        args += [o, wo, bo]
        in_specs += [pl.BlockSpec((tm, D), tile), pl.BlockSpec((D, D), const2, pipeline_mode=once),
                     pl.BlockSpec((1, D), const2)]
    wr_hi = wr.astype(BF16)
    wr_lo = (wr - wr_hi.astype(F32)).astype(BF16)
    args += [nf, jnp.concatenate([wr_hi, wr_lo], axis=0), br, wg, wu, wd]
    in_specs += [
        pl.BlockSpec((1, D), const2),
        pl.BlockSpec((2 * ROUTER_ROWS, D), const2, pipeline_mode=once),
        pl.BlockSpec((ROUTER_ROWS, 1), const2),
        pl.BlockSpec((None, N_EXPERTS, D, D_EXPERT), of_layer, pipeline_mode=once),
        pl.BlockSpec((None, N_EXPERTS, D, D_EXPERT), of_layer, pipeline_mode=once),
        pl.BlockSpec((None, N_EXPERTS, D_EXPERT, D), of_layer, pipeline_mode=once),
    ]
    if has_final:
        args.append(final)
        in_specs.append(pl.BlockSpec((1, D), const2))
    cap = (tm // bm + N_EXPERT_GROUPS) * bm
    return pl.pallas_call(
        functools.partial(_moe_kernel, has_oproj=has_oproj, has_final=has_final, tm=tm, bm=bm),
        grid=(N // tm,),
        in_specs=in_specs,
        out_specs=pl.BlockSpec((tm, D), tile),
        out_shape=jax.ShapeDtypeStruct((N, D), F32),
        scratch_shapes=[
            pltpu.VMEM((cap, D), BF16),
            pltpu.VMEM((cap, LANES), F32),
            pltpu.VMEM((cap, D), BF16),
            pltpu.VMEM((cap, tm), BF16),
            pltpu.VMEM((tm, tm), BF16),
        ],
        compiler_params=pltpu.CompilerParams(
            dimension_semantics=("arbitrary",), vmem_limit_bytes=VMEM_LIMIT),
        name="moe",
    )(*args)


def _swap_halves(x, low):
    return jnp.where(low, pltpu.roll(x, LANES - 32, axis=1), pltpu.roll(x, 32, axis=1))


def _qkv_kernel(x_ref, g_ref, w_ref, b_ref, cos_ref, sin_ref, q_ref, kd_ref, vd_ref, k_ref, v_ref, *, keep):
    tm = x_ref.shape[0]
    h = _rms(x_ref[...], g_ref[...]).astype(BF16)
    cos = cos_ref[...]
    sin = sin_ref[...]
    lane = lax.broadcasted_iota(jnp.int32, cos.shape, 1)
    low32 = (lane % HEAD_DIM) < (HEAD_DIM // 2)
    low64 = lane < HEAD_DIM
    nq = N_HEADS * HEAD_DIM
    nk = N_KV_HEADS * HEAD_DIM
    cw = 2 * LANES

    def project(c0):
        y = jnp.dot(h, w_ref[:, c0:c0 + cw], preferred_element_type=F32) + b_ref[:, c0:c0 + cw]
        return [y[:, i * LANES:(i + 1) * LANES] for i in range(cw // LANES)]

    def rope(c):
        return c * cos + _swap_halves(c, low32) * sin

    def dup(c):
        r = pltpu.roll(c, HEAD_DIM, axis=1)
        return jnp.where(low64, c, r), jnp.where(low64, r, c)

    for c0 in range(0, nq, cw):
        for i, c in enumerate(project(c0)):
            lanes = slice(c0 + i * LANES, c0 + (i + 1) * LANES)
            q_ref[:, lanes] = (rope(c) * Q_SCALE).astype(BF16)
    for c0 in range(0, nk, cw):
        for (src, rot, f_ref, d_ref) in ((nq, True, k_ref, kd_ref), (nq + nk, False, v_ref, vd_ref)):
            for i, c in enumerate(project(src + c0)):
                j = c0 // LANES + i
                c = rope(c) if rot else c
                f_ref[:, j * LANES:(j + 1) * LANES] = c[tm - keep:]
                d0, d1 = dup(c)
                d_ref[:, 2 * j * LANES:(2 * j + 1) * LANES] = d0.astype(BF16)
                d_ref[:, (2 * j + 1) * LANES:(2 * j + 2) * LANES] = d1.astype(BF16)


def _qkv(x, g, w, b, cos, sin, tm, n_pos_tiles, keep):
    N, D = x.shape
    nq, nk = N_HEADS * HEAD_DIM, N_KV_HEADS * HEAD_DIM
    const = lambda i: (0, 0)
    tile = lambda i: (i, 0)
    ptile = lambda i: (i % n_pos_tiles, 0)
    seq = lambda i: (i // n_pos_tiles, 0)
    n_seq = N // (tm * n_pos_tiles)
    return pl.pallas_call(
        functools.partial(_qkv_kernel, keep=keep),
        grid=(N // tm,),
        in_specs=[
            pl.BlockSpec((tm, D), tile),
            pl.BlockSpec((1, D), const),
            pl.BlockSpec((D, nq + 2 * nk), const),
            pl.BlockSpec((1, nq + 2 * nk), const),
            pl.BlockSpec((tm, LANES), ptile),
            pl.BlockSpec((tm, LANES), ptile),
        ],
        out_specs=[
            pl.BlockSpec((tm, nq), tile),
            pl.BlockSpec((tm, 2 * nk), tile),
            pl.BlockSpec((tm, 2 * nk), tile),
            pl.BlockSpec((keep, nk), seq),
            pl.BlockSpec((keep, nk), seq),
        ],
        out_shape=[
            jax.ShapeDtypeStruct((N, nq), BF16),
            jax.ShapeDtypeStruct((N, 2 * nk), BF16),
            jax.ShapeDtypeStruct((N, 2 * nk), BF16),
            jax.ShapeDtypeStruct((n_seq * keep, nk), F32),
            jax.ShapeDtypeStruct((n_seq * keep, nk), F32),
        ],
        compiler_params=pltpu.CompilerParams(
            dimension_semantics=("arbitrary",), vmem_limit_bytes=VMEM_LIMIT),
        name="qkv",
    )(x, g, w, b, cos, sin)


def _stack_heads(qc_list):
    lane = lax.broadcasted_iota(jnp.int32, qc_list[0].shape, 1)
    low = lane < HEAD_DIM
    zero = jnp.zeros_like(qc_list[0])
    stack = []
    for qc in qc_list:
        stack.append(jnp.where(low, qc, zero))
        stack.append(jnp.where(low, zero, qc))
    return jnp.concatenate(stack, axis=0)


def _softmax_keys_on_rows(s, sinkrow):
    m = jnp.maximum(jnp.max(s, axis=0, keepdims=True), sinkrow)
    p = jnp.exp2(s - m)
    den = jnp.sum(p, axis=0, keepdims=True) + jnp.exp2(sinkrow - m)
    return (p * (1.0 / den)).astype(BF16)


def _unstack_heads(o, R):
    lowf = lax.broadcasted_iota(jnp.int32, (R, LANES), 1) < HEAD_DIM
    return (jnp.where(lowf, o[0:R], o[R:2 * R]), jnp.where(lowf, o[2 * R:3 * R], o[3 * R:4 * R]))


NT_DIMS = (((1,), (1,)), ((), ()))
TN_DIMS = (((0,), (0,)), ((), ()))
TT_DIMS = (((0,), (1,)), ((), ()))


def _attend(qc_list, kd, vd, bias, sinkrow):
    qs = _stack_heads(qc_list)
    s = lax.dot_general(kd, qs, NT_DIMS, preferred_element_type=F32) + bias
    p = _softmax_keys_on_rows(s, sinkrow)
    o = lax.dot_general(p, vd, TN_DIMS, preferred_element_type=F32)
    return _unstack_heads(o, qc_list[0].shape[0])


def _attn_prompt_kernel(q_ref, kp_ref, kc_ref, vp_ref, vc_ref, sink_ref, o_ref, *, tq):
    n = pl.program_id(1)
    B = WINDOW
    kd = jnp.concatenate([kp_ref[...], kc_ref[...]], axis=0)
    vd = jnp.concatenate([vp_ref[...], vc_ref[...]], axis=0)
    kj = lax.broadcasted_iota(jnp.int32, (2 * B, B), 0)
    qi = lax.broadcasted_iota(jnp.int32, (2 * B, B), 1)
    rel = B + qi - kj
    band = (rel >= 0) & (rel < WINDOW)
    neg = jnp.float32(-jnp.inf)
    band_bias = jnp.concatenate([jnp.where(band, 0.0, neg)] * 4, axis=1)
    first_bias = jnp.concatenate([jnp.where(band & ((kj >= B) | (n > 0)), 0.0, neg)] * 4, axis=1)
    for j in range(tq // B):
        bias = first_bias if j == 0 else band_bias
        for g in range(N_KV_HEADS):
            qc = [q_ref[j * B:(j + 1) * B, (2 * g + i) * LANES:(2 * g + i + 1) * LANES] for i in range(2)]
            kg = kd[j * B:(j + 2) * B, g * LANES:(g + 1) * LANES]
            vg = vd[j * B:(j + 2) * B, g * LANES:(g + 1) * LANES]
            o0, o1 = _attend(qc, kg, vg, bias, sink_ref[g])
            o_ref[j * B:(j + 1) * B, 2 * g * LANES:(2 * g + 1) * LANES] = o0.astype(BF16)
            o_ref[j * B:(j + 1) * B, (2 * g + 1) * LANES:(2 * g + 2) * LANES] = o1.astype(BF16)


def _attn_prompt(q, kd, vd, sinkrow, batch, seq, tq=512):
    N = q.shape[0]
    nt = seq // tq
    r = tq // WINDOW
    cur = lambda b, n: (b * nt + n, 0)
    prev = lambda b, n: (jnp.maximum((b * nt + n) * r - 1, 0), 0)
    kvw = 2 * N_KV_HEADS * HEAD_DIM
    return pl.pallas_call(
        functools.partial(_attn_prompt_kernel, tq=tq),
        grid=(batch, nt),
        in_specs=[
            pl.BlockSpec((tq, N_HEADS * HEAD_DIM), cur),
            pl.BlockSpec((WINDOW, kvw), prev),
            pl.BlockSpec((tq, kvw), cur),
            pl.BlockSpec((WINDOW, kvw), prev),
            pl.BlockSpec((tq, kvw), cur),
            pl.BlockSpec((N_KV_HEADS, 1, 4 * WINDOW), lambda b, n: (0, 0, 0)),
        ],
        out_specs=pl.BlockSpec((tq, N_HEADS * HEAD_DIM), cur),
        out_shape=jax.ShapeDtypeStruct((N, N_HEADS * HEAD_DIM), BF16),
        compiler_params=pltpu.CompilerParams(
            dimension_semantics=("arbitrary", "arbitrary"), vmem_limit_bytes=VMEM_LIMIT),
        name="attn_prompt",
    )(q, kd, kd, vd, vd, sinkrow)


def _attn_sample_kernel(q_ref, kdn_ref, vdn_ref, kn_ref, vn_ref, ck_ref, cv_ref, mask_ref, kval_ref, sink_ref,
                        o_ref, nk_ref, nv_ref, *, bs, t):
    W = WINDOW
    rows = bs * t
    bias = jnp.where((mask_ref[...] > 0) & (kval_ref[0] > 0), 0.0, -jnp.inf).astype(F32)
    lane = lax.broadcasted_iota(jnp.int32, (HEAD_DIM, W), 1)
    fresh = lane >= W - t

    def new_cols(n_ref):
        n = jnp.concatenate([n_ref[...], jnp.zeros((LANES - rows, N_KV_HEADS * HEAD_DIM), F32)], axis=0)
        nt_ = jnp.transpose(n)
        return [pltpu.roll(nt_, (W - t - s * t) % LANES, axis=1) for s in range(bs)]

    k_cols, v_cols = new_cols(kn_ref), new_cols(vn_ref)
    for g in range(N_KV_HEADS):
        hd = slice(g * HEAD_DIM, (g + 1) * HEAD_DIM)
        kt = jnp.concatenate([ck_ref[s, g] for s in range(bs)], axis=1).astype(BF16)
        vt = jnp.concatenate([cv_ref[s, g] for s in range(bs)], axis=1).astype(BF16)
        kt2 = jnp.concatenate([kt, kt], axis=0)
        vt2 = jnp.concatenate([vt, vt], axis=0)
        qs = _stack_heads([q_ref[:, (2 * g + i) * LANES:(2 * g + i + 1) * LANES] for i in range(2)])
        s_old = lax.dot_general(kt2, qs, TT_DIMS, preferred_element_type=F32)
        s_new = lax.dot_general(kdn_ref[:, g * LANES:(g + 1) * LANES], qs, NT_DIMS, preferred_element_type=F32)
        p = _softmax_keys_on_rows(jnp.concatenate([s_old, s_new], axis=0) + bias, sink_ref[g])
        o = (lax.dot_general(p[:bs * W], vt2, TT_DIMS, preferred_element_type=F32)
             + lax.dot_general(p[bs * W:], vdn_ref[:, g * LANES:(g + 1) * LANES], TN_DIMS,
                               preferred_element_type=F32))
        o0, o1 = _unstack_heads(o, rows)
        o_ref[:, 2 * g * LANES:(2 * g + 1) * LANES] = o0.astype(BF16)
        o_ref[:, (2 * g + 1) * LANES:(2 * g + 2) * LANES] = o1.astype(BF16)
        for s in range(bs):
            nk_ref[s, g] = jnp.where(fresh, k_cols[s][hd], pltpu.roll(ck_ref[s, g], W - t, axis=1))
            nv_ref[s, g] = jnp.where(fresh, v_cols[s][hd], pltpu.roll(cv_ref[s, g], W - t, axis=1))


def _attn_sample(q, kdn, vdn, kn, vn, ck, cv, mask, kval, sinkrow, bs, t):
    Bd = ck.shape[0]
    rows = bs * t
    kvw = N_KV_HEADS * HEAD_DIM
    S = bs * WINDOW + rows
    tile = lambda i: (i, 0)
    tile3 = lambda i: (i, 0, 0)
    tile4 = lambda i: (i, 0, 0, 0)
    cblk = (bs, N_KV_HEADS, HEAD_DIM, WINDOW)
    return pl.pallas_call(
        functools.partial(_attn_sample_kernel, bs=bs, t=t),
        grid=(Bd // bs,),
        in_specs=[
            pl.BlockSpec((rows, N_HEADS * HEAD_DIM), tile),
            pl.BlockSpec((rows, 2 * kvw), tile),
            pl.BlockSpec((rows, 2 * kvw), tile),
            pl.BlockSpec((rows, kvw), tile),
            pl.BlockSpec((rows, kvw), tile),
            pl.BlockSpec(cblk, tile4),
            pl.BlockSpec(cblk, tile4),
            pl.BlockSpec((S, 4 * rows), lambda i: (0, 0)),
            pl.BlockSpec((1, S, 1), tile3),
            pl.BlockSpec((N_KV_HEADS, 1, 4 * rows), lambda i: (0, 0, 0)),
        ],
        out_specs=[
            pl.BlockSpec((rows, N_HEADS * HEAD_DIM), tile),
            pl.BlockSpec(cblk, tile4),
            pl.BlockSpec(cblk, tile4),
        ],
        out_shape=[
            jax.ShapeDtypeStruct((Bd * t, N_HEADS * HEAD_DIM), BF16),
            jax.ShapeDtypeStruct((Bd,) + cblk[1:], F32),
            jax.ShapeDtypeStruct((Bd,) + cblk[1:], F32),
        ],
        compiler_params=pltpu.CompilerParams(
            dimension_semantics=("arbitrary",), vmem_limit_bytes=VMEM_LIMIT),
        name="attn_sample",
    )(q, kdn, vdn, kn, vn, ck, cv, mask, kval, sinkrow)


def _rope_tables(pos):
    inv = ROPE_THETA ** (-jnp.arange(0, HEAD_DIM, 2, dtype=F32) / HEAD_DIM)
    inv_l = jnp.tile(inv, LANES // (HEAD_DIM // 2))
    sign = jnp.asarray(np.where((np.arange(LANES) % HEAD_DIM) < HEAD_DIM // 2, -1.0, 1.0), F32)
    ang = pos.astype(F32)[:, None] * inv_l[None, :]
    return jnp.cos(ang), jnp.sin(ang) * sign[None, :]


def _sink_rows(sinks, rows):
    s = (sinks.astype(F32) * LOG2E).reshape(N_KV_HEADS, 1, N_HEADS // N_KV_HEADS, 1)
    return jnp.broadcast_to(s, (N_KV_HEADS, 1, 4, rows)).reshape(N_KV_HEADS, 1, 4 * rows)


def kernel(x_prompt, x_sample, state_pool, cache_k, cache_v, sample_start, norm_mix, norm_ffn, norm_final,
           w_pool, ls_pool, w_qkv, b_qkv, sinks, w_o, b_o, w_rg, b_rg, w_re, b_re, w_gate, w_up, w_down):
    B, T, D = x_prompt.shape
    Bd, Td, _ = x_sample.shape
    kvw = N_KV_HEADS * HEAD_DIM
    row = lambda v: v.reshape(1, -1).astype(F32)

    wp = w_pool[0].astype(BF16)
    wqkv = w_qkv[0].astype(BF16)
    wo = w_o[0].astype(BF16)
    wg, wu, wd = w_gate.astype(BF16), w_up.astype(BF16), w_down.astype(BF16)
    def router_rows(g_part, e_part):
        z = lambda n: jnp.zeros((n,) + g_part.shape[1:], F32)
        return jnp.concatenate([g_part, z(EXPERT_ROW0 - N_EXPERT_GROUPS), e_part,
                                z(ROUTER_ROWS - EXPERT_ROW0 - N_EXPERTS)], axis=0)

    wr = [router_rows(w_rg[l].T, w_re[l].T) for l in range(2)]
    br = [router_rows(b_rg[l][:, None], b_re[l][:, None]) for l in range(2)]

    start = sample_start.astype(jnp.int32)

    x1p, pool_p16 = _pool_prompt(x_prompt, jnp.zeros((B, HALO, D), F32), row(norm_mix[0]), wp, row(ls_pool[0]))
    pos_s = (start[:, None] + jnp.arange(Td, dtype=jnp.int32)[None, :]).reshape(-1)
    x1s, pool_s_t = _pool_sample(x_sample, jnp.transpose(state_pool[0], (1, 0, 2)), start[:, None],
                                 row(norm_mix[0]), wp, row(ls_pool[0]))
    x1s = x1s.reshape(Bd * Td, D)
    pool_s = jnp.transpose(pool_s_t, (1, 0, 2))
    pool_p = pool_p16[:, HALO - POOL_BUF:]

    moe0 = functools.partial(_moe, nf=row(norm_ffn[0]), wr=wr[0], br=br[0], wg=wg, wu=wu, wd=wd, layer=0)
    x2p = moe0(x1p.reshape(B * T, D))
    x2s = moe0(x1s)

    cos_p, sin_p = _rope_tables(jnp.arange(T, dtype=jnp.int32))
    cos_s, sin_s = _rope_tables(pos_s)
    g1 = row(norm_mix[1])
    tmq = 1024
    keep = min(WINDOW, T)
    qp, kdp, vdp, kp, vp = _qkv(x2p, g1, wqkv, row(b_qkv[0]), cos_p, sin_p, tmq, T // tmq, keep)
    qs, kds, vds, ks, vs = _qkv(x2s, g1, wqkv, row(b_qkv[0]), cos_s, sin_s, tmq, 1, tmq)

    op = _attn_prompt(qp, kdp, vdp, _sink_rows(sinks[0], WINDOW), B, T)

    bs = 8
    rows = bs * Td
    W = cache_k.shape[2]
    qrow = np.arange(rows)
    ccol = np.arange(bs * W)
    ncol = np.arange(rows)
    samp_q, t_q = qrow // Td, qrow % Td
    m_cache = (samp_q[:, None] == (ccol // W)[None, :]) & ((ccol % W)[None, :] > t_q[:, None])
    m_new = (samp_q[:, None] == (ncol // Td)[None, :]) & ((ncol % Td)[None, :] <= t_q[:, None])
    amask = jnp.asarray(np.tile(np.concatenate([m_cache, m_new], axis=1).T, (1, 4)), F32)
    kv_cache = (jnp.arange(W, dtype=jnp.int32)[None, :] >= (W - start)[:, None]).reshape(Bd // bs, bs * W)
    kval = jnp.concatenate([kv_cache, jnp.ones((Bd // bs, rows), bool)], axis=1).astype(F32)
    kval = kval.reshape(Bd // bs, bs * W + rows, 1)
    to_stored = lambda c: jnp.transpose(c[0], (0, 2, 3, 1))
    from_stored = lambda c: jnp.transpose(c, (0, 3, 1, 2))[None]
    osamp, nk_s, nv_s = _attn_sample(
        qs, kds, vds, ks, vs, to_stored(cache_k), to_stored(cache_v),
        amask, kval, _sink_rows(sinks[0], rows), bs, Td)

    moe1 = functools.partial(_moe, nf=row(norm_ffn[1]), wr=wr[1], br=br[1], wg=wg, wu=wu, wd=wd, layer=1,
                             final=row(norm_final))
    yp = moe1(x2p, oproj=(op, wo, row(b_o[0])))
    ys = moe1(x2s, oproj=(osamp, wo, row(b_o[0])))

    k_p = kp.reshape(1, B, keep, N_KV_HEADS, HEAD_DIM)
    v_p = vp.reshape(1, B, keep, N_KV_HEADS, HEAD_DIM)
    return (yp.reshape(B, T, D), ys.reshape(Bd, Td, D), pool_p[None], k_p, v_p, pool_s[None],
            from_stored(nk_s), from_stored(nv_s))
```

```python
import functools

import jax
import jax.numpy as jnp
import numpy as np
from jax import lax
from jax.experimental import pallas as pl
from jax.experimental.pallas import tpu as pltpu

F32 = jnp.float32
BF16 = jnp.bfloat16

D_MODEL = 1024
POOL_WINDOWS = (2, 4, 8, 16)
POOL_GROUP_DIM = 256
POOL_BUF = 15
HALO = 16
HEAD_DIM = 64
N_HEADS = 16
N_KV_HEADS = 4
WINDOW = 128
ROPE_THETA = 10000.0
N_EXPERT_GROUPS = 4
EXPERTS_PER_GROUP = 4
N_EXPERTS = 16
D_EXPERT = 256
RMS_EPS = 1e-6
LANES = 128
ROUTER_ROWS = 32
EXPERT_ROW0 = 8
VMEM_LIMIT = 56 * 1024 * 1024
LOG2E = 1.4426950408889634
Q_SCALE = LOG2E * HEAD_DIM ** -0.5


def _rms(x, g):
    ms = jnp.mean(x * x, axis=-1, keepdims=True)
    return x * lax.rsqrt(ms + RMS_EPS) * g


def _pool_project(h, wins, pos, wp_ref, ls):
    G = POOL_GROUP_DIM
    outs = []
    for g, w in enumerate(POOL_WINDOWS):
        cnt = jnp.minimum(w, pos + 1).astype(F32)
        d = wins[g] / cnt - h[:, g * G:(g + 1) * G]
        outs.append(jnp.dot(d.astype(BF16), wp_ref[g], preferred_element_type=F32))
    return jnp.concatenate(outs, axis=-1) * ls


POOL_BLOCK = 128


def _pool_band():
    t = np.arange(POOL_BLOCK)[:, None] + HALO
    k = np.arange(POOL_BLOCK + HALO)[None, :]
    return jnp.asarray(np.stack([(k <= t) & (k > t - w) for w in POOL_WINDOWS]), BF16)


def _pool_prompt_kernel(x_ref, buf_ref, g_ref, wp_ref, ls_ref, band_ref, o_ref, nb_ref, hc, *, tq):
    t = pl.program_id(1)
    G = POOL_GROUP_DIM

    @pl.when(t == 0)
    def _():
        hc[pl.ds(0, HALO), :] = buf_ref[0]

    @pl.when(t > 0)
    def _():
        hc[pl.ds(0, HALO), :] = hc[pl.ds(tq, HALO), :]

    x = x_ref[0]
    h = _rms(x, g_ref[...])
    hc[pl.ds(HALO, tq), :] = h
    hb = hc[...].astype(BF16)
    wins = []
    for g in range(len(POOL_WINDOWS)):
        blocks = [jnp.dot(band_ref[g], hb[b * POOL_BLOCK:(b + 1) * POOL_BLOCK + HALO, g * G:(g + 1) * G],
                          preferred_element_type=F32) for b in range(tq // POOL_BLOCK)]
        wins.append(jnp.concatenate(blocks, axis=0))
    pos = t * tq + lax.broadcasted_iota(jnp.int32, (tq, 1), 0)
    o_ref[0] = x + _pool_project(h, wins, pos, wp_ref, ls_ref[...])

    @pl.when(t == pl.num_programs(1) - 1)
    def _():
        nb_ref[0] = hc[pl.ds(tq, HALO), :]


def _pool_prompt(x, buf16, g, wp, ls, tq=1024):
    B, T, D = x.shape
    return pl.pallas_call(
        functools.partial(_pool_prompt_kernel, tq=tq),
        grid=(B, T // tq),
        in_specs=[
            pl.BlockSpec((1, tq, D), lambda b, t: (b, t, 0)),
            pl.BlockSpec((1, HALO, D), lambda b, t: (b, 0, 0)),
            pl.BlockSpec((1, D), lambda b, t: (0, 0)),
            pl.BlockSpec((4, POOL_GROUP_DIM, POOL_GROUP_DIM), lambda b, t: (0, 0, 0)),
            pl.BlockSpec((1, D), lambda b, t: (0, 0)),
            pl.BlockSpec((4, POOL_BLOCK, POOL_BLOCK + HALO), lambda b, t: (0, 0, 0)),
        ],
        out_specs=[
            pl.BlockSpec((1, tq, D), lambda b, t: (b, t, 0)),
            pl.BlockSpec((1, HALO, D), lambda b, t: (b, 0, 0)),
        ],
        out_shape=[jax.ShapeDtypeStruct((B, T, D), F32), jax.ShapeDtypeStruct((B, HALO, D), F32)],
        scratch_shapes=[pltpu.VMEM((HALO + tq, D), F32)],
        compiler_params=pltpu.CompilerParams(
            dimension_semantics=("arbitrary", "arbitrary"), vmem_limit_bytes=VMEM_LIMIT),
        name="pool_prompt",
    )(x, buf16, g, wp, ls, _pool_band())


def _pool_sample_kernel(x_ref, st_ref, start_ref, g_ref, wp_ref, ls_ref, o_ref, ns_ref, *, bs, t):
    G = POOL_GROUP_DIM
    xs = [x_ref[:, i, :] for i in range(t)]
    hn = [_rms(x, g_ref[...]) for x in xs]
    hist = [st_ref[r] for r in range(POOL_BUF)] + hn

    def doubled(prev, lag, lo):
        out = [None] * len(prev)
        for i in range(len(prev)):
            if i >= lag and prev[i] is not None and prev[i - lag] is not None:
                out[i] = prev[i][:, lo:] + prev[i - lag][:, lo:]
        return out

    s2 = doubled(hist, 1, 0)
    s4 = doubled(s2, 2, G)
    s8 = doubled(s4, 4, G)
    s16 = doubled(s8, 8, G)
    wins = (s2, s4, s8, s16)
    start = start_ref[...]
    ds = [[] for _ in POOL_WINDOWS]
    for i in range(t):
        for g, w in enumerate(POOL_WINDOWS):
            cnt = jnp.minimum(w, start + (i + 1)).astype(F32)
            ds[g].append(wins[g][POOL_BUF + i][:, :G] / cnt - hn[i][:, g * G:(g + 1) * G])
    ys = [jnp.dot(jnp.concatenate(ds[g], axis=0).astype(BF16), wp_ref[g], preferred_element_type=F32)
          for g in range(len(POOL_WINDOWS))]
    y = jnp.concatenate(ys, axis=-1) * ls_ref[...]
    for i in range(t):
        o_ref[:, i, :] = xs[i] + y[i * bs:(i + 1) * bs]
    for r in range(POOL_BUF):
        ns_ref[r] = hist[r + t]


def _pool_sample(x, st, start, g, wp, ls, bs=32):
    Bd, t, D = x.shape
    return pl.pallas_call(
        functools.partial(_pool_sample_kernel, bs=bs, t=t),
        grid=(Bd // bs,),
        in_specs=[
            pl.BlockSpec((bs, t, D), lambda i: (i, 0, 0)),
            pl.BlockSpec((POOL_BUF, bs, D), lambda i: (0, i, 0)),
            pl.BlockSpec((bs, 1), lambda i: (i, 0)),
            pl.BlockSpec((1, D), lambda i: (0, 0)),
            pl.BlockSpec((4, POOL_GROUP_DIM, POOL_GROUP_DIM), lambda i: (0, 0, 0)),
            pl.BlockSpec((1, D), lambda i: (0, 0)),
        ],
        out_specs=[pl.BlockSpec((bs, t, D), lambda i: (i, 0, 0)),
                   pl.BlockSpec((POOL_BUF, bs, D), lambda i: (0, i, 0))],
        out_shape=[jax.ShapeDtypeStruct((Bd, t, D), F32), jax.ShapeDtypeStruct((POOL_BUF, Bd, D), F32)],
        compiler_params=pltpu.CompilerParams(
            dimension_semantics=("arbitrary",), vmem_limit_bytes=VMEM_LIMIT),
        name="pool_sample",
    )(x, st, start, g, wp, ls)


def _route(lg):
    R = lg.shape[1]
    big = jnp.float32(1 << 20)
    neg = jnp.float32(-jnp.inf)
    r8 = lax.broadcasted_iota(jnp.int32, (8, R), 0).astype(F32)
    r16 = lax.broadcasted_iota(jnp.int32, (N_EXPERTS, R), 0).astype(F32)
    is_g = r8 < N_EXPERT_GROUPS
    gl = jnp.where(is_g, lg[0:8], neg)
    m = jnp.max(gl, axis=0, keepdims=True)
    gidx = jnp.min(jnp.where(gl == m, r8, big), axis=0, keepdims=True)
    z = jnp.sum(jnp.where(is_g, jnp.exp(gl - m), 0.0), axis=0, keepdims=True)
    gw = 1.0 / z
    lo = gidx * EXPERTS_PER_GROUP
    in_grp = (r16 >= lo) & (r16 < lo + EXPERTS_PER_GROUP)
    el = jnp.where(in_grp, lg[EXPERT_ROW0:EXPERT_ROW0 + N_EXPERTS], neg)
    v1 = jnp.max(el, axis=0, keepdims=True)
    i1 = jnp.min(jnp.where(el == v1, r16, big), axis=0, keepdims=True)
    el2 = jnp.where(r16 == i1, neg, el)
    v2 = jnp.max(el2, axis=0, keepdims=True)
    i2 = jnp.min(jnp.where(el2 == v2, r16, big), axis=0, keepdims=True)
    t = jnp.exp(v2 - v1)
    w1 = 1.0 / (1.0 + t)
    w2 = t * w1
    e8 = jnp.where(is_g, r8, r8 - EXPERTS_PER_GROUP)
    cw = gw * (jnp.where(e8 == i1 - lo, w1, 0.0) + jnp.where(e8 == i2 - lo, w2, 0.0))
    return gidx, cw


def _moe_kernel(*refs, has_oproj, has_final, tm, bm):
    it = iter(refs)
    x_ref = next(it)
    if has_oproj:
        oin_ref, wo_ref, bo_ref = next(it), next(it), next(it)
    nf_ref, wr_ref, br_ref, wg_ref, wu_ref, wd_ref = (next(it) for _ in range(6))
    if has_final:
        fn_ref = next(it)
    out_ref = next(it)
    hs, cs, ys, sel_s, earlier = (next(it) for _ in range(5))
    min_blk = -(-tm // bm)
    nblk = tm // bm + N_EXPERT_GROUPS
    cap = nblk * bm
    assert bm % 16 == 0
    nt = (((1,), (1,)), ((), ()))

    @pl.when(pl.program_id(0) == 0)
    def _():
        ri = lax.broadcasted_iota(jnp.int32, (tm, tm), 0)
        ci = lax.broadcasted_iota(jnp.int32, (tm, tm), 1)
        earlier[...] = jnp.where(ri < ci, 1.0, 0.0).astype(BF16)

    x = x_ref[...]
    if has_oproj:
        x = x + jnp.dot(oin_ref[...], wo_ref[...], preferred_element_type=F32) + bo_ref[...]
    out_ref[...] = x
    h = _rms(x, nf_ref[...])
    hb = h.astype(BF16)
    h_lo = (h - hb.astype(F32)).astype(BF16)
    wr = wr_ref[...]
    both = lax.dot_general(wr, hb, nt, preferred_element_type=F32)
    lg = (both[0:ROUTER_ROWS] + both[ROUTER_ROWS:]
          + lax.dot_general(wr[0:ROUTER_ROWS], h_lo, nt, preferred_element_type=F32)) + br_ref[...]
    gidx, cw = _route(lg)
    r8 = lax.broadcasted_iota(jnp.int32, (8, tm), 0)
    cw_hi = cw.astype(BF16).astype(F32)
    cwt = jnp.concatenate([jnp.where(r8 < EXPERTS_PER_GROUP, cw_hi, cw - cw_hi),
                           jnp.zeros((LANES - 8, tm), F32)], axis=0).astype(BF16)

    oh = jnp.where(r8.astype(F32) == gidx, 1.0, 0.0)
    cnt_before = jnp.dot(oh.astype(BF16), earlier[...], preferred_element_type=F32)
    pos_row = jnp.sum(oh * cnt_before, axis=0, keepdims=True)
    start_blk, n_blk = [], []
    off = jnp.int32(0)
    for g in range(N_EXPERT_GROUPS):
        in_g = gidx == g
        n_g = jnp.sum(jnp.where(in_g, 1.0, 0.0)).astype(jnp.int32)
        blocks = sum(jnp.where(n_g > k * bm, 1, 0) for k in range(min_blk))
        start_blk.append(off)
        n_blk.append(blocks)
        pos_row = pos_row + jnp.where(in_g, (off * bm).astype(F32), 0.0)
        off = off + blocks

    total_blk = off
    main = min_blk * bm
    rare = total_blk > min_blk

    def sort_rows(r0, n):
        sub = (r0 + lax.broadcasted_iota(jnp.int32, (n, tm), 0)).astype(F32)
        sel = jnp.where(pos_row == sub, 1.0, 0.0).astype(BF16)
        sel_s[r0:r0 + n, :] = sel
        hs[r0:r0 + n, :] = jnp.dot(sel, hb, preferred_element_type=F32).astype(BF16)
        cs[r0:r0 + n, :] = lax.dot_general(sel, cwt, nt, preferred_element_type=F32)

    sort_rows(0, main)
    pl.when(rare)(functools.partial(sort_rows, main, cap - main))
    ys[min_blk * bm:, :] = jnp.zeros((cap - min_blk * bm, D_MODEL), BF16)

    for g in range(N_EXPERT_GROUPS):
        wd = wd_ref[g * EXPERTS_PER_GROUP:(g + 1) * EXPERTS_PER_GROUP].reshape(
            EXPERTS_PER_GROUP * D_EXPERT, D_MODEL)

        def block(b, carry, g=g, wd=wd):
            rows = pl.ds(pl.multiple_of(b * bm, 16), bm)
            hblk = hs[rows, :]
            cblk = cs[rows, :]
            ln = lax.broadcasted_iota(jnp.int32, cblk.shape, 1)
            parts = []
            for e in range(EXPERTS_PER_GROUP):
                ce = jnp.sum(jnp.where((ln == e) | (ln == e + EXPERTS_PER_GROUP), cblk, 0.0),
                             axis=-1, keepdims=True)
                gt = jnp.dot(hblk, wg_ref[g * EXPERTS_PER_GROUP + e], preferred_element_type=F32)
                up = jnp.dot(hblk, wu_ref[g * EXPERTS_PER_GROUP + e], preferred_element_type=F32)
                a = gt / (1.0 + jnp.exp(-gt)) * up * ce
                parts.append(a.astype(BF16))
            a_all = jnp.concatenate(parts, axis=-1)
            ys[rows, :] = jnp.dot(a_all, wd, preferred_element_type=F32).astype(BF16)
            return carry

        lax.fori_loop(start_blk[g], start_blk[g] + n_blk[g], block, 0)

    def unsort_rows(r0, n):
        out_ref[...] += lax.dot_general(sel_s[r0:r0 + n, :], ys[r0:r0 + n, :],
                                        (((0,), (0,)), ((), ())), preferred_element_type=F32)

    unsort_rows(0, main)
    pl.when(rare)(functools.partial(unsort_rows, main, cap - main))
    if has_final:
        out_ref[...] = _rms(out_ref[...], fn_ref[...])


def _moe(x, nf, wr, br, wg, wu, wd, layer, oproj=None, final=None, tm=512, bm=144):
    N, D = x.shape
    has_oproj = oproj is not None
    has_final = final is not None
    const2 = lambda i: (0, 0)
    of_layer = lambda i: (layer, 0, 0, 0)
    tile = lambda i: (i, 0)
    once = pl.Buffered(1)
    args = [x]
    in_specs = [pl.BlockSpec((tm, D), tile)]
    if has_oproj:
        o, wo, bo = oproj
        args += [o, wo, bo]
        in_specs += [pl.BlockSpec((tm, D), tile), pl.BlockSpec((D, D), const2, pipeline_mode=once),
                     pl.BlockSpec((1, D), const2)]
    wr_hi = wr.astype(BF16)
    wr_lo = (wr - wr_hi.astype(F32)).astype(BF16)
    args += [nf, jnp.concatenate([wr_hi, wr_lo], axis=0), br, wg, wu, wd]
    in_specs += [
        pl.BlockSpec((1, D), const2),
        pl.BlockSpec((2 * ROUTER_ROWS, D), const2, pipeline_mode=once),
        pl.BlockSpec((ROUTER_ROWS, 1), const2),
        pl.BlockSpec((None, N_EXPERTS, D, D_EXPERT), of_layer, pipeline_mode=once),
        pl.BlockSpec((None, N_EXPERTS, D, D_EXPERT), of_layer, pipeline_mode=once),
        pl.BlockSpec((None, N_EXPERTS, D_EXPERT, D), of_layer, pipeline_mode=once),
    ]
    if has_final:
        args.append(final)
        in_specs.append(pl.BlockSpec((1, D), const2))
    cap = (tm // bm + N_EXPERT_GROUPS) * bm
    return pl.pallas_call(
        functools.partial(_moe_kernel, has_oproj=has_oproj, has_final=has_final, tm=tm, bm=bm),
        grid=(N // tm,),
        in_specs=in_specs,
        out_specs=pl.BlockSpec((tm, D), tile),
        out_shape=jax.ShapeDtypeStruct((N, D), F32),
        scratch_shapes=[
            pltpu.VMEM((cap, D), BF16),
            pltpu.VMEM((cap, LANES), F32),
            pltpu.VMEM((cap, D), BF16),
            pltpu.VMEM((cap, tm), BF16),
            pltpu.VMEM((tm, tm), BF16),
        ],
        compiler_params=pltpu.CompilerParams(
            dimension_semantics=("arbitrary",), vmem_limit_bytes=VMEM_LIMIT),
        name="moe",
    )(*args)


def _swap_halves(x, low):
    return jnp.where(low, pltpu.roll(x, LANES - 32, axis=1), pltpu.roll(x, 32, axis=1))


def _qkv_kernel(x_ref, g_ref, w_ref, b_ref, cos_ref, sin_ref, q_ref, kd_ref, vd_ref, k_ref, v_ref, *, keep):
    tm = x_ref.shape[0]
    h = _rms(x_ref[...], g_ref[...]).astype(BF16)
    cos = cos_ref[...]
    sin = sin_ref[...]
    lane = lax.broadcasted_iota(jnp.int32, cos.shape, 1)
    low32 = (lane % HEAD_DIM) < (HEAD_DIM // 2)
    low64 = lane < HEAD_DIM
    nq = N_HEADS * HEAD_DIM
    nk = N_KV_HEADS * HEAD_DIM
    cw = 2 * LANES

    def project(c0):
        y = jnp.dot(h, w_ref[:, c0:c0 + cw], preferred_element_type=F32) + b_ref[:, c0:c0 + cw]
        return [y[:, i * LANES:(i + 1) * LANES] for i in range(cw // LANES)]

    def rope(c):
        return c * cos + _swap_halves(c, low32) * sin

    def dup(c):
        r = pltpu.roll(c, HEAD_DIM, axis=1)
        return jnp.where(low64, c, r), jnp.where(low64, r, c)

    for c0 in range(0, nq, cw):
        for i, c in enumerate(project(c0)):
            lanes = slice(c0 + i * LANES, c0 + (i + 1) * LANES)
            q_ref[:, lanes] = (rope(c) * Q_SCALE).astype(BF16)
    for c0 in range(0, nk, cw):
        for (src, rot, f_ref, d_ref) in ((nq, True, k_ref, kd_ref), (nq + nk, False, v_ref, vd_ref)):
            for i, c in enumerate(project(src + c0)):
                j = c0 // LANES + i
                c = rope(c) if rot else c
                f_ref[:, j * LANES:(j + 1) * LANES] = c[tm - keep:]
                d0, d1 = dup(c)
                d_ref[:, 2 * j * LANES:(2 * j + 1) * LANES] = d0.astype(BF16)
                d_ref[:, (2 * j + 1) * LANES:(2 * j + 2) * LANES] = d1.astype(BF16)


def _qkv(x, g, w, b, cos, sin, tm, n_pos_tiles, keep):
    N, D = x.shape
    nq, nk = N_HEADS * HEAD_DIM, N_KV_HEADS * HEAD_DIM
    const = lambda i: (0, 0)
    tile = lambda i: (i, 0)
    ptile = lambda i: (i % n_pos_tiles, 0)
    seq = lambda i: (i // n_pos_tiles, 0)
    n_seq = N // (tm * n_pos_tiles)
    return pl.pallas_call(
        functools.partial(_qkv_kernel, keep=keep),
        grid=(N // tm,),
        in_specs=[
            pl.BlockSpec((tm, D), tile),
            pl.BlockSpec((1, D), const),
            pl.BlockSpec((D, nq + 2 * nk), const),
            pl.BlockSpec((1, nq + 2 * nk), const),
            pl.BlockSpec((tm, LANES), ptile),
            pl.BlockSpec((tm, LANES), ptile),
        ],
        out_specs=[
            pl.BlockSpec((tm, nq), tile),
            pl.BlockSpec((tm, 2 * nk), tile),
            pl.BlockSpec((tm, 2 * nk), tile),
            pl.BlockSpec((keep, nk), seq),
            pl.BlockSpec((keep, nk), seq),
        ],
        out_shape=[
            jax.ShapeDtypeStruct((N, nq), BF16),
            jax.ShapeDtypeStruct((N, 2 * nk), BF16),
            jax.ShapeDtypeStruct((N, 2 * nk), BF16),
            jax.ShapeDtypeStruct((n_seq * keep, nk), F32),
            jax.ShapeDtypeStruct((n_seq * keep, nk), F32),
        ],
        compiler_params=pltpu.CompilerParams(
            dimension_semantics=("arbitrary",), vmem_limit_bytes=VMEM_LIMIT),
        name="qkv",
    )(x, g, w, b, cos, sin)


def _stack_heads(qc_list):
    lane = lax.broadcasted_iota(jnp.int32, qc_list[0].shape, 1)
    low = lane < HEAD_DIM
    zero = jnp.zeros_like(qc_list[0])
    stack = []
    for qc in qc_list:
        stack.append(jnp.where(low, qc, zero))
        stack.append(jnp.where(low, zero, qc))
    return jnp.concatenate(stack, axis=0)


def _softmax_keys_on_rows(s, sinkrow):
    m = jnp.maximum(jnp.max(s, axis=0, keepdims=True), sinkrow)
    p = jnp.exp2(s - m)
    den = jnp.sum(p, axis=0, keepdims=True) + jnp.exp2(sinkrow - m)
    return (p * (1.0 / den)).astype(BF16)


def _unstack_heads(o, R):
    lowf = lax.broadcasted_iota(jnp.int32, (R, LANES), 1) < HEAD_DIM
    return (jnp.where(lowf, o[0:R], o[R:2 * R]), jnp.where(lowf, o[2 * R:3 * R], o[3 * R:4 * R]))


NT_DIMS = (((1,), (1,)), ((), ()))
TN_DIMS = (((0,), (0,)), ((), ()))
TT_DIMS = (((0,), (1,)), ((), ()))


def _attend(qc_list, kd, vd, bias, sinkrow):
    qs = _stack_heads(qc_list)
    s = lax.dot_general(kd, qs, NT_DIMS, preferred_element_type=F32) + bias
    p = _softmax_keys_on_rows(s, sinkrow)
    o = lax.dot_general(p, vd, TN_DIMS, preferred_element_type=F32)
    return _unstack_heads(o, qc_list[0].shape[0])


def _attn_prompt_kernel(q_ref, kp_ref, kc_ref, vp_ref, vc_ref, sink_ref, o_ref, *, tq):
    n = pl.program_id(1)
    B = WINDOW
    kd = jnp.concatenate([kp_ref[...], kc_ref[...]], axis=0)
    vd = jnp.concatenate([vp_ref[...], vc_ref[...]], axis=0)
    kj = lax.broadcasted_iota(jnp.int32, (2 * B, B), 0)
    qi = lax.broadcasted_iota(jnp.int32, (2 * B, B), 1)
    rel = B + qi - kj
    band = (rel >= 0) & (rel < WINDOW)
    neg = jnp.float32(-jnp.inf)
    band_bias = jnp.concatenate([jnp.where(band, 0.0, neg)] * 4, axis=1)
    first_bias = jnp.concatenate([jnp.where(band & ((kj >= B) | (n > 0)), 0.0, neg)] * 4, axis=1)
    for j in range(tq // B):
        bias = first_bias if j == 0 else band_bias
        for g in range(N_KV_HEADS):
            qc = [q_ref[j * B:(j + 1) * B, (2 * g + i) * LANES:(2 * g + i + 1) * LANES] for i in range(2)]
            kg = kd[j * B:(j + 2) * B, g * LANES:(g + 1) * LANES]
            vg = vd[j * B:(j + 2) * B, g * LANES:(g + 1) * LANES]
            o0, o1 = _attend(qc, kg, vg, bias, sink_ref[g])
            o_ref[j * B:(j + 1) * B, 2 * g * LANES:(2 * g + 1) * LANES] = o0.astype(BF16)
            o_ref[j * B:(j + 1) * B, (2 * g + 1) * LANES:(2 * g + 2) * LANES] = o1.astype(BF16)


def _attn_prompt(q, kd, vd, sinkrow, batch, seq, tq=1024):
    N = q.shape[0]
    nt = seq // tq
    r = tq // WINDOW
    cur = lambda b, n: (b * nt + n, 0)
    prev = lambda b, n: (jnp.maximum((b * nt + n) * r - 1, 0), 0)
    kvw = 2 * N_KV_HEADS * HEAD_DIM
    return pl.pallas_call(
        functools.partial(_attn_prompt_kernel, tq=tq),
        grid=(batch, nt),
        in_specs=[
            pl.BlockSpec((tq, N_HEADS * HEAD_DIM), cur),
            pl.BlockSpec((WINDOW, kvw), prev),
            pl.BlockSpec((tq, kvw), cur),
            pl.BlockSpec((WINDOW, kvw), prev),
            pl.BlockSpec((tq, kvw), cur),
            pl.BlockSpec((N_KV_HEADS, 1, 4 * WINDOW), lambda b, n: (0, 0, 0)),
        ],
        out_specs=pl.BlockSpec((tq, N_HEADS * HEAD_DIM), cur),
        out_shape=jax.ShapeDtypeStruct((N, N_HEADS * HEAD_DIM), BF16),
        compiler_params=pltpu.CompilerParams(
            dimension_semantics=("arbitrary", "arbitrary"), vmem_limit_bytes=VMEM_LIMIT),
        name="attn_prompt",
    )(q, kd, kd, vd, vd, sinkrow)


def _attn_sample_kernel(q_ref, kdn_ref, vdn_ref, kn_ref, vn_ref, ck_ref, cv_ref, mask_ref, kval_ref, sink_ref,
                        o_ref, nk_ref, nv_ref, *, bs, t):
    W = WINDOW
    rows = bs * t
    bias = jnp.where((mask_ref[...] > 0) & (kval_ref[0] > 0), 0.0, -jnp.inf).astype(F32)
    lane = lax.broadcasted_iota(jnp.int32, (HEAD_DIM, W), 1)
    fresh = lane >= W - t

    def new_cols(n_ref):
        n = jnp.concatenate([n_ref[...], jnp.zeros((LANES - rows, N_KV_HEADS * HEAD_DIM), F32)], axis=0)
        nt_ = jnp.transpose(n)
        return [pltpu.roll(nt_, (W - t - s * t) % LANES, axis=1) for s in range(bs)]

    k_cols, v_cols = new_cols(kn_ref), new_cols(vn_ref)
    for g in range(N_KV_HEADS):
        hd = slice(g * HEAD_DIM, (g + 1) * HEAD_DIM)
        kt = jnp.concatenate([ck_ref[s, g] for s in range(bs)], axis=1).astype(BF16)
        vt = jnp.concatenate([cv_ref[s, g] for s in range(bs)], axis=1).astype(BF16)
        kt2 = jnp.concatenate([kt, kt], axis=0)
        vt2 = jnp.concatenate([vt, vt], axis=0)
        qs = _stack_heads([q_ref[:, (2 * g + i) * LANES:(2 * g + i + 1) * LANES] for i in range(2)])
        s_old = lax.dot_general(kt2, qs, TT_DIMS, preferred_element_type=F32)
        s_new = lax.dot_general(kdn_ref[:, g * LANES:(g + 1) * LANES], qs, NT_DIMS, preferred_element_type=F32)
        p = _softmax_keys_on_rows(jnp.concatenate([s_old, s_new], axis=0) + bias, sink_ref[g])
        o = (lax.dot_general(p[:bs * W], vt2, TT_DIMS, preferred_element_type=F32)
             + lax.dot_general(p[bs * W:], vdn_ref[:, g * LANES:(g + 1) * LANES], TN_DIMS,
                               preferred_element_type=F32))
        o0, o1 = _unstack_heads(o, rows)
        o_ref[:, 2 * g * LANES:(2 * g + 1) * LANES] = o0.astype(BF16)
        o_ref[:, (2 * g + 1) * LANES:(2 * g + 2) * LANES] = o1.astype(BF16)
        for s in range(bs):
            nk_ref[s, g] = jnp.where(fresh, k_cols[s][hd], pltpu.roll(ck_ref[s, g], W - t, axis=1))
            nv_ref[s, g] = jnp.where(fresh, v_cols[s][hd], pltpu.roll(cv_ref[s, g], W - t, axis=1))


def _attn_sample(q, kdn, vdn, kn, vn, ck, cv, mask, kval, sinkrow, bs, t):
    Bd = ck.shape[0]
    rows = bs * t
    kvw = N_KV_HEADS * HEAD_DIM
    S = bs * WINDOW + rows
    tile = lambda i: (i, 0)
    tile3 = lambda i: (i, 0, 0)
    tile4 = lambda i: (i, 0, 0, 0)
    cblk = (bs, N_KV_HEADS, HEAD_DIM, WINDOW)
    return pl.pallas_call(
        functools.partial(_attn_sample_kernel, bs=bs, t=t),
        grid=(Bd // bs,),
        in_specs=[
            pl.BlockSpec((rows, N_HEADS * HEAD_DIM), tile),
            pl.BlockSpec((rows, 2 * kvw), tile),
            pl.BlockSpec((rows, 2 * kvw), tile),
            pl.BlockSpec((rows, kvw), tile),
            pl.BlockSpec((rows, kvw), tile),
            pl.BlockSpec(cblk, tile4),
            pl.BlockSpec(cblk, tile4),
            pl.BlockSpec((S, 4 * rows), lambda i: (0, 0)),
            pl.BlockSpec((1, S, 1), tile3),
            pl.BlockSpec((N_KV_HEADS, 1, 4 * rows), lambda i: (0, 0, 0)),
        ],
        out_specs=[
            pl.BlockSpec((rows, N_HEADS * HEAD_DIM), tile),
            pl.BlockSpec(cblk, tile4),
            pl.BlockSpec(cblk, tile4),
        ],
        out_shape=[
            jax.ShapeDtypeStruct((Bd * t, N_HEADS * HEAD_DIM), BF16),
            jax.ShapeDtypeStruct((Bd,) + cblk[1:], F32),
            jax.ShapeDtypeStruct((Bd,) + cblk[1:], F32),
        ],
        compiler_params=pltpu.CompilerParams(
            dimension_semantics=("arbitrary",), vmem_limit_bytes=VMEM_LIMIT),
        name="attn_sample",
    )(q, kdn, vdn, kn, vn, ck, cv, mask, kval, sinkrow)


def _rope_tables(pos):
    inv = ROPE_THETA ** (-jnp.arange(0, HEAD_DIM, 2, dtype=F32) / HEAD_DIM)
    inv_l = jnp.tile(inv, LANES // (HEAD_DIM // 2))
    sign = jnp.asarray(np.where((np.arange(LANES) % HEAD_DIM) < HEAD_DIM // 2, -1.0, 1.0), F32)
    ang = pos.astype(F32)[:, None] * inv_l[None, :]
    return jnp.cos(ang), jnp.sin(ang) * sign[None, :]


def _sink_rows(sinks, rows):
    s = (sinks.astype(F32) * LOG2E).reshape(N_KV_HEADS, 1, N_HEADS // N_KV_HEADS, 1)
    return jnp.broadcast_to(s, (N_KV_HEADS, 1, 4, rows)).reshape(N_KV_HEADS, 1, 4 * rows)


def kernel(x_prompt, x_sample, state_pool, cache_k, cache_v, sample_start, norm_mix, norm_ffn, norm_final,
           w_pool, ls_pool, w_qkv, b_qkv, sinks, w_o, b_o, w_rg, b_rg, w_re, b_re, w_gate, w_up, w_down):
    B, T, D = x_prompt.shape
    Bd, Td, _ = x_sample.shape
    kvw = N_KV_HEADS * HEAD_DIM
    row = lambda v: v.reshape(1, -1).astype(F32)

    wp = w_pool[0].astype(BF16)
    wqkv = w_qkv[0].astype(BF16)
    wo = w_o[0].astype(BF16)
    wg, wu, wd = w_gate.astype(BF16), w_up.astype(BF16), w_down.astype(BF16)
    def router_rows(g_part, e_part):
        z = lambda n: jnp.zeros((n,) + g_part.shape[1:], F32)
        return jnp.concatenate([g_part, z(EXPERT_ROW0 - N_EXPERT_GROUPS), e_part,
                                z(ROUTER_ROWS - EXPERT_ROW0 - N_EXPERTS)], axis=0)

    wr = [router_rows(w_rg[l].T, w_re[l].T) for l in range(2)]
    br = [router_rows(b_rg[l][:, None], b_re[l][:, None]) for l in range(2)]

    start = sample_start.astype(jnp.int32)

    x1p, pool_p16 = _pool_prompt(x_prompt, jnp.zeros((B, HALO, D), F32), row(norm_mix[0]), wp, row(ls_pool[0]))
    pos_s = (start[:, None] + jnp.arange(Td, dtype=jnp.int32)[None, :]).reshape(-1)
    x1s, pool_s_t = _pool_sample(x_sample, jnp.transpose(state_pool[0], (1, 0, 2)), start[:, None],
                                 row(norm_mix[0]), wp, row(ls_pool[0]))
    x1s = x1s.reshape(Bd * Td, D)
    pool_s = jnp.transpose(pool_s_t, (1, 0, 2))
    pool_p = pool_p16[:, HALO - POOL_BUF:]

    moe0 = functools.partial(_moe, nf=row(norm_ffn[0]), wr=wr[0], br=br[0], wg=wg, wu=wu, wd=wd, layer=0)
    x2p = moe0(x1p.reshape(B * T, D))
    x2s = moe0(x1s)

    cos_p, sin_p = _rope_tables(jnp.arange(T, dtype=jnp.int32))
    cos_s, sin_s = _rope_tables(pos_s)
    g1 = row(norm_mix[1])
    tmq = 1024
    keep = min(WINDOW, T)
    qp, kdp, vdp, kp, vp = _qkv(x2p, g1, wqkv, row(b_qkv[0]), cos_p, sin_p, tmq, T // tmq, keep)
    qs, kds, vds, ks, vs = _qkv(x2s, g1, wqkv, row(b_qkv[0]), cos_s, sin_s, tmq, 1, tmq)

    op = _attn_prompt(qp, kdp, vdp, _sink_rows(sinks[0], WINDOW), B, T)

    bs = 8
    rows = bs * Td
    W = cache_k.shape[2]
    qrow = np.arange(rows)
    ccol = np.arange(bs * W)
    ncol = np.arange(rows)
    samp_q, t_q = qrow // Td, qrow % Td
    m_cache = (samp_q[:, None] == (ccol // W)[None, :]) & ((ccol % W)[None, :] > t_q[:, None])
    m_new = (samp_q[:, None] == (ncol // Td)[None, :]) & ((ncol % Td)[None, :] <= t_q[:, None])
    amask = jnp.asarray(np.tile(np.concatenate([m_cache, m_new], axis=1).T, (1, 4)), F32)
    kv_cache = (jnp.arange(W, dtype=jnp.int32)[None, :] >= (W - start)[:, None]).reshape(Bd // bs, bs * W)
    kval = jnp.concatenate([kv_cache, jnp.ones((Bd // bs, rows), bool)], axis=1).astype(F32)
    kval = kval.reshape(Bd // bs, bs * W + rows, 1)
    to_stored = lambda c: jnp.transpose(c[0], (0, 2, 3, 1))
    from_stored = lambda c: jnp.transpose(c, (0, 3, 1, 2))[None]
    osamp, nk_s, nv_s = _attn_sample(
        qs, kds, vds, ks, vs, to_stored(cache_k), to_stored(cache_v),
        amask, kval, _sink_rows(sinks[0], rows), bs, Td)

    moe1 = functools.partial(_moe, nf=row(norm_ffn[1]), wr=wr[1], br=br[1], wg=wg, wu=wu, wd=wd, layer=1,
                             final=row(norm_final))
    yp = moe1(x2p, oproj=(op, wo, row(b_o[0])))
    ys = moe1(x2s, oproj=(osamp, wo, row(b_o[0])))

    k_p = kp.reshape(1, B, keep, N_KV_HEADS, HEAD_DIM)
    v_p = vp.reshape(1, B, keep, N_KV_HEADS, HEAD_DIM)
    return (yp.reshape(B, T, D), ys.reshape(Bd, Td, D), pool_p[None], k_p, v_p, pool_s[None],
            from_stored(nk_s), from_stored(nv_s))
```

```python
import functools

import jax
import jax.numpy as jnp
import numpy as np
from jax import lax
from jax.experimental import pallas as pl
from jax.experimental.pallas import tpu as pltpu

F32 = jnp.float32
BF16 = jnp.bfloat16

D_MODEL = 1024
POOL_WINDOWS = (2, 4, 8, 16)
POOL_GROUP_DIM = 256
POOL_BUF = 15
HALO = 16
HEAD_DIM = 64
N_HEADS = 16
N_KV_HEADS = 4
WINDOW = 128
ROPE_THETA = 10000.0
N_EXPERT_GROUPS = 4
EXPERTS_PER_GROUP = 4
N_EXPERTS = 16
D_EXPERT = 256
RMS_EPS = 1e-6
LANES = 128
ROUTER_ROWS = 32
EXPERT_ROW0 = 8
VMEM_LIMIT = 56 * 1024 * 1024
LOG2E = 1.4426950408889634
Q_SCALE = LOG2E * HEAD_DIM ** -0.5


def _rms(x, g):
    ms = jnp.mean(x * x, axis=-1, keepdims=True)
    return x * lax.rsqrt(ms + RMS_EPS) * g


def _pool_project(h, wins, pos, wp_ref, ls):
    G = POOL_GROUP_DIM
    outs = []
    for g, w in enumerate(POOL_WINDOWS):
        cnt = jnp.minimum(w, pos + 1).astype(F32)
        d = wins[g] / cnt - h[:, g * G:(g + 1) * G]
        outs.append(jnp.dot(d.astype(BF16), wp_ref[g], preferred_element_type=F32))
    return jnp.concatenate(outs, axis=-1) * ls


POOL_BLOCK = 128


def _pool_band():
    t = np.arange(POOL_BLOCK)[:, None] + HALO
    k = np.arange(POOL_BLOCK + HALO)[None, :]
    return jnp.asarray(np.stack([(k <= t) & (k > t - w) for w in POOL_WINDOWS]), BF16)


def _pool_prompt_kernel(x_ref, buf_ref, g_ref, wp_ref, ls_ref, band_ref, o_ref, nb_ref, hc, *, tq):
    t = pl.program_id(1)
    G = POOL_GROUP_DIM

    @pl.when(t == 0)
    def _():
        hc[pl.ds(0, HALO), :] = buf_ref[0]

    @pl.when(t > 0)
    def _():
        hc[pl.ds(0, HALO), :] = hc[pl.ds(tq, HALO), :]

    x = x_ref[0]
    h = _rms(x, g_ref[...])
    hc[pl.ds(HALO, tq), :] = h
    hb = hc[...].astype(BF16)
    wins = []
    for g in range(len(POOL_WINDOWS)):
        blocks = [jnp.dot(band_ref[g], hb[b * POOL_BLOCK:(b + 1) * POOL_BLOCK + HALO, g * G:(g + 1) * G],
                          preferred_element_type=F32) for b in range(tq // POOL_BLOCK)]
        wins.append(jnp.concatenate(blocks, axis=0))
    pos = t * tq + lax.broadcasted_iota(jnp.int32, (tq, 1), 0)
    o_ref[0] = x + _pool_project(h, wins, pos, wp_ref, ls_ref[...])

    @pl.when(t == pl.num_programs(1) - 1)
    def _():
        nb_ref[0] = hc[pl.ds(tq, HALO), :]


def _pool_prompt(x, buf16, g, wp, ls, tq=1024):
    B, T, D = x.shape
    return pl.pallas_call(
        functools.partial(_pool_prompt_kernel, tq=tq),
        grid=(B, T // tq),
        in_specs=[
            pl.BlockSpec((1, tq, D), lambda b, t: (b, t, 0)),
            pl.BlockSpec((1, HALO, D), lambda b, t: (b, 0, 0)),
            pl.BlockSpec((1, D), lambda b, t: (0, 0)),
            pl.BlockSpec((4, POOL_GROUP_DIM, POOL_GROUP_DIM), lambda b, t: (0, 0, 0)),
            pl.BlockSpec((1, D), lambda b, t: (0, 0)),
            pl.BlockSpec((4, POOL_BLOCK, POOL_BLOCK + HALO), lambda b, t: (0, 0, 0)),
        ],
        out_specs=[
            pl.BlockSpec((1, tq, D), lambda b, t: (b, t, 0)),
            pl.BlockSpec((1, HALO, D), lambda b, t: (b, 0, 0)),
        ],
        out_shape=[jax.ShapeDtypeStruct((B, T, D), F32), jax.ShapeDtypeStruct((B, HALO, D), F32)],
        scratch_shapes=[pltpu.VMEM((HALO + tq, D), F32)],
        compiler_params=pltpu.CompilerParams(
            dimension_semantics=("arbitrary", "arbitrary"), vmem_limit_bytes=VMEM_LIMIT),
        name="pool_prompt",
    )(x, buf16, g, wp, ls, _pool_band())


def _pool_sample_kernel(x_ref, st_ref, start_ref, g_ref, wp_ref, ls_ref, o_ref, ns_ref, *, bs, t):
    G = POOL_GROUP_DIM
    xs = [x_ref[:, i, :] for i in range(t)]
    hn = [_rms(x, g_ref[...]) for x in xs]
    hist = [st_ref[r] for r in range(POOL_BUF)] + hn

    def doubled(prev, lag, lo):
        out = [None] * len(prev)
        for i in range(len(prev)):
            if i >= lag and prev[i] is not None and prev[i - lag] is not None:
                out[i] = prev[i][:, lo:] + prev[i - lag][:, lo:]
        return out

    s2 = doubled(hist, 1, 0)
    s4 = doubled(s2, 2, G)
    s8 = doubled(s4, 4, G)
    s16 = doubled(s8, 8, G)
    wins = (s2, s4, s8, s16)
    start = start_ref[...]
    ds = [[] for _ in POOL_WINDOWS]
    for i in range(t):
        for g, w in enumerate(POOL_WINDOWS):
            cnt = jnp.minimum(w, start + (i + 1)).astype(F32)
            ds[g].append(wins[g][POOL_BUF + i][:, :G] / cnt - hn[i][:, g * G:(g + 1) * G])
    ys = [jnp.dot(jnp.concatenate(ds[g], axis=0).astype(BF16), wp_ref[g], preferred_element_type=F32)
          for g in range(len(POOL_WINDOWS))]
    y = jnp.concatenate(ys, axis=-1) * ls_ref[...]
    for i in range(t):
        o_ref[:, i, :] = xs[i] + y[i * bs:(i + 1) * bs]
    for r in range(POOL_BUF):
        ns_ref[r] = hist[r + t]


def _pool_sample(x, st, start, g, wp, ls, bs=32):
    Bd, t, D = x.shape
    return pl.pallas_call(
        functools.partial(_pool_sample_kernel, bs=bs, t=t),
        grid=(Bd // bs,),
        in_specs=[
            pl.BlockSpec((bs, t, D), lambda i: (i, 0, 0)),
            pl.BlockSpec((POOL_BUF, bs, D), lambda i: (0, i, 0)),
            pl.BlockSpec((bs, 1), lambda i: (i, 0)),
            pl.BlockSpec((1, D), lambda i: (0, 0)),
            pl.BlockSpec((4, POOL_GROUP_DIM, POOL_GROUP_DIM), lambda i: (0, 0, 0)),
            pl.BlockSpec((1, D), lambda i: (0, 0)),
        ],
        out_specs=[pl.BlockSpec((bs, t, D), lambda i: (i, 0, 0)),
                   pl.BlockSpec((POOL_BUF, bs, D), lambda i: (0, i, 0))],
        out_shape=[jax.ShapeDtypeStruct((Bd, t, D), F32), jax.ShapeDtypeStruct((POOL_BUF, Bd, D), F32)],
        compiler_params=pltpu.CompilerParams(
            dimension_semantics=("arbitrary",), vmem_limit_bytes=VMEM_LIMIT),
        name="pool_sample",
    )(x, st, start, g, wp, ls)


def _route(lg):
    R = lg.shape[1]
    big = jnp.float32(1 << 20)
    neg = jnp.float32(-jnp.inf)
    r8 = lax.broadcasted_iota(jnp.int32, (8, R), 0).astype(F32)
    r16 = lax.broadcasted_iota(jnp.int32, (N_EXPERTS, R), 0).astype(F32)
    is_g = r8 < N_EXPERT_GROUPS
    gl = jnp.where(is_g, lg[0:8], neg)
    m = jnp.max(gl, axis=0, keepdims=True)
    gidx = jnp.min(jnp.where(gl == m, r8, big), axis=0, keepdims=True)
    z = jnp.sum(jnp.where(is_g, jnp.exp(gl - m), 0.0), axis=0, keepdims=True)
    gw = 1.0 / z
    lo = gidx * EXPERTS_PER_GROUP
    in_grp = (r16 >= lo) & (r16 < lo + EXPERTS_PER_GROUP)
    el = jnp.where(in_grp, lg[EXPERT_ROW0:EXPERT_ROW0 + N_EXPERTS], neg)
    v1 = jnp.max(el, axis=0, keepdims=True)
    i1 = jnp.min(jnp.where(el == v1, r16, big), axis=0, keepdims=True)
    el2 = jnp.where(r16 == i1, neg, el)
    v2 = jnp.max(el2, axis=0, keepdims=True)
    i2 = jnp.min(jnp.where(el2 == v2, r16, big), axis=0, keepdims=True)
    t = jnp.exp(v2 - v1)
    w1 = 1.0 / (1.0 + t)
    w2 = t * w1
    e8 = jnp.where(is_g, r8, r8 - EXPERTS_PER_GROUP)
    cw = gw * (jnp.where(e8 == i1 - lo, w1, 0.0) + jnp.where(e8 == i2 - lo, w2, 0.0))
    return gidx, cw


def _moe_kernel(*refs, has_oproj, has_final, tm, bm):
    it = iter(refs)
    x_ref = next(it)
    if has_oproj:
        oin_ref, wo_ref, bo_ref = next(it), next(it), next(it)
    nf_ref, wr_ref, br_ref, wg_ref, wu_ref, wd_ref = (next(it) for _ in range(6))
    if has_final:
        fn_ref = next(it)
    out_ref = next(it)
    hs, cs, ys, sel_s, earlier = (next(it) for _ in range(5))
    min_blk = -(-tm // bm)
    nblk = tm // bm + N_EXPERT_GROUPS
    cap = nblk * bm
    assert bm % 16 == 0
    nt = (((1,), (1,)), ((), ()))

    @pl.when(pl.program_id(0) == 0)
    def _():
        ri = lax.broadcasted_iota(jnp.int32, (tm, tm), 0)
        ci = lax.broadcasted_iota(jnp.int32, (tm, tm), 1)
        earlier[...] = jnp.where(ri < ci, 1.0, 0.0).astype(BF16)

    x = x_ref[...]
    if has_oproj:
        x = x + jnp.dot(oin_ref[...], wo_ref[...], preferred_element_type=F32) + bo_ref[...]
    if has_oproj:
        out_ref[...] = x
    h = _rms(x, nf_ref[...])
    hb = h.astype(BF16)
    h_lo = (h - hb.astype(F32)).astype(BF16)
    wr = wr_ref[...]
    both = lax.dot_general(wr, hb, nt, preferred_element_type=F32)
    lg = (both[0:ROUTER_ROWS] + both[ROUTER_ROWS:]
          + lax.dot_general(wr[0:ROUTER_ROWS], h_lo, nt, preferred_element_type=F32)) + br_ref[...]
    gidx, cw = _route(lg)
    r8 = lax.broadcasted_iota(jnp.int32, (8, tm), 0)
    cw_hi = cw.astype(BF16).astype(F32)
    cwt = jnp.concatenate([jnp.where(r8 < EXPERTS_PER_GROUP, cw_hi, cw - cw_hi),
                           jnp.zeros((LANES - 8, tm), F32)], axis=0).astype(BF16)

    oh = jnp.where(r8.astype(F32) == gidx, 1.0, 0.0)
    cnt_before = jnp.dot(oh.astype(BF16), earlier[...], preferred_element_type=F32)
    pos_row = jnp.sum(oh * cnt_before, axis=0, keepdims=True)
    start_blk, n_blk = [], []
    off = jnp.int32(0)
    for g in range(N_EXPERT_GROUPS):
        in_g = gidx == g
        n_g = jnp.sum(jnp.where(in_g, 1.0, 0.0)).astype(jnp.int32)
        blocks = sum(jnp.where(n_g > k * bm, 1, 0) for k in range(min_blk))
        start_blk.append(off)
        n_blk.append(blocks)
        pos_row = pos_row + jnp.where(in_g, (off * bm).astype(F32), 0.0)
        off = off + blocks

    total_blk = off
    main = min_blk * bm
    extra = ((main, bm, total_blk > min_blk), (main + bm, cap - main - bm, total_blk > min_blk + 1))

    def sort_rows(r0, n):
        sub = (r0 + lax.broadcasted_iota(jnp.int32, (n, tm), 0)).astype(F32)
        sel = jnp.where(pos_row == sub, 1.0, 0.0).astype(BF16)
        sel_s[r0:r0 + n, :] = sel
        hs[r0:r0 + n, :] = jnp.dot(sel, hb, preferred_element_type=F32).astype(BF16)
        cs[r0:r0 + n, :] = lax.dot_general(sel, cwt, nt, preferred_element_type=F32)

    sort_rows(0, main)
    for r0, n, needed in extra:
        pl.when(needed)(functools.partial(sort_rows, r0, n))
    ys[min_blk * bm:, :] = jnp.zeros((cap - min_blk * bm, D_MODEL), BF16)

    for g in range(N_EXPERT_GROUPS):
        wd = wd_ref[g * EXPERTS_PER_GROUP:(g + 1) * EXPERTS_PER_GROUP].reshape(
            EXPERTS_PER_GROUP * D_EXPERT, D_MODEL)

        def block(b, carry, g=g, wd=wd):
            rows = pl.ds(pl.multiple_of(b * bm, 16), bm)
            hblk = hs[rows, :]
            cblk = cs[rows, :]
            ln = lax.broadcasted_iota(jnp.int32, cblk.shape, 1)
            parts = []
            for e in range(EXPERTS_PER_GROUP):
                ce = jnp.sum(jnp.where((ln == e) | (ln == e + EXPERTS_PER_GROUP), cblk, 0.0),
                             axis=-1, keepdims=True)
                gt = jnp.dot(hblk, wg_ref[g * EXPERTS_PER_GROUP + e], preferred_element_type=F32)
                up = jnp.dot(hblk, wu_ref[g * EXPERTS_PER_GROUP + e], preferred_element_type=F32)
                a = gt / (1.0 + jnp.exp(-gt)) * up * ce
                parts.append(a.astype(BF16))
            a_all = jnp.concatenate(parts, axis=-1)
            ys[rows, :] = jnp.dot(a_all, wd, preferred_element_type=F32).astype(BF16)
            return carry

        lax.fori_loop(start_blk[g], start_blk[g] + n_blk[g], block, 0)

    def unsorted(r0, n):
        return lax.dot_general(sel_s[r0:r0 + n, :], ys[r0:r0 + n, :], (((0,), (0,)), ((), ())),
                               preferred_element_type=F32)

    def unsort_more(r0, n):
        out_ref[...] += unsorted(r0, n)

    out_ref[...] = (out_ref[...] if has_oproj else x_ref[...]) + unsorted(0, main)
    for r0, n, needed in extra:
        pl.when(needed)(functools.partial(unsort_more, r0, n))
    if has_final:
        out_ref[...] = _rms(out_ref[...], fn_ref[...])


def _moe(x, nf, wr, br, wg, wu, wd, layer, oproj=None, final=None, tm=512, bm=144):
    N, D = x.shape
    has_oproj = oproj is not None
    has_final = final is not None
    const2 = lambda i: (0, 0)
    of_layer = lambda i: (layer, 0, 0, 0)
    tile = lambda i: (i, 0)
    once = pl.Buffered(1)
    args = [x]
    in_specs = [pl.BlockSpec((tm, D), tile)]
    if has_oproj:
        o, wo, bo = oproj
        args += [o, wo, bo]
        in_specs += [pl.BlockSpec((tm, D), tile), pl.BlockSpec((D, D), const2, pipeline_mode=once),
                     pl.BlockSpec((1, D), const2)]
    wr_hi = wr.astype(BF16)
    wr_lo = (wr - wr_hi.astype(F32)).astype(BF16)
    args += [nf, jnp.concatenate([wr_hi, wr_lo], axis=0), br, wg, wu, wd]
    in_specs += [
        pl.BlockSpec((1, D), const2),
        pl.BlockSpec((2 * ROUTER_ROWS, D), const2, pipeline_mode=once),
        pl.BlockSpec((ROUTER_ROWS, 1), const2),
        pl.BlockSpec((None, N_EXPERTS, D, D_EXPERT), of_layer, pipeline_mode=once),
        pl.BlockSpec((None, N_EXPERTS, D, D_EXPERT), of_layer, pipeline_mode=once),
        pl.BlockSpec((None, N_EXPERTS, D_EXPERT, D), of_layer, pipeline_mode=once),
    ]
    if has_final:
        args.append(final)
        in_specs.append(pl.BlockSpec((1, D), const2))
    cap = (tm // bm + N_EXPERT_GROUPS) * bm
    return pl.pallas_call(
        functools.partial(_moe_kernel, has_oproj=has_oproj, has_final=has_final, tm=tm, bm=bm),
        grid=(N // tm,),
        in_specs=in_specs,
        out_specs=pl.BlockSpec((tm, D), tile),
        out_shape=jax.ShapeDtypeStruct((N, D), F32),
        scratch_shapes=[
            pltpu.VMEM((cap, D), BF16),
            pltpu.VMEM((cap, LANES), F32),
            pltpu.VMEM((cap, D), BF16),
            pltpu.VMEM((cap, tm), BF16),
            pltpu.VMEM((tm, tm), BF16),
        ],
        compiler_params=pltpu.CompilerParams(
            dimension_semantics=("arbitrary",), vmem_limit_bytes=VMEM_LIMIT),
        name="moe",
    )(*args)


def _swap_halves(x, low):
    return jnp.where(low, pltpu.roll(x, LANES - 32, axis=1), pltpu.roll(x, 32, axis=1))


def _qkv_kernel(x_ref, g_ref, w_ref, b_ref, cos_ref, sin_ref, q_ref, kd_ref, vd_ref, k_ref, v_ref, *, keep):
    tm = x_ref.shape[0]
    h = _rms(x_ref[...], g_ref[...]).astype(BF16)
    cos = cos_ref[...]
    sin = sin_ref[...]
    lane = lax.broadcasted_iota(jnp.int32, cos.shape, 1)
    low32 = (lane % HEAD_DIM) < (HEAD_DIM // 2)
    low64 = lane < HEAD_DIM
    nq = N_HEADS * HEAD_DIM
    nk = N_KV_HEADS * HEAD_DIM
    cw = 2 * LANES

    def project(c0):
        y = jnp.dot(h, w_ref[:, c0:c0 + cw], preferred_element_type=F32) + b_ref[:, c0:c0 + cw]
        return [y[:, i * LANES:(i + 1) * LANES] for i in range(cw // LANES)]

    def rope(c):
        return c * cos + _swap_halves(c, low32) * sin

    def dup(c):
        r = pltpu.roll(c, HEAD_DIM, axis=1)
        return jnp.where(low64, c, r), jnp.where(low64, r, c)

    for c0 in range(0, nq, cw):
        for i, c in enumerate(project(c0)):
            lanes = slice(c0 + i * LANES, c0 + (i + 1) * LANES)
            q_ref[:, lanes] = (rope(c) * Q_SCALE).astype(BF16)
    for c0 in range(0, nk, cw):
        for (src, rot, f_ref, d_ref) in ((nq, True, k_ref, kd_ref), (nq + nk, False, v_ref, vd_ref)):
            for i, c in enumerate(project(src + c0)):
                j = c0 // LANES + i
                c = rope(c) if rot else c
                f_ref[:, j * LANES:(j + 1) * LANES] = c[tm - keep:]
                d0, d1 = dup(c)
                d_ref[:, 2 * j * LANES:(2 * j + 1) * LANES] = d0.astype(BF16)
                d_ref[:, (2 * j + 1) * LANES:(2 * j + 2) * LANES] = d1.astype(BF16)


def _qkv(x, g, w, b, cos, sin, tm, n_pos_tiles, keep):
    N, D = x.shape
    nq, nk = N_HEADS * HEAD_DIM, N_KV_HEADS * HEAD_DIM
    const = lambda i: (0, 0)
    tile = lambda i: (i, 0)
    ptile = lambda i: (i % n_pos_tiles, 0)
    seq = lambda i: (i // n_pos_tiles, 0)
    n_seq = N // (tm * n_pos_tiles)
    return pl.pallas_call(
        functools.partial(_qkv_kernel, keep=keep),
        grid=(N // tm,),
        in_specs=[
            pl.BlockSpec((tm, D), tile),
            pl.BlockSpec((1, D), const),
            pl.BlockSpec((D, nq + 2 * nk), const),
            pl.BlockSpec((1, nq + 2 * nk), const),
            pl.BlockSpec((tm, LANES), ptile),
            pl.BlockSpec((tm, LANES), ptile),
        ],
        out_specs=[
            pl.BlockSpec((tm, nq), tile),
            pl.BlockSpec((tm, 2 * nk), tile),
            pl.BlockSpec((tm, 2 * nk), tile),
            pl.BlockSpec((keep, nk), seq),
            pl.BlockSpec((keep, nk), seq),
        ],
        out_shape=[
            jax.ShapeDtypeStruct((N, nq), BF16),
            jax.ShapeDtypeStruct((N, 2 * nk), BF16),
            jax.ShapeDtypeStruct((N, 2 * nk), BF16),
            jax.ShapeDtypeStruct((n_seq * keep, nk), F32),
            jax.ShapeDtypeStruct((n_seq * keep, nk), F32),
        ],
        compiler_params=pltpu.CompilerParams(
            dimension_semantics=("arbitrary",), vmem_limit_bytes=VMEM_LIMIT),
        name="qkv",
    )(x, g, w, b, cos, sin)


def _stack_heads(qc_list):
    lane = lax.broadcasted_iota(jnp.int32, qc_list[0].shape, 1)
    low = lane < HEAD_DIM
    zero = jnp.zeros_like(qc_list[0])
    stack = []
    for qc in qc_list:
        stack.append(jnp.where(low, qc, zero))
        stack.append(jnp.where(low, zero, qc))
    return jnp.concatenate(stack, axis=0)


def _softmax_keys_on_rows(s, sinkrow):
    m = jnp.maximum(jnp.max(s, axis=0, keepdims=True), sinkrow)
    p = jnp.exp2(s - m)
    den = jnp.sum(p, axis=0, keepdims=True) + jnp.exp2(sinkrow - m)
    return (p * (1.0 / den)).astype(BF16)


def _unstack_heads(o, R):
    lowf = lax.broadcasted_iota(jnp.int32, (R, LANES), 1) < HEAD_DIM
    return (jnp.where(lowf, o[0:R], o[R:2 * R]), jnp.where(lowf, o[2 * R:3 * R], o[3 * R:4 * R]))


NT_DIMS = (((1,), (1,)), ((), ()))
TN_DIMS = (((0,), (0,)), ((), ()))
TT_DIMS = (((0,), (1,)), ((), ()))


def _attend(qc_list, kd, vd, bias, sinkrow):
    qs = _stack_heads(qc_list)
    s = lax.dot_general(kd, qs, NT_DIMS, preferred_element_type=F32) + bias
    p = _softmax_keys_on_rows(s, sinkrow)
    o = lax.dot_general(p, vd, TN_DIMS, preferred_element_type=F32)
    return _unstack_heads(o, qc_list[0].shape[0])


def _attn_prompt_kernel(q_ref, kp_ref, kc_ref, vp_ref, vc_ref, sink_ref, o_ref, *, tq):
    n = pl.program_id(1)
    B = WINDOW
    kd = jnp.concatenate([kp_ref[...], kc_ref[...]], axis=0)
    vd = jnp.concatenate([vp_ref[...], vc_ref[...]], axis=0)
    kj = lax.broadcasted_iota(jnp.int32, (2 * B, B), 0)
    qi = lax.broadcasted_iota(jnp.int32, (2 * B, B), 1)
    rel = B + qi - kj
    band = (rel >= 0) & (rel < WINDOW)
    neg = jnp.float32(-jnp.inf)
    band_bias = jnp.concatenate([jnp.where(band, 0.0, neg)] * 4, axis=1)
    first_bias = jnp.concatenate([jnp.where(band & ((kj >= B) | (n > 0)), 0.0, neg)] * 4, axis=1)
    for j in range(tq // B):
        bias = first_bias if j == 0 else band_bias
        for g in range(N_KV_HEADS):
            qc = [q_ref[j * B:(j + 1) * B, (2 * g + i) * LANES:(2 * g + i + 1) * LANES] for i in range(2)]
            kg = kd[j * B:(j + 2) * B, g * LANES:(g + 1) * LANES]
            vg = vd[j * B:(j + 2) * B, g * LANES:(g + 1) * LANES]
            o0, o1 = _attend(qc, kg, vg, bias, sink_ref[g])
            o_ref[j * B:(j + 1) * B, 2 * g * LANES:(2 * g + 1) * LANES] = o0.astype(BF16)
            o_ref[j * B:(j + 1) * B, (2 * g + 1) * LANES:(2 * g + 2) * LANES] = o1.astype(BF16)


def _attn_prompt(q, kd, vd, sinkrow, batch, seq, tq=1024):
    N = q.shape[0]
    nt = seq // tq
    r = tq // WINDOW
    cur = lambda b, n: (b * nt + n, 0)
    prev = lambda b, n: (jnp.maximum((b * nt + n) * r - 1, 0), 0)
    kvw = 2 * N_KV_HEADS * HEAD_DIM
    return pl.pallas_call(
        functools.partial(_attn_prompt_kernel, tq=tq),
        grid=(batch, nt),
        in_specs=[
            pl.BlockSpec((tq, N_HEADS * HEAD_DIM), cur),
            pl.BlockSpec((WINDOW, kvw), prev),
            pl.BlockSpec((tq, kvw), cur),
            pl.BlockSpec((WINDOW, kvw), prev),
            pl.BlockSpec((tq, kvw), cur),
            pl.BlockSpec((N_KV_HEADS, 1, 4 * WINDOW), lambda b, n: (0, 0, 0)),
        ],
        out_specs=pl.BlockSpec((tq, N_HEADS * HEAD_DIM), cur),
        out_shape=jax.ShapeDtypeStruct((N, N_HEADS * HEAD_DIM), BF16),
        compiler_params=pltpu.CompilerParams(
            dimension_semantics=("arbitrary", "arbitrary"), vmem_limit_bytes=VMEM_LIMIT),
        name="attn_prompt",
    )(q, kd, kd, vd, vd, sinkrow)


def _attn_sample_kernel(q_ref, kdn_ref, vdn_ref, kn_ref, vn_ref, ck_ref, cv_ref, mask_ref, kval_ref, sink_ref,
                        o_ref, nk_ref, nv_ref, *, bs, t):
    W = WINDOW
    rows = bs * t
    bias = jnp.where((mask_ref[...] > 0) & (kval_ref[0] > 0), 0.0, -jnp.inf).astype(F32)
    lane = lax.broadcasted_iota(jnp.int32, (HEAD_DIM, W), 1)
    fresh = lane >= W - t

    def new_cols(n_ref):
        n = jnp.concatenate([n_ref[...], jnp.zeros((LANES - rows, N_KV_HEADS * HEAD_DIM), F32)], axis=0)
        nt_ = jnp.transpose(n)
        return [pltpu.roll(nt_, (W - t - s * t) % LANES, axis=1) for s in range(bs)]

    k_cols, v_cols = new_cols(kn_ref), new_cols(vn_ref)
    for g in range(N_KV_HEADS):
        hd = slice(g * HEAD_DIM, (g + 1) * HEAD_DIM)
        kt = jnp.concatenate([ck_ref[s, g] for s in range(bs)], axis=1).astype(BF16)
        vt = jnp.concatenate([cv_ref[s, g] for s in range(bs)], axis=1).astype(BF16)
        kt2 = jnp.concatenate([kt, kt], axis=0)
        vt2 = jnp.concatenate([vt, vt], axis=0)
        qs = _stack_heads([q_ref[:, (2 * g + i) * LANES:(2 * g + i + 1) * LANES] for i in range(2)])
        s_old = lax.dot_general(kt2, qs, TT_DIMS, preferred_element_type=F32)
        s_new = lax.dot_general(kdn_ref[:, g * LANES:(g + 1) * LANES], qs, NT_DIMS, preferred_element_type=F32)
        p = _softmax_keys_on_rows(jnp.concatenate([s_old, s_new], axis=0) + bias, sink_ref[g])
        o = (lax.dot_general(p[:bs * W], vt2, TT_DIMS, preferred_element_type=F32)
             + lax.dot_general(p[bs * W:], vdn_ref[:, g * LANES:(g + 1) * LANES], TN_DIMS,
                               preferred_element_type=F32))
        o0, o1 = _unstack_heads(o, rows)
        o_ref[:, 2 * g * LANES:(2 * g + 1) * LANES] = o0.astype(BF16)
        o_ref[:, (2 * g + 1) * LANES:(2 * g + 2) * LANES] = o1.astype(BF16)
        for s in range(bs):
            nk_ref[s, g] = jnp.where(fresh, k_cols[s][hd], pltpu.roll(ck_ref[s, g], W - t, axis=1))
            nv_ref[s, g] = jnp.where(fresh, v_cols[s][hd], pltpu.roll(cv_ref[s, g], W - t, axis=1))


def _attn_sample(q, kdn, vdn, kn, vn, ck, cv, mask, kval, sinkrow, bs, t):
    Bd = ck.shape[0]
    rows = bs * t
    kvw = N_KV_HEADS * HEAD_DIM
    S = bs * WINDOW + rows
    tile = lambda i: (i, 0)
    tile3 = lambda i: (i, 0, 0)
    tile4 = lambda i: (i, 0, 0, 0)
    cblk = (bs, N_KV_HEADS, HEAD_DIM, WINDOW)
    return pl.pallas_call(
        functools.partial(_attn_sample_kernel, bs=bs, t=t),
        grid=(Bd // bs,),
        in_specs=[
            pl.BlockSpec((rows, N_HEADS * HEAD_DIM), tile),
            pl.BlockSpec((rows, 2 * kvw), tile),
            pl.BlockSpec((rows, 2 * kvw), tile),
            pl.BlockSpec((rows, kvw), tile),
            pl.BlockSpec((rows, kvw), tile),
            pl.BlockSpec(cblk, tile4),
            pl.BlockSpec(cblk, tile4),
            pl.BlockSpec((S, 4 * rows), lambda i: (0, 0)),
            pl.BlockSpec((1, S, 1), tile3),
            pl.BlockSpec((N_KV_HEADS, 1, 4 * rows), lambda i: (0, 0, 0)),
        ],
        out_specs=[
            pl.BlockSpec((rows, N_HEADS * HEAD_DIM), tile),
            pl.BlockSpec(cblk, tile4),
            pl.BlockSpec(cblk, tile4),
        ],
        out_shape=[
            jax.ShapeDtypeStruct((Bd * t, N_HEADS * HEAD_DIM), BF16),
            jax.ShapeDtypeStruct((Bd,) + cblk[1:], F32),
            jax.ShapeDtypeStruct((Bd,) + cblk[1:], F32),
        ],
        compiler_params=pltpu.CompilerParams(
            dimension_semantics=("arbitrary",), vmem_limit_bytes=VMEM_LIMIT),
        name="attn_sample",
    )(q, kdn, vdn, kn, vn, ck, cv, mask, kval, sinkrow)


def _rope_tables(pos):
    inv = ROPE_THETA ** (-jnp.arange(0, HEAD_DIM, 2, dtype=F32) / HEAD_DIM)
    reps = LANES // (HEAD_DIM // 2)
    sign = jnp.asarray(np.where((np.arange(LANES) % HEAD_DIM) < HEAD_DIM // 2, -1.0, 1.0), F32)
    ang = pos.astype(F32)[:, None] * inv[None, :]
    cos, sin = lax.optimization_barrier((jnp.cos(ang), jnp.sin(ang)))
    return jnp.tile(cos, (1, reps)), jnp.tile(sin, (1, reps)) * sign[None, :]


def _sink_rows(sinks, rows):
    s = (sinks.astype(F32) * LOG2E).reshape(N_KV_HEADS, 1, N_HEADS // N_KV_HEADS, 1)
    return jnp.broadcast_to(s, (N_KV_HEADS, 1, 4, rows)).reshape(N_KV_HEADS, 1, 4 * rows)


def kernel(x_prompt, x_sample, state_pool, cache_k, cache_v, sample_start, norm_mix, norm_ffn, norm_final,
           w_pool, ls_pool, w_qkv, b_qkv, sinks, w_o, b_o, w_rg, b_rg, w_re, b_re, w_gate, w_up, w_down):
    B, T, D = x_prompt.shape
    Bd, Td, _ = x_sample.shape
    kvw = N_KV_HEADS * HEAD_DIM
    row = lambda v: v.reshape(1, -1).astype(F32)

    wp = w_pool[0].astype(BF16)
    wqkv = w_qkv[0].astype(BF16)
    wo = w_o[0].astype(BF16)
    wg, wu, wd = w_gate.astype(BF16), w_up.astype(BF16), w_down.astype(BF16)
    def router_rows(g_part, e_part):
        z = lambda n: jnp.zeros((n,) + g_part.shape[1:], F32)
        return jnp.concatenate([g_part, z(EXPERT_ROW0 - N_EXPERT_GROUPS), e_part,
                                z(ROUTER_ROWS - EXPERT_ROW0 - N_EXPERTS)], axis=0)

    wr = [router_rows(w_rg[l].T, w_re[l].T) for l in range(2)]
    br = [router_rows(b_rg[l][:, None], b_re[l][:, None]) for l in range(2)]

    start = sample_start.astype(jnp.int32)

    x1p, pool_p16 = _pool_prompt(x_prompt, jnp.zeros((B, HALO, D), F32), row(norm_mix[0]), wp, row(ls_pool[0]))
    pos_s = (start[:, None] + jnp.arange(Td, dtype=jnp.int32)[None, :]).reshape(-1)
    x1s, pool_s_t = _pool_sample(x_sample, jnp.transpose(state_pool[0], (1, 0, 2)), start[:, None],
                                 row(norm_mix[0]), wp, row(ls_pool[0]))
    x1s = x1s.reshape(Bd * Td, D)
    pool_s = jnp.transpose(pool_s_t, (1, 0, 2))
    pool_p = pool_p16[:, HALO - POOL_BUF:]

    moe0 = functools.partial(_moe, nf=row(norm_ffn[0]), wr=wr[0], br=br[0], wg=wg, wu=wu, wd=wd, layer=0)
    x2p = moe0(x1p.reshape(B * T, D))
    x2s = moe0(x1s)

    cos_p, sin_p = _rope_tables(jnp.arange(T, dtype=jnp.int32))
    cos_s, sin_s = _rope_tables(pos_s)
    g1 = row(norm_mix[1])
    tmq = 1024
    keep = min(WINDOW, T)
    qp, kdp, vdp, kp, vp = _qkv(x2p, g1, wqkv, row(b_qkv[0]), cos_p, sin_p, tmq, T // tmq, keep)
    qs, kds, vds, ks, vs = _qkv(x2s, g1, wqkv, row(b_qkv[0]), cos_s, sin_s, tmq, 1, tmq)

    op = _attn_prompt(qp, kdp, vdp, _sink_rows(sinks[0], WINDOW), B, T)

    bs = 8
    rows = bs * Td
    W = cache_k.shape[2]
    qrow = np.arange(rows)
    ccol = np.arange(bs * W)
    ncol = np.arange(rows)
    samp_q, t_q = qrow // Td, qrow % Td
    m_cache = (samp_q[:, None] == (ccol // W)[None, :]) & ((ccol % W)[None, :] > t_q[:, None])
    m_new = (samp_q[:, None] == (ncol // Td)[None, :]) & ((ncol % Td)[None, :] <= t_q[:, None])
    amask = jnp.asarray(np.tile(np.concatenate([m_cache, m_new], axis=1).T, (1, 4)), F32)
    kv_cache = (jnp.arange(W, dtype=jnp.int32)[None, :] >= (W - start)[:, None]).reshape(Bd // bs, bs * W)
    kval = jnp.concatenate([kv_cache, jnp.ones((Bd // bs, rows), bool)], axis=1).astype(F32)
    kval = kval.reshape(Bd // bs, bs * W + rows, 1)
    to_stored = lambda c: jnp.transpose(c[0], (0, 2, 3, 1))
    from_stored = lambda c: jnp.transpose(c, (0, 3, 1, 2))[None]
    osamp, nk_s, nv_s = _attn_sample(
        qs, kds, vds, ks, vs, to_stored(cache_k), to_stored(cache_v),
        amask, kval, _sink_rows(sinks[0], rows), bs, Td)

    moe1 = functools.partial(_moe, nf=row(norm_ffn[1]), wr=wr[1], br=br[1], wg=wg, wu=wu, wd=wd, layer=1,
                             final=row(norm_final))
    yp = moe1(x2p, oproj=(op, wo, row(b_o[0])))
    ys = moe1(x2s, oproj=(osamp, wo, row(b_o[0])))

    k_p = kp.reshape(1, B, keep, N_KV_HEADS, HEAD_DIM)
    v_p = vp.reshape(1, B, keep, N_KV_HEADS, HEAD_DIM)
    return (yp.reshape(B, T, D), ys.reshape(Bd, Td, D), pool_p[None], k_p, v_p, pool_s[None],
            from_stored(nk_s), from_stored(nv_s))
```

```python
import functools

import jax
import jax.numpy as jnp
import numpy as np
from jax import lax
from jax.experimental import pallas as pl
from jax.experimental.pallas import tpu as pltpu

F32 = jnp.float32
BF16 = jnp.bfloat16

D_MODEL = 1024
POOL_WINDOWS = (2, 4, 8, 16)
POOL_GROUP_DIM = 256
POOL_BUF = 15
HALO = 16
HEAD_DIM = 64
N_HEADS = 16
N_KV_HEADS = 4
WINDOW = 128
ROPE_THETA = 10000.0
N_EXPERT_GROUPS = 4
EXPERTS_PER_GROUP = 4
N_EXPERTS = 16
D_EXPERT = 256
RMS_EPS = 1e-6
LANES = 128
ROUTER_ROWS = 32
EXPERT_ROW0 = 8
VMEM_LIMIT = 56 * 1024 * 1024
LOG2E = 1.4426950408889634
Q_SCALE = LOG2E * HEAD_DIM ** -0.5


def _rms(x, g):
    ms = jnp.mean(x * x, axis=-1, keepdims=True)
    return x * lax.rsqrt(ms + RMS_EPS) * g


def _pool_project(h, wins, pos, wp_ref, ls):
    G = POOL_GROUP_DIM
    outs = []
    for g, w in enumerate(POOL_WINDOWS):
        cnt = jnp.minimum(w, pos + 1).astype(F32)
        d = wins[g] / cnt - h[:, g * G:(g + 1) * G]
        outs.append(jnp.dot(d.astype(BF16), wp_ref[g], preferred_element_type=F32))
    return jnp.concatenate(outs, axis=-1) * ls


POOL_BLOCK = 128


def _pool_band():
    t = np.arange(POOL_BLOCK)[:, None] + HALO
    k = np.arange(POOL_BLOCK + HALO)[None, :]
    return jnp.asarray(np.stack([(k <= t) & (k > t - w) for w in POOL_WINDOWS]), BF16)


def _pool_prompt_kernel(x_ref, buf_ref, g_ref, wp_ref, ls_ref, band_ref, o_ref, nb_ref, hc, *, tq):
    t = pl.program_id(1)
    G = POOL_GROUP_DIM

    @pl.when(t == 0)
    def _():
        hc[pl.ds(0, HALO), :] = buf_ref[0]

    @pl.when(t > 0)
    def _():
        hc[pl.ds(0, HALO), :] = hc[pl.ds(tq, HALO), :]

    x = x_ref[0]
    h = _rms(x, g_ref[...])
    hc[pl.ds(HALO, tq), :] = h
    hb = hc[...].astype(BF16)
    wins = []
    for g in range(len(POOL_WINDOWS)):
        blocks = [jnp.dot(band_ref[g], hb[b * POOL_BLOCK:(b + 1) * POOL_BLOCK + HALO, g * G:(g + 1) * G],
                          preferred_element_type=F32) for b in range(tq // POOL_BLOCK)]
        wins.append(jnp.concatenate(blocks, axis=0))
    pos = t * tq + lax.broadcasted_iota(jnp.int32, (tq, 1), 0)
    o_ref[0] = x + _pool_project(h, wins, pos, wp_ref, ls_ref[...])

    @pl.when(t == pl.num_programs(1) - 1)
    def _():
        nb_ref[0] = hc[pl.ds(tq, HALO), :]


def _pool_prompt(x, buf16, g, wp, ls, tq=1024):
    B, T, D = x.shape
    return pl.pallas_call(
        functools.partial(_pool_prompt_kernel, tq=tq),
        grid=(B, T // tq),
        in_specs=[
            pl.BlockSpec((1, tq, D), lambda b, t: (b, t, 0)),
            pl.BlockSpec((1, HALO, D), lambda b, t: (b, 0, 0)),
            pl.BlockSpec((1, D), lambda b, t: (0, 0)),
            pl.BlockSpec((4, POOL_GROUP_DIM, POOL_GROUP_DIM), lambda b, t: (0, 0, 0)),
            pl.BlockSpec((1, D), lambda b, t: (0, 0)),
            pl.BlockSpec((4, POOL_BLOCK, POOL_BLOCK + HALO), lambda b, t: (0, 0, 0)),
        ],
        out_specs=[
            pl.BlockSpec((1, tq, D), lambda b, t: (b, t, 0)),
            pl.BlockSpec((1, HALO, D), lambda b, t: (b, 0, 0)),
        ],
        out_shape=[jax.ShapeDtypeStruct((B, T, D), F32), jax.ShapeDtypeStruct((B, HALO, D), F32)],
        scratch_shapes=[pltpu.VMEM((HALO + tq, D), F32)],
        compiler_params=pltpu.CompilerParams(
            dimension_semantics=("arbitrary", "arbitrary"), vmem_limit_bytes=VMEM_LIMIT),
        name="pool_prompt",
    )(x, buf16, g, wp, ls, _pool_band())


def _pool_sample_kernel(x_ref, st_ref, start_ref, g_ref, wp_ref, ls_ref, o_ref, ns_ref, *, bs, t):
    G = POOL_GROUP_DIM
    xs = [x_ref[:, i, :] for i in range(t)]
    hn = [_rms(x, g_ref[...]) for x in xs]
    hist = [st_ref[r] for r in range(POOL_BUF)] + hn

    def doubled(prev, lag, lo):
        out = [None] * len(prev)
        for i in range(len(prev)):
            if i >= lag and prev[i] is not None and prev[i - lag] is not None:
                out[i] = prev[i][:, lo:] + prev[i - lag][:, lo:]
        return out

    s2 = doubled(hist, 1, 0)
    s4 = doubled(s2, 2, G)
    s8 = doubled(s4, 4, G)
    s16 = doubled(s8, 8, G)
    wins = (s2, s4, s8, s16)
    start = start_ref[...]
    ds = [[] for _ in POOL_WINDOWS]
    for i in range(t):
        for g, w in enumerate(POOL_WINDOWS):
            cnt = jnp.minimum(w, start + (i + 1)).astype(F32)
            ds[g].append(wins[g][POOL_BUF + i][:, :G] / cnt - hn[i][:, g * G:(g + 1) * G])
    ys = [jnp.dot(jnp.concatenate(ds[g], axis=0).astype(BF16), wp_ref[g], preferred_element_type=F32)
          for g in range(len(POOL_WINDOWS))]
    y = jnp.concatenate(ys, axis=-1) * ls_ref[...]
    for i in range(t):
        o_ref[:, i, :] = xs[i] + y[i * bs:(i + 1) * bs]
    for r in range(POOL_BUF):
        ns_ref[r] = hist[r + t]


def _pool_sample(x, st, start, g, wp, ls, bs=32):
    Bd, t, D = x.shape
    return pl.pallas_call(
        functools.partial(_pool_sample_kernel, bs=bs, t=t),
        grid=(Bd // bs,),
        in_specs=[
            pl.BlockSpec((bs, t, D), lambda i: (i, 0, 0)),
            pl.BlockSpec((POOL_BUF, bs, D), lambda i: (0, i, 0)),
            pl.BlockSpec((bs, 1), lambda i: (i, 0)),
            pl.BlockSpec((1, D), lambda i: (0, 0)),
            pl.BlockSpec((4, POOL_GROUP_DIM, POOL_GROUP_DIM), lambda i: (0, 0, 0)),
            pl.BlockSpec((1, D), lambda i: (0, 0)),
        ],
        out_specs=[pl.BlockSpec((bs, t, D), lambda i: (i, 0, 0)),
                   pl.BlockSpec((POOL_BUF, bs, D), lambda i: (0, i, 0))],
        out_shape=[jax.ShapeDtypeStruct((Bd, t, D), F32), jax.ShapeDtypeStruct((POOL_BUF, Bd, D), F32)],
        compiler_params=pltpu.CompilerParams(
            dimension_semantics=("arbitrary",), vmem_limit_bytes=VMEM_LIMIT),
        name="pool_sample",
    )(x, st, start, g, wp, ls)


def _route(lg):
    R = lg.shape[1]
    big = jnp.float32(1 << 20)
    neg = jnp.float32(-jnp.inf)
    r8 = lax.broadcasted_iota(jnp.int32, (8, R), 0).astype(F32)
    r16 = lax.broadcasted_iota(jnp.int32, (N_EXPERTS, R), 0).astype(F32)
    is_g = r8 < N_EXPERT_GROUPS
    gl = jnp.where(is_g, lg[0:8], neg)
    m = jnp.max(gl, axis=0, keepdims=True)
    gidx = jnp.min(jnp.where(gl == m, r8, big), axis=0, keepdims=True)
    z = jnp.sum(jnp.where(is_g, jnp.exp(gl - m), 0.0), axis=0, keepdims=True)
    gw = 1.0 / z
    lo = gidx * EXPERTS_PER_GROUP
    in_grp = (r16 >= lo) & (r16 < lo + EXPERTS_PER_GROUP)
    el = jnp.where(in_grp, lg[EXPERT_ROW0:EXPERT_ROW0 + N_EXPERTS], neg)
    v1 = jnp.max(el, axis=0, keepdims=True)
    i1 = jnp.min(jnp.where(el == v1, r16, big), axis=0, keepdims=True)
    el2 = jnp.where(r16 == i1, neg, el)
    v2 = jnp.max(el2, axis=0, keepdims=True)
    i2 = jnp.min(jnp.where(el2 == v2, r16, big), axis=0, keepdims=True)
    t = jnp.exp(v2 - v1)
    w1 = 1.0 / (1.0 + t)
    w2 = t * w1
    e8 = jnp.where(is_g, r8, r8 - EXPERTS_PER_GROUP)
    cw = gw * (jnp.where(e8 == i1 - lo, w1, 0.0) + jnp.where(e8 == i2 - lo, w2, 0.0))
    return gidx, cw


def _moe_kernel(*refs, has_oproj, has_final, cast_layer, tm, bm):
    it = iter(refs)
    x_ref = next(it)
    if has_oproj:
        oin_ref, wo_ref, bo_ref = next(it), next(it), next(it)
    nf_ref, wr_ref, br_ref, wg_in, wu_in, wd_in = (next(it) for _ in range(6))
    if has_final:
        fn_ref = next(it)
    out_ref = next(it)
    if cast_layer is not None:
        wg_out, wu_out, wd_out = (next(it) for _ in range(3))
    hs, cs, ys, sel_s, earlier = (next(it) for _ in range(5))
    if cast_layer is None:
        wg_ref, wu_ref, wd_ref = wg_in, wu_in, wd_in
    else:
        wg_ref, wu_ref, wd_ref, stage_gu, stage_d, load_sem, store_sem = (next(it) for _ in range(7))
        pairs = ((wg_in, wg_ref, wg_out, stage_gu), (wu_in, wu_ref, wu_out, stage_gu), (wd_in, wd_ref, wd_out, stage_d))

        def store_copy(k):
            return pltpu.make_async_copy(pairs[k][1], pairs[k][2], store_sem.at[k])

        @pl.when(pl.program_id(0) == 0)
        def _():
            for src, dst, _, stage in pairs:
                def load(e, src=src, stage=stage):
                    return pltpu.make_async_copy(src.at[cast_layer, e], stage.at[e % 2], load_sem.at[e % 2])

                load(0).start()
                for e in range(N_EXPERTS):
                    if e + 1 < N_EXPERTS:
                        load(e + 1).start()
                    load(e).wait()
                    dst[e] = stage[e % 2].astype(BF16)
            for k in range(len(pairs)):
                store_copy(k).start()

        @pl.when(pl.program_id(0) == pl.num_programs(0) - 1)
        def _():
            for k in range(len(pairs)):
                store_copy(k).wait()
    min_blk = -(-tm // bm)
    nblk = tm // bm + N_EXPERT_GROUPS
    cap = nblk * bm
    assert bm % 16 == 0
    nt = (((1,), (1,)), ((), ()))

    @pl.when(pl.program_id(0) == 0)
    def _():
        ri = lax.broadcasted_iota(jnp.int32, (tm, tm), 0)
        ci = lax.broadcasted_iota(jnp.int32, (tm, tm), 1)
        earlier[...] = jnp.where(ri < ci, 1.0, 0.0).astype(BF16)

    x = x_ref[...]
    if has_oproj:
        x = x + jnp.dot(oin_ref[...], wo_ref[...], preferred_element_type=F32) + bo_ref[...]
    if has_oproj:
        out_ref[...] = x
    h = _rms(x, nf_ref[...])
    hb = h.astype(BF16)
    h_lo = (h - hb.astype(F32)).astype(BF16)
    wr = wr_ref[...]
    both = lax.dot_general(wr, hb, nt, preferred_element_type=F32)
    lg = (both[0:ROUTER_ROWS] + both[ROUTER_ROWS:]
          + lax.dot_general(wr[0:ROUTER_ROWS], h_lo, nt, preferred_element_type=F32)) + br_ref[...]
    gidx, cw = _route(lg)
    r8 = lax.broadcasted_iota(jnp.int32, (8, tm), 0)
    cw_hi = cw.astype(BF16).astype(F32)
    cwt = jnp.concatenate([jnp.where(r8 < EXPERTS_PER_GROUP, cw_hi, cw - cw_hi),
                           jnp.zeros((LANES - 8, tm), F32)], axis=0).astype(BF16)

    oh = jnp.where(r8.astype(F32) == gidx, 1.0, 0.0)
    cnt_before = jnp.dot(oh.astype(BF16), earlier[...], preferred_element_type=F32)
    pos_row = jnp.sum(oh * cnt_before, axis=0, keepdims=True)
    start_blk, n_blk = [], []
    off = jnp.int32(0)
    for g in range(N_EXPERT_GROUPS):
        in_g = gidx == g
        n_g = jnp.sum(jnp.where(in_g, 1.0, 0.0)).astype(jnp.int32)
        blocks = sum(jnp.where(n_g > k * bm, 1, 0) for k in range(min_blk))
        start_blk.append(off)
        n_blk.append(blocks)
        pos_row = pos_row + jnp.where(in_g, (off * bm).astype(F32), 0.0)
        off = off + blocks

    total_blk = off
    main = min_blk * bm
    extra = ((main, bm, total_blk > min_blk), (main + bm, cap - main - bm, total_blk > min_blk + 1))

    def sort_rows(r0, n):
        sub = (r0 + lax.broadcasted_iota(jnp.int32, (n, tm), 0)).astype(F32)
        sel = jnp.where(pos_row == sub, 1.0, 0.0).astype(BF16)
        sel_s[r0:r0 + n, :] = sel
        hs[r0:r0 + n, :] = jnp.dot(sel, hb, preferred_element_type=F32).astype(BF16)
        cs[r0:r0 + n, :] = lax.dot_general(sel, cwt, nt, preferred_element_type=F32)

    sort_rows(0, main)
    for r0, n, needed in extra:
        pl.when(needed)(functools.partial(sort_rows, r0, n))
    ys[min_blk * bm:, :] = jnp.zeros((cap - min_blk * bm, D_MODEL), BF16)

    for g in range(N_EXPERT_GROUPS):
        wd = wd_ref[g * EXPERTS_PER_GROUP:(g + 1) * EXPERTS_PER_GROUP].reshape(
            EXPERTS_PER_GROUP * D_EXPERT, D_MODEL)

        def block(b, carry, g=g, wd=wd):
            rows = pl.ds(pl.multiple_of(b * bm, 16), bm)
            hblk = hs[rows, :]
            cblk = cs[rows, :]
            ln = lax.broadcasted_iota(jnp.int32, cblk.shape, 1)
            parts = []
            for e in range(EXPERTS_PER_GROUP):
                ce = jnp.sum(jnp.where((ln == e) | (ln == e + EXPERTS_PER_GROUP), cblk, 0.0),
                             axis=-1, keepdims=True)
                gt = jnp.dot(hblk, wg_ref[g * EXPERTS_PER_GROUP + e], preferred_element_type=F32)
                up = jnp.dot(hblk, wu_ref[g * EXPERTS_PER_GROUP + e], preferred_element_type=F32)
                a = gt / (1.0 + jnp.exp(-gt)) * up * ce
                parts.append(a.astype(BF16))
            a_all = jnp.concatenate(parts, axis=-1)
            ys[rows, :] = jnp.dot(a_all, wd, preferred_element_type=F32).astype(BF16)
            return carry

        lax.fori_loop(start_blk[g], start_blk[g] + n_blk[g], block, 0)

    def unsorted(r0, n):
        return lax.dot_general(sel_s[r0:r0 + n, :], ys[r0:r0 + n, :], (((0,), (0,)), ((), ())),
                               preferred_element_type=F32)

    def unsort_more(r0, n):
        out_ref[...] += unsorted(r0, n)

    out_ref[...] = (out_ref[...] if has_oproj else x_ref[...]) + unsorted(0, main)
    for r0, n, needed in extra:
        pl.when(needed)(functools.partial(unsort_more, r0, n))
    if has_final:
        out_ref[...] = _rms(out_ref[...], fn_ref[...])


def _moe(x, nf, wr, br, wg, wu, wd, layer, oproj=None, final=None, tm=512, bm=144):
    N, D = x.shape
    has_oproj = oproj is not None
    has_final = final is not None
    const2 = lambda i: (0, 0)
    const3 = lambda i: (0, 0, 0)
    cast = layer is not None
    tile = lambda i: (i, 0)
    once = pl.Buffered(1)
    args = [x]
    in_specs = [pl.BlockSpec((tm, D), tile)]
    if has_oproj:
        o, wo, bo = oproj
        args += [o, wo, bo]
        in_specs += [pl.BlockSpec((tm, D), tile), pl.BlockSpec((D, D), const2, pipeline_mode=once),
                     pl.BlockSpec((1, D), const2)]
    wr_hi = wr.astype(BF16)
    wr_lo = (wr - wr_hi.astype(F32)).astype(BF16)
    args += [nf, jnp.concatenate([wr_hi, wr_lo], axis=0), br, wg, wu, wd]
    in_specs += [
        pl.BlockSpec((1, D), const2),
        pl.BlockSpec((2 * ROUTER_ROWS, D), const2, pipeline_mode=once),
        pl.BlockSpec((ROUTER_ROWS, 1), const2),
    ]
    gu_shape, d_shape = (N_EXPERTS, D, D_EXPERT), (N_EXPERTS, D_EXPERT, D)
    if cast:
        in_specs += [pl.BlockSpec(memory_space=pl.ANY)] * 3
    else:
        in_specs += [pl.BlockSpec(gu_shape, const3, pipeline_mode=once), pl.BlockSpec(gu_shape, const3, pipeline_mode=once),
                     pl.BlockSpec(d_shape, const3, pipeline_mode=once)]
    if has_final:
        args.append(final)
        in_specs.append(pl.BlockSpec((1, D), const2))
    cap = (tm // bm + N_EXPERT_GROUPS) * bm
    out_specs = [pl.BlockSpec((tm, D), tile)]
    out_shape = [jax.ShapeDtypeStruct((N, D), F32)]
    scratch = [
        pltpu.VMEM((cap, D), BF16),
        pltpu.VMEM((cap, LANES), F32),
        pltpu.VMEM((cap, D), BF16),
        pltpu.VMEM((cap, tm), BF16),
        pltpu.VMEM((tm, tm), BF16),
    ]
    if cast:
        out_specs += [pl.BlockSpec(memory_space=pl.ANY)] * 3
        out_shape += [jax.ShapeDtypeStruct(gu_shape, BF16)] * 2 + [jax.ShapeDtypeStruct(d_shape, BF16)]
        scratch += [pltpu.VMEM(gu_shape, BF16), pltpu.VMEM(gu_shape, BF16), pltpu.VMEM(d_shape, BF16),
                    pltpu.VMEM((2,) + gu_shape[1:], F32), pltpu.VMEM((2,) + d_shape[1:], F32),
                    pltpu.SemaphoreType.DMA((2,)), pltpu.SemaphoreType.DMA((3,))]
    res = pl.pallas_call(
        functools.partial(_moe_kernel, has_oproj=has_oproj, has_final=has_final, cast_layer=layer, tm=tm, bm=bm),
        grid=(N // tm,),
        in_specs=in_specs,
        out_specs=out_specs,
        out_shape=out_shape,
        scratch_shapes=scratch,
        compiler_params=pltpu.CompilerParams(
            dimension_semantics=("arbitrary",), vmem_limit_bytes=VMEM_LIMIT),
        name="moe",
    )(*args)
    return res if cast else res[0]


def _swap_halves(x, low):
    return jnp.where(low, pltpu.roll(x, LANES - 32, axis=1), pltpu.roll(x, 32, axis=1))


def _qkv_kernel(x_ref, g_ref, w_ref, b_ref, cos_ref, sin_ref, q_ref, kd_ref, vd_ref, k_ref, v_ref, *, keep):
    tm = x_ref.shape[0]
    h = _rms(x_ref[...], g_ref[...]).astype(BF16)
    cos = cos_ref[...]
    sin = sin_ref[...]
    lane = lax.broadcasted_iota(jnp.int32, cos.shape, 1)
    low32 = (lane % HEAD_DIM) < (HEAD_DIM // 2)
    low64 = lane < HEAD_DIM
    nq = N_HEADS * HEAD_DIM
    nk = N_KV_HEADS * HEAD_DIM
    cw = 2 * LANES

    def project(c0):
        y = jnp.dot(h, w_ref[:, c0:c0 + cw], preferred_element_type=F32) + b_ref[:, c0:c0 + cw]
        return [y[:, i * LANES:(i + 1) * LANES] for i in range(cw // LANES)]

    def rope(c):
        return c * cos + _swap_halves(c, low32) * sin

    def dup(c):
        r = pltpu.roll(c, HEAD_DIM, axis=1)
        return jnp.where(low64, c, r), jnp.where(low64, r, c)

    for c0 in range(0, nq, cw):
        for i, c in enumerate(project(c0)):
            lanes = slice(c0 + i * LANES, c0 + (i + 1) * LANES)
            q_ref[:, lanes] = (rope(c) * Q_SCALE).astype(BF16)
    for c0 in range(0, nk, cw):
        for (src, rot, f_ref, d_ref) in ((nq, True, k_ref, kd_ref), (nq + nk, False, v_ref, vd_ref)):
            for i, c in enumerate(project(src + c0)):
                j = c0 // LANES + i
                c = rope(c) if rot else c
                f_ref[:, j * LANES:(j + 1) * LANES] = c[tm - keep:]
                d0, d1 = dup(c)
                d_ref[:, 2 * j * LANES:(2 * j + 1) * LANES] = d0.astype(BF16)
                d_ref[:, (2 * j + 1) * LANES:(2 * j + 2) * LANES] = d1.astype(BF16)


def _qkv(x, g, w, b, cos, sin, tm, n_pos_tiles, keep):
    N, D = x.shape
    nq, nk = N_HEADS * HEAD_DIM, N_KV_HEADS * HEAD_DIM
    const = lambda i: (0, 0)
    tile = lambda i: (i, 0)
    ptile = lambda i: (i % n_pos_tiles, 0)
    seq = lambda i: (i // n_pos_tiles, 0)
    n_seq = N // (tm * n_pos_tiles)
    return pl.pallas_call(
        functools.partial(_qkv_kernel, keep=keep),
        grid=(N // tm,),
        in_specs=[
            pl.BlockSpec((tm, D), tile),
            pl.BlockSpec((1, D), const),
            pl.BlockSpec((D, nq + 2 * nk), const),
            pl.BlockSpec((1, nq + 2 * nk), const),
            pl.BlockSpec((tm, LANES), ptile),
            pl.BlockSpec((tm, LANES), ptile),
        ],
        out_specs=[
            pl.BlockSpec((tm, nq), tile),
            pl.BlockSpec((tm, 2 * nk), tile),
            pl.BlockSpec((tm, 2 * nk), tile),
            pl.BlockSpec((keep, nk), seq),
            pl.BlockSpec((keep, nk), seq),
        ],
        out_shape=[
            jax.ShapeDtypeStruct((N, nq), BF16),
            jax.ShapeDtypeStruct((N, 2 * nk), BF16),
            jax.ShapeDtypeStruct((N, 2 * nk), BF16),
            jax.ShapeDtypeStruct((n_seq * keep, nk), F32),
            jax.ShapeDtypeStruct((n_seq * keep, nk), F32),
        ],
        compiler_params=pltpu.CompilerParams(
            dimension_semantics=("arbitrary",), vmem_limit_bytes=VMEM_LIMIT),
        name="qkv",
    )(x, g, w, b, cos, sin)


def _stack_heads(qc_list):
    lane = lax.broadcasted_iota(jnp.int32, qc_list[0].shape, 1)
    low = lane < HEAD_DIM
    zero = jnp.zeros_like(qc_list[0])
    stack = []
    for qc in qc_list:
        stack.append(jnp.where(low, qc, zero))
        stack.append(jnp.where(low, zero, qc))
    return jnp.concatenate(stack, axis=0)


def _softmax_keys_on_rows(s, sinkrow):
    m = jnp.maximum(jnp.max(s, axis=0, keepdims=True), sinkrow)
    p = jnp.exp2(s - m)
    den = jnp.sum(p, axis=0, keepdims=True) + jnp.exp2(sinkrow - m)
    return (p * (1.0 / den)).astype(BF16)


def _unstack_heads(o, R):
    lowf = lax.broadcasted_iota(jnp.int32, (R, LANES), 1) < HEAD_DIM
    return (jnp.where(lowf, o[0:R], o[R:2 * R]), jnp.where(lowf, o[2 * R:3 * R], o[3 * R:4 * R]))


NT_DIMS = (((1,), (1,)), ((), ()))
TN_DIMS = (((0,), (0,)), ((), ()))
TT_DIMS = (((0,), (1,)), ((), ()))


def _attend(qc_list, kd, vd, bias, sinkrow):
    qs = _stack_heads(qc_list)
    s = lax.dot_general(kd, qs, NT_DIMS, preferred_element_type=F32) + bias
    p = _softmax_keys_on_rows(s, sinkrow)
    o = lax.dot_general(p, vd, TN_DIMS, preferred_element_type=F32)
    return _unstack_heads(o, qc_list[0].shape[0])


def _attn_prompt_kernel(q_ref, kp_ref, kc_ref, vp_ref, vc_ref, sink_ref, o_ref, *, tq):
    n = pl.program_id(1)
    B = WINDOW
    kd = jnp.concatenate([kp_ref[...], kc_ref[...]], axis=0)
    vd = jnp.concatenate([vp_ref[...], vc_ref[...]], axis=0)
    kj = lax.broadcasted_iota(jnp.int32, (2 * B, B), 0)
    qi = lax.broadcasted_iota(jnp.int32, (2 * B, B), 1)
    rel = B + qi - kj
    band = (rel >= 0) & (rel < WINDOW)
    neg = jnp.float32(-jnp.inf)
    band_bias = jnp.concatenate([jnp.where(band, 0.0, neg)] * 4, axis=1)
    first_bias = jnp.concatenate([jnp.where(band & ((kj >= B) | (n > 0)), 0.0, neg)] * 4, axis=1)
    for j in range(tq // B):
        bias = first_bias if j == 0 else band_bias
        for g in range(N_KV_HEADS):
            qc = [q_ref[j * B:(j + 1) * B, (2 * g + i) * LANES:(2 * g + i + 1) * LANES] for i in range(2)]
            kg = kd[j * B:(j + 2) * B, g * LANES:(g + 1) * LANES]
            vg = vd[j * B:(j + 2) * B, g * LANES:(g + 1) * LANES]
            o0, o1 = _attend(qc, kg, vg, bias, sink_ref[g])
            o_ref[j * B:(j + 1) * B, 2 * g * LANES:(2 * g + 1) * LANES] = o0.astype(BF16)
            o_ref[j * B:(j + 1) * B, (2 * g + 1) * LANES:(2 * g + 2) * LANES] = o1.astype(BF16)


def _attn_prompt(q, kd, vd, sinkrow, batch, seq, tq=1024):
    N = q.shape[0]
    nt = seq // tq
    r = tq // WINDOW
    cur = lambda b, n: (b * nt + n, 0)
    prev = lambda b, n: (jnp.maximum((b * nt + n) * r - 1, 0), 0)
    kvw = 2 * N_KV_HEADS * HEAD_DIM
    return pl.pallas_call(
        functools.partial(_attn_prompt_kernel, tq=tq),
        grid=(batch, nt),
        in_specs=[
            pl.BlockSpec((tq, N_HEADS * HEAD_DIM), cur),
            pl.BlockSpec((WINDOW, kvw), prev),
            pl.BlockSpec((tq, kvw), cur),
            pl.BlockSpec((WINDOW, kvw), prev),
            pl.BlockSpec((tq, kvw), cur),
            pl.BlockSpec((N_KV_HEADS, 1, 4 * WINDOW), lambda b, n: (0, 0, 0)),
        ],
        out_specs=pl.BlockSpec((tq, N_HEADS * HEAD_DIM), cur),
        out_shape=jax.ShapeDtypeStruct((N, N_HEADS * HEAD_DIM), BF16),
        compiler_params=pltpu.CompilerParams(
            dimension_semantics=("arbitrary", "arbitrary"), vmem_limit_bytes=VMEM_LIMIT),
        name="attn_prompt",
    )(q, kd, kd, vd, vd, sinkrow)


def _attn_sample_kernel(q_ref, kdn_ref, vdn_ref, kn_ref, vn_ref, ck_ref, cv_ref, mask_ref, kval_ref, sink_ref,
                        o_ref, nk_ref, nv_ref, *, bs, t):
    W = WINDOW
    rows = bs * t
    bias = jnp.where((mask_ref[...] > 0) & (kval_ref[0] > 0), 0.0, -jnp.inf).astype(F32)
    lane = lax.broadcasted_iota(jnp.int32, (HEAD_DIM, W), 1)
    fresh = lane >= W - t

    def new_cols(n_ref):
        n = jnp.concatenate([n_ref[...], jnp.zeros((LANES - rows, N_KV_HEADS * HEAD_DIM), F32)], axis=0)
        nt_ = jnp.transpose(n)
        return [pltpu.roll(nt_, (W - t - s * t) % LANES, axis=1) for s in range(bs)]

    k_cols, v_cols = new_cols(kn_ref), new_cols(vn_ref)
    for g in range(N_KV_HEADS):
        hd = slice(g * HEAD_DIM, (g + 1) * HEAD_DIM)
        kt = jnp.concatenate([ck_ref[s, g] for s in range(bs)], axis=1).astype(BF16)
        vt = jnp.concatenate([cv_ref[s, g] for s in range(bs)], axis=1).astype(BF16)
        kt2 = jnp.concatenate([kt, kt], axis=0)
        vt2 = jnp.concatenate([vt, vt], axis=0)
        qs = _stack_heads([q_ref[:, (2 * g + i) * LANES:(2 * g + i + 1) * LANES] for i in range(2)])
        s_old = lax.dot_general(kt2, qs, TT_DIMS, preferred_element_type=F32)
        s_new = lax.dot_general(kdn_ref[:, g * LANES:(g + 1) * LANES], qs, NT_DIMS, preferred_element_type=F32)
        p = _softmax_keys_on_rows(jnp.concatenate([s_old, s_new], axis=0) + bias, sink_ref[g])
        o = (lax.dot_general(p[:bs * W], vt2, TT_DIMS, preferred_element_type=F32)
             + lax.dot_general(p[bs * W:], vdn_ref[:, g * LANES:(g + 1) * LANES], TN_DIMS,
                               preferred_element_type=F32))
        o0, o1 = _unstack_heads(o, rows)
        o_ref[:, 2 * g * LANES:(2 * g + 1) * LANES] = o0.astype(BF16)
        o_ref[:, (2 * g + 1) * LANES:(2 * g + 2) * LANES] = o1.astype(BF16)
        for s in range(bs):
            nk_ref[s, g] = jnp.where(fresh, k_cols[s][hd], pltpu.roll(ck_ref[s, g], W - t, axis=1))
            nv_ref[s, g] = jnp.where(fresh, v_cols[s][hd], pltpu.roll(cv_ref[s, g], W - t, axis=1))


def _attn_sample(q, kdn, vdn, kn, vn, ck, cv, mask, kval, sinkrow, bs, t):
    Bd = ck.shape[0]
    rows = bs * t
    kvw = N_KV_HEADS * HEAD_DIM
    S = bs * WINDOW + rows
    tile = lambda i: (i, 0)
    tile3 = lambda i: (i, 0, 0)
    tile4 = lambda i: (i, 0, 0, 0)
    cblk = (bs, N_KV_HEADS, HEAD_DIM, WINDOW)
    return pl.pallas_call(
        functools.partial(_attn_sample_kernel, bs=bs, t=t),
        grid=(Bd // bs,),
        in_specs=[
            pl.BlockSpec((rows, N_HEADS * HEAD_DIM), tile),
            pl.BlockSpec((rows, 2 * kvw), tile),
            pl.BlockSpec((rows, 2 * kvw), tile),
            pl.BlockSpec((rows, kvw), tile),
            pl.BlockSpec((rows, kvw), tile),
            pl.BlockSpec(cblk, tile4),
            pl.BlockSpec(cblk, tile4),
            pl.BlockSpec((S, 4 * rows), lambda i: (0, 0)),
            pl.BlockSpec((1, S, 1), tile3),
            pl.BlockSpec((N_KV_HEADS, 1, 4 * rows), lambda i: (0, 0, 0)),
        ],
        out_specs=[
            pl.BlockSpec((rows, N_HEADS * HEAD_DIM), tile),
            pl.BlockSpec(cblk, tile4),
            pl.BlockSpec(cblk, tile4),
        ],
        out_shape=[
            jax.ShapeDtypeStruct((Bd * t, N_HEADS * HEAD_DIM), BF16),
            jax.ShapeDtypeStruct((Bd,) + cblk[1:], F32),
            jax.ShapeDtypeStruct((Bd,) + cblk[1:], F32),
        ],
        compiler_params=pltpu.CompilerParams(
            dimension_semantics=("arbitrary",), vmem_limit_bytes=VMEM_LIMIT),
        name="attn_sample",
    )(q, kdn, vdn, kn, vn, ck, cv, mask, kval, sinkrow)


def _rope_tables(pos):
    inv = ROPE_THETA ** (-jnp.arange(0, HEAD_DIM, 2, dtype=F32) / HEAD_DIM)
    reps = LANES // (HEAD_DIM // 2)
    sign = jnp.asarray(np.where((np.arange(LANES) % HEAD_DIM) < HEAD_DIM // 2, -1.0, 1.0), F32)
    ang = pos.astype(F32)[:, None] * inv[None, :]
    cos, sin = lax.optimization_barrier((jnp.cos(ang), jnp.sin(ang)))
    return jnp.tile(cos, (1, reps)), jnp.tile(sin, (1, reps)) * sign[None, :]


def _sink_rows(sinks, rows):
    s = (sinks.astype(F32) * LOG2E).reshape(N_KV_HEADS, 1, N_HEADS // N_KV_HEADS, 1)
    return jnp.broadcast_to(s, (N_KV_HEADS, 1, 4, rows)).reshape(N_KV_HEADS, 1, 4 * rows)


def kernel(x_prompt, x_sample, state_pool, cache_k, cache_v, sample_start, norm_mix, norm_ffn, norm_final,
           w_pool, ls_pool, w_qkv, b_qkv, sinks, w_o, b_o, w_rg, b_rg, w_re, b_re, w_gate, w_up, w_down):
    B, T, D = x_prompt.shape
    Bd, Td, _ = x_sample.shape
    kvw = N_KV_HEADS * HEAD_DIM
    row = lambda v: v.reshape(1, -1).astype(F32)

    wp = w_pool[0].astype(BF16)
    wqkv = w_qkv[0].astype(BF16)
    wo = w_o[0].astype(BF16)
    def router_rows(g_part, e_part):
        z = lambda n: jnp.zeros((n,) + g_part.shape[1:], F32)
        return jnp.concatenate([g_part, z(EXPERT_ROW0 - N_EXPERT_GROUPS), e_part,
                                z(ROUTER_ROWS - EXPERT_ROW0 - N_EXPERTS)], axis=0)

    wr = [router_rows(w_rg[l].T, w_re[l].T) for l in range(2)]
    br = [router_rows(b_rg[l][:, None], b_re[l][:, None]) for l in range(2)]

    start = sample_start.astype(jnp.int32)

    x1p, pool_p16 = _pool_prompt(x_prompt, jnp.zeros((B, HALO, D), F32), row(norm_mix[0]), wp, row(ls_pool[0]))
    pos_s = (start[:, None] + jnp.arange(Td, dtype=jnp.int32)[None, :]).reshape(-1)
    x1s, pool_s_t = _pool_sample(x_sample, jnp.transpose(state_pool[0], (1, 0, 2)), start[:, None],
                                 row(norm_mix[0]), wp, row(ls_pool[0]))
    x1s = x1s.reshape(Bd * Td, D)
    pool_s = jnp.transpose(pool_s_t, (1, 0, 2))
    pool_p = pool_p16[:, HALO - POOL_BUF:]

    moe0 = functools.partial(_moe, nf=row(norm_ffn[0]), wr=wr[0], br=br[0])
    x2p, *w0 = moe0(x1p.reshape(B * T, D), wg=w_gate, wu=w_up, wd=w_down, layer=0)
    x2s = moe0(x1s, wg=w0[0], wu=w0[1], wd=w0[2], layer=None)

    cos_p, sin_p = _rope_tables(jnp.arange(T, dtype=jnp.int32))
    cos_s, sin_s = _rope_tables(pos_s)
    g1 = row(norm_mix[1])
    tmq = 1024
    keep = min(WINDOW, T)
    qp, kdp, vdp, kp, vp = _qkv(x2p, g1, wqkv, row(b_qkv[0]), cos_p, sin_p, tmq, T // tmq, keep)
    qs, kds, vds, ks, vs = _qkv(x2s, g1, wqkv, row(b_qkv[0]), cos_s, sin_s, tmq, 1, tmq)

    op = _attn_prompt(qp, kdp, vdp, _sink_rows(sinks[0], WINDOW), B, T)

    bs = 8
    rows = bs * Td
    W = cache_k.shape[2]
    qrow = np.arange(rows)
    ccol = np.arange(bs * W)
    ncol = np.arange(rows)
    samp_q, t_q = qrow // Td, qrow % Td
    m_cache = (samp_q[:, None] == (ccol // W)[None, :]) & ((ccol % W)[None, :] > t_q[:, None])
    m_new = (samp_q[:, None] == (ncol // Td)[None, :]) & ((ncol % Td)[None, :] <= t_q[:, None])
    amask = jnp.asarray(np.tile(np.concatenate([m_cache, m_new], axis=1).T, (1, 4)), F32)
    kv_cache = (jnp.arange(W, dtype=jnp.int32)[None, :] >= (W - start)[:, None]).reshape(Bd // bs, bs * W)
    kval = jnp.concatenate([kv_cache, jnp.ones((Bd // bs, rows), bool)], axis=1).astype(F32)
    kval = kval.reshape(Bd // bs, bs * W + rows, 1)
    to_stored = lambda c: jnp.transpose(c[0], (0, 2, 3, 1))
    from_stored = lambda c: jnp.transpose(c, (0, 3, 1, 2))[None]
    osamp, nk_s, nv_s = _attn_sample(
        qs, kds, vds, ks, vs, to_stored(cache_k), to_stored(cache_v),
        amask, kval, _sink_rows(sinks[0], rows), bs, Td)

    moe1 = functools.partial(_moe, nf=row(norm_ffn[1]), wr=wr[1], br=br[1], final=row(norm_final))
    yp, *w1 = moe1(x2p, wg=w_gate, wu=w_up, wd=w_down, layer=1, oproj=(op, wo, row(b_o[0])))
    ys = moe1(x2s, wg=w1[0], wu=w1[1], wd=w1[2], layer=None, oproj=(osamp, wo, row(b_o[0])))

    k_p = kp.reshape(1, B, keep, N_KV_HEADS, HEAD_DIM)
    v_p = vp.reshape(1, B, keep, N_KV_HEADS, HEAD_DIM)
    return (yp.reshape(B, T, D), ys.reshape(Bd, Td, D), pool_p[None], k_p, v_p, pool_s[None],
            from_stored(nk_s), from_stored(nv_s))
```

```python
import functools

import jax
import jax.numpy as jnp
import numpy as np
from jax import lax
from jax.experimental import pallas as pl
from jax.experimental.pallas import tpu as pltpu

F32 = jnp.float32
BF16 = jnp.bfloat16

D_MODEL = 1024
POOL_WINDOWS = (2, 4, 8, 16)
POOL_GROUP_DIM = 256
POOL_BUF = 15
HALO = 16
HEAD_DIM = 64
N_HEADS = 16
N_KV_HEADS = 4
WINDOW = 128
ROPE_THETA = 10000.0
N_EXPERT_GROUPS = 4
EXPERTS_PER_GROUP = 4
N_EXPERTS = 16
D_EXPERT = 256
RMS_EPS = 1e-6
LANES = 128
ROUTER_ROWS = 32
EXPERT_ROW0 = 8
CAST_SLOTS = 4
VMEM_LIMIT = 56 * 1024 * 1024
LOG2E = 1.4426950408889634
Q_SCALE = LOG2E * HEAD_DIM ** -0.5


def _rms(x, g):
    ms = jnp.mean(x * x, axis=-1, keepdims=True)
    return x * lax.rsqrt(ms + RMS_EPS) * g


def _pool_project(h, wins, pos, wp_ref, ls):
    G = POOL_GROUP_DIM
    outs = []
    for g, w in enumerate(POOL_WINDOWS):
        cnt = jnp.minimum(w, pos + 1).astype(F32)
        d = wins[g] / cnt - h[:, g * G:(g + 1) * G]
        outs.append(jnp.dot(d.astype(BF16), wp_ref[g], preferred_element_type=F32))
    return jnp.concatenate(outs, axis=-1) * ls


POOL_BLOCK = 128


def _pool_band():
    t = np.arange(POOL_BLOCK)[:, None] + HALO
    k = np.arange(POOL_BLOCK + HALO)[None, :]
    return jnp.asarray(np.stack([(k <= t) & (k > t - w) for w in POOL_WINDOWS]), BF16)


def _pool_prompt_kernel(x_ref, buf_ref, g_ref, wp_ref, ls_ref, band_ref, o_ref, nb_ref, hc, *, tq):
    t = pl.program_id(1)
    G = POOL_GROUP_DIM

    @pl.when(t == 0)
    def _():
        hc[pl.ds(0, HALO), :] = buf_ref[0]

    @pl.when(t > 0)
    def _():
        hc[pl.ds(0, HALO), :] = hc[pl.ds(tq, HALO), :]

    x = x_ref[0]
    h = _rms(x, g_ref[...])
    hc[pl.ds(HALO, tq), :] = h
    hb = hc[...].astype(BF16)
    wins = []
    for g in range(len(POOL_WINDOWS)):
        blocks = [jnp.dot(band_ref[g], hb[b * POOL_BLOCK:(b + 1) * POOL_BLOCK + HALO, g * G:(g + 1) * G],
                          preferred_element_type=F32) for b in range(tq // POOL_BLOCK)]
        wins.append(jnp.concatenate(blocks, axis=0))
    pos = t * tq + lax.broadcasted_iota(jnp.int32, (tq, 1), 0)
    o_ref[0] = x + _pool_project(h, wins, pos, wp_ref, ls_ref[...])

    @pl.when(t == pl.num_programs(1) - 1)
    def _():
        nb_ref[0] = hc[pl.ds(tq, HALO), :]


def _pool_prompt(x, buf16, g, wp, ls, tq=1024):
    B, T, D = x.shape
    return pl.pallas_call(
        functools.partial(_pool_prompt_kernel, tq=tq),
        grid=(B, T // tq),
        in_specs=[
            pl.BlockSpec((1, tq, D), lambda b, t: (b, t, 0)),
            pl.BlockSpec((1, HALO, D), lambda b, t: (b, 0, 0)),
            pl.BlockSpec((1, D), lambda b, t: (0, 0)),
            pl.BlockSpec((4, POOL_GROUP_DIM, POOL_GROUP_DIM), lambda b, t: (0, 0, 0)),
            pl.BlockSpec((1, D), lambda b, t: (0, 0)),
            pl.BlockSpec((4, POOL_BLOCK, POOL_BLOCK + HALO), lambda b, t: (0, 0, 0)),
        ],
        out_specs=[
            pl.BlockSpec((1, tq, D), lambda b, t: (b, t, 0)),
            pl.BlockSpec((1, HALO, D), lambda b, t: (b, 0, 0)),
        ],
        out_shape=[jax.ShapeDtypeStruct((B, T, D), F32), jax.ShapeDtypeStruct((B, HALO, D), F32)],
        scratch_shapes=[pltpu.VMEM((HALO + tq, D), F32)],
        compiler_params=pltpu.CompilerParams(
            dimension_semantics=("arbitrary", "arbitrary"), vmem_limit_bytes=VMEM_LIMIT),
        name="pool_prompt",
    )(x, buf16, g, wp, ls, _pool_band())


def _pool_sample_kernel(x_ref, st_ref, start_ref, g_ref, wp_ref, ls_ref, o_ref, ns_ref, *, bs, t):
    G = POOL_GROUP_DIM
    xs = [x_ref[:, i, :] for i in range(t)]
    hn = [_rms(x, g_ref[...]) for x in xs]
    hist = [st_ref[r] for r in range(POOL_BUF)] + hn

    def doubled(prev, lag, lo):
        out = [None] * len(prev)
        for i in range(len(prev)):
            if i >= lag and prev[i] is not None and prev[i - lag] is not None:
                out[i] = prev[i][:, lo:] + prev[i - lag][:, lo:]
        return out

    s2 = doubled(hist, 1, 0)
    s4 = doubled(s2, 2, G)
    s8 = doubled(s4, 4, G)
    s16 = doubled(s8, 8, G)
    wins = (s2, s4, s8, s16)
    start = start_ref[...]
    ds = [[] for _ in POOL_WINDOWS]
    for i in range(t):
        for g, w in enumerate(POOL_WINDOWS):
            cnt = jnp.minimum(w, start + (i + 1)).astype(F32)
            ds[g].append(wins[g][POOL_BUF + i][:, :G] / cnt - hn[i][:, g * G:(g + 1) * G])
    ys = [jnp.dot(jnp.concatenate(ds[g], axis=0).astype(BF16), wp_ref[g], preferred_element_type=F32)
          for g in range(len(POOL_WINDOWS))]
    y = jnp.concatenate(ys, axis=-1) * ls_ref[...]
    for i in range(t):
        o_ref[:, i, :] = xs[i] + y[i * bs:(i + 1) * bs]
    for r in range(POOL_BUF):
        ns_ref[r] = hist[r + t]


def _pool_sample(x, st, start, g, wp, ls, bs=32):
    Bd, t, D = x.shape
    return pl.pallas_call(
        functools.partial(_pool_sample_kernel, bs=bs, t=t),
        grid=(Bd // bs,),
        in_specs=[
            pl.BlockSpec((bs, t, D), lambda i: (i, 0, 0)),
            pl.BlockSpec((POOL_BUF, bs, D), lambda i: (0, i, 0)),
            pl.BlockSpec((bs, 1), lambda i: (i, 0)),
            pl.BlockSpec((1, D), lambda i: (0, 0)),
            pl.BlockSpec((4, POOL_GROUP_DIM, POOL_GROUP_DIM), lambda i: (0, 0, 0)),
            pl.BlockSpec((1, D), lambda i: (0, 0)),
        ],
        out_specs=[pl.BlockSpec((bs, t, D), lambda i: (i, 0, 0)),
                   pl.BlockSpec((POOL_BUF, bs, D), lambda i: (0, i, 0))],
        out_shape=[jax.ShapeDtypeStruct((Bd, t, D), F32), jax.ShapeDtypeStruct((POOL_BUF, Bd, D), F32)],
        compiler_params=pltpu.CompilerParams(
            dimension_semantics=("arbitrary",), vmem_limit_bytes=VMEM_LIMIT),
        name="pool_sample",
    )(x, st, start, g, wp, ls)


def _route(lg):
    R = lg.shape[1]
    big = jnp.float32(1 << 20)
    neg = jnp.float32(-jnp.inf)
    r8 = lax.broadcasted_iota(jnp.int32, (8, R), 0).astype(F32)
    r16 = lax.broadcasted_iota(jnp.int32, (N_EXPERTS, R), 0).astype(F32)
    is_g = r8 < N_EXPERT_GROUPS
    gl = jnp.where(is_g, lg[0:8], neg)
    m = jnp.max(gl, axis=0, keepdims=True)
    gidx = jnp.min(jnp.where(gl == m, r8, big), axis=0, keepdims=True)
    z = jnp.sum(jnp.where(is_g, jnp.exp(gl - m), 0.0), axis=0, keepdims=True)
    gw = 1.0 / z
    lo = gidx * EXPERTS_PER_GROUP
    in_grp = (r16 >= lo) & (r16 < lo + EXPERTS_PER_GROUP)
    el = jnp.where(in_grp, lg[EXPERT_ROW0:EXPERT_ROW0 + N_EXPERTS], neg)
    v1 = jnp.max(el, axis=0, keepdims=True)
    i1 = jnp.min(jnp.where(el == v1, r16, big), axis=0, keepdims=True)
    el2 = jnp.where(r16 == i1, neg, el)
    v2 = jnp.max(el2, axis=0, keepdims=True)
    i2 = jnp.min(jnp.where(el2 == v2, r16, big), axis=0, keepdims=True)
    t = jnp.exp(v2 - v1)
    w1 = 1.0 / (1.0 + t)
    w2 = t * w1
    e8 = jnp.where(is_g, r8, r8 - EXPERTS_PER_GROUP)
    cw = gw * (jnp.where(e8 == i1 - lo, w1, 0.0) + jnp.where(e8 == i2 - lo, w2, 0.0))
    return gidx, cw


def _moe_kernel(*refs, has_oproj, has_final, cast_layer, tm, bm):
    it = iter(refs)
    x_ref = next(it)
    if has_oproj:
        oin_ref, wo_ref, bo_ref = next(it), next(it), next(it)
    nf_ref, wr_ref, br_ref, wg_in, wu_in, wd_in = (next(it) for _ in range(6))
    if has_final:
        fn_ref = next(it)
    out_ref = next(it)
    if cast_layer is not None:
        wg_out, wu_out, wd_out = (next(it) for _ in range(3))
    hs, cs, ys, sel_s, earlier = (next(it) for _ in range(5))
    if cast_layer is None:
        wg_ref, wu_ref, wd_ref = wg_in, wu_in, wd_in
    else:
        wg_ref, wu_ref, wd_ref, stage_gu, stage_d, load_sem, store_sem = (next(it) for _ in range(7))
        pairs = ((wg_in, wg_ref, wg_out, stage_gu), (wu_in, wu_ref, wu_out, stage_gu), (wd_in, wd_ref, wd_out, stage_d))

        def store_copy(k):
            return pltpu.make_async_copy(pairs[k][1], pairs[k][2], store_sem.at[k])

        @pl.when(pl.program_id(0) == 0)
        def _():
            ahead = stage_gu.shape[0] - 1
            for src, dst, _, stage in pairs:
                def load(e, src=src, stage=stage):
                    slot = e % (ahead + 1)
                    return pltpu.make_async_copy(src.at[cast_layer, e], stage.at[slot], load_sem.at[slot])

                for e in range(ahead):
                    load(e).start()
                for e in range(N_EXPERTS):
                    if e + ahead < N_EXPERTS:
                        load(e + ahead).start()
                    load(e).wait()
                    dst[e] = stage[e % (ahead + 1)].astype(BF16)
            for k in range(len(pairs)):
                store_copy(k).start()

        @pl.when(pl.program_id(0) == pl.num_programs(0) - 1)
        def _():
            for k in range(len(pairs)):
                store_copy(k).wait()
    min_blk = -(-tm // bm)
    nblk = tm // bm + N_EXPERT_GROUPS
    cap = nblk * bm
    assert bm % 16 == 0
    nt = (((1,), (1,)), ((), ()))

    @pl.when(pl.program_id(0) == 0)
    def _():
        ri = lax.broadcasted_iota(jnp.int32, (tm, tm), 0)
        ci = lax.broadcasted_iota(jnp.int32, (tm, tm), 1)
        earlier[...] = jnp.where(ri < ci, 1.0, 0.0).astype(BF16)

    x = x_ref[...]
    if has_oproj:
        x = x + jnp.dot(oin_ref[...], wo_ref[...], preferred_element_type=F32) + bo_ref[...]
    if has_oproj:
        out_ref[...] = x
    h = _rms(x, nf_ref[...])
    hb = h.astype(BF16)
    h_lo = (h - hb.astype(F32)).astype(BF16)
    wr = wr_ref[...]
    both = lax.dot_general(wr, hb, nt, preferred_element_type=F32)
    lg = (both[0:ROUTER_ROWS] + both[ROUTER_ROWS:]
          + lax.dot_general(wr[0:ROUTER_ROWS], h_lo, nt, preferred_element_type=F32)) + br_ref[...]
    gidx, cw = _route(lg)
    r8 = lax.broadcasted_iota(jnp.int32, (8, tm), 0)
    cw_hi = cw.astype(BF16).astype(F32)
    cwt = jnp.concatenate([jnp.where(r8 < EXPERTS_PER_GROUP, cw_hi, cw - cw_hi),
                           jnp.zeros((LANES - 8, tm), F32)], axis=0).astype(BF16)

    oh = jnp.where(r8.astype(F32) == gidx, 1.0, 0.0)
    cnt_before = jnp.dot(oh.astype(BF16), earlier[...], preferred_element_type=F32)
    pos_row = jnp.sum(oh * cnt_before, axis=0, keepdims=True)
    start_blk, n_blk = [], []
    off = jnp.int32(0)
    for g in range(N_EXPERT_GROUPS):
        in_g = gidx == g
        n_g = jnp.sum(jnp.where(in_g, 1.0, 0.0)).astype(jnp.int32)
        blocks = sum(jnp.where(n_g > k * bm, 1, 0) for k in range(min_blk))
        start_blk.append(off)
        n_blk.append(blocks)
        pos_row = pos_row + jnp.where(in_g, (off * bm).astype(F32), 0.0)
        off = off + blocks

    total_blk = off
    main = min_blk * bm
    extra = ((main, bm, total_blk > min_blk), (main + bm, cap - main - bm, total_blk > min_blk + 1))

    def sort_rows(r0, n):
        sub = (r0 + lax.broadcasted_iota(jnp.int32, (n, tm), 0)).astype(F32)
        sel = jnp.where(pos_row == sub, 1.0, 0.0).astype(BF16)
        sel_s[r0:r0 + n, :] = sel
        hs[r0:r0 + n, :] = jnp.dot(sel, hb, preferred_element_type=F32).astype(BF16)
        cs[r0:r0 + n, :] = lax.dot_general(sel, cwt, nt, preferred_element_type=F32)

    sort_rows(0, main)
    for r0, n, needed in extra:
        pl.when(needed)(functools.partial(sort_rows, r0, n))
    ys[min_blk * bm:, :] = jnp.zeros((cap - min_blk * bm, D_MODEL), BF16)

    for g in range(N_EXPERT_GROUPS):
        wd = wd_ref[g * EXPERTS_PER_GROUP:(g + 1) * EXPERTS_PER_GROUP].reshape(
            EXPERTS_PER_GROUP * D_EXPERT, D_MODEL)

        def block(b, carry, g=g, wd=wd):
            rows = pl.ds(pl.multiple_of(b * bm, 16), bm)
            hblk = hs[rows, :]
            cblk = cs[rows, :]
            ln = lax.broadcasted_iota(jnp.int32, cblk.shape, 1)
            parts = []
            for e in range(EXPERTS_PER_GROUP):
                ce = jnp.sum(jnp.where((ln == e) | (ln == e + EXPERTS_PER_GROUP), cblk, 0.0),
                             axis=-1, keepdims=True)
                gt = jnp.dot(hblk, wg_ref[g * EXPERTS_PER_GROUP + e], preferred_element_type=F32)
                up = jnp.dot(hblk, wu_ref[g * EXPERTS_PER_GROUP + e], preferred_element_type=F32)
                a = gt / (1.0 + jnp.exp(-gt)) * up * ce
                parts.append(a.astype(BF16))
            a_all = jnp.concatenate(parts, axis=-1)
            ys[rows, :] = jnp.dot(a_all, wd, preferred_element_type=F32).astype(BF16)
            return carry

        lax.fori_loop(start_blk[g], start_blk[g] + n_blk[g], block, 0)

    def unsorted(r0, n):
        return lax.dot_general(sel_s[r0:r0 + n, :], ys[r0:r0 + n, :], (((0,), (0,)), ((), ())),
                               preferred_element_type=F32)

    def unsort_more(r0, n):
        out_ref[...] += unsorted(r0, n)

    out_ref[...] = (out_ref[...] if has_oproj else x_ref[...]) + unsorted(0, main)
    for r0, n, needed in extra:
        pl.when(needed)(functools.partial(unsort_more, r0, n))
    if has_final:
        out_ref[...] = _rms(out_ref[...], fn_ref[...])


def _moe(x, nf, wr, br, wg, wu, wd, layer, oproj=None, final=None, tm=512, bm=144):
    N, D = x.shape
    has_oproj = oproj is not None
    has_final = final is not None
    const2 = lambda i: (0, 0)
    const3 = lambda i: (0, 0, 0)
    cast = layer is not None
    tile = lambda i: (i, 0)
    once = pl.Buffered(1)
    args = [x]
    in_specs = [pl.BlockSpec((tm, D), tile)]
    if has_oproj:
        o, wo, bo = oproj
        args += [o, wo, bo]
        in_specs += [pl.BlockSpec((tm, D), tile), pl.BlockSpec((D, D), const2, pipeline_mode=once),
                     pl.BlockSpec((1, D), const2)]
    wr_hi = wr.astype(BF16)
    wr_lo = (wr - wr_hi.astype(F32)).astype(BF16)
    args += [nf, jnp.concatenate([wr_hi, wr_lo], axis=0), br, wg, wu, wd]
    in_specs += [
        pl.BlockSpec((1, D), const2),
        pl.BlockSpec((2 * ROUTER_ROWS, D), const2, pipeline_mode=once),
        pl.BlockSpec((ROUTER_ROWS, 1), const2),
    ]
    gu_shape, d_shape = (N_EXPERTS, D, D_EXPERT), (N_EXPERTS, D_EXPERT, D)
    if cast:
        in_specs += [pl.BlockSpec(memory_space=pl.ANY)] * 3
    else:
        in_specs += [pl.BlockSpec(gu_shape, const3, pipeline_mode=once), pl.BlockSpec(gu_shape, const3, pipeline_mode=once),
                     pl.BlockSpec(d_shape, const3, pipeline_mode=once)]
    if has_final:
        args.append(final)
        in_specs.append(pl.BlockSpec((1, D), const2))
    cap = (tm // bm + N_EXPERT_GROUPS) * bm
    out_specs = [pl.BlockSpec((tm, D), tile)]
    out_shape = [jax.ShapeDtypeStruct((N, D), F32)]
    scratch = [
        pltpu.VMEM((cap, D), BF16),
        pltpu.VMEM((cap, LANES), F32),
        pltpu.VMEM((cap, D), BF16),
        pltpu.VMEM((cap, tm), BF16),
        pltpu.VMEM((tm, tm), BF16),
    ]
    if cast:
        out_specs += [pl.BlockSpec(memory_space=pl.ANY)] * 3
        out_shape += [jax.ShapeDtypeStruct(gu_shape, BF16)] * 2 + [jax.ShapeDtypeStruct(d_shape, BF16)]
        scratch += [pltpu.VMEM(gu_shape, BF16), pltpu.VMEM(gu_shape, BF16), pltpu.VMEM(d_shape, BF16),
                    pltpu.VMEM((CAST_SLOTS,) + gu_shape[1:], F32), pltpu.VMEM((CAST_SLOTS,) + d_shape[1:], F32),
                    pltpu.SemaphoreType.DMA((CAST_SLOTS,)), pltpu.SemaphoreType.DMA((3,))]
    res = pl.pallas_call(
        functools.partial(_moe_kernel, has_oproj=has_oproj, has_final=has_final, cast_layer=layer, tm=tm, bm=bm),
        grid=(N // tm,),
        in_specs=in_specs,
        out_specs=out_specs,
        out_shape=out_shape,
        scratch_shapes=scratch,
        compiler_params=pltpu.CompilerParams(
            dimension_semantics=("arbitrary",), vmem_limit_bytes=VMEM_LIMIT),
        name="moe",
    )(*args)
    return res if cast else res[0]


def _swap_halves(x, low):
    return jnp.where(low, pltpu.roll(x, LANES - 32, axis=1), pltpu.roll(x, 32, axis=1))


def _qkv_kernel(x_ref, g_ref, w_ref, b_ref, cos_ref, sin_ref, q_ref, kd_ref, vd_ref, k_ref, v_ref, *, keep):
    tm = x_ref.shape[0]
    h = _rms(x_ref[...], g_ref[...]).astype(BF16)
    cos = cos_ref[...]
    sin = sin_ref[...]
    lane = lax.broadcasted_iota(jnp.int32, cos.shape, 1)
    low32 = (lane % HEAD_DIM) < (HEAD_DIM // 2)
    low64 = lane < HEAD_DIM
    nq = N_HEADS * HEAD_DIM
    nk = N_KV_HEADS * HEAD_DIM
    cw = 2 * LANES

    def project(c0):
        y = jnp.dot(h, w_ref[:, c0:c0 + cw], preferred_element_type=F32) + b_ref[:, c0:c0 + cw]
        return [y[:, i * LANES:(i + 1) * LANES] for i in range(cw // LANES)]

    def rope(c):
        return c * cos + _swap_halves(c, low32) * sin

    def dup(c):
        r = pltpu.roll(c, HEAD_DIM, axis=1)
        return jnp.where(low64, c, r), jnp.where(low64, r, c)

    for c0 in range(0, nq, cw):
        for i, c in enumerate(project(c0)):
            lanes = slice(c0 + i * LANES, c0 + (i + 1) * LANES)
            q_ref[:, lanes] = (rope(c) * Q_SCALE).astype(BF16)
    for c0 in range(0, nk, cw):
        for (src, rot, f_ref, d_ref) in ((nq, True, k_ref, kd_ref), (nq + nk, False, v_ref, vd_ref)):
            for i, c in enumerate(project(src + c0)):
                j = c0 // LANES + i
                c = rope(c) if rot else c
                f_ref[:, j * LANES:(j + 1) * LANES] = c[tm - keep:]
                d0, d1 = dup(c)
                d_ref[:, 2 * j * LANES:(2 * j + 1) * LANES] = d0.astype(BF16)
                d_ref[:, (2 * j + 1) * LANES:(2 * j + 2) * LANES] = d1.astype(BF16)


def _qkv(x, g, w, b, cos, sin, tm, n_pos_tiles, keep):
    N, D = x.shape
    nq, nk = N_HEADS * HEAD_DIM, N_KV_HEADS * HEAD_DIM
    const = lambda i: (0, 0)
    tile = lambda i: (i, 0)
    ptile = lambda i: (i % n_pos_tiles, 0)
    seq = lambda i: (i // n_pos_tiles, 0)
    n_seq = N // (tm * n_pos_tiles)
    return pl.pallas_call(
        functools.partial(_qkv_kernel, keep=keep),
        grid=(N // tm,),
        in_specs=[
            pl.BlockSpec((tm, D), tile),
            pl.BlockSpec((1, D), const),
            pl.BlockSpec((D, nq + 2 * nk), const),
            pl.BlockSpec((1, nq + 2 * nk), const),
            pl.BlockSpec((tm, LANES), ptile),
            pl.BlockSpec((tm, LANES), ptile),
        ],
        out_specs=[
            pl.BlockSpec((tm, nq), tile),
            pl.BlockSpec((tm, 2 * nk), tile),
            pl.BlockSpec((tm, 2 * nk), tile),
            pl.BlockSpec((keep, nk), seq),
            pl.BlockSpec((keep, nk), seq),
        ],
        out_shape=[
            jax.ShapeDtypeStruct((N, nq), BF16),
            jax.ShapeDtypeStruct((N, 2 * nk), BF16),
            jax.ShapeDtypeStruct((N, 2 * nk), BF16),
            jax.ShapeDtypeStruct((n_seq * keep, nk), F32),
            jax.ShapeDtypeStruct((n_seq * keep, nk), F32),
        ],
        compiler_params=pltpu.CompilerParams(
            dimension_semantics=("arbitrary",), vmem_limit_bytes=VMEM_LIMIT),
        name="qkv",
    )(x, g, w, b, cos, sin)


def _stack_heads(qc_list):
    lane = lax.broadcasted_iota(jnp.int32, qc_list[0].shape, 1)
    low = lane < HEAD_DIM
    zero = jnp.zeros_like(qc_list[0])
    stack = []
    for qc in qc_list:
        stack.append(jnp.where(low, qc, zero))
        stack.append(jnp.where(low, zero, qc))
    return jnp.concatenate(stack, axis=0)


def _softmax_keys_on_rows(s, sinkrow):
    m = jnp.maximum(jnp.max(s, axis=0, keepdims=True), sinkrow)
    p = jnp.exp2(s - m)
    den = jnp.sum(p, axis=0, keepdims=True) + jnp.exp2(sinkrow - m)
    return (p * (1.0 / den)).astype(BF16)


def _unstack_heads(o, R):
    lowf = lax.broadcasted_iota(jnp.int32, (R, LANES), 1) < HEAD_DIM
    return (jnp.where(lowf, o[0:R], o[R:2 * R]), jnp.where(lowf, o[2 * R:3 * R], o[3 * R:4 * R]))


NT_DIMS = (((1,), (1,)), ((), ()))
TN_DIMS = (((0,), (0,)), ((), ()))
TT_DIMS = (((0,), (1,)), ((), ()))


def _attend(qc_list, kd, vd, bias, sinkrow):
    qs = _stack_heads(qc_list)
    s = lax.dot_general(kd, qs, NT_DIMS, preferred_element_type=F32) + bias
    p = _softmax_keys_on_rows(s, sinkrow)
    o = lax.dot_general(p, vd, TN_DIMS, preferred_element_type=F32)
    return _unstack_heads(o, qc_list[0].shape[0])


def _attn_prompt_kernel(q_ref, kp_ref, kc_ref, vp_ref, vc_ref, sink_ref, o_ref, *, tq):
    n = pl.program_id(1)
    B = WINDOW
    kd = jnp.concatenate([kp_ref[...], kc_ref[...]], axis=0)
    vd = jnp.concatenate([vp_ref[...], vc_ref[...]], axis=0)
    kj = lax.broadcasted_iota(jnp.int32, (2 * B, B), 0)
    qi = lax.broadcasted_iota(jnp.int32, (2 * B, B), 1)
    rel = B + qi - kj
    band = (rel >= 0) & (rel < WINDOW)
    neg = jnp.float32(-jnp.inf)
    band_bias = jnp.concatenate([jnp.where(band, 0.0, neg)] * 4, axis=1)
    first_bias = jnp.concatenate([jnp.where(band & ((kj >= B) | (n > 0)), 0.0, neg)] * 4, axis=1)
    for j in range(tq // B):
        bias = first_bias if j == 0 else band_bias
        for g in range(N_KV_HEADS):
            qc = [q_ref[j * B:(j + 1) * B, (2 * g + i) * LANES:(2 * g + i + 1) * LANES] for i in range(2)]
            kg = kd[j * B:(j + 2) * B, g * LANES:(g + 1) * LANES]
            vg = vd[j * B:(j + 2) * B, g * LANES:(g + 1) * LANES]
            o0, o1 = _attend(qc, kg, vg, bias, sink_ref[g])
            o_ref[j * B:(j + 1) * B, 2 * g * LANES:(2 * g + 1) * LANES] = o0.astype(BF16)
            o_ref[j * B:(j + 1) * B, (2 * g + 1) * LANES:(2 * g + 2) * LANES] = o1.astype(BF16)


def _attn_prompt(q, kd, vd, sinkrow, batch, seq, tq=1024):
    N = q.shape[0]
    nt = seq // tq
    r = tq // WINDOW
    cur = lambda b, n: (b * nt + n, 0)
    prev = lambda b, n: (jnp.maximum((b * nt + n) * r - 1, 0), 0)
    kvw = 2 * N_KV_HEADS * HEAD_DIM
    return pl.pallas_call(
        functools.partial(_attn_prompt_kernel, tq=tq),
        grid=(batch, nt),
        in_specs=[
            pl.BlockSpec((tq, N_HEADS * HEAD_DIM), cur),
            pl.BlockSpec((WINDOW, kvw), prev),
            pl.BlockSpec((tq, kvw), cur),
            pl.BlockSpec((WINDOW, kvw), prev),
            pl.BlockSpec((tq, kvw), cur),
            pl.BlockSpec((N_KV_HEADS, 1, 4 * WINDOW), lambda b, n: (0, 0, 0)),
        ],
        out_specs=pl.BlockSpec((tq, N_HEADS * HEAD_DIM), cur),
        out_shape=jax.ShapeDtypeStruct((N, N_HEADS * HEAD_DIM), BF16),
        compiler_params=pltpu.CompilerParams(
            dimension_semantics=("arbitrary", "arbitrary"), vmem_limit_bytes=VMEM_LIMIT),
        name="attn_prompt",
    )(q, kd, kd, vd, vd, sinkrow)


def _attn_sample_kernel(q_ref, kdn_ref, vdn_ref, kn_ref, vn_ref, ck_ref, cv_ref, mask_ref, kval_ref, sink_ref,
                        o_ref, nk_ref, nv_ref, *, bs, t):
    W = WINDOW
    rows = bs * t
    bias = jnp.where((mask_ref[...] > 0) & (kval_ref[0] > 0), 0.0, -jnp.inf).astype(F32)
    lane = lax.broadcasted_iota(jnp.int32, (HEAD_DIM, W), 1)
    fresh = lane >= W - t

    def new_cols(n_ref):
        n = jnp.concatenate([n_ref[...], jnp.zeros((LANES - rows, N_KV_HEADS * HEAD_DIM), F32)], axis=0)
        nt_ = jnp.transpose(n)
        return [pltpu.roll(nt_, (W - t - s * t) % LANES, axis=1) for s in range(bs)]

    k_cols, v_cols = new_cols(kn_ref), new_cols(vn_ref)
    for g in range(N_KV_HEADS):
        hd = slice(g * HEAD_DIM, (g + 1) * HEAD_DIM)
        kt = jnp.concatenate([ck_ref[s, g] for s in range(bs)], axis=1).astype(BF16)
        vt = jnp.concatenate([cv_ref[s, g] for s in range(bs)], axis=1).astype(BF16)
        kt2 = jnp.concatenate([kt, kt], axis=0)
        vt2 = jnp.concatenate([vt, vt], axis=0)
        qs = _stack_heads([q_ref[:, (2 * g + i) * LANES:(2 * g + i + 1) * LANES] for i in range(2)])
        s_old = lax.dot_general(kt2, qs, TT_DIMS, preferred_element_type=F32)
        s_new = lax.dot_general(kdn_ref[:, g * LANES:(g + 1) * LANES], qs, NT_DIMS, preferred_element_type=F32)
        p = _softmax_keys_on_rows(jnp.concatenate([s_old, s_new], axis=0) + bias, sink_ref[g])
        o = (lax.dot_general(p[:bs * W], vt2, TT_DIMS, preferred_element_type=F32)
             + lax.dot_general(p[bs * W:], vdn_ref[:, g * LANES:(g + 1) * LANES], TN_DIMS,
                               preferred_element_type=F32))
        o0, o1 = _unstack_heads(o, rows)
        o_ref[:, 2 * g * LANES:(2 * g + 1) * LANES] = o0.astype(BF16)
        o_ref[:, (2 * g + 1) * LANES:(2 * g + 2) * LANES] = o1.astype(BF16)
        for s in range(bs):
            nk_ref[s, g] = jnp.where(fresh, k_cols[s][hd], pltpu.roll(ck_ref[s, g], W - t, axis=1))
            nv_ref[s, g] = jnp.where(fresh, v_cols[s][hd], pltpu.roll(cv_ref[s, g], W - t, axis=1))


def _attn_sample(q, kdn, vdn, kn, vn, ck, cv, mask, kval, sinkrow, bs, t):
    Bd = ck.shape[0]
    rows = bs * t
    kvw = N_KV_HEADS * HEAD_DIM
    S = bs * WINDOW + rows
    tile = lambda i: (i, 0)
    tile3 = lambda i: (i, 0, 0)
    tile4 = lambda i: (i, 0, 0, 0)
    cblk = (bs, N_KV_HEADS, HEAD_DIM, WINDOW)
    return pl.pallas_call(
        functools.partial(_attn_sample_kernel, bs=bs, t=t),
        grid=(Bd // bs,),
        in_specs=[
            pl.BlockSpec((rows, N_HEADS * HEAD_DIM), tile),
            pl.BlockSpec((rows, 2 * kvw), tile),
            pl.BlockSpec((rows, 2 * kvw), tile),
            pl.BlockSpec((rows, kvw), tile),
            pl.BlockSpec((rows, kvw), tile),
            pl.BlockSpec(cblk, tile4),
            pl.BlockSpec(cblk, tile4),
            pl.BlockSpec((S, 4 * rows), lambda i: (0, 0)),
            pl.BlockSpec((1, S, 1), tile3),
            pl.BlockSpec((N_KV_HEADS, 1, 4 * rows), lambda i: (0, 0, 0)),
        ],
        out_specs=[
            pl.BlockSpec((rows, N_HEADS * HEAD_DIM), tile),
            pl.BlockSpec(cblk, tile4),
            pl.BlockSpec(cblk, tile4),
        ],
        out_shape=[
            jax.ShapeDtypeStruct((Bd * t, N_HEADS * HEAD_DIM), BF16),
            jax.ShapeDtypeStruct((Bd,) + cblk[1:], F32),
            jax.ShapeDtypeStruct((Bd,) + cblk[1:], F32),
        ],
        compiler_params=pltpu.CompilerParams(
            dimension_semantics=("arbitrary",), vmem_limit_bytes=VMEM_LIMIT),
        name="attn_sample",
    )(q, kdn, vdn, kn, vn, ck, cv, mask, kval, sinkrow)


def _rope_tables(pos):
    inv = ROPE_THETA ** (-jnp.arange(0, HEAD_DIM, 2, dtype=F32) / HEAD_DIM)
    reps = LANES // (HEAD_DIM // 2)
    sign = jnp.asarray(np.where((np.arange(LANES) % HEAD_DIM) < HEAD_DIM // 2, -1.0, 1.0), F32)
    ang = pos.astype(F32)[:, None] * inv[None, :]
    cos, sin = lax.optimization_barrier((jnp.cos(ang), jnp.sin(ang)))
    return jnp.tile(cos, (1, reps)), jnp.tile(sin, (1, reps)) * sign[None, :]


def _sink_rows(sinks, rows):
    s = (sinks.astype(F32) * LOG2E).reshape(N_KV_HEADS, 1, N_HEADS // N_KV_HEADS, 1)
    return jnp.broadcast_to(s, (N_KV_HEADS, 1, 4, rows)).reshape(N_KV_HEADS, 1, 4 * rows)


def kernel(x_prompt, x_sample, state_pool, cache_k, cache_v, sample_start, norm_mix, norm_ffn, norm_final,
           w_pool, ls_pool, w_qkv, b_qkv, sinks, w_o, b_o, w_rg, b_rg, w_re, b_re, w_gate, w_up, w_down):
    B, T, D = x_prompt.shape
    Bd, Td, _ = x_sample.shape
    kvw = N_KV_HEADS * HEAD_DIM
    row = lambda v: v.reshape(1, -1).astype(F32)

    wp = w_pool[0].astype(BF16)
    wqkv = w_qkv[0].astype(BF16)
    wo = w_o[0].astype(BF16)
    def router_rows(g_part, e_part):
        z = lambda n: jnp.zeros((n,) + g_part.shape[1:], F32)
        return jnp.concatenate([g_part, z(EXPERT_ROW0 - N_EXPERT_GROUPS), e_part,
                                z(ROUTER_ROWS - EXPERT_ROW0 - N_EXPERTS)], axis=0)

    wr = [router_rows(w_rg[l].T, w_re[l].T) for l in range(2)]
    br = [router_rows(b_rg[l][:, None], b_re[l][:, None]) for l in range(2)]

    start = sample_start.astype(jnp.int32)

    x1p, pool_p16 = _pool_prompt(x_prompt, jnp.zeros((B, HALO, D), F32), row(norm_mix[0]), wp, row(ls_pool[0]))
    pos_s = (start[:, None] + jnp.arange(Td, dtype=jnp.int32)[None, :]).reshape(-1)
    x1s, pool_s_t = _pool_sample(x_sample, jnp.transpose(state_pool[0], (1, 0, 2)), start[:, None],
                                 row(norm_mix[0]), wp, row(ls_pool[0]))
    x1s = x1s.reshape(Bd * Td, D)
    pool_s = jnp.transpose(pool_s_t, (1, 0, 2))
    pool_p = pool_p16[:, HALO - POOL_BUF:]

    moe0 = functools.partial(_moe, nf=row(norm_ffn[0]), wr=wr[0], br=br[0])
    x2p, *w0 = moe0(x1p.reshape(B * T, D), wg=w_gate, wu=w_up, wd=w_down, layer=0)
    x2s = moe0(x1s, wg=w0[0], wu=w0[1], wd=w0[2], layer=None)

    cos_p, sin_p = _rope_tables(jnp.arange(T, dtype=jnp.int32))
    cos_s, sin_s = _rope_tables(pos_s)
    g1 = row(norm_mix[1])
    tmq = 1024
    keep = min(WINDOW, T)
    qp, kdp, vdp, kp, vp = _qkv(x2p, g1, wqkv, row(b_qkv[0]), cos_p, sin_p, tmq, T // tmq, keep)
    qs, kds, vds, ks, vs = _qkv(x2s, g1, wqkv, row(b_qkv[0]), cos_s, sin_s, tmq, 1, tmq)

    op = _attn_prompt(qp, kdp, vdp, _sink_rows(sinks[0], WINDOW), B, T)

    bs = 8
    rows = bs * Td
    W = cache_k.shape[2]
    qrow = np.arange(rows)
    ccol = np.arange(bs * W)
    ncol = np.arange(rows)
    samp_q, t_q = qrow // Td, qrow % Td
    m_cache = (samp_q[:, None] == (ccol // W)[None, :]) & ((ccol % W)[None, :] > t_q[:, None])
    m_new = (samp_q[:, None] == (ncol // Td)[None, :]) & ((ncol % Td)[None, :] <= t_q[:, None])
    amask = jnp.asarray(np.tile(np.concatenate([m_cache, m_new], axis=1).T, (1, 4)), F32)
    kv_cache = (jnp.arange(W, dtype=jnp.int32)[None, :] >= (W - start)[:, None]).reshape(Bd // bs, bs * W)
    kval = jnp.concatenate([kv_cache, jnp.ones((Bd // bs, rows), bool)], axis=1).astype(F32)
    kval = kval.reshape(Bd // bs, bs * W + rows, 1)
    to_stored = lambda c: jnp.transpose(c[0], (0, 2, 3, 1))
    from_stored = lambda c: jnp.transpose(c, (0, 3, 1, 2))[None]
    osamp, nk_s, nv_s = _attn_sample(
        qs, kds, vds, ks, vs, to_stored(cache_k), to_stored(cache_v),
        amask, kval, _sink_rows(sinks[0], rows), bs, Td)

    moe1 = functools.partial(_moe, nf=row(norm_ffn[1]), wr=wr[1], br=br[1], final=row(norm_final))
    yp, *w1 = moe1(x2p, wg=w_gate, wu=w_up, wd=w_down, layer=1, oproj=(op, wo, row(b_o[0])))
    ys = moe1(x2s, wg=w1[0], wu=w1[1], wd=w1[2], layer=None, oproj=(osamp, wo, row(b_o[0])))

    k_p = kp.reshape(1, B, keep, N_KV_HEADS, HEAD_DIM)
    v_p = vp.reshape(1, B, keep, N_KV_HEADS, HEAD_DIM)
    return (yp.reshape(B, T, D), ys.reshape(Bd, Td, D), pool_p[None], k_p, v_p, pool_s[None],
            from_stored(nk_s), from_stored(nv_s))
```

```python
import functools

import jax
import jax.numpy as jnp
import numpy as np
from jax import lax
from jax.experimental import pallas as pl
from jax.experimental.pallas import tpu as pltpu

F32 = jnp.float32
BF16 = jnp.bfloat16

D_MODEL = 1024
POOL_WINDOWS = (2, 4, 8, 16)
POOL_GROUP_DIM = 256
POOL_BUF = 15
HALO = 16
HEAD_DIM = 64
N_HEADS = 16
N_KV_HEADS = 4
WINDOW = 128
ROPE_THETA = 10000.0
N_EXPERT_GROUPS = 4
EXPERTS_PER_GROUP = 4
N_EXPERTS = 16
D_EXPERT = 256
RMS_EPS = 1e-6
LANES = 128
ROUTER_ROWS = 32
EXPERT_ROW0 = 8
CAST_SLOTS = 4
VMEM_LIMIT = 56 * 1024 * 1024
LOG2E = 1.4426950408889634
Q_SCALE = LOG2E * HEAD_DIM ** -0.5


def _rms(x, g):
    ms = jnp.mean(x * x, axis=-1, keepdims=True)
    return x * lax.rsqrt(ms + RMS_EPS) * g


def _pool_project(h, wins, pos, wp_ref, ls):
    G = POOL_GROUP_DIM
    outs = []
    for g, w in enumerate(POOL_WINDOWS):
        cnt = jnp.minimum(w, pos + 1).astype(F32)
        d = wins[g] / cnt - h[:, g * G:(g + 1) * G]
        outs.append(jnp.dot(d.astype(BF16), wp_ref[g], preferred_element_type=F32))
    return jnp.concatenate(outs, axis=-1) * ls


POOL_BLOCK = 128


def _pool_band():
    t = np.arange(POOL_BLOCK)[:, None] + HALO
    k = np.arange(POOL_BLOCK + HALO)[None, :]
    return jnp.asarray(np.stack([(k <= t) & (k > t - w) for w in POOL_WINDOWS]), BF16)


def _pool_prompt_kernel(x_ref, buf_ref, g_ref, wp_ref, ls_ref, band_ref, o_ref, nb_ref, hc, *, tq):
    t = pl.program_id(1)
    G = POOL_GROUP_DIM

    @pl.when(t == 0)
    def _():
        hc[pl.ds(0, HALO), :] = buf_ref[0]

    @pl.when(t > 0)
    def _():
        hc[pl.ds(0, HALO), :] = hc[pl.ds(tq, HALO), :]

    x = x_ref[0]
    h = _rms(x, g_ref[...])
    hc[pl.ds(HALO, tq), :] = h
    hb = hc[...].astype(BF16)
    wins = []
    for g in range(len(POOL_WINDOWS)):
        blocks = [jnp.dot(band_ref[g], hb[b * POOL_BLOCK:(b + 1) * POOL_BLOCK + HALO, g * G:(g + 1) * G],
                          preferred_element_type=F32) for b in range(tq // POOL_BLOCK)]
        wins.append(jnp.concatenate(blocks, axis=0))
    pos = t * tq + lax.broadcasted_iota(jnp.int32, (tq, 1), 0)
    o_ref[0] = x + _pool_project(h, wins, pos, wp_ref, ls_ref[...])

    @pl.when(t == pl.num_programs(1) - 1)
    def _():
        nb_ref[0] = hc[pl.ds(tq, HALO), :]


def _pool_prompt(x, buf16, g, wp, ls, tq=1024):
    B, T, D = x.shape
    return pl.pallas_call(
        functools.partial(_pool_prompt_kernel, tq=tq),
        grid=(B, T // tq),
        in_specs=[
            pl.BlockSpec((1, tq, D), lambda b, t: (b, t, 0)),
            pl.BlockSpec((1, HALO, D), lambda b, t: (b, 0, 0)),
            pl.BlockSpec((1, D), lambda b, t: (0, 0)),
            pl.BlockSpec((4, POOL_GROUP_DIM, POOL_GROUP_DIM), lambda b, t: (0, 0, 0)),
            pl.BlockSpec((1, D), lambda b, t: (0, 0)),
            pl.BlockSpec((4, POOL_BLOCK, POOL_BLOCK + HALO), lambda b, t: (0, 0, 0)),
        ],
        out_specs=[
            pl.BlockSpec((1, tq, D), lambda b, t: (b, t, 0)),
            pl.BlockSpec((1, HALO, D), lambda b, t: (b, 0, 0)),
        ],
        out_shape=[jax.ShapeDtypeStruct((B, T, D), F32), jax.ShapeDtypeStruct((B, HALO, D), F32)],
        scratch_shapes=[pltpu.VMEM((HALO + tq, D), F32)],
        compiler_params=pltpu.CompilerParams(
            dimension_semantics=("arbitrary", "arbitrary"), vmem_limit_bytes=VMEM_LIMIT),
        name="pool_prompt",
    )(x, buf16, g, wp, ls, _pool_band())


def _pool_sample_kernel(x_ref, st_ref, start_ref, g_ref, wp_ref, ls_ref, o_ref, ns_ref, *, bs, t):
    G = POOL_GROUP_DIM
    xs = [x_ref[:, i, :] for i in range(t)]
    hn = [_rms(x, g_ref[...]) for x in xs]
    hist = [st_ref[r] for r in range(POOL_BUF)] + hn

    def doubled(prev, lag, lo):
        out = [None] * len(prev)
        for i in range(len(prev)):
            if i >= lag and prev[i] is not None and prev[i - lag] is not None:
                out[i] = prev[i][:, lo:] + prev[i - lag][:, lo:]
        return out

    s2 = doubled(hist, 1, 0)
    s4 = doubled(s2, 2, G)
    s8 = doubled(s4, 4, G)
    s16 = doubled(s8, 8, G)
    wins = (s2, s4, s8, s16)
    start = start_ref[...]
    ds = [[] for _ in POOL_WINDOWS]
    for i in range(t):
        for g, w in enumerate(POOL_WINDOWS):
            cnt = jnp.minimum(w, start + (i + 1)).astype(F32)
            ds[g].append(wins[g][POOL_BUF + i][:, :G] / cnt - hn[i][:, g * G:(g + 1) * G])
    ys = [jnp.dot(jnp.concatenate(ds[g], axis=0).astype(BF16), wp_ref[g], preferred_element_type=F32)
          for g in range(len(POOL_WINDOWS))]
    y = jnp.concatenate(ys, axis=-1) * ls_ref[...]
    for i in range(t):
        o_ref[:, i, :] = xs[i] + y[i * bs:(i + 1) * bs]
    for r in range(POOL_BUF):
        ns_ref[r] = hist[r + t]


def _pool_sample(x, st, start, g, wp, ls, bs=32):
    Bd, t, D = x.shape
    return pl.pallas_call(
        functools.partial(_pool_sample_kernel, bs=bs, t=t),
        grid=(Bd // bs,),
        in_specs=[
            pl.BlockSpec((bs, t, D), lambda i: (i, 0, 0)),
            pl.BlockSpec((POOL_BUF, bs, D), lambda i: (0, i, 0)),
            pl.BlockSpec((bs, 1), lambda i: (i, 0)),
            pl.BlockSpec((1, D), lambda i: (0, 0)),
            pl.BlockSpec((4, POOL_GROUP_DIM, POOL_GROUP_DIM), lambda i: (0, 0, 0)),
            pl.BlockSpec((1, D), lambda i: (0, 0)),
        ],
        out_specs=[pl.BlockSpec((bs, t, D), lambda i: (i, 0, 0)),
                   pl.BlockSpec((POOL_BUF, bs, D), lambda i: (0, i, 0))],
        out_shape=[jax.ShapeDtypeStruct((Bd, t, D), F32), jax.ShapeDtypeStruct((POOL_BUF, Bd, D), F32)],
        compiler_params=pltpu.CompilerParams(
            dimension_semantics=("arbitrary",), vmem_limit_bytes=VMEM_LIMIT),
        name="pool_sample",
    )(x, st, start, g, wp, ls)


def _route(lg):
    R = lg.shape[1]
    big = jnp.float32(1 << 20)
    neg = jnp.float32(-jnp.inf)
    r8 = lax.broadcasted_iota(jnp.int32, (8, R), 0).astype(F32)
    r16 = lax.broadcasted_iota(jnp.int32, (N_EXPERTS, R), 0).astype(F32)
    is_g = r8 < N_EXPERT_GROUPS
    gl = jnp.where(is_g, lg[0:8], neg)
    m = jnp.max(gl, axis=0, keepdims=True)
    gidx = jnp.min(jnp.where(gl == m, r8, big), axis=0, keepdims=True)
    z = jnp.sum(jnp.where(is_g, jnp.exp(gl - m), 0.0), axis=0, keepdims=True)
    gw = 1.0 / z
    lo = gidx * EXPERTS_PER_GROUP
    in_grp = (r16 >= lo) & (r16 < lo + EXPERTS_PER_GROUP)
    el = jnp.where(in_grp, lg[EXPERT_ROW0:EXPERT_ROW0 + N_EXPERTS], neg)
    v1 = jnp.max(el, axis=0, keepdims=True)
    i1 = jnp.min(jnp.where(el == v1, r16, big), axis=0, keepdims=True)
    el2 = jnp.where(r16 == i1, neg, el)
    v2 = jnp.max(el2, axis=0, keepdims=True)
    i2 = jnp.min(jnp.where(el2 == v2, r16, big), axis=0, keepdims=True)
    t = jnp.exp(v2 - v1)
    w1 = 1.0 / (1.0 + t)
    w2 = t * w1
    e8 = jnp.where(is_g, r8, r8 - EXPERTS_PER_GROUP)
    cw = gw * (jnp.where(e8 == i1 - lo, w1, 0.0) + jnp.where(e8 == i2 - lo, w2, 0.0))
    return gidx, cw


def _moe_kernel(*refs, has_oproj, has_final, cast_layer, tm, bm):
    it = iter(refs)
    x_ref = next(it)
    if has_oproj:
        oin_ref, wo_ref, bo_ref = next(it), next(it), next(it)
    nf_ref, wr_ref, br_ref, wg_in, wu_in, wd_in = (next(it) for _ in range(6))
    if has_final:
        fn_ref = next(it)
    out_ref = next(it)
    if cast_layer is not None:
        wg_out, wu_out, wd_out = (next(it) for _ in range(3))
    hs, cs, ys, sel_s, earlier = (next(it) for _ in range(5))
    if cast_layer is None:
        wg_ref, wu_ref, wd_ref = wg_in, wu_in, wd_in
    else:
        wg_ref, wu_ref, wd_ref, stage_gu, stage_d, load_sem, store_sem = (next(it) for _ in range(7))
        pairs = ((wg_in, wg_ref, wg_out, stage_gu), (wu_in, wu_ref, wu_out, stage_gu), (wd_in, wd_ref, wd_out, stage_d))

        def store_copy(k):
            return pltpu.make_async_copy(pairs[k][1], pairs[k][2], store_sem.at[k])

        @pl.when(pl.program_id(0) == 0)
        def _():
            ahead = stage_gu.shape[0] - 1
            for src, dst, _, stage in pairs:
                def load(e, src=src, stage=stage):
                    slot = e % (ahead + 1)
                    return pltpu.make_async_copy(src.at[cast_layer, e], stage.at[slot], load_sem.at[slot])

                for e in range(ahead):
                    load(e).start(priority=e % 2)
                for e in range(N_EXPERTS):
                    if e + ahead < N_EXPERTS:
                        load(e + ahead).start(priority=(e + ahead) % 2)
                    load(e).wait()
                    dst[e] = stage[e % (ahead + 1)].astype(BF16)
            for k in range(len(pairs)):
                store_copy(k).start()

        @pl.when(pl.program_id(0) == pl.num_programs(0) - 1)
        def _():
            for k in range(len(pairs)):
                store_copy(k).wait()
    min_blk = -(-tm // bm)
    nblk = tm // bm + N_EXPERT_GROUPS
    cap = nblk * bm
    assert bm % 16 == 0
    nt = (((1,), (1,)), ((), ()))

    @pl.when(pl.program_id(0) == 0)
    def _():
        ri = lax.broadcasted_iota(jnp.int32, (tm, tm), 0)
        ci = lax.broadcasted_iota(jnp.int32, (tm, tm), 1)
        earlier[...] = jnp.where(ri < ci, 1.0, 0.0).astype(BF16)

    x = x_ref[...]
    if has_oproj:
        x = x + jnp.dot(oin_ref[...], wo_ref[...], preferred_element_type=F32) + bo_ref[...]
    if has_oproj:
        out_ref[...] = x
    h = _rms(x, nf_ref[...])
    hb = h.astype(BF16)
    h_lo = (h - hb.astype(F32)).astype(BF16)
    wr = wr_ref[...]
    both = lax.dot_general(wr, hb, nt, preferred_element_type=F32)
    lg = (both[0:ROUTER_ROWS] + both[ROUTER_ROWS:]
          + lax.dot_general(wr[0:ROUTER_ROWS], h_lo, nt, preferred_element_type=F32)) + br_ref[...]
    gidx, cw = _route(lg)
    r8 = lax.broadcasted_iota(jnp.int32, (8, tm), 0)
    cw_hi = cw.astype(BF16).astype(F32)
    cwt = jnp.concatenate([jnp.where(r8 < EXPERTS_PER_GROUP, cw_hi, cw - cw_hi),
                           jnp.zeros((LANES - 8, tm), F32)], axis=0).astype(BF16)

    oh = jnp.where(r8.astype(F32) == gidx, 1.0, 0.0)
    cnt_before = jnp.dot(oh.astype(BF16), earlier[...], preferred_element_type=F32)
    pos_row = jnp.sum(oh * cnt_before, axis=0, keepdims=True)
    start_blk, n_blk = [], []
    off = jnp.int32(0)
    for g in range(N_EXPERT_GROUPS):
        in_g = gidx == g
        n_g = jnp.sum(jnp.where(in_g, 1.0, 0.0)).astype(jnp.int32)
        blocks = sum(jnp.where(n_g > k * bm, 1, 0) for k in range(min_blk))
        start_blk.append(off)
        n_blk.append(blocks)
        pos_row = pos_row + jnp.where(in_g, (off * bm).astype(F32), 0.0)
        off = off + blocks

    total_blk = off
    main = min_blk * bm
    extra = ((main, bm, total_blk > min_blk), (main + bm, cap - main - bm, total_blk > min_blk + 1))

    def sort_rows(r0, n):
        sub = (r0 + lax.broadcasted_iota(jnp.int32, (n, tm), 0)).astype(F32)
        sel = jnp.where(pos_row == sub, 1.0, 0.0).astype(BF16)
        sel_s[r0:r0 + n, :] = sel
        hs[r0:r0 + n, :] = jnp.dot(sel, hb, preferred_element_type=F32).astype(BF16)
        cs[r0:r0 + n, :] = lax.dot_general(sel, cwt, nt, preferred_element_type=F32)

    sort_rows(0, main)
    for r0, n, needed in extra:
        pl.when(needed)(functools.partial(sort_rows, r0, n))
    ys[min_blk * bm:, :] = jnp.zeros((cap - min_blk * bm, D_MODEL), BF16)

    for g in range(N_EXPERT_GROUPS):
        wd = wd_ref[g * EXPERTS_PER_GROUP:(g + 1) * EXPERTS_PER_GROUP].reshape(
            EXPERTS_PER_GROUP * D_EXPERT, D_MODEL)

        def block(b, carry, g=g, wd=wd):
            rows = pl.ds(pl.multiple_of(b * bm, 16), bm)
            hblk = hs[rows, :]
            cblk = cs[rows, :]
            ln = lax.broadcasted_iota(jnp.int32, cblk.shape, 1)
            parts = []
            for e in range(EXPERTS_PER_GROUP):
                ce = jnp.sum(jnp.where((ln == e) | (ln == e + EXPERTS_PER_GROUP), cblk, 0.0),
                             axis=-1, keepdims=True)
                gt = jnp.dot(hblk, wg_ref[g * EXPERTS_PER_GROUP + e], preferred_element_type=F32)
                up = jnp.dot(hblk, wu_ref[g * EXPERTS_PER_GROUP + e], preferred_element_type=F32)
                a = gt / (1.0 + jnp.exp(-gt)) * up * ce
                parts.append(a.astype(BF16))
            a_all = jnp.concatenate(parts, axis=-1)
            ys[rows, :] = jnp.dot(a_all, wd, preferred_element_type=F32).astype(BF16)
            return carry

        lax.fori_loop(start_blk[g], start_blk[g] + n_blk[g], block, 0)

    def unsorted(r0, n):
        return lax.dot_general(sel_s[r0:r0 + n, :], ys[r0:r0 + n, :], (((0,), (0,)), ((), ())),
                               preferred_element_type=F32)

    def unsort_more(r0, n):
        out_ref[...] += unsorted(r0, n)

    out_ref[...] = (out_ref[...] if has_oproj else x_ref[...]) + unsorted(0, main)
    for r0, n, needed in extra:
        pl.when(needed)(functools.partial(unsort_more, r0, n))
    if has_final:
        out_ref[...] = _rms(out_ref[...], fn_ref[...])


def _moe(x, nf, wr, br, wg, wu, wd, layer, oproj=None, final=None, tm=512, bm=144):
    N, D = x.shape
    has_oproj = oproj is not None
    has_final = final is not None
    const2 = lambda i: (0, 0)
    const3 = lambda i: (0, 0, 0)
    cast = layer is not None
    tile = lambda i: (i, 0)
    once = pl.Buffered(1)
    args = [x]
    in_specs = [pl.BlockSpec((tm, D), tile)]
    if has_oproj:
        o, wo, bo = oproj
        args += [o, wo, bo]
        in_specs += [pl.BlockSpec((tm, D), tile), pl.BlockSpec((D, D), const2, pipeline_mode=once),
                     pl.BlockSpec((1, D), const2)]
    wr_hi = wr.astype(BF16)
    wr_lo = (wr - wr_hi.astype(F32)).astype(BF16)
    args += [nf, jnp.concatenate([wr_hi, wr_lo], axis=0), br, wg, wu, wd]
    in_specs += [
        pl.BlockSpec((1, D), const2),
        pl.BlockSpec((2 * ROUTER_ROWS, D), const2, pipeline_mode=once),
        pl.BlockSpec((ROUTER_ROWS, 1), const2),
    ]
    gu_shape, d_shape = (N_EXPERTS, D, D_EXPERT), (N_EXPERTS, D_EXPERT, D)
    if cast:
        in_specs += [pl.BlockSpec(memory_space=pl.ANY)] * 3
    else:
        in_specs += [pl.BlockSpec(gu_shape, const3, pipeline_mode=once), pl.BlockSpec(gu_shape, const3, pipeline_mode=once),
                     pl.BlockSpec(d_shape, const3, pipeline_mode=once)]
    if has_final:
        args.append(final)
        in_specs.append(pl.BlockSpec((1, D), const2))
    cap = (tm // bm + N_EXPERT_GROUPS) * bm
    out_specs = [pl.BlockSpec((tm, D), tile)]
    out_shape = [jax.ShapeDtypeStruct((N, D), F32)]
    scratch = [
        pltpu.VMEM((cap, D), BF16),
        pltpu.VMEM((cap, LANES), F32),
        pltpu.VMEM((cap, D), BF16),
        pltpu.VMEM((cap, tm), BF16),
        pltpu.VMEM((tm, tm), BF16),
    ]
    if cast:
        out_specs += [pl.BlockSpec(memory_space=pl.ANY)] * 3
        out_shape += [jax.ShapeDtypeStruct(gu_shape, BF16)] * 2 + [jax.ShapeDtypeStruct(d_shape, BF16)]
        scratch += [pltpu.VMEM(gu_shape, BF16), pltpu.VMEM(gu_shape, BF16), pltpu.VMEM(d_shape, BF16),
                    pltpu.VMEM((CAST_SLOTS,) + gu_shape[1:], F32), pltpu.VMEM((CAST_SLOTS,) + d_shape[1:], F32),
                    pltpu.SemaphoreType.DMA((CAST_SLOTS,)), pltpu.SemaphoreType.DMA((3,))]
    res = pl.pallas_call(
        functools.partial(_moe_kernel, has_oproj=has_oproj, has_final=has_final, cast_layer=layer, tm=tm, bm=bm),
        grid=(N // tm,),
        in_specs=in_specs,
        out_specs=out_specs,
        out_shape=out_shape,
        scratch_shapes=scratch,
        compiler_params=pltpu.CompilerParams(
            dimension_semantics=("arbitrary",), vmem_limit_bytes=VMEM_LIMIT),
        name="moe",
    )(*args)
    return res if cast else res[0]


def _swap_halves(x, low):
    return jnp.where(low, pltpu.roll(x, LANES - 32, axis=1), pltpu.roll(x, 32, axis=1))


def _qkv_kernel(x_ref, g_ref, w_ref, b_ref, cos_ref, sin_ref, q_ref, kd_ref, vd_ref, k_ref, v_ref, *, keep):
    tm = x_ref.shape[0]
    h = _rms(x_ref[...], g_ref[...]).astype(BF16)
    cos = cos_ref[...]
    sin = sin_ref[...]
    lane = lax.broadcasted_iota(jnp.int32, cos.shape, 1)
    low32 = (lane % HEAD_DIM) < (HEAD_DIM // 2)
    low64 = lane < HEAD_DIM
    nq = N_HEADS * HEAD_DIM
    nk = N_KV_HEADS * HEAD_DIM
    cw = 2 * LANES

    def project(c0):
        y = jnp.dot(h, w_ref[:, c0:c0 + cw], preferred_element_type=F32) + b_ref[:, c0:c0 + cw]
        return [y[:, i * LANES:(i + 1) * LANES] for i in range(cw // LANES)]

    def rope(c):
        return c * cos + _swap_halves(c, low32) * sin

    def dup(c):
        r = pltpu.roll(c, HEAD_DIM, axis=1)
        return jnp.where(low64, c, r), jnp.where(low64, r, c)

    for c0 in range(0, nq, cw):
        for i, c in enumerate(project(c0)):
            lanes = slice(c0 + i * LANES, c0 + (i + 1) * LANES)
            q_ref[:, lanes] = (rope(c) * Q_SCALE).astype(BF16)
    for c0 in range(0, nk, cw):
        for (src, rot, f_ref, d_ref) in ((nq, True, k_ref, kd_ref), (nq + nk, False, v_ref, vd_ref)):
            for i, c in enumerate(project(src + c0)):
                j = c0 // LANES + i
                c = rope(c) if rot else c
                f_ref[:, j * LANES:(j + 1) * LANES] = c[tm - keep:]
                d0, d1 = dup(c)
                d_ref[:, 2 * j * LANES:(2 * j + 1) * LANES] = d0.astype(BF16)
                d_ref[:, (2 * j + 1) * LANES:(2 * j + 2) * LANES] = d1.astype(BF16)


def _qkv(x, g, w, b, cos, sin, tm, n_pos_tiles, keep):
    N, D = x.shape
    nq, nk = N_HEADS * HEAD_DIM, N_KV_HEADS * HEAD_DIM
    const = lambda i: (0, 0)
    tile = lambda i: (i, 0)
    ptile = lambda i: (i % n_pos_tiles, 0)
    seq = lambda i: (i // n_pos_tiles, 0)
    n_seq = N // (tm * n_pos_tiles)
    return pl.pallas_call(
        functools.partial(_qkv_kernel, keep=keep),
        grid=(N // tm,),
        in_specs=[
            pl.BlockSpec((tm, D), tile),
            pl.BlockSpec((1, D), const),
            pl.BlockSpec((D, nq + 2 * nk), const),
            pl.BlockSpec((1, nq + 2 * nk), const),
            pl.BlockSpec((tm, LANES), ptile),
            pl.BlockSpec((tm, LANES), ptile),
        ],
        out_specs=[
            pl.BlockSpec((tm, nq), tile),
            pl.BlockSpec((tm, 2 * nk), tile),
            pl.BlockSpec((tm, 2 * nk), tile),
            pl.BlockSpec((keep, nk), seq),
            pl.BlockSpec((keep, nk), seq),
        ],
        out_shape=[
            jax.ShapeDtypeStruct((N, nq), BF16),
            jax.ShapeDtypeStruct((N, 2 * nk), BF16),
            jax.ShapeDtypeStruct((N, 2 * nk), BF16),
            jax.ShapeDtypeStruct((n_seq * keep, nk), F32),
            jax.ShapeDtypeStruct((n_seq * keep, nk), F32),
        ],
        compiler_params=pltpu.CompilerParams(
            dimension_semantics=("arbitrary",), vmem_limit_bytes=VMEM_LIMIT),
        name="qkv",
    )(x, g, w, b, cos, sin)


def _stack_heads(qc_list):
    lane = lax.broadcasted_iota(jnp.int32, qc_list[0].shape, 1)
    low = lane < HEAD_DIM
    zero = jnp.zeros_like(qc_list[0])
    stack = []
    for qc in qc_list:
        stack.append(jnp.where(low, qc, zero))
        stack.append(jnp.where(low, zero, qc))
    return jnp.concatenate(stack, axis=0)


def _softmax_keys_on_rows(s, sinkrow):
    m = jnp.maximum(jnp.max(s, axis=0, keepdims=True), sinkrow)
    p = jnp.exp2(s - m)
    den = jnp.sum(p, axis=0, keepdims=True) + jnp.exp2(sinkrow - m)
    return (p * (1.0 / den)).astype(BF16)


def _unstack_heads(o, R):
    lowf = lax.broadcasted_iota(jnp.int32, (R, LANES), 1) < HEAD_DIM
    return (jnp.where(lowf, o[0:R], o[R:2 * R]), jnp.where(lowf, o[2 * R:3 * R], o[3 * R:4 * R]))


NT_DIMS = (((1,), (1,)), ((), ()))
TN_DIMS = (((0,), (0,)), ((), ()))
TT_DIMS = (((0,), (1,)), ((), ()))


def _attend(qc_list, kd, vd, bias, sinkrow):
    qs = _stack_heads(qc_list)
    s = lax.dot_general(kd, qs, NT_DIMS, preferred_element_type=F32) + bias
    p = _softmax_keys_on_rows(s, sinkrow)
    o = lax.dot_general(p, vd, TN_DIMS, preferred_element_type=F32)
    return _unstack_heads(o, qc_list[0].shape[0])


def _attn_prompt_kernel(q_ref, kp_ref, kc_ref, vp_ref, vc_ref, sink_ref, o_ref, *, tq):
    n = pl.program_id(1)
    B = WINDOW
    kd = jnp.concatenate([kp_ref[...], kc_ref[...]], axis=0)
    vd = jnp.concatenate([vp_ref[...], vc_ref[...]], axis=0)
    kj = lax.broadcasted_iota(jnp.int32, (2 * B, B), 0)
    qi = lax.broadcasted_iota(jnp.int32, (2 * B, B), 1)
    rel = B + qi - kj
    band = (rel >= 0) & (rel < WINDOW)
    neg = jnp.float32(-jnp.inf)
    band_bias = jnp.concatenate([jnp.where(band, 0.0, neg)] * 4, axis=1)
    first_bias = jnp.concatenate([jnp.where(band & ((kj >= B) | (n > 0)), 0.0, neg)] * 4, axis=1)
    for j in range(tq // B):
        bias = first_bias if j == 0 else band_bias
        for g in range(N_KV_HEADS):
            qc = [q_ref[j * B:(j + 1) * B, (2 * g + i) * LANES:(2 * g + i + 1) * LANES] for i in range(2)]
            kg = kd[j * B:(j + 2) * B, g * LANES:(g + 1) * LANES]
            vg = vd[j * B:(j + 2) * B, g * LANES:(g + 1) * LANES]
            o0, o1 = _attend(qc, kg, vg, bias, sink_ref[g])
            o_ref[j * B:(j + 1) * B, 2 * g * LANES:(2 * g + 1) * LANES] = o0.astype(BF16)
            o_ref[j * B:(j + 1) * B, (2 * g + 1) * LANES:(2 * g + 2) * LANES] = o1.astype(BF16)


def _attn_prompt(q, kd, vd, sinkrow, batch, seq, tq=1024):
    N = q.shape[0]
    nt = seq // tq
    r = tq // WINDOW
    cur = lambda b, n: (b * nt + n, 0)
    prev = lambda b, n: (jnp.maximum((b * nt + n) * r - 1, 0), 0)
    kvw = 2 * N_KV_HEADS * HEAD_DIM
    return pl.pallas_call(
        functools.partial(_attn_prompt_kernel, tq=tq),
        grid=(batch, nt),
        in_specs=[
            pl.BlockSpec((tq, N_HEADS * HEAD_DIM), cur),
            pl.BlockSpec((WINDOW, kvw), prev),
            pl.BlockSpec((tq, kvw), cur),
            pl.BlockSpec((WINDOW, kvw), prev),
            pl.BlockSpec((tq, kvw), cur),
            pl.BlockSpec((N_KV_HEADS, 1, 4 * WINDOW), lambda b, n: (0, 0, 0)),
        ],
        out_specs=pl.BlockSpec((tq, N_HEADS * HEAD_DIM), cur),
        out_shape=jax.ShapeDtypeStruct((N, N_HEADS * HEAD_DIM), BF16),
        compiler_params=pltpu.CompilerParams(
            dimension_semantics=("arbitrary", "arbitrary"), vmem_limit_bytes=VMEM_LIMIT),
        name="attn_prompt",
    )(q, kd, kd, vd, vd, sinkrow)


def _attn_sample_kernel(q_ref, kdn_ref, vdn_ref, kn_ref, vn_ref, ck_ref, cv_ref, mask_ref, kval_ref, sink_ref,
                        o_ref, nk_ref, nv_ref, *, bs, t):
    W = WINDOW
    rows = bs * t
    bias = jnp.where((mask_ref[...] > 0) & (kval_ref[0] > 0), 0.0, -jnp.inf).astype(F32)
    lane = lax.broadcasted_iota(jnp.int32, (HEAD_DIM, W), 1)
    fresh = lane >= W - t

    def new_cols(n_ref):
        n = jnp.concatenate([n_ref[...], jnp.zeros((LANES - rows, N_KV_HEADS * HEAD_DIM), F32)], axis=0)
        nt_ = jnp.transpose(n)
        return [pltpu.roll(nt_, (W - t - s * t) % LANES, axis=1) for s in range(bs)]

    k_cols, v_cols = new_cols(kn_ref), new_cols(vn_ref)
    for g in range(N_KV_HEADS):
        hd = slice(g * HEAD_DIM, (g + 1) * HEAD_DIM)
        kt = jnp.concatenate([ck_ref[s, g] for s in range(bs)], axis=1).astype(BF16)
        vt = jnp.concatenate([cv_ref[s, g] for s in range(bs)], axis=1).astype(BF16)
        kt2 = jnp.concatenate([kt, kt], axis=0)
        vt2 = jnp.concatenate([vt, vt], axis=0)
        qs = _stack_heads([q_ref[:, (2 * g + i) * LANES:(2 * g + i + 1) * LANES] for i in range(2)])
        s_old = lax.dot_general(kt2, qs, TT_DIMS, preferred_element_type=F32)
        s_new = lax.dot_general(kdn_ref[:, g * LANES:(g + 1) * LANES], qs, NT_DIMS, preferred_element_type=F32)
        p = _softmax_keys_on_rows(jnp.concatenate([s_old, s_new], axis=0) + bias, sink_ref[g])
        o = (lax.dot_general(p[:bs * W], vt2, TT_DIMS, preferred_element_type=F32)
             + lax.dot_general(p[bs * W:], vdn_ref[:, g * LANES:(g + 1) * LANES], TN_DIMS,
                               preferred_element_type=F32))
        o0, o1 = _unstack_heads(o, rows)
        o_ref[:, 2 * g * LANES:(2 * g + 1) * LANES] = o0.astype(BF16)
        o_ref[:, (2 * g + 1) * LANES:(2 * g + 2) * LANES] = o1.astype(BF16)
        for s in range(bs):
            nk_ref[s, g] = jnp.where(fresh, k_cols[s][hd], pltpu.roll(ck_ref[s, g], W - t, axis=1))
            nv_ref[s, g] = jnp.where(fresh, v_cols[s][hd], pltpu.roll(cv_ref[s, g], W - t, axis=1))


def _attn_sample(q, kdn, vdn, kn, vn, ck, cv, mask, kval, sinkrow, bs, t):
    Bd = ck.shape[0]
    rows = bs * t
    kvw = N_KV_HEADS * HEAD_DIM
    S = bs * WINDOW + rows
    tile = lambda i: (i, 0)
    tile3 = lambda i: (i, 0, 0)
    tile4 = lambda i: (i, 0, 0, 0)
    cblk = (bs, N_KV_HEADS, HEAD_DIM, WINDOW)
    return pl.pallas_call(
        functools.partial(_attn_sample_kernel, bs=bs, t=t),
        grid=(Bd // bs,),
        in_specs=[
            pl.BlockSpec((rows, N_HEADS * HEAD_DIM), tile),
            pl.BlockSpec((rows, 2 * kvw), tile),
            pl.BlockSpec((rows, 2 * kvw), tile),
            pl.BlockSpec((rows, kvw), tile),
            pl.BlockSpec((rows, kvw), tile),
            pl.BlockSpec(cblk, tile4),
            pl.BlockSpec(cblk, tile4),
            pl.BlockSpec((S, 4 * rows), lambda i: (0, 0)),
            pl.BlockSpec((1, S, 1), tile3),
            pl.BlockSpec((N_KV_HEADS, 1, 4 * rows), lambda i: (0, 0, 0)),
        ],
        out_specs=[
            pl.BlockSpec((rows, N_HEADS * HEAD_DIM), tile),
            pl.BlockSpec(cblk, tile4),
            pl.BlockSpec(cblk, tile4),
        ],
        out_shape=[
            jax.ShapeDtypeStruct((Bd * t, N_HEADS * HEAD_DIM), BF16),
            jax.ShapeDtypeStruct((Bd,) + cblk[1:], F32),
            jax.ShapeDtypeStruct((Bd,) + cblk[1:], F32),
        ],
        compiler_params=pltpu.CompilerParams(
            dimension_semantics=("arbitrary",), vmem_limit_bytes=VMEM_LIMIT),
        name="attn_sample",
    )(q, kdn, vdn, kn, vn, ck, cv, mask, kval, sinkrow)


def _rope_tables(pos):
    inv = ROPE_THETA ** (-jnp.arange(0, HEAD_DIM, 2, dtype=F32) / HEAD_DIM)
    reps = LANES // (HEAD_DIM // 2)
    sign = jnp.asarray(np.where((np.arange(LANES) % HEAD_DIM) < HEAD_DIM // 2, -1.0, 1.0), F32)
    ang = pos.astype(F32)[:, None] * inv[None, :]
    cos, sin = lax.optimization_barrier((jnp.cos(ang), jnp.sin(ang)))
    return jnp.tile(cos, (1, reps)), jnp.tile(sin, (1, reps)) * sign[None, :]


def _sink_rows(sinks, rows):
    s = (sinks.astype(F32) * LOG2E).reshape(N_KV_HEADS, 1, N_HEADS // N_KV_HEADS, 1)
    return jnp.broadcast_to(s, (N_KV_HEADS, 1, 4, rows)).reshape(N_KV_HEADS, 1, 4 * rows)


def kernel(x_prompt, x_sample, state_pool, cache_k, cache_v, sample_start, norm_mix, norm_ffn, norm_final,
           w_pool, ls_pool, w_qkv, b_qkv, sinks, w_o, b_o, w_rg, b_rg, w_re, b_re, w_gate, w_up, w_down):
    B, T, D = x_prompt.shape
    Bd, Td, _ = x_sample.shape
    kvw = N_KV_HEADS * HEAD_DIM
    row = lambda v: v.reshape(1, -1).astype(F32)

    wp = w_pool[0].astype(BF16)
    wqkv = w_qkv[0].astype(BF16)
    wo = w_o[0].astype(BF16)
    def router_rows(g_part, e_part):
        z = lambda n: jnp.zeros((n,) + g_part.shape[1:], F32)
        return jnp.concatenate([g_part, z(EXPERT_ROW0 - N_EXPERT_GROUPS), e_part,
                                z(ROUTER_ROWS - EXPERT_ROW0 - N_EXPERTS)], axis=0)

    wr = [router_rows(w_rg[l].T, w_re[l].T) for l in range(2)]
    br = [router_rows(b_rg[l][:, None], b_re[l][:, None]) for l in range(2)]

    start = sample_start.astype(jnp.int32)

    x1p, pool_p16 = _pool_prompt(x_prompt, jnp.zeros((B, HALO, D), F32), row(norm_mix[0]), wp, row(ls_pool[0]))
    pos_s = (start[:, None] + jnp.arange(Td, dtype=jnp.int32)[None, :]).reshape(-1)
    x1s, pool_s_t = _pool_sample(x_sample, jnp.transpose(state_pool[0], (1, 0, 2)), start[:, None],
                                 row(norm_mix[0]), wp, row(ls_pool[0]))
    x1s = x1s.reshape(Bd * Td, D)
    pool_s = jnp.transpose(pool_s_t, (1, 0, 2))
    pool_p = pool_p16[:, HALO - POOL_BUF:]

    moe0 = functools.partial(_moe, nf=row(norm_ffn[0]), wr=wr[0], br=br[0])
    x2p, *w0 = moe0(x1p.reshape(B * T, D), wg=w_gate, wu=w_up, wd=w_down, layer=0)
    x2s = moe0(x1s, wg=w0[0], wu=w0[1], wd=w0[2], layer=None)

    cos_p, sin_p = _rope_tables(jnp.arange(T, dtype=jnp.int32))
    cos_s, sin_s = _rope_tables(pos_s)
    g1 = row(norm_mix[1])
    tmq = 1024
    keep = min(WINDOW, T)
    qp, kdp, vdp, kp, vp = _qkv(x2p, g1, wqkv, row(b_qkv[0]), cos_p, sin_p, tmq, T // tmq, keep)
    qs, kds, vds, ks, vs = _qkv(x2s, g1, wqkv, row(b_qkv[0]), cos_s, sin_s, tmq, 1, tmq)

    op = _attn_prompt(qp, kdp, vdp, _sink_rows(sinks[0], WINDOW), B, T)

    bs = 8
    rows = bs * Td
    W = cache_k.shape[2]
    qrow = np.arange(rows)
    ccol = np.arange(bs * W)
    ncol = np.arange(rows)
    samp_q, t_q = qrow // Td, qrow % Td
    m_cache = (samp_q[:, None] == (ccol // W)[None, :]) & ((ccol % W)[None, :] > t_q[:, None])
    m_new = (samp_q[:, None] == (ncol // Td)[None, :]) & ((ncol % Td)[None, :] <= t_q[:, None])
    amask = jnp.asarray(np.tile(np.concatenate([m_cache, m_new], axis=1).T, (1, 4)), F32)
    kv_cache = (jnp.arange(W, dtype=jnp.int32)[None, :] >= (W - start)[:, None]).reshape(Bd // bs, bs * W)
    kval = jnp.concatenate([kv_cache, jnp.ones((Bd // bs, rows), bool)], axis=1).astype(F32)
    kval = kval.reshape(Bd // bs, bs * W + rows, 1)
    to_stored = lambda c: jnp.transpose(c[0], (0, 2, 3, 1))
    from_stored = lambda c: jnp.transpose(c, (0, 3, 1, 2))[None]
    osamp, nk_s, nv_s = _attn_sample(
        qs, kds, vds, ks, vs, to_stored(cache_k), to_stored(cache_v),
        amask, kval, _sink_rows(sinks[0], rows), bs, Td)

    moe1 = functools.partial(_moe, nf=row(norm_ffn[1]), wr=wr[1], br=br[1], final=row(norm_final))
    yp, *w1 = moe1(x2p, wg=w_gate, wu=w_up, wd=w_down, layer=1, oproj=(op, wo, row(b_o[0])))
    ys = moe1(x2s, wg=w1[0], wu=w1[1], wd=w1[2], layer=None, oproj=(osamp, wo, row(b_o[0])))

    k_p = kp.reshape(1, B, keep, N_KV_HEADS, HEAD_DIM)
    v_p = vp.reshape(1, B, keep, N_KV_HEADS, HEAD_DIM)
    return (yp.reshape(B, T, D), ys.reshape(Bd, Td, D), pool_p[None], k_p, v_p, pool_s[None],
            from_stored(nk_s), from_stored(nv_s))
```

```python
import functools

import jax
import jax.numpy as jnp
import numpy as np
from jax import lax
from jax.experimental import pallas as pl
from jax.experimental.pallas import tpu as pltpu

F32 = jnp.float32
BF16 = jnp.bfloat16

D_MODEL = 1024
POOL_WINDOWS = (2, 4, 8, 16)
POOL_GROUP_DIM = 256
POOL_BUF = 15
HALO = 16
HEAD_DIM = 64
N_HEADS = 16
N_KV_HEADS = 4
WINDOW = 128
ROPE_THETA = 10000.0
N_EXPERT_GROUPS = 4
EXPERTS_PER_GROUP = 4
N_EXPERTS = 16
D_EXPERT = 256
RMS_EPS = 1e-6
LANES = 128
ROUTER_ROWS = 32
EXPERT_ROW0 = 8
CAST_SLOTS = 4

POOL_PROMPT_ROWS = 1024
POOL_SAMPLE_SEQS = 32
MOE_ROWS = 512
MOE_BLOCK_ROWS = 144
QKV_ROWS = 1024
ATTN_PROMPT_ROWS = 1024
ATTN_SAMPLE_SEQS = 8
VMEM_LIMIT = 56 * 1024 * 1024
LOG2E = 1.4426950408889634
Q_SCALE = LOG2E * HEAD_DIM ** -0.5


def _rms(x, g):
    ms = jnp.mean(x * x, axis=-1, keepdims=True)
    return x * lax.rsqrt(ms + RMS_EPS) * g


def _pool_project(h, wins, pos, wp_ref, ls):
    G = POOL_GROUP_DIM
    outs = []
    for g, w in enumerate(POOL_WINDOWS):
        cnt = jnp.minimum(w, pos + 1).astype(F32)
        d = wins[g] / cnt - h[:, g * G:(g + 1) * G]
        outs.append(jnp.dot(d.astype(BF16), wp_ref[g], preferred_element_type=F32))
    return jnp.concatenate(outs, axis=-1) * ls


POOL_BLOCK = 128


def _pool_band():
    t = np.arange(POOL_BLOCK)[:, None] + HALO
    k = np.arange(POOL_BLOCK + HALO)[None, :]
    return jnp.asarray(np.stack([(k <= t) & (k > t - w) for w in POOL_WINDOWS]), BF16)


def _pool_prompt_kernel(x_ref, buf_ref, g_ref, wp_ref, ls_ref, band_ref, o_ref, nb_ref, hc, *, tq):
    t = pl.program_id(1)
    G = POOL_GROUP_DIM

    @pl.when(t == 0)
    def _():
        hc[pl.ds(0, HALO), :] = buf_ref[0]

    @pl.when(t > 0)
    def _():
        hc[pl.ds(0, HALO), :] = hc[pl.ds(tq, HALO), :]

    x = x_ref[0]
    h = _rms(x, g_ref[...])
    hc[pl.ds(HALO, tq), :] = h
    hb = hc[...].astype(BF16)
    wins = []
    for g in range(len(POOL_WINDOWS)):
        blocks = [jnp.dot(band_ref[g], hb[b * POOL_BLOCK:(b + 1) * POOL_BLOCK + HALO, g * G:(g + 1) * G],
                          preferred_element_type=F32) for b in range(tq // POOL_BLOCK)]
        wins.append(jnp.concatenate(blocks, axis=0))
    pos = t * tq + lax.broadcasted_iota(jnp.int32, (tq, 1), 0)
    o_ref[0] = x + _pool_project(h, wins, pos, wp_ref, ls_ref[...])

    @pl.when(t == pl.num_programs(1) - 1)
    def _():
        nb_ref[0] = hc[pl.ds(tq, HALO), :]


def _pool_prompt(x, buf16, g, wp, ls, tq=POOL_PROMPT_ROWS):
    B, T, D = x.shape
    return pl.pallas_call(
        functools.partial(_pool_prompt_kernel, tq=tq),
        grid=(B, T // tq),
        in_specs=[
            pl.BlockSpec((1, tq, D), lambda b, t: (b, t, 0)),
            pl.BlockSpec((1, HALO, D), lambda b, t: (b, 0, 0)),
            pl.BlockSpec((1, D), lambda b, t: (0, 0)),
            pl.BlockSpec((4, POOL_GROUP_DIM, POOL_GROUP_DIM), lambda b, t: (0, 0, 0)),
            pl.BlockSpec((1, D), lambda b, t: (0, 0)),
            pl.BlockSpec((4, POOL_BLOCK, POOL_BLOCK + HALO), lambda b, t: (0, 0, 0)),
        ],
        out_specs=[
            pl.BlockSpec((1, tq, D), lambda b, t: (b, t, 0)),
            pl.BlockSpec((1, HALO, D), lambda b, t: (b, 0, 0)),
        ],
        out_shape=[jax.ShapeDtypeStruct((B, T, D), F32), jax.ShapeDtypeStruct((B, HALO, D), F32)],
        scratch_shapes=[pltpu.VMEM((HALO + tq, D), F32)],
        compiler_params=pltpu.CompilerParams(
            dimension_semantics=("arbitrary", "arbitrary"), vmem_limit_bytes=VMEM_LIMIT),
        name="pool_prompt",
    )(x, buf16, g, wp, ls, _pool_band())


def _pool_sample_kernel(x_ref, st_ref, start_ref, g_ref, wp_ref, ls_ref, o_ref, ns_ref, *, bs, t):
    G = POOL_GROUP_DIM
    xs = [x_ref[:, i, :] for i in range(t)]
    hn = [_rms(x, g_ref[...]) for x in xs]
    hist = [st_ref[r] for r in range(POOL_BUF)] + hn

    def doubled(prev, lag, lo):
        out = [None] * len(prev)
        for i in range(len(prev)):
            if i >= lag and prev[i] is not None and prev[i - lag] is not None:
                out[i] = prev[i][:, lo:] + prev[i - lag][:, lo:]
        return out

    s2 = doubled(hist, 1, 0)
    s4 = doubled(s2, 2, G)
    s8 = doubled(s4, 4, G)
    s16 = doubled(s8, 8, G)
    wins = (s2, s4, s8, s16)
    start = start_ref[...]
    ds = [[] for _ in POOL_WINDOWS]
    for i in range(t):
        for g, w in enumerate(POOL_WINDOWS):
            cnt = jnp.minimum(w, start + (i + 1)).astype(F32)
            ds[g].append(wins[g][POOL_BUF + i][:, :G] / cnt - hn[i][:, g * G:(g + 1) * G])
    ys = [jnp.dot(jnp.concatenate(ds[g], axis=0).astype(BF16), wp_ref[g], preferred_element_type=F32)
          for g in range(len(POOL_WINDOWS))]
    y = jnp.concatenate(ys, axis=-1) * ls_ref[...]
    for i in range(t):
        o_ref[:, i, :] = xs[i] + y[i * bs:(i + 1) * bs]
    for r in range(POOL_BUF):
        ns_ref[r] = hist[r + t]


def _pool_sample(x, st, start, g, wp, ls, bs=POOL_SAMPLE_SEQS):
    Bd, t, D = x.shape
    return pl.pallas_call(
        functools.partial(_pool_sample_kernel, bs=bs, t=t),
        grid=(Bd // bs,),
        in_specs=[
            pl.BlockSpec((bs, t, D), lambda i: (i, 0, 0)),
            pl.BlockSpec((POOL_BUF, bs, D), lambda i: (0, i, 0)),
            pl.BlockSpec((bs, 1), lambda i: (i, 0)),
            pl.BlockSpec((1, D), lambda i: (0, 0)),
            pl.BlockSpec((4, POOL_GROUP_DIM, POOL_GROUP_DIM), lambda i: (0, 0, 0)),
            pl.BlockSpec((1, D), lambda i: (0, 0)),
        ],
        out_specs=[pl.BlockSpec((bs, t, D), lambda i: (i, 0, 0)),
                   pl.BlockSpec((POOL_BUF, bs, D), lambda i: (0, i, 0))],
        out_shape=[jax.ShapeDtypeStruct((Bd, t, D), F32), jax.ShapeDtypeStruct((POOL_BUF, Bd, D), F32)],
        compiler_params=pltpu.CompilerParams(
            dimension_semantics=("arbitrary",), vmem_limit_bytes=VMEM_LIMIT),
        name="pool_sample",
    )(x, st, start, g, wp, ls)


def _route(lg):
    R = lg.shape[1]
    big = jnp.float32(1 << 20)
    neg = jnp.float32(-jnp.inf)
    r8 = lax.broadcasted_iota(jnp.int32, (8, R), 0).astype(F32)
    r16 = lax.broadcasted_iota(jnp.int32, (N_EXPERTS, R), 0).astype(F32)
    is_g = r8 < N_EXPERT_GROUPS
    gl = jnp.where(is_g, lg[0:8], neg)
    m = jnp.max(gl, axis=0, keepdims=True)
    gidx = jnp.min(jnp.where(gl == m, r8, big), axis=0, keepdims=True)
    z = jnp.sum(jnp.where(is_g, jnp.exp(gl - m), 0.0), axis=0, keepdims=True)
    gw = 1.0 / z
    lo = gidx * EXPERTS_PER_GROUP
    in_grp = (r16 >= lo) & (r16 < lo + EXPERTS_PER_GROUP)
    el = jnp.where(in_grp, lg[EXPERT_ROW0:EXPERT_ROW0 + N_EXPERTS], neg)
    v1 = jnp.max(el, axis=0, keepdims=True)
    i1 = jnp.min(jnp.where(el == v1, r16, big), axis=0, keepdims=True)
    el2 = jnp.where(r16 == i1, neg, el)
    v2 = jnp.max(el2, axis=0, keepdims=True)
    i2 = jnp.min(jnp.where(el2 == v2, r16, big), axis=0, keepdims=True)
    t = jnp.exp(v2 - v1)
    w1 = 1.0 / (1.0 + t)
    w2 = t * w1
    e8 = jnp.where(is_g, r8, r8 - EXPERTS_PER_GROUP)
    cw = gw * (jnp.where(e8 == i1 - lo, w1, 0.0) + jnp.where(e8 == i2 - lo, w2, 0.0))
    return gidx, cw


def _moe_kernel(*refs, has_oproj, has_final, cast_layer, tm, bm):
    it = iter(refs)
    x_ref = next(it)
    if has_oproj:
        oin_ref, wo_ref, bo_ref = next(it), next(it), next(it)
    nf_ref, wr_ref, br_ref, wg_in, wu_in, wd_in = (next(it) for _ in range(6))
    if has_final:
        fn_ref = next(it)
    out_ref = next(it)
    if cast_layer is not None:
        wg_out, wu_out, wd_out = (next(it) for _ in range(3))
    hs, cs, ys, sel_s, earlier = (next(it) for _ in range(5))
    if cast_layer is None:
        wg_ref, wu_ref, wd_ref = wg_in, wu_in, wd_in
    else:
        wg_ref, wu_ref, wd_ref, stage_gu, stage_d, load_sem, store_sem = (next(it) for _ in range(7))
        pairs = ((wg_in, wg_ref, wg_out, stage_gu), (wu_in, wu_ref, wu_out, stage_gu), (wd_in, wd_ref, wd_out, stage_d))

        def store_copy(k):
            return pltpu.make_async_copy(pairs[k][1], pairs[k][2], store_sem.at[k])

        @pl.when(pl.program_id(0) == 0)
        def _():
            ahead = stage_gu.shape[0] - 1
            for src, dst, _, stage in pairs:
                def load(e, src=src, stage=stage):
                    slot = e % (ahead + 1)
                    return pltpu.make_async_copy(src.at[cast_layer, e], stage.at[slot], load_sem.at[slot])

                for e in range(ahead):
                    load(e).start()
                for e in range(N_EXPERTS):
                    if e + ahead < N_EXPERTS:
                        load(e + ahead).start()
                    load(e).wait()
                    dst[e] = stage[e % (ahead + 1)].astype(BF16)
            for k in range(len(pairs)):
                store_copy(k).start()

        @pl.when(pl.program_id(0) == pl.num_programs(0) - 1)
        def _():
            for k in range(len(pairs)):
                store_copy(k).wait()
    min_blk = -(-tm // bm)
    nblk = tm // bm + N_EXPERT_GROUPS
    cap = nblk * bm
    assert bm % 16 == 0
    nt = (((1,), (1,)), ((), ()))

    @pl.when(pl.program_id(0) == 0)
    def _():
        ri = lax.broadcasted_iota(jnp.int32, (tm, tm), 0)
        ci = lax.broadcasted_iota(jnp.int32, (tm, tm), 1)
        earlier[...] = jnp.where(ri < ci, 1.0, 0.0).astype(BF16)

    x = x_ref[...]
    if has_oproj:
        x = x + jnp.dot(oin_ref[...], wo_ref[...], preferred_element_type=F32) + bo_ref[...]
    if has_oproj:
        out_ref[...] = x
    h = _rms(x, nf_ref[...])
    hb = h.astype(BF16)
    h_lo = (h - hb.astype(F32)).astype(BF16)
    wr = wr_ref[...]
    both = lax.dot_general(wr, hb, nt, preferred_element_type=F32)
    lg = (both[0:ROUTER_ROWS] + both[ROUTER_ROWS:]
          + lax.dot_general(wr[0:ROUTER_ROWS], h_lo, nt, preferred_element_type=F32)) + br_ref[...]
    gidx, cw = _route(lg)
    r8 = lax.broadcasted_iota(jnp.int32, (8, tm), 0)
    cw_hi = cw.astype(BF16).astype(F32)
    cwt = jnp.concatenate([jnp.where(r8 < EXPERTS_PER_GROUP, cw_hi, cw - cw_hi),
                           jnp.zeros((LANES - 8, tm), F32)], axis=0).astype(BF16)

    oh = jnp.where(r8.astype(F32) == gidx, 1.0, 0.0)
    cnt_before = jnp.dot(oh.astype(BF16), earlier[...], preferred_element_type=F32)
    pos_row = jnp.sum(oh * cnt_before, axis=0, keepdims=True)
    start_blk, n_blk = [], []
    off = jnp.int32(0)
    for g in range(N_EXPERT_GROUPS):
        in_g = gidx == g
        n_g = jnp.sum(jnp.where(in_g, 1.0, 0.0)).astype(jnp.int32)
        blocks = sum(jnp.where(n_g > k * bm, 1, 0) for k in range(min_blk))
        start_blk.append(off)
        n_blk.append(blocks)
        pos_row = pos_row + jnp.where(in_g, (off * bm).astype(F32), 0.0)
        off = off + blocks

    total_blk = off
    main = min_blk * bm
    extra = ((main, bm, total_blk > min_blk), (main + bm, cap - main - bm, total_blk > min_blk + 1))

    def sort_rows(r0, n):
        sub = (r0 + lax.broadcasted_iota(jnp.int32, (n, tm), 0)).astype(F32)
        sel = jnp.where(pos_row == sub, 1.0, 0.0).astype(BF16)
        sel_s[r0:r0 + n, :] = sel
        hs[r0:r0 + n, :] = jnp.dot(sel, hb, preferred_element_type=F32).astype(BF16)
        cs[r0:r0 + n, :] = lax.dot_general(sel, cwt, nt, preferred_element_type=F32)

    sort_rows(0, main)
    for r0, n, needed in extra:
        pl.when(needed)(functools.partial(sort_rows, r0, n))
    ys[min_blk * bm:, :] = jnp.zeros((cap - min_blk * bm, D_MODEL), BF16)

    for g in range(N_EXPERT_GROUPS):
        wd = wd_ref[g * EXPERTS_PER_GROUP:(g + 1) * EXPERTS_PER_GROUP].reshape(
            EXPERTS_PER_GROUP * D_EXPERT, D_MODEL)

        def block(b, carry, g=g, wd=wd):
            rows = pl.ds(pl.multiple_of(b * bm, 16), bm)
            hblk = hs[rows, :]
            cblk = cs[rows, :]
            ln = lax.broadcasted_iota(jnp.int32, cblk.shape, 1)
            parts = []
            for e in range(EXPERTS_PER_GROUP):
                ce = jnp.sum(jnp.where((ln == e) | (ln == e + EXPERTS_PER_GROUP), cblk, 0.0),
                             axis=-1, keepdims=True)
                gt = jnp.dot(hblk, wg_ref[g * EXPERTS_PER_GROUP + e], preferred_element_type=F32)
                up = jnp.dot(hblk, wu_ref[g * EXPERTS_PER_GROUP + e], preferred_element_type=F32)
                a = gt / (1.0 + jnp.exp(-gt)) * up * ce
                parts.append(a.astype(BF16))
            a_all = jnp.concatenate(parts, axis=-1)
            ys[rows, :] = jnp.dot(a_all, wd, preferred_element_type=F32).astype(BF16)
            return carry

        lax.fori_loop(start_blk[g], start_blk[g] + n_blk[g], block, 0)

    def unsorted(r0, n):
        return lax.dot_general(sel_s[r0:r0 + n, :], ys[r0:r0 + n, :], (((0,), (0,)), ((), ())),
                               preferred_element_type=F32)

    def unsort_more(r0, n):
        out_ref[...] += unsorted(r0, n)

    out_ref[...] = (out_ref[...] if has_oproj else x_ref[...]) + unsorted(0, main)
    for r0, n, needed in extra:
        pl.when(needed)(functools.partial(unsort_more, r0, n))
    if has_final:
        out_ref[...] = _rms(out_ref[...], fn_ref[...])


def _moe(x, nf, wr, br, wg, wu, wd, layer, oproj=None, final=None, tm=MOE_ROWS, bm=MOE_BLOCK_ROWS):
    N, D = x.shape
    has_oproj = oproj is not None
    has_final = final is not None
    const2 = lambda i: (0, 0)
    const3 = lambda i: (0, 0, 0)
    cast = layer is not None
    tile = lambda i: (i, 0)
    once = pl.Buffered(1)
    args = [x]
    in_specs = [pl.BlockSpec((tm, D), tile)]
    if has_oproj:
        o, wo, bo = oproj
        args += [o, wo, bo]
        in_specs += [pl.BlockSpec((tm, D), tile), pl.BlockSpec((D, D), const2, pipeline_mode=once),
                     pl.BlockSpec((1, D), const2)]
    wr_hi = wr.astype(BF16)
    wr_lo = (wr - wr_hi.astype(F32)).astype(BF16)
    args += [nf, jnp.concatenate([wr_hi, wr_lo], axis=0), br, wg, wu, wd]
    in_specs += [
        pl.BlockSpec((1, D), const2),
        pl.BlockSpec((2 * ROUTER_ROWS, D), const2, pipeline_mode=once),
        pl.BlockSpec((ROUTER_ROWS, 1), const2),
    ]
    gu_shape, d_shape = (N_EXPERTS, D, D_EXPERT), (N_EXPERTS, D_EXPERT, D)
    if cast:
        in_specs += [pl.BlockSpec(memory_space=pl.ANY)] * 3
    else:
        in_specs += [pl.BlockSpec(gu_shape, const3, pipeline_mode=once), pl.BlockSpec(gu_shape, const3, pipeline_mode=once),
                     pl.BlockSpec(d_shape, const3, pipeline_mode=once)]
    if has_final:
        args.append(final)
        in_specs.append(pl.BlockSpec((1, D), const2))
    cap = (tm // bm + N_EXPERT_GROUPS) * bm
    out_specs = [pl.BlockSpec((tm, D), tile)]
    out_shape = [jax.ShapeDtypeStruct((N, D), F32)]
    scratch = [
        pltpu.VMEM((cap, D), BF16),
        pltpu.VMEM((cap, LANES), F32),
        pltpu.VMEM((cap, D), BF16),
        pltpu.VMEM((cap, tm), BF16),
        pltpu.VMEM((tm, tm), BF16),
    ]
    if cast:
        out_specs += [pl.BlockSpec(memory_space=pl.ANY)] * 3
        out_shape += [jax.ShapeDtypeStruct(gu_shape, BF16)] * 2 + [jax.ShapeDtypeStruct(d_shape, BF16)]
        scratch += [pltpu.VMEM(gu_shape, BF16), pltpu.VMEM(gu_shape, BF16), pltpu.VMEM(d_shape, BF16),
                    pltpu.VMEM((CAST_SLOTS,) + gu_shape[1:], F32), pltpu.VMEM((CAST_SLOTS,) + d_shape[1:], F32),
                    pltpu.SemaphoreType.DMA((CAST_SLOTS,)), pltpu.SemaphoreType.DMA((3,))]
    res = pl.pallas_call(
        functools.partial(_moe_kernel, has_oproj=has_oproj, has_final=has_final, cast_layer=layer, tm=tm, bm=bm),
        grid=(N // tm,),
        in_specs=in_specs,
        out_specs=out_specs,
        out_shape=out_shape,
        scratch_shapes=scratch,
        compiler_params=pltpu.CompilerParams(
            dimension_semantics=("arbitrary",), vmem_limit_bytes=VMEM_LIMIT),
        name="moe",
    )(*args)
    return res if cast else res[0]


def _swap_halves(x, low):
    return jnp.where(low, pltpu.roll(x, LANES - 32, axis=1), pltpu.roll(x, 32, axis=1))


def _qkv_kernel(x_ref, g_ref, w_ref, b_ref, cos_ref, sin_ref, q_ref, kd_ref, vd_ref, k_ref, v_ref, *, keep):
    tm = x_ref.shape[0]
    h = _rms(x_ref[...], g_ref[...]).astype(BF16)
    cos = cos_ref[...]
    sin = sin_ref[...]
    lane = lax.broadcasted_iota(jnp.int32, cos.shape, 1)
    low32 = (lane % HEAD_DIM) < (HEAD_DIM // 2)
    low64 = lane < HEAD_DIM
    nq = N_HEADS * HEAD_DIM
    nk = N_KV_HEADS * HEAD_DIM
    cw = 2 * LANES

    def project(c0):
        y = jnp.dot(h, w_ref[:, c0:c0 + cw], preferred_element_type=F32) + b_ref[:, c0:c0 + cw]
        return [y[:, i * LANES:(i + 1) * LANES] for i in range(cw // LANES)]

    def rope(c):
        return c * cos + _swap_halves(c, low32) * sin

    def dup(c):
        r = pltpu.roll(c, HEAD_DIM, axis=1)
        return jnp.where(low64, c, r), jnp.where(low64, r, c)

    for c0 in range(0, nq, cw):
        for i, c in enumerate(project(c0)):
            lanes = slice(c0 + i * LANES, c0 + (i + 1) * LANES)
            q_ref[:, lanes] = (rope(c) * Q_SCALE).astype(BF16)
    for c0 in range(0, nk, cw):
        for (src, rot, f_ref, d_ref) in ((nq, True, k_ref, kd_ref), (nq + nk, False, v_ref, vd_ref)):
            for i, c in enumerate(project(src + c0)):
                j = c0 // LANES + i
                c = rope(c) if rot else c
                f_ref[:, j * LANES:(j + 1) * LANES] = c[tm - keep:]
                d0, d1 = dup(c)
                d_ref[:, 2 * j * LANES:(2 * j + 1) * LANES] = d0.astype(BF16)
                d_ref[:, (2 * j + 1) * LANES:(2 * j + 2) * LANES] = d1.astype(BF16)


def _qkv(x, g, w, b, cos, sin, tm, n_pos_tiles, keep):
    N, D = x.shape
    nq, nk = N_HEADS * HEAD_DIM, N_KV_HEADS * HEAD_DIM
    const = lambda i: (0, 0)
    tile = lambda i: (i, 0)
    ptile = lambda i: (i % n_pos_tiles, 0)
    seq = lambda i: (i // n_pos_tiles, 0)
    n_seq = N // (tm * n_pos_tiles)
    return pl.pallas_call(
        functools.partial(_qkv_kernel, keep=keep),
        grid=(N // tm,),
        in_specs=[
            pl.BlockSpec((tm, D), tile),
            pl.BlockSpec((1, D), const),
            pl.BlockSpec((D, nq + 2 * nk), const),
            pl.BlockSpec((1, nq + 2 * nk), const),
            pl.BlockSpec((tm, LANES), ptile),
            pl.BlockSpec((tm, LANES), ptile),
        ],
        out_specs=[
            pl.BlockSpec((tm, nq), tile),
            pl.BlockSpec((tm, 2 * nk), tile),
            pl.BlockSpec((tm, 2 * nk), tile),
            pl.BlockSpec((keep, nk), seq),
            pl.BlockSpec((keep, nk), seq),
        ],
        out_shape=[
            jax.ShapeDtypeStruct((N, nq), BF16),
            jax.ShapeDtypeStruct((N, 2 * nk), BF16),
            jax.ShapeDtypeStruct((N, 2 * nk), BF16),
            jax.ShapeDtypeStruct((n_seq * keep, nk), F32),
            jax.ShapeDtypeStruct((n_seq * keep, nk), F32),
        ],
        compiler_params=pltpu.CompilerParams(
            dimension_semantics=("arbitrary",), vmem_limit_bytes=VMEM_LIMIT),
        name="qkv",
    )(x, g, w, b, cos, sin)


def _stack_heads(qc_list):
    lane = lax.broadcasted_iota(jnp.int32, qc_list[0].shape, 1)
    low = lane < HEAD_DIM
    zero = jnp.zeros_like(qc_list[0])
    stack = []
    for qc in qc_list:
        stack.append(jnp.where(low, qc, zero))
        stack.append(jnp.where(low, zero, qc))
    return jnp.concatenate(stack, axis=0)


def _softmax_keys_on_rows(s, sinkrow):
    m = jnp.maximum(jnp.max(s, axis=0, keepdims=True), sinkrow)
    p = jnp.exp2(s - m)
    den = jnp.sum(p, axis=0, keepdims=True) + jnp.exp2(sinkrow - m)
    return (p * (1.0 / den)).astype(BF16)


def _unstack_heads(o, R):
    lowf = lax.broadcasted_iota(jnp.int32, (R, LANES), 1) < HEAD_DIM
    return (jnp.where(lowf, o[0:R], o[R:2 * R]), jnp.where(lowf, o[2 * R:3 * R], o[3 * R:4 * R]))


NT_DIMS = (((1,), (1,)), ((), ()))
TN_DIMS = (((0,), (0,)), ((), ()))
TT_DIMS = (((0,), (1,)), ((), ()))


def _attend(qc_list, kd, vd, bias, sinkrow):
    qs = _stack_heads(qc_list)
    s = lax.dot_general(kd, qs, NT_DIMS, preferred_element_type=F32) + bias
    p = _softmax_keys_on_rows(s, sinkrow)
    o = lax.dot_general(p, vd, TN_DIMS, preferred_element_type=F32)
    return _unstack_heads(o, qc_list[0].shape[0])


def _attn_prompt_kernel(q_ref, kp_ref, kc_ref, vp_ref, vc_ref, sink_ref, o_ref, *, tq):
    n = pl.program_id(1)
    B = WINDOW
    kd = jnp.concatenate([kp_ref[...], kc_ref[...]], axis=0)
    vd = jnp.concatenate([vp_ref[...], vc_ref[...]], axis=0)
    kj = lax.broadcasted_iota(jnp.int32, (2 * B, B), 0)
    qi = lax.broadcasted_iota(jnp.int32, (2 * B, B), 1)
    rel = B + qi - kj
    band = (rel >= 0) & (rel < WINDOW)
    neg = jnp.float32(-jnp.inf)
    band_bias = jnp.concatenate([jnp.where(band, 0.0, neg)] * 4, axis=1)
    first_bias = jnp.concatenate([jnp.where(band & ((kj >= B) | (n > 0)), 0.0, neg)] * 4, axis=1)
    for j in range(tq // B):
        bias = first_bias if j == 0 else band_bias
        for g in range(N_KV_HEADS):
            qc = [q_ref[j * B:(j + 1) * B, (2 * g + i) * LANES:(2 * g + i + 1) * LANES] for i in range(2)]
            kg = kd[j * B:(j + 2) * B, g * LANES:(g + 1) * LANES]
            vg = vd[j * B:(j + 2) * B, g * LANES:(g + 1) * LANES]
            o0, o1 = _attend(qc, kg, vg, bias, sink_ref[g])
            o_ref[j * B:(j + 1) * B, 2 * g * LANES:(2 * g + 1) * LANES] = o0.astype(BF16)
            o_ref[j * B:(j + 1) * B, (2 * g + 1) * LANES:(2 * g + 2) * LANES] = o1.astype(BF16)


def _attn_prompt(q, kd, vd, sinkrow, batch, seq, tq=ATTN_PROMPT_ROWS):
    N = q.shape[0]
    nt = seq // tq
    r = tq // WINDOW
    cur = lambda b, n: (b * nt + n, 0)
    prev = lambda b, n: (jnp.maximum((b * nt + n) * r - 1, 0), 0)
    kvw = 2 * N_KV_HEADS * HEAD_DIM
    return pl.pallas_call(
        functools.partial(_attn_prompt_kernel, tq=tq),
        grid=(batch, nt),
        in_specs=[
            pl.BlockSpec((tq, N_HEADS * HEAD_DIM), cur),
            pl.BlockSpec((WINDOW, kvw), prev),
            pl.BlockSpec((tq, kvw), cur),
            pl.BlockSpec((WINDOW, kvw), prev),
            pl.BlockSpec((tq, kvw), cur),
            pl.BlockSpec((N_KV_HEADS, 1, 4 * WINDOW), lambda b, n: (0, 0, 0)),
        ],
        out_specs=pl.BlockSpec((tq, N_HEADS * HEAD_DIM), cur),
        out_shape=jax.ShapeDtypeStruct((N, N_HEADS * HEAD_DIM), BF16),
        compiler_params=pltpu.CompilerParams(
            dimension_semantics=("arbitrary", "arbitrary"), vmem_limit_bytes=VMEM_LIMIT),
        name="attn_prompt",
    )(q, kd, kd, vd, vd, sinkrow)


def _attn_sample_kernel(q_ref, kdn_ref, vdn_ref, kn_ref, vn_ref, ck_ref, cv_ref, mask_ref, kval_ref, sink_ref,
                        o_ref, nk_ref, nv_ref, *, bs, t):
    W = WINDOW
    rows = bs * t
    bias = jnp.where((mask_ref[...] > 0) & (kval_ref[0] > 0), 0.0, -jnp.inf).astype(F32)
    lane = lax.broadcasted_iota(jnp.int32, (HEAD_DIM, W), 1)
    fresh = lane >= W - t

    def new_cols(n_ref):
        n = jnp.concatenate([n_ref[...], jnp.zeros((LANES - rows, N_KV_HEADS * HEAD_DIM), F32)], axis=0)
        nt_ = jnp.transpose(n)
        return [pltpu.roll(nt_, (W - t - s * t) % LANES, axis=1) for s in range(bs)]

    k_cols, v_cols = new_cols(kn_ref), new_cols(vn_ref)
    for g in range(N_KV_HEADS):
        hd = slice(g * HEAD_DIM, (g + 1) * HEAD_DIM)
        kt = jnp.concatenate([ck_ref[s, g] for s in range(bs)], axis=1).astype(BF16)
        vt = jnp.concatenate([cv_ref[s, g] for s in range(bs)], axis=1).astype(BF16)
        kt2 = jnp.concatenate([kt, kt], axis=0)
        vt2 = jnp.concatenate([vt, vt], axis=0)
        qs = _stack_heads([q_ref[:, (2 * g + i) * LANES:(2 * g + i + 1) * LANES] for i in range(2)])
        s_old = lax.dot_general(kt2, qs, TT_DIMS, preferred_element_type=F32)
        s_new = lax.dot_general(kdn_ref[:, g * LANES:(g + 1) * LANES], qs, NT_DIMS, preferred_element_type=F32)
        p = _softmax_keys_on_rows(jnp.concatenate([s_old, s_new], axis=0) + bias, sink_ref[g])
        o = (lax.dot_general(p[:bs * W], vt2, TT_DIMS, preferred_element_type=F32)
             + lax.dot_general(p[bs * W:], vdn_ref[:, g * LANES:(g + 1) * LANES], TN_DIMS,
                               preferred_element_type=F32))
        o0, o1 = _unstack_heads(o, rows)
        o_ref[:, 2 * g * LANES:(2 * g + 1) * LANES] = o0.astype(BF16)
        o_ref[:, (2 * g + 1) * LANES:(2 * g + 2) * LANES] = o1.astype(BF16)
        for s in range(bs):
            nk_ref[s, g] = jnp.where(fresh, k_cols[s][hd], pltpu.roll(ck_ref[s, g], W - t, axis=1))
            nv_ref[s, g] = jnp.where(fresh, v_cols[s][hd], pltpu.roll(cv_ref[s, g], W - t, axis=1))


def _attn_sample(q, kdn, vdn, kn, vn, ck, cv, mask, kval, sinkrow, bs, t):
    Bd = ck.shape[0]
    rows = bs * t
    kvw = N_KV_HEADS * HEAD_DIM
    S = bs * WINDOW + rows
    tile = lambda i: (i, 0)
    tile3 = lambda i: (i, 0, 0)
    tile4 = lambda i: (i, 0, 0, 0)
    cblk = (bs, N_KV_HEADS, HEAD_DIM, WINDOW)
    return pl.pallas_call(
        functools.partial(_attn_sample_kernel, bs=bs, t=t),
        grid=(Bd // bs,),
        in_specs=[
            pl.BlockSpec((rows, N_HEADS * HEAD_DIM), tile),
            pl.BlockSpec((rows, 2 * kvw), tile),
            pl.BlockSpec((rows, 2 * kvw), tile),
            pl.BlockSpec((rows, kvw), tile),
            pl.BlockSpec((rows, kvw), tile),
            pl.BlockSpec(cblk, tile4),
            pl.BlockSpec(cblk, tile4),
            pl.BlockSpec((S, 4 * rows), lambda i: (0, 0)),
            pl.BlockSpec((1, S, 1), tile3),
            pl.BlockSpec((N_KV_HEADS, 1, 4 * rows), lambda i: (0, 0, 0)),
        ],
        out_specs=[
            pl.BlockSpec((rows, N_HEADS * HEAD_DIM), tile),
            pl.BlockSpec(cblk, tile4),
            pl.BlockSpec(cblk, tile4),
        ],
        out_shape=[
            jax.ShapeDtypeStruct((Bd * t, N_HEADS * HEAD_DIM), BF16),
            jax.ShapeDtypeStruct((Bd,) + cblk[1:], F32),
            jax.ShapeDtypeStruct((Bd,) + cblk[1:], F32),
        ],
        compiler_params=pltpu.CompilerParams(
            dimension_semantics=("arbitrary",), vmem_limit_bytes=VMEM_LIMIT),
        name="attn_sample",
    )(q, kdn, vdn, kn, vn, ck, cv, mask, kval, sinkrow)


def _rope_tables(pos):
    inv = ROPE_THETA ** (-jnp.arange(0, HEAD_DIM, 2, dtype=F32) / HEAD_DIM)
    reps = LANES // (HEAD_DIM // 2)
    sign = jnp.asarray(np.where((np.arange(LANES) % HEAD_DIM) < HEAD_DIM // 2, -1.0, 1.0), F32)
    ang = pos.astype(F32)[:, None] * inv[None, :]
    cos, sin = lax.optimization_barrier((jnp.cos(ang), jnp.sin(ang)))
    return jnp.tile(cos, (1, reps)), jnp.tile(sin, (1, reps)) * sign[None, :]


def _sink_rows(sinks, rows):
    s = (sinks.astype(F32) * LOG2E).reshape(N_KV_HEADS, 1, N_HEADS // N_KV_HEADS, 1)
    return jnp.broadcast_to(s, (N_KV_HEADS, 1, 4, rows)).reshape(N_KV_HEADS, 1, 4 * rows)


def kernel(x_prompt, x_sample, state_pool, cache_k, cache_v, sample_start, norm_mix, norm_ffn, norm_final,
           w_pool, ls_pool, w_qkv, b_qkv, sinks, w_o, b_o, w_rg, b_rg, w_re, b_re, w_gate, w_up, w_down):
    B, T, D = x_prompt.shape
    Bd, Td, _ = x_sample.shape
    row = lambda v: v.reshape(1, -1).astype(F32)

    wp = w_pool[0].astype(BF16)
    wqkv = w_qkv[0].astype(BF16)
    wo = w_o[0].astype(BF16)
    def router_rows(g_part, e_part):
        z = lambda n: jnp.zeros((n,) + g_part.shape[1:], F32)
        return jnp.concatenate([g_part, z(EXPERT_ROW0 - N_EXPERT_GROUPS), e_part,
                                z(ROUTER_ROWS - EXPERT_ROW0 - N_EXPERTS)], axis=0)

    wr = [router_rows(w_rg[l].T, w_re[l].T) for l in range(2)]
    br = [router_rows(b_rg[l][:, None], b_re[l][:, None]) for l in range(2)]

    start = sample_start.astype(jnp.int32)

    x1p, pool_p16 = _pool_prompt(x_prompt, jnp.zeros((B, HALO, D), F32), row(norm_mix[0]), wp, row(ls_pool[0]))
    pos_s = (start[:, None] + jnp.arange(Td, dtype=jnp.int32)[None, :]).reshape(-1)
    x1s, pool_s_t = _pool_sample(x_sample, jnp.transpose(state_pool[0], (1, 0, 2)), start[:, None],
                                 row(norm_mix[0]), wp, row(ls_pool[0]))
    x1s = x1s.reshape(Bd * Td, D)
    pool_s = jnp.transpose(pool_s_t, (1, 0, 2))
    pool_p = pool_p16[:, HALO - POOL_BUF:]

    moe0 = functools.partial(_moe, nf=row(norm_ffn[0]), wr=wr[0], br=br[0])
    x2p, *w0 = moe0(x1p.reshape(B * T, D), wg=w_gate, wu=w_up, wd=w_down, layer=0)
    x2s = moe0(x1s, wg=w0[0], wu=w0[1], wd=w0[2], layer=None)

    cos_p, sin_p = _rope_tables(jnp.arange(T, dtype=jnp.int32))
    cos_s, sin_s = _rope_tables(pos_s)
    g1 = row(norm_mix[1])
    tmq = QKV_ROWS
    keep = min(WINDOW, T)
    qp, kdp, vdp, kp, vp = _qkv(x2p, g1, wqkv, row(b_qkv[0]), cos_p, sin_p, tmq, T // tmq, keep)
    qs, kds, vds, ks, vs = _qkv(x2s, g1, wqkv, row(b_qkv[0]), cos_s, sin_s, tmq, 1, tmq)

    op = _attn_prompt(qp, kdp, vdp, _sink_rows(sinks[0], WINDOW), B, T)

    bs = ATTN_SAMPLE_SEQS
    rows = bs * Td
    W = cache_k.shape[2]
    qrow = np.arange(rows)
    ccol = np.arange(bs * W)
    ncol = np.arange(rows)
    samp_q, t_q = qrow // Td, qrow % Td
    m_cache = (samp_q[:, None] == (ccol // W)[None, :]) & ((ccol % W)[None, :] > t_q[:, None])
    m_new = (samp_q[:, None] == (ncol // Td)[None, :]) & ((ncol % Td)[None, :] <= t_q[:, None])
    amask = jnp.asarray(np.tile(np.concatenate([m_cache, m_new], axis=1).T, (1, 4)), F32)
    kv_cache = (jnp.arange(W, dtype=jnp.int32)[None, :] >= (W - start)[:, None]).reshape(Bd // bs, bs * W)
    kval = jnp.concatenate([kv_cache, jnp.ones((Bd // bs, rows), bool)], axis=1).astype(F32)
    kval = kval.reshape(Bd // bs, bs * W + rows, 1)
    to_stored = lambda c: jnp.transpose(c[0], (0, 2, 3, 1))
    from_stored = lambda c: jnp.transpose(c, (0, 3, 1, 2))[None]
    osamp, nk_s, nv_s = _attn_sample(
        qs, kds, vds, ks, vs, to_stored(cache_k), to_stored(cache_v),
        amask, kval, _sink_rows(sinks[0], rows), bs, Td)

    moe1 = functools.partial(_moe, nf=row(norm_ffn[1]), wr=wr[1], br=br[1], final=row(norm_final))
    yp, *w1 = moe1(x2p, wg=w_gate, wu=w_up, wd=w_down, layer=1, oproj=(op, wo, row(b_o[0])))
    ys = moe1(x2s, wg=w1[0], wu=w1[1], wd=w1[2], layer=None, oproj=(osamp, wo, row(b_o[0])))

    k_p = kp.reshape(1, B, keep, N_KV_HEADS, HEAD_DIM)
    v_p = vp.reshape(1, B, keep, N_KV_HEADS, HEAD_DIM)
    return (yp.reshape(B, T, D), ys.reshape(Bd, Td, D), pool_p[None], k_p, v_p, pool_s[None],
            from_stored(nk_s), from_stored(nv_s))
```

```python
import functools

import jax
import jax.numpy as jnp
import numpy as np
from jax import lax
from jax.experimental import pallas as pl
from jax.experimental.pallas import tpu as pltpu

F32 = jnp.float32
BF16 = jnp.bfloat16

D_MODEL = 1024
POOL_WINDOWS = (2, 4, 8, 16)
POOL_GROUP_DIM = 256
POOL_BUF = 15
HALO = 16
HEAD_DIM = 64
N_HEADS = 16
N_KV_HEADS = 4
WINDOW = 128
ROPE_THETA = 10000.0
N_EXPERT_GROUPS = 4
EXPERTS_PER_GROUP = 4
N_EXPERTS = 16
D_EXPERT = 256
RMS_EPS = 1e-6
LANES = 128
ROUTER_ROWS = 32
EXPERT_ROW0 = 8
CAST_SLOTS = 4

POOL_PROMPT_ROWS = 1024
POOL_SAMPLE_SEQS = 32
MOE_ROWS = 512
MOE_BLOCK_ROWS = 144
QKV_ROWS = 1024
ATTN_PROMPT_ROWS = 1024
ATTN_SAMPLE_SEQS = 8
VMEM_LIMIT = 56 * 1024 * 1024
LOG2E = 1.4426950408889634
Q_SCALE = LOG2E * HEAD_DIM ** -0.5


def _rms(x, g):
    ms = jnp.mean(x * x, axis=-1, keepdims=True)
    return x * lax.rsqrt(ms + RMS_EPS) * g


def _pool_project(h, wins, pos, wp_ref, ls):
    G = POOL_GROUP_DIM
    outs = []
    for g, w in enumerate(POOL_WINDOWS):
        cnt = jnp.minimum(w, pos + 1).astype(F32)
        d = wins[g] / cnt - h[:, g * G:(g + 1) * G]
        outs.append(jnp.dot(d.astype(BF16), wp_ref[g], preferred_element_type=F32))
    return jnp.concatenate(outs, axis=-1) * ls


POOL_BLOCK = 128


def _pool_band():
    t = np.arange(POOL_BLOCK)[:, None] + HALO
    k = np.arange(POOL_BLOCK + HALO)[None, :]
    return jnp.asarray(np.stack([(k <= t) & (k > t - w) for w in POOL_WINDOWS]), BF16)


def _pool_prompt_kernel(x_ref, buf_ref, g_ref, wp_ref, ls_ref, band_ref, o_ref, nb_ref, hc, *, tq):
    t = pl.program_id(1)
    G = POOL_GROUP_DIM

    @pl.when(t == 0)
    def _():
        hc[pl.ds(0, HALO), :] = buf_ref[0]

    @pl.when(t > 0)
    def _():
        hc[pl.ds(0, HALO), :] = hc[pl.ds(tq, HALO), :]

    x = x_ref[0]
    h = _rms(x, g_ref[...])
    hc[pl.ds(HALO, tq), :] = h
    hb = hc[...].astype(BF16)
    wins = []
    for g in range(len(POOL_WINDOWS)):
        blocks = [jnp.dot(band_ref[g], hb[b * POOL_BLOCK:(b + 1) * POOL_BLOCK + HALO, g * G:(g + 1) * G],
                          preferred_element_type=F32) for b in range(tq // POOL_BLOCK)]
        wins.append(jnp.concatenate(blocks, axis=0))
    pos = t * tq + lax.broadcasted_iota(jnp.int32, (tq, 1), 0)
    o_ref[0] = x + _pool_project(h, wins, pos, wp_ref, ls_ref[...])

    @pl.when(t == pl.num_programs(1) - 1)
    def _():
        nb_ref[0] = hc[pl.ds(tq, HALO), :]


def _pool_prompt(x, buf16, g, wp, ls, tq=POOL_PROMPT_ROWS):
    B, T, D = x.shape
    return pl.pallas_call(
        functools.partial(_pool_prompt_kernel, tq=tq),
        grid=(B, T // tq),
        in_specs=[
            pl.BlockSpec((1, tq, D), lambda b, t: (b, t, 0)),
            pl.BlockSpec((1, HALO, D), lambda b, t: (b, 0, 0)),
            pl.BlockSpec((1, D), lambda b, t: (0, 0)),
            pl.BlockSpec((4, POOL_GROUP_DIM, POOL_GROUP_DIM), lambda b, t: (0, 0, 0)),
            pl.BlockSpec((1, D), lambda b, t: (0, 0)),
            pl.BlockSpec((4, POOL_BLOCK, POOL_BLOCK + HALO), lambda b, t: (0, 0, 0)),
        ],
        out_specs=[
            pl.BlockSpec((1, tq, D), lambda b, t: (b, t, 0)),
            pl.BlockSpec((1, HALO, D), lambda b, t: (b, 0, 0)),
        ],
        out_shape=[jax.ShapeDtypeStruct((B, T, D), F32), jax.ShapeDtypeStruct((B, HALO, D), F32)],
        scratch_shapes=[pltpu.VMEM((HALO + tq, D), F32)],
        compiler_params=pltpu.CompilerParams(
            dimension_semantics=("arbitrary", "arbitrary"), vmem_limit_bytes=VMEM_LIMIT),
        name="pool_prompt",
    )(x, buf16, g, wp, ls, _pool_band())


def _pool_sample_kernel(x_ref, st_ref, start_ref, g_ref, wp_ref, ls_ref, o_ref, ns_ref, *, bs, t):
    G = POOL_GROUP_DIM
    xs = [x_ref[:, i, :] for i in range(t)]
    hn = [_rms(x, g_ref[...]) for x in xs]
    hist = [st_ref[r] for r in range(POOL_BUF)] + hn

    def doubled(prev, lag, lo):
        out = [None] * len(prev)
        for i in range(len(prev)):
            if i >= lag and prev[i] is not None and prev[i - lag] is not None:
                out[i] = prev[i][:, lo:] + prev[i - lag][:, lo:]
        return out

    s2 = doubled(hist, 1, 0)
    s4 = doubled(s2, 2, G)
    s8 = doubled(s4, 4, G)
    s16 = doubled(s8, 8, G)
    wins = (s2, s4, s8, s16)
    start = start_ref[...]
    ds = [[] for _ in POOL_WINDOWS]
    for i in range(t):
        for g, w in enumerate(POOL_WINDOWS):
            cnt = jnp.minimum(w, start + (i + 1)).astype(F32)
            ds[g].append(wins[g][POOL_BUF + i][:, :G] / cnt - hn[i][:, g * G:(g + 1) * G])
    ys = [jnp.dot(jnp.concatenate(ds[g], axis=0).astype(BF16), wp_ref[g], preferred_element_type=F32)
          for g in range(len(POOL_WINDOWS))]
    y = jnp.concatenate(ys, axis=-1) * ls_ref[...]
    for i in range(t):
        o_ref[:, i, :] = xs[i] + y[i * bs:(i + 1) * bs]
    for r in range(POOL_BUF):
        ns_ref[r] = hist[r + t]


def _pool_sample(x, st, start, g, wp, ls, bs=POOL_SAMPLE_SEQS):
    Bd, t, D = x.shape
    return pl.pallas_call(
        functools.partial(_pool_sample_kernel, bs=bs, t=t),
        grid=(Bd // bs,),
        in_specs=[
            pl.BlockSpec((bs, t, D), lambda i: (i, 0, 0)),
            pl.BlockSpec((POOL_BUF, bs, D), lambda i: (0, i, 0)),
            pl.BlockSpec((bs, 1), lambda i: (i, 0)),
            pl.BlockSpec((1, D), lambda i: (0, 0)),
            pl.BlockSpec((4, POOL_GROUP_DIM, POOL_GROUP_DIM), lambda i: (0, 0, 0)),
            pl.BlockSpec((1, D), lambda i: (0, 0)),
        ],
        out_specs=[pl.BlockSpec((bs, t, D), lambda i: (i, 0, 0)),
                   pl.BlockSpec((POOL_BUF, bs, D), lambda i: (0, i, 0))],
        out_shape=[jax.ShapeDtypeStruct((Bd, t, D), F32), jax.ShapeDtypeStruct((POOL_BUF, Bd, D), F32)],
        compiler_params=pltpu.CompilerParams(
            dimension_semantics=("arbitrary",), vmem_limit_bytes=VMEM_LIMIT),
        name="pool_sample",
    )(x, st, start, g, wp, ls)


def _route(lg):
    R = lg.shape[1]
    big = jnp.float32(1 << 20)
    neg = jnp.float32(-jnp.inf)
    r8 = lax.broadcasted_iota(jnp.int32, (8, R), 0).astype(F32)
    r16 = lax.broadcasted_iota(jnp.int32, (N_EXPERTS, R), 0).astype(F32)
    is_g = r8 < N_EXPERT_GROUPS
    gl = jnp.where(is_g, lg[0:8], neg)
    m = jnp.max(gl, axis=0, keepdims=True)
    gidx = jnp.min(jnp.where(gl == m, r8, big), axis=0, keepdims=True)
    z = jnp.sum(jnp.where(is_g, jnp.exp(gl - m), 0.0), axis=0, keepdims=True)
    gw = 1.0 / z
    lo = gidx * EXPERTS_PER_GROUP
    in_grp = (r16 >= lo) & (r16 < lo + EXPERTS_PER_GROUP)
    el = jnp.where(in_grp, lg[EXPERT_ROW0:EXPERT_ROW0 + N_EXPERTS], neg)
    v1 = jnp.max(el, axis=0, keepdims=True)
    i1 = jnp.min(jnp.where(el == v1, r16, big), axis=0, keepdims=True)
    el2 = jnp.where(r16 == i1, neg, el)
    v2 = jnp.max(el2, axis=0, keepdims=True)
    i2 = jnp.min(jnp.where(el2 == v2, r16, big), axis=0, keepdims=True)
    t = jnp.exp(v2 - v1)
    w1 = 1.0 / (1.0 + t)
    w2 = t * w1
    e8 = jnp.where(is_g, r8, r8 - EXPERTS_PER_GROUP)
    cw = gw * (jnp.where(e8 == i1 - lo, w1, 0.0) + jnp.where(e8 == i2 - lo, w2, 0.0))
    return gidx, cw


def _moe_kernel(*refs, has_oproj, has_final, cast_layer, tm, bm):
    it = iter(refs)
    x_ref = next(it)
    if has_oproj:
        oin_ref, wo_ref, bo_ref = next(it), next(it), next(it)
    nf_ref, wr_ref, br_ref, wg_in, wu_in, wd_in = (next(it) for _ in range(6))
    if has_final:
        fn_ref = next(it)
    out_ref = next(it)
    if cast_layer is not None:
        wg_out, wu_out, wd_out = (next(it) for _ in range(3))
    hs, cs, ys, sel_s, earlier = (next(it) for _ in range(5))
    if cast_layer is None:
        wg_ref, wu_ref, wd_ref = wg_in, wu_in, wd_in
    else:
        wg_ref, wu_ref, wd_ref, stage_gu, stage_d, load_sem, store_sem = (next(it) for _ in range(7))
        pairs = ((wg_in, wg_ref, wg_out, stage_gu), (wu_in, wu_ref, wu_out, stage_gu), (wd_in, wd_ref, wd_out, stage_d))

        def store_copy(k):
            return pltpu.make_async_copy(pairs[k][1], pairs[k][2], store_sem.at[k])

        @pl.when(pl.program_id(0) == 0)
        def _():
            ahead = stage_gu.shape[0] - 1
            for src, dst, _, stage in pairs:
                def load(e, src=src, stage=stage):
                    slot = e % (ahead + 1)
                    return pltpu.make_async_copy(src.at[cast_layer, e], stage.at[slot], load_sem.at[slot])

                for e in range(ahead):
                    load(e).start()
                for e in range(N_EXPERTS):
                    if e + ahead < N_EXPERTS:
                        load(e + ahead).start()
                    load(e).wait()
                    dst[e] = stage[e % (ahead + 1)].astype(BF16)
            for k in range(len(pairs)):
                store_copy(k).start()

        @pl.when(pl.program_id(0) == pl.num_programs(0) - 1)
        def _():
            for k in range(len(pairs)):
                store_copy(k).wait()
    min_blk = -(-tm // bm)
    nblk = tm // bm + N_EXPERT_GROUPS
    cap = nblk * bm
    assert bm % 16 == 0
    nt = (((1,), (1,)), ((), ()))

    @pl.when(pl.program_id(0) == 0)
    def _():
        ri = lax.broadcasted_iota(jnp.int32, (tm, tm), 0)
        ci = lax.broadcasted_iota(jnp.int32, (tm, tm), 1)
        earlier[...] = jnp.where(ri < ci, 1.0, 0.0).astype(BF16)

    x = x_ref[...]
    if has_oproj:
        x = x + jnp.dot(oin_ref[...], wo_ref[...], preferred_element_type=F32) + bo_ref[...]
    if has_oproj:
        out_ref[...] = x
    h = _rms(x, nf_ref[...])
    hb = h.astype(BF16)
    h_lo = (h - hb.astype(F32)).astype(BF16)
    wr = wr_ref[...]
    both = lax.dot_general(wr, hb, nt, preferred_element_type=F32)
    lg = (both[0:ROUTER_ROWS] + both[ROUTER_ROWS:]
          + lax.dot_general(wr[0:ROUTER_ROWS], h_lo, nt, preferred_element_type=F32)) + br_ref[...]
    gidx, cw = _route(lg)
    r8 = lax.broadcasted_iota(jnp.int32, (8, tm), 0)
    cw_hi = cw.astype(BF16).astype(F32)
    cwt = jnp.concatenate([jnp.where(r8 < EXPERTS_PER_GROUP, cw_hi, cw - cw_hi),
                           jnp.zeros((LANES - 8, tm), F32)], axis=0).astype(BF16)

    oh = jnp.where(r8.astype(F32) == gidx, 1.0, 0.0)
    cnt_before = jnp.dot(oh.astype(BF16), earlier[...], preferred_element_type=F32)
    pos_row = jnp.sum(oh * cnt_before, axis=0, keepdims=True)
    start_blk, n_blk = [], []
    off = jnp.int32(0)
    for g in range(N_EXPERT_GROUPS):
        in_g = gidx == g
        n_g = jnp.sum(jnp.where(in_g, 1.0, 0.0)).astype(jnp.int32)
        blocks = sum(jnp.where(n_g > k * bm, 1, 0) for k in range(min_blk))
        start_blk.append(off)
        n_blk.append(blocks)
        pos_row = pos_row + jnp.where(in_g, (off * bm).astype(F32), 0.0)
        off = off + blocks

    total_blk = off
    main = min_blk * bm
    extra = ((main, bm, total_blk > min_blk), (main + bm, cap - main - bm, total_blk > min_blk + 1))

    def sort_rows(r0, n):
        sub = (r0 + lax.broadcasted_iota(jnp.int32, (n, tm), 0)).astype(F32)
        sel = jnp.where(pos_row == sub, 1.0, 0.0).astype(BF16)
        sel_s[r0:r0 + n, :] = sel
        hs[r0:r0 + n, :] = jnp.dot(sel, hb, preferred_element_type=F32).astype(BF16)
        cs[r0:r0 + n, :] = lax.dot_general(sel, cwt, nt, preferred_element_type=F32)

    sort_rows(0, main)
    for r0, n, needed in extra:
        pl.when(needed)(functools.partial(sort_rows, r0, n))
    ys[min_blk * bm:, :] = jnp.zeros((cap - min_blk * bm, D_MODEL), BF16)

    for g in range(N_EXPERT_GROUPS):
        wd = wd_ref[g * EXPERTS_PER_GROUP:(g + 1) * EXPERTS_PER_GROUP].reshape(
            EXPERTS_PER_GROUP * D_EXPERT, D_MODEL)

        def block(b, carry, g=g, wd=wd):
            rows = pl.ds(pl.multiple_of(b * bm, 16), bm)
            hblk = hs[rows, :]
            cblk = cs[rows, :]
            ln = lax.broadcasted_iota(jnp.int32, cblk.shape, 1)
            parts = []
            for e in range(EXPERTS_PER_GROUP):
                ce = jnp.sum(jnp.where((ln == e) | (ln == e + EXPERTS_PER_GROUP), cblk, 0.0),
                             axis=-1, keepdims=True)
                gt = jnp.dot(hblk, wg_ref[g * EXPERTS_PER_GROUP + e], preferred_element_type=F32)
                up = jnp.dot(hblk, wu_ref[g * EXPERTS_PER_GROUP + e], preferred_element_type=F32)
                a = gt / (1.0 + jnp.exp(-gt)) * up * ce
                parts.append(a.astype(BF16))
            a_all = jnp.concatenate(parts, axis=-1)
            ys[rows, :] = jnp.dot(a_all, wd, preferred_element_type=F32).astype(BF16)
            return carry

        lax.fori_loop(start_blk[g], start_blk[g] + n_blk[g], block, 0)

    def unsorted(r0, n):
        return lax.dot_general(sel_s[r0:r0 + n, :], ys[r0:r0 + n, :], (((0,), (0,)), ((), ())),
                               preferred_element_type=F32)

    def unsort_more(r0, n):
        out_ref[...] += unsorted(r0, n)

    out_ref[...] = (out_ref[...] if has_oproj else x_ref[...]) + unsorted(0, main)
    for r0, n, needed in extra:
        pl.when(needed)(functools.partial(unsort_more, r0, n))
    if has_final:
        out_ref[...] = _rms(out_ref[...], fn_ref[...])


def _moe(x, nf, wr, br, wg, wu, wd, layer, oproj=None, final=None, tm=MOE_ROWS, bm=MOE_BLOCK_ROWS):
    N, D = x.shape
    has_oproj = oproj is not None
    has_final = final is not None
    const2 = lambda i: (0, 0)
    const3 = lambda i: (0, 0, 0)
    cast = layer is not None
    tile = lambda i: (i, 0)
    once = pl.Buffered(1)
    args = [x]
    in_specs = [pl.BlockSpec((tm, D), tile)]
    if has_oproj:
        o, wo, bo = oproj
        args += [o, wo, bo]
        in_specs += [pl.BlockSpec((tm, D), tile), pl.BlockSpec((D, D), const2, pipeline_mode=once),
                     pl.BlockSpec((1, D), const2)]
    wr_hi = wr.astype(BF16)
    wr_lo = (wr - wr_hi.astype(F32)).astype(BF16)
    args += [nf, jnp.concatenate([wr_hi, wr_lo], axis=0), br, wg, wu, wd]
    in_specs += [
        pl.BlockSpec((1, D), const2),
        pl.BlockSpec((2 * ROUTER_ROWS, D), const2, pipeline_mode=once),
        pl.BlockSpec((ROUTER_ROWS, 1), const2),
    ]
    gu_shape, d_shape = (N_EXPERTS, D, D_EXPERT), (N_EXPERTS, D_EXPERT, D)
    if cast:
        in_specs += [pl.BlockSpec(memory_space=pl.ANY)] * 3
    else:
        in_specs += [pl.BlockSpec(gu_shape, const3, pipeline_mode=once), pl.BlockSpec(gu_shape, const3, pipeline_mode=once),
                     pl.BlockSpec(d_shape, const3, pipeline_mode=once)]
    if has_final:
        args.append(final)
        in_specs.append(pl.BlockSpec((1, D), const2))
    cap = (tm // bm + N_EXPERT_GROUPS) * bm
    out_specs = [pl.BlockSpec((tm, D), tile)]
    out_shape = [jax.ShapeDtypeStruct((N, D), F32)]
    scratch = [
        pltpu.VMEM((cap, D), BF16),
        pltpu.VMEM((cap, LANES), F32),
        pltpu.VMEM((cap, D), BF16),
        pltpu.VMEM((cap, tm), BF16),
        pltpu.VMEM((tm, tm), BF16),
    ]
    if cast:
        out_specs += [pl.BlockSpec(memory_space=pl.ANY)] * 3
        out_shape += [jax.ShapeDtypeStruct(gu_shape, BF16)] * 2 + [jax.ShapeDtypeStruct(d_shape, BF16)]
        scratch += [pltpu.VMEM(gu_shape, BF16), pltpu.VMEM(gu_shape, BF16), pltpu.VMEM(d_shape, BF16),
                    pltpu.VMEM((CAST_SLOTS,) + gu_shape[1:], F32), pltpu.VMEM((CAST_SLOTS,) + d_shape[1:], F32),
                    pltpu.SemaphoreType.DMA((CAST_SLOTS,)), pltpu.SemaphoreType.DMA((3,))]
    res = pl.pallas_call(
        functools.partial(_moe_kernel, has_oproj=has_oproj, has_final=has_final, cast_layer=layer, tm=tm, bm=bm),
        grid=(N // tm,),
        in_specs=in_specs,
        out_specs=out_specs,
        out_shape=out_shape,
        scratch_shapes=scratch,
        compiler_params=pltpu.CompilerParams(
            dimension_semantics=("arbitrary",), vmem_limit_bytes=VMEM_LIMIT),
        name="moe",
    )(*args)
    return res if cast else res[0]


def _swap_halves(x, low):
    return jnp.where(low, pltpu.roll(x, LANES - 32, axis=1), pltpu.roll(x, 32, axis=1))


def _qkv_kernel(x_ref, g_ref, w_ref, b_ref, cos_ref, sin_ref, q_ref, kd_ref, vd_ref, k_ref, v_ref, *, keep):
    tm = x_ref.shape[0]
    h = _rms(x_ref[...], g_ref[...]).astype(BF16)
    cos = cos_ref[...]
    sin = sin_ref[...]
    lane = lax.broadcasted_iota(jnp.int32, cos.shape, 1)
    low32 = (lane % HEAD_DIM) < (HEAD_DIM // 2)
    low64 = lane < HEAD_DIM
    nq = N_HEADS * HEAD_DIM
    nk = N_KV_HEADS * HEAD_DIM
    cw = 2 * LANES

    def project(c0):
        y = jnp.dot(h, w_ref[:, c0:c0 + cw], preferred_element_type=F32) + b_ref[:, c0:c0 + cw]
        return [y[:, i * LANES:(i + 1) * LANES] for i in range(cw // LANES)]

    def rope(c):
        return c * cos + _swap_halves(c, low32) * sin

    def dup(c):
        r = pltpu.roll(c, HEAD_DIM, axis=1)
        return jnp.where(low64, c, r), jnp.where(low64, r, c)

    for c0 in range(0, nq, cw):
        for i, c in enumerate(project(c0)):
            lanes = slice(c0 + i * LANES, c0 + (i + 1) * LANES)
            q_ref[:, lanes] = (rope(c) * Q_SCALE).astype(BF16)
    for c0 in range(0, nk, cw):
        for (src, rot, f_ref, d_ref) in ((nq, True, k_ref, kd_ref), (nq + nk, False, v_ref, vd_ref)):
            for i, c in enumerate(project(src + c0)):
                j = c0 // LANES + i
                c = rope(c) if rot else c
                f_ref[:, j * LANES:(j + 1) * LANES] = c[tm - keep:]
                d0, d1 = dup(c)
                d_ref[:, 2 * j * LANES:(2 * j + 1) * LANES] = d0.astype(BF16)
                d_ref[:, (2 * j + 1) * LANES:(2 * j + 2) * LANES] = d1.astype(BF16)


def _qkv(x, g, w, b, cos, sin, tm, n_pos_tiles, keep):
    N, D = x.shape
    nq, nk = N_HEADS * HEAD_DIM, N_KV_HEADS * HEAD_DIM
    const = lambda i: (0, 0)
    tile = lambda i: (i, 0)
    ptile = lambda i: (i % n_pos_tiles, 0)
    seq = lambda i: (i // n_pos_tiles, 0)
    n_seq = N // (tm * n_pos_tiles)
    return pl.pallas_call(
        functools.partial(_qkv_kernel, keep=keep),
        grid=(N // tm,),
        in_specs=[
            pl.BlockSpec((tm, D), tile),
            pl.BlockSpec((1, D), const),
            pl.BlockSpec((D, nq + 2 * nk), const),
            pl.BlockSpec((1, nq + 2 * nk), const),
            pl.BlockSpec((tm, LANES), ptile),
            pl.BlockSpec((tm, LANES), ptile),
        ],
        out_specs=[
            pl.BlockSpec((tm, nq), tile),
            pl.BlockSpec((tm, 2 * nk), tile),
            pl.BlockSpec((tm, 2 * nk), tile),
            pl.BlockSpec((keep, nk), seq),
            pl.BlockSpec((keep, nk), seq),
        ],
        out_shape=[
            jax.ShapeDtypeStruct((N, nq), BF16),
            jax.ShapeDtypeStruct((N, 2 * nk), BF16),
            jax.ShapeDtypeStruct((N, 2 * nk), BF16),
            jax.ShapeDtypeStruct((n_seq * keep, nk), F32),
            jax.ShapeDtypeStruct((n_seq * keep, nk), F32),
        ],
        compiler_params=pltpu.CompilerParams(
            dimension_semantics=("arbitrary",), vmem_limit_bytes=VMEM_LIMIT),
        name="qkv",
    )(x, g, w, b, cos, sin)


def _stack_heads(qc_list):
    lane = lax.broadcasted_iota(jnp.int32, qc_list[0].shape, 1)
    low = lane < HEAD_DIM
    zero = jnp.zeros_like(qc_list[0])
    stack = []
    for qc in qc_list:
        stack.append(jnp.where(low, qc, zero))
        stack.append(jnp.where(low, zero, qc))
    return jnp.concatenate(stack, axis=0)


def _softmax_keys_on_rows(s, sinkrow):
    m = jnp.maximum(jnp.max(s, axis=0, keepdims=True), sinkrow)
    p = jnp.exp2(s - m)
    den = jnp.sum(p, axis=0, keepdims=True) + jnp.exp2(sinkrow - m)
    return (p * (1.0 / den)).astype(BF16)


def _unstack_heads(o, R):
    lowf = lax.broadcasted_iota(jnp.int32, (R, LANES), 1) < HEAD_DIM
    return (jnp.where(lowf, o[0:R], o[R:2 * R]), jnp.where(lowf, o[2 * R:3 * R], o[3 * R:4 * R]))


NT_DIMS = (((1,), (1,)), ((), ()))
TN_DIMS = (((0,), (0,)), ((), ()))
TT_DIMS = (((0,), (1,)), ((), ()))


def _attend(qc_list, kd, vd, bias, sinkrow):
    qs = _stack_heads(qc_list)
    s = lax.dot_general(kd, qs, NT_DIMS, preferred_element_type=F32) + bias
    p = _softmax_keys_on_rows(s, sinkrow)
    o = lax.dot_general(p, vd, TN_DIMS, preferred_element_type=F32)
    return _unstack_heads(o, qc_list[0].shape[0])


def _attn_prompt_kernel(q_ref, kp_ref, kc_ref, vp_ref, vc_ref, sink_ref, wg_in, wu_in, wd_in,
                        o_ref, wg_out, wu_out, wd_out, f32_g, f32_u, f32_d, b16_g, b16_u, b16_d, load_sem, store_sem,
                        *, tq, cast_layer):
    n = pl.program_id(1)
    step = pl.program_id(0) * pl.num_programs(1) + n
    kinds = ((wg_in, f32_g, b16_g, wg_out), (wu_in, f32_u, b16_u, wu_out), (wd_in, f32_d, b16_d, wd_out))

    def load(k, e):
        return pltpu.make_async_copy(kinds[k][0].at[cast_layer, e], kinds[k][1], load_sem.at[k])

    def store(k, e):
        return pltpu.make_async_copy(kinds[k][2], kinds[k][3].at[e], store_sem.at[k])

    @pl.when(step == 0)
    def _():
        for k in range(3):
            load(k, 0).start()

    for k in range(3):
        load(k, step).wait()

    @pl.when(step > 0)
    def _():
        for k in range(3):
            store(k, step - 1).wait()

    for k in range(3):
        kinds[k][2][...] = kinds[k][1][...].astype(BF16)
        store(k, step).start()

    @pl.when(step < N_EXPERTS - 1)
    def _():
        for k in range(3):
            load(k, step + 1).start()

    @pl.when(step == N_EXPERTS - 1)
    def _():
        for k in range(3):
            store(k, step).wait()

    B = WINDOW
    kd = jnp.concatenate([kp_ref[...], kc_ref[...]], axis=0)
    vd = jnp.concatenate([vp_ref[...], vc_ref[...]], axis=0)
    kj = lax.broadcasted_iota(jnp.int32, (2 * B, B), 0)
    qi = lax.broadcasted_iota(jnp.int32, (2 * B, B), 1)
    rel = B + qi - kj
    band = (rel >= 0) & (rel < WINDOW)
    neg = jnp.float32(-jnp.inf)
    band_bias = jnp.concatenate([jnp.where(band, 0.0, neg)] * 4, axis=1)
    first_bias = jnp.concatenate([jnp.where(band & ((kj >= B) | (n > 0)), 0.0, neg)] * 4, axis=1)
    for j in range(tq // B):
        bias = first_bias if j == 0 else band_bias
        for g in range(N_KV_HEADS):
            qc = [q_ref[j * B:(j + 1) * B, (2 * g + i) * LANES:(2 * g + i + 1) * LANES] for i in range(2)]
            kg = kd[j * B:(j + 2) * B, g * LANES:(g + 1) * LANES]
            vg = vd[j * B:(j + 2) * B, g * LANES:(g + 1) * LANES]
            o0, o1 = _attend(qc, kg, vg, bias, sink_ref[g])
            o_ref[j * B:(j + 1) * B, 2 * g * LANES:(2 * g + 1) * LANES] = o0.astype(BF16)
            o_ref[j * B:(j + 1) * B, (2 * g + 1) * LANES:(2 * g + 2) * LANES] = o1.astype(BF16)


def _attn_prompt(q, kd, vd, sinkrow, wg, wu, wd, layer, batch, seq, tq=ATTN_PROMPT_ROWS):
    N = q.shape[0]
    nt = seq // tq
    assert batch * nt == N_EXPERTS
    gu_shape, d_shape = (N_EXPERTS, D_MODEL, D_EXPERT), (N_EXPERTS, D_EXPERT, D_MODEL)
    r = tq // WINDOW
    cur = lambda b, n: (b * nt + n, 0)
    prev = lambda b, n: (jnp.maximum((b * nt + n) * r - 1, 0), 0)
    kvw = 2 * N_KV_HEADS * HEAD_DIM
    return pl.pallas_call(
        functools.partial(_attn_prompt_kernel, tq=tq, cast_layer=layer),
        grid=(batch, nt),
        in_specs=[
            pl.BlockSpec((tq, N_HEADS * HEAD_DIM), cur),
            pl.BlockSpec((WINDOW, kvw), prev),
            pl.BlockSpec((tq, kvw), cur),
            pl.BlockSpec((WINDOW, kvw), prev),
            pl.BlockSpec((tq, kvw), cur),
            pl.BlockSpec((N_KV_HEADS, 1, 4 * WINDOW), lambda b, n: (0, 0, 0)),
        ] + [pl.BlockSpec(memory_space=pl.ANY)] * 3,
        out_specs=[pl.BlockSpec((tq, N_HEADS * HEAD_DIM), cur)] + [pl.BlockSpec(memory_space=pl.ANY)] * 3,
        out_shape=[jax.ShapeDtypeStruct((N, N_HEADS * HEAD_DIM), BF16), jax.ShapeDtypeStruct(gu_shape, BF16),
                   jax.ShapeDtypeStruct(gu_shape, BF16), jax.ShapeDtypeStruct(d_shape, BF16)],
        scratch_shapes=[pltpu.VMEM(gu_shape[1:], F32), pltpu.VMEM(gu_shape[1:], F32), pltpu.VMEM(d_shape[1:], F32),
                        pltpu.VMEM(gu_shape[1:], BF16), pltpu.VMEM(gu_shape[1:], BF16), pltpu.VMEM(d_shape[1:], BF16),
                        pltpu.SemaphoreType.DMA((3,)), pltpu.SemaphoreType.DMA((3,))],
        compiler_params=pltpu.CompilerParams(
            dimension_semantics=("arbitrary", "arbitrary"), vmem_limit_bytes=VMEM_LIMIT),
        name="attn_prompt",
    )(q, kd, kd, vd, vd, sinkrow, wg, wu, wd)


def _attn_sample_kernel(q_ref, kdn_ref, vdn_ref, kn_ref, vn_ref, ck_ref, cv_ref, mask_ref, kval_ref, sink_ref,
                        o_ref, nk_ref, nv_ref, *, bs, t):
    W = WINDOW
    rows = bs * t
    bias = jnp.where((mask_ref[...] > 0) & (kval_ref[0] > 0), 0.0, -jnp.inf).astype(F32)
    lane = lax.broadcasted_iota(jnp.int32, (HEAD_DIM, W), 1)
    fresh = lane >= W - t

    def new_cols(n_ref):
        n = jnp.concatenate([n_ref[...], jnp.zeros((LANES - rows, N_KV_HEADS * HEAD_DIM), F32)], axis=0)
        nt_ = jnp.transpose(n)
        return [pltpu.roll(nt_, (W - t - s * t) % LANES, axis=1) for s in range(bs)]

    k_cols, v_cols = new_cols(kn_ref), new_cols(vn_ref)
    for g in range(N_KV_HEADS):
        hd = slice(g * HEAD_DIM, (g + 1) * HEAD_DIM)
        kt = jnp.concatenate([ck_ref[s, g] for s in range(bs)], axis=1).astype(BF16)
        vt = jnp.concatenate([cv_ref[s, g] for s in range(bs)], axis=1).astype(BF16)
        kt2 = jnp.concatenate([kt, kt], axis=0)
        vt2 = jnp.concatenate([vt, vt], axis=0)
        qs = _stack_heads([q_ref[:, (2 * g + i) * LANES:(2 * g + i + 1) * LANES] for i in range(2)])
        s_old = lax.dot_general(kt2, qs, TT_DIMS, preferred_element_type=F32)
        s_new = lax.dot_general(kdn_ref[:, g * LANES:(g + 1) * LANES], qs, NT_DIMS, preferred_element_type=F32)
        p = _softmax_keys_on_rows(jnp.concatenate([s_old, s_new], axis=0) + bias, sink_ref[g])
        o = (lax.dot_general(p[:bs * W], vt2, TT_DIMS, preferred_element_type=F32)
             + lax.dot_general(p[bs * W:], vdn_ref[:, g * LANES:(g + 1) * LANES], TN_DIMS,
                               preferred_element_type=F32))
        o0, o1 = _unstack_heads(o, rows)
        o_ref[:, 2 * g * LANES:(2 * g + 1) * LANES] = o0.astype(BF16)
        o_ref[:, (2 * g + 1) * LANES:(2 * g + 2) * LANES] = o1.astype(BF16)
        for s in range(bs):
            nk_ref[s, g] = jnp.where(fresh, k_cols[s][hd], pltpu.roll(ck_ref[s, g], W - t, axis=1))
            nv_ref[s, g] = jnp.where(fresh, v_cols[s][hd], pltpu.roll(cv_ref[s, g], W - t, axis=1))


def _attn_sample(q, kdn, vdn, kn, vn, ck, cv, mask, kval, sinkrow, bs, t):
    Bd = ck.shape[0]
    rows = bs * t
    kvw = N_KV_HEADS * HEAD_DIM
    S = bs * WINDOW + rows
    tile = lambda i: (i, 0)
    tile3 = lambda i: (i, 0, 0)
    tile4 = lambda i: (i, 0, 0, 0)
    cblk = (bs, N_KV_HEADS, HEAD_DIM, WINDOW)
    return pl.pallas_call(
        functools.partial(_attn_sample_kernel, bs=bs, t=t),
        grid=(Bd // bs,),
        in_specs=[
            pl.BlockSpec((rows, N_HEADS * HEAD_DIM), tile),
            pl.BlockSpec((rows, 2 * kvw), tile),
            pl.BlockSpec((rows, 2 * kvw), tile),
            pl.BlockSpec((rows, kvw), tile),
            pl.BlockSpec((rows, kvw), tile),
            pl.BlockSpec(cblk, tile4),
            pl.BlockSpec(cblk, tile4),
            pl.BlockSpec((S, 4 * rows), lambda i: (0, 0)),
            pl.BlockSpec((1, S, 1), tile3),
            pl.BlockSpec((N_KV_HEADS, 1, 4 * rows), lambda i: (0, 0, 0)),
        ],
        out_specs=[
            pl.BlockSpec((rows, N_HEADS * HEAD_DIM), tile),
            pl.BlockSpec(cblk, tile4),
            pl.BlockSpec(cblk, tile4),
        ],
        out_shape=[
            jax.ShapeDtypeStruct((Bd * t, N_HEADS * HEAD_DIM), BF16),
            jax.ShapeDtypeStruct((Bd,) + cblk[1:], F32),
            jax.ShapeDtypeStruct((Bd,) + cblk[1:], F32),
        ],
        compiler_params=pltpu.CompilerParams(
            dimension_semantics=("arbitrary",), vmem_limit_bytes=VMEM_LIMIT),
        name="attn_sample",
    )(q, kdn, vdn, kn, vn, ck, cv, mask, kval, sinkrow)


def _rope_tables(pos):
    inv = ROPE_THETA ** (-jnp.arange(0, HEAD_DIM, 2, dtype=F32) / HEAD_DIM)
    reps = LANES // (HEAD_DIM // 2)
    sign = jnp.asarray(np.where((np.arange(LANES) % HEAD_DIM) < HEAD_DIM // 2, -1.0, 1.0), F32)
    ang = pos.astype(F32)[:, None] * inv[None, :]
    cos, sin = lax.optimization_barrier((jnp.cos(ang), jnp.sin(ang)))
    return jnp.tile(cos, (1, reps)), jnp.tile(sin, (1, reps)) * sign[None, :]


def _sink_rows(sinks, rows):
    s = (sinks.astype(F32) * LOG2E).reshape(N_KV_HEADS, 1, N_HEADS // N_KV_HEADS, 1)
    return jnp.broadcast_to(s, (N_KV_HEADS, 1, 4, rows)).reshape(N_KV_HEADS, 1, 4 * rows)


def kernel(x_prompt, x_sample, state_pool, cache_k, cache_v, sample_start, norm_mix, norm_ffn, norm_final,
           w_pool, ls_pool, w_qkv, b_qkv, sinks, w_o, b_o, w_rg, b_rg, w_re, b_re, w_gate, w_up, w_down):
    B, T, D = x_prompt.shape
    Bd, Td, _ = x_sample.shape
    row = lambda v: v.reshape(1, -1).astype(F32)

    wp = w_pool[0].astype(BF16)
    wqkv = w_qkv[0].astype(BF16)
    wo = w_o[0].astype(BF16)
    def router_rows(g_part, e_part):
        z = lambda n: jnp.zeros((n,) + g_part.shape[1:], F32)
        return jnp.concatenate([g_part, z(EXPERT_ROW0 - N_EXPERT_GROUPS), e_part,
                                z(ROUTER_ROWS - EXPERT_ROW0 - N_EXPERTS)], axis=0)

    wr = [router_rows(w_rg[l].T, w_re[l].T) for l in range(2)]
    br = [router_rows(b_rg[l][:, None], b_re[l][:, None]) for l in range(2)]

    start = sample_start.astype(jnp.int32)

    x1p, pool_p16 = _pool_prompt(x_prompt, jnp.zeros((B, HALO, D), F32), row(norm_mix[0]), wp, row(ls_pool[0]))
    pos_s = (start[:, None] + jnp.arange(Td, dtype=jnp.int32)[None, :]).reshape(-1)
    x1s, pool_s_t = _pool_sample(x_sample, jnp.transpose(state_pool[0], (1, 0, 2)), start[:, None],
                                 row(norm_mix[0]), wp, row(ls_pool[0]))
    x1s = x1s.reshape(Bd * Td, D)
    pool_s = jnp.transpose(pool_s_t, (1, 0, 2))
    pool_p = pool_p16[:, HALO - POOL_BUF:]

    moe0 = functools.partial(_moe, nf=row(norm_ffn[0]), wr=wr[0], br=br[0])
    x2p, *w0 = moe0(x1p.reshape(B * T, D), wg=w_gate, wu=w_up, wd=w_down, layer=0)
    x2s = moe0(x1s, wg=w0[0], wu=w0[1], wd=w0[2], layer=None)

    cos_p, sin_p = _rope_tables(jnp.arange(T, dtype=jnp.int32))
    cos_s, sin_s = _rope_tables(pos_s)
    g1 = row(norm_mix[1])
    tmq = QKV_ROWS
    keep = min(WINDOW, T)
    qp, kdp, vdp, kp, vp = _qkv(x2p, g1, wqkv, row(b_qkv[0]), cos_p, sin_p, tmq, T // tmq, keep)
    qs, kds, vds, ks, vs = _qkv(x2s, g1, wqkv, row(b_qkv[0]), cos_s, sin_s, tmq, 1, tmq)

    op, *w1 = _attn_prompt(qp, kdp, vdp, _sink_rows(sinks[0], WINDOW), w_gate, w_up, w_down, 1, B, T)

    bs = ATTN_SAMPLE_SEQS
    rows = bs * Td
    W = cache_k.shape[2]
    qrow = np.arange(rows)
    ccol = np.arange(bs * W)
    ncol = np.arange(rows)
    samp_q, t_q = qrow // Td, qrow % Td
    m_cache = (samp_q[:, None] == (ccol // W)[None, :]) & ((ccol % W)[None, :] > t_q[:, None])
    m_new = (samp_q[:, None] == (ncol // Td)[None, :]) & ((ncol % Td)[None, :] <= t_q[:, None])
    amask = jnp.asarray(np.tile(np.concatenate([m_cache, m_new], axis=1).T, (1, 4)), F32)
    kv_cache = (jnp.arange(W, dtype=jnp.int32)[None, :] >= (W - start)[:, None]).reshape(Bd // bs, bs * W)
    kval = jnp.concatenate([kv_cache, jnp.ones((Bd // bs, rows), bool)], axis=1).astype(F32)
    kval = kval.reshape(Bd // bs, bs * W + rows, 1)
    to_stored = lambda c: jnp.transpose(c[0], (0, 2, 3, 1))
    from_stored = lambda c: jnp.transpose(c, (0, 3, 1, 2))[None]
    osamp, nk_s, nv_s = _attn_sample(
        qs, kds, vds, ks, vs, to_stored(cache_k), to_stored(cache_v),
        amask, kval, _sink_rows(sinks[0], rows), bs, Td)

    moe1 = functools.partial(_moe, nf=row(norm_ffn[1]), wr=wr[1], br=br[1], final=row(norm_final))
    yp = moe1(x2p, wg=w1[0], wu=w1[1], wd=w1[2], layer=None, oproj=(op, wo, row(b_o[0])))
    ys = moe1(x2s, wg=w1[0], wu=w1[1], wd=w1[2], layer=None, oproj=(osamp, wo, row(b_o[0])))

    k_p = kp.reshape(1, B, keep, N_KV_HEADS, HEAD_DIM)
    v_p = vp.reshape(1, B, keep, N_KV_HEADS, HEAD_DIM)
    return (yp.reshape(B, T, D), ys.reshape(Bd, Td, D), pool_p[None], k_p, v_p, pool_s[None],
            from_stored(nk_s), from_stored(nv_s))
```

```python
import functools

import jax
import jax.numpy as jnp
import numpy as np
from jax import lax
from jax.experimental import pallas as pl
from jax.experimental.pallas import tpu as pltpu

F32 = jnp.float32
BF16 = jnp.bfloat16

D_MODEL = 1024
POOL_WINDOWS = (2, 4, 8, 16)
POOL_GROUP_DIM = 256
POOL_BUF = 15
HALO = 16
HEAD_DIM = 64
N_HEADS = 16
N_KV_HEADS = 4
WINDOW = 128
ROPE_THETA = 10000.0
N_EXPERT_GROUPS = 4
EXPERTS_PER_GROUP = 4
N_EXPERTS = 16
D_EXPERT = 256
RMS_EPS = 1e-6
LANES = 128
ROUTER_ROWS = 32
EXPERT_ROW0 = 8
CAST_SLOTS = 6

POOL_PROMPT_ROWS = 1024
POOL_SAMPLE_SEQS = 32
MOE_ROWS = 512
MOE_BLOCK_ROWS = 144
QKV_ROWS = 1024
ATTN_PROMPT_ROWS = 1024
ATTN_SAMPLE_SEQS = 8
VMEM_LIMIT = 56 * 1024 * 1024
LOG2E = 1.4426950408889634
Q_SCALE = LOG2E * HEAD_DIM ** -0.5


def _rms(x, g):
    ms = jnp.mean(x * x, axis=-1, keepdims=True)
    return x * lax.rsqrt(ms + RMS_EPS) * g


def _pool_project(h, wins, pos, wp_ref, ls):
    G = POOL_GROUP_DIM
    outs = []
    for g, w in enumerate(POOL_WINDOWS):
        cnt = jnp.minimum(w, pos + 1).astype(F32)
        d = wins[g] / cnt - h[:, g * G:(g + 1) * G]
        outs.append(jnp.dot(d.astype(BF16), wp_ref[g], preferred_element_type=F32))
    return jnp.concatenate(outs, axis=-1) * ls


POOL_BLOCK = 128


def _pool_band():
    t = np.arange(POOL_BLOCK)[:, None] + HALO
    k = np.arange(POOL_BLOCK + HALO)[None, :]
    return jnp.asarray(np.stack([(k <= t) & (k > t - w) for w in POOL_WINDOWS]), BF16)


def _pool_prompt_kernel(x_ref, buf_ref, g_ref, wp_ref, ls_ref, band_ref, o_ref, nb_ref, hc, *, tq):
    t = pl.program_id(1)
    G = POOL_GROUP_DIM

    @pl.when(t == 0)
    def _():
        hc[pl.ds(0, HALO), :] = buf_ref[0]

    @pl.when(t > 0)
    def _():
        hc[pl.ds(0, HALO), :] = hc[pl.ds(tq, HALO), :]

    x = x_ref[0]
    h = _rms(x, g_ref[...])
    hc[pl.ds(HALO, tq), :] = h
    hb = hc[...].astype(BF16)
    wins = []
    for g in range(len(POOL_WINDOWS)):
        blocks = [jnp.dot(band_ref[g], hb[b * POOL_BLOCK:(b + 1) * POOL_BLOCK + HALO, g * G:(g + 1) * G],
                          preferred_element_type=F32) for b in range(tq // POOL_BLOCK)]
        wins.append(jnp.concatenate(blocks, axis=0))
    pos = t * tq + lax.broadcasted_iota(jnp.int32, (tq, 1), 0)
    o_ref[0] = x + _pool_project(h, wins, pos, wp_ref, ls_ref[...])

    @pl.when(t == pl.num_programs(1) - 1)
    def _():
        nb_ref[0] = hc[pl.ds(tq, HALO), :]


def _pool_prompt(x, buf16, g, wp, ls, tq=POOL_PROMPT_ROWS):
    B, T, D = x.shape
    return pl.pallas_call(
        functools.partial(_pool_prompt_kernel, tq=tq),
        grid=(B, T // tq),
        in_specs=[
            pl.BlockSpec((1, tq, D), lambda b, t: (b, t, 0)),
            pl.BlockSpec((1, HALO, D), lambda b, t: (b, 0, 0)),
            pl.BlockSpec((1, D), lambda b, t: (0, 0)),
            pl.BlockSpec((4, POOL_GROUP_DIM, POOL_GROUP_DIM), lambda b, t: (0, 0, 0)),
            pl.BlockSpec((1, D), lambda b, t: (0, 0)),
            pl.BlockSpec((4, POOL_BLOCK, POOL_BLOCK + HALO), lambda b, t: (0, 0, 0)),
        ],
        out_specs=[
            pl.BlockSpec((1, tq, D), lambda b, t: (b, t, 0)),
            pl.BlockSpec((1, HALO, D), lambda b, t: (b, 0, 0)),
        ],
        out_shape=[jax.ShapeDtypeStruct((B, T, D), F32), jax.ShapeDtypeStruct((B, HALO, D), F32)],
        scratch_shapes=[pltpu.VMEM((HALO + tq, D), F32)],
        compiler_params=pltpu.CompilerParams(
            dimension_semantics=("arbitrary", "arbitrary"), vmem_limit_bytes=VMEM_LIMIT),
        name="pool_prompt",
    )(x, buf16, g, wp, ls, _pool_band())


def _pool_sample_kernel(x_ref, st_ref, start_ref, g_ref, wp_ref, ls_ref, o_ref, ns_ref, *, bs, t):
    G = POOL_GROUP_DIM
    xs = [x_ref[:, i, :] for i in range(t)]
    hn = [_rms(x, g_ref[...]) for x in xs]
    hist = [st_ref[r] for r in range(POOL_BUF)] + hn

    def doubled(prev, lag, lo):
        out = [None] * len(prev)
        for i in range(len(prev)):
            if i >= lag and prev[i] is not None and prev[i - lag] is not None:
                out[i] = prev[i][:, lo:] + prev[i - lag][:, lo:]
        return out

    s2 = doubled(hist, 1, 0)
    s4 = doubled(s2, 2, G)
    s8 = doubled(s4, 4, G)
    s16 = doubled(s8, 8, G)
    wins = (s2, s4, s8, s16)
    start = start_ref[...]
    ds = [[] for _ in POOL_WINDOWS]
    for i in range(t):
        for g, w in enumerate(POOL_WINDOWS):
            cnt = jnp.minimum(w, start + (i + 1)).astype(F32)
            ds[g].append(wins[g][POOL_BUF + i][:, :G] / cnt - hn[i][:, g * G:(g + 1) * G])
    ys = [jnp.dot(jnp.concatenate(ds[g], axis=0).astype(BF16), wp_ref[g], preferred_element_type=F32)
          for g in range(len(POOL_WINDOWS))]
    y = jnp.concatenate(ys, axis=-1) * ls_ref[...]
    for i in range(t):
        o_ref[:, i, :] = xs[i] + y[i * bs:(i + 1) * bs]
    for r in range(POOL_BUF):
        ns_ref[r] = hist[r + t]


def _pool_sample(x, st, start, g, wp, ls, bs=POOL_SAMPLE_SEQS):
    Bd, t, D = x.shape
    return pl.pallas_call(
        functools.partial(_pool_sample_kernel, bs=bs, t=t),
        grid=(Bd // bs,),
        in_specs=[
            pl.BlockSpec((bs, t, D), lambda i: (i, 0, 0)),
            pl.BlockSpec((POOL_BUF, bs, D), lambda i: (0, i, 0)),
            pl.BlockSpec((bs, 1), lambda i: (i, 0)),
            pl.BlockSpec((1, D), lambda i: (0, 0)),
            pl.BlockSpec((4, POOL_GROUP_DIM, POOL_GROUP_DIM), lambda i: (0, 0, 0)),
            pl.BlockSpec((1, D), lambda i: (0, 0)),
        ],
        out_specs=[pl.BlockSpec((bs, t, D), lambda i: (i, 0, 0)),
                   pl.BlockSpec((POOL_BUF, bs, D), lambda i: (0, i, 0))],
        out_shape=[jax.ShapeDtypeStruct((Bd, t, D), F32), jax.ShapeDtypeStruct((POOL_BUF, Bd, D), F32)],
        compiler_params=pltpu.CompilerParams(
            dimension_semantics=("arbitrary",), vmem_limit_bytes=VMEM_LIMIT),
        name="pool_sample",
    )(x, st, start, g, wp, ls)


def _route(lg):
    R = lg.shape[1]
    big = jnp.float32(1 << 20)
    neg = jnp.float32(-jnp.inf)
    r8 = lax.broadcasted_iota(jnp.int32, (8, R), 0).astype(F32)
    r16 = lax.broadcasted_iota(jnp.int32, (N_EXPERTS, R), 0).astype(F32)
    is_g = r8 < N_EXPERT_GROUPS
    gl = jnp.where(is_g, lg[0:8], neg)
    m = jnp.max(gl, axis=0, keepdims=True)
    gidx = jnp.min(jnp.where(gl == m, r8, big), axis=0, keepdims=True)
    z = jnp.sum(jnp.where(is_g, jnp.exp(gl - m), 0.0), axis=0, keepdims=True)
    gw = 1.0 / z
    lo = gidx * EXPERTS_PER_GROUP
    in_grp = (r16 >= lo) & (r16 < lo + EXPERTS_PER_GROUP)
    el = jnp.where(in_grp, lg[EXPERT_ROW0:EXPERT_ROW0 + N_EXPERTS], neg)
    v1 = jnp.max(el, axis=0, keepdims=True)
    i1 = jnp.min(jnp.where(el == v1, r16, big), axis=0, keepdims=True)
    el2 = jnp.where(r16 == i1, neg, el)
    v2 = jnp.max(el2, axis=0, keepdims=True)
    i2 = jnp.min(jnp.where(el2 == v2, r16, big), axis=0, keepdims=True)
    t = jnp.exp(v2 - v1)
    w1 = 1.0 / (1.0 + t)
    w2 = t * w1
    e8 = jnp.where(is_g, r8, r8 - EXPERTS_PER_GROUP)
    cw = gw * (jnp.where(e8 == i1 - lo, w1, 0.0) + jnp.where(e8 == i2 - lo, w2, 0.0))
    return gidx, cw


def _moe_kernel(*refs, has_oproj, has_final, cast_layer, tm, bm):
    it = iter(refs)
    x_ref = next(it)
    if has_oproj:
        oin_ref, wo_ref, bo_ref = next(it), next(it), next(it)
    nf_ref, wr_ref, br_ref, wg_in, wu_in, wd_in = (next(it) for _ in range(6))
    if has_final:
        fn_ref = next(it)
    out_ref = next(it)
    if cast_layer is not None:
        wg_out, wu_out, wd_out = (next(it) for _ in range(3))
    hs, cs, ys, sel_s, earlier = (next(it) for _ in range(5))
    if cast_layer is None:
        wg_ref, wu_ref, wd_ref = wg_in, wu_in, wd_in
    else:
        wg_ref, wu_ref, wd_ref, stage_gu, stage_d, load_sem, store_sem = (next(it) for _ in range(7))
        pairs = ((wg_in, wg_ref, wg_out, stage_gu), (wu_in, wu_ref, wu_out, stage_gu), (wd_in, wd_ref, wd_out, stage_d))

        def store_copy(k):
            return pltpu.make_async_copy(pairs[k][1], pairs[k][2], store_sem.at[k])

        @pl.when(pl.program_id(0) == 0)
        def _():
            ahead = stage_gu.shape[0] - 1
            for src, dst, _, stage in pairs:
                def load(e, src=src, stage=stage):
                    slot = e % (ahead + 1)
                    return pltpu.make_async_copy(src.at[cast_layer, e], stage.at[slot], load_sem.at[slot])

                for e in range(ahead):
                    load(e).start()
                for e in range(N_EXPERTS):
                    if e + ahead < N_EXPERTS:
                        load(e + ahead).start()
                    load(e).wait()
                    dst[e] = stage[e % (ahead + 1)].astype(BF16)
            for k in range(len(pairs)):
                store_copy(k).start()

        @pl.when(pl.program_id(0) == pl.num_programs(0) - 1)
        def _():
            for k in range(len(pairs)):
                store_copy(k).wait()
    min_blk = -(-tm // bm)
    nblk = tm // bm + N_EXPERT_GROUPS
    cap = nblk * bm
    assert bm % 16 == 0
    nt = (((1,), (1,)), ((), ()))

    @pl.when(pl.program_id(0) == 0)
    def _():
        ri = lax.broadcasted_iota(jnp.int32, (tm, tm), 0)
        ci = lax.broadcasted_iota(jnp.int32, (tm, tm), 1)
        earlier[...] = jnp.where(ri < ci, 1.0, 0.0).astype(BF16)

    x = x_ref[...]
    if has_oproj:
        x = x + jnp.dot(oin_ref[...], wo_ref[...], preferred_element_type=F32) + bo_ref[...]
    if has_oproj:
        out_ref[...] = x
    h = _rms(x, nf_ref[...])
    hb = h.astype(BF16)
    h_lo = (h - hb.astype(F32)).astype(BF16)
    wr = wr_ref[...]
    both = lax.dot_general(wr, hb, nt, preferred_element_type=F32)
    lg = (both[0:ROUTER_ROWS] + both[ROUTER_ROWS:]
          + lax.dot_general(wr[0:ROUTER_ROWS], h_lo, nt, preferred_element_type=F32)) + br_ref[...]
    gidx, cw = _route(lg)
    r8 = lax.broadcasted_iota(jnp.int32, (8, tm), 0)
    cw_hi = cw.astype(BF16).astype(F32)
    cwt = jnp.concatenate([jnp.where(r8 < EXPERTS_PER_GROUP, cw_hi, cw - cw_hi),
                           jnp.zeros((LANES - 8, tm), F32)], axis=0).astype(BF16)

    oh = jnp.where(r8.astype(F32) == gidx, 1.0, 0.0)
    cnt_before = jnp.dot(oh.astype(BF16), earlier[...], preferred_element_type=F32)
    pos_row = jnp.sum(oh * cnt_before, axis=0, keepdims=True)
    start_blk, n_blk = [], []
    off = jnp.int32(0)
    for g in range(N_EXPERT_GROUPS):
        in_g = gidx == g
        n_g = jnp.sum(jnp.where(in_g, 1.0, 0.0)).astype(jnp.int32)
        blocks = sum(jnp.where(n_g > k * bm, 1, 0) for k in range(min_blk))
        start_blk.append(off)
        n_blk.append(blocks)
        pos_row = pos_row + jnp.where(in_g, (off * bm).astype(F32), 0.0)
        off = off + blocks

    total_blk = off
    main = min_blk * bm
    extra = ((main, bm, total_blk > min_blk), (main + bm, cap - main - bm, total_blk > min_blk + 1))

    def sort_rows(r0, n):
        sub = (r0 + lax.broadcasted_iota(jnp.int32, (n, tm), 0)).astype(F32)
        sel = jnp.where(pos_row == sub, 1.0, 0.0).astype(BF16)
        sel_s[r0:r0 + n, :] = sel
        hs[r0:r0 + n, :] = jnp.dot(sel, hb, preferred_element_type=F32).astype(BF16)
        cs[r0:r0 + n, :] = lax.dot_general(sel, cwt, nt, preferred_element_type=F32)

    sort_rows(0, main)
    for r0, n, needed in extra:
        pl.when(needed)(functools.partial(sort_rows, r0, n))
    ys[min_blk * bm:, :] = jnp.zeros((cap - min_blk * bm, D_MODEL), BF16)

    for g in range(N_EXPERT_GROUPS):
        wd = wd_ref[g * EXPERTS_PER_GROUP:(g + 1) * EXPERTS_PER_GROUP].reshape(
            EXPERTS_PER_GROUP * D_EXPERT, D_MODEL)

        def block(b, carry, g=g, wd=wd):
            rows = pl.ds(pl.multiple_of(b * bm, 16), bm)
            hblk = hs[rows, :]
            cblk = cs[rows, :]
            ln = lax.broadcasted_iota(jnp.int32, cblk.shape, 1)
            parts = []
            for e in range(EXPERTS_PER_GROUP):
                ce = jnp.sum(jnp.where((ln == e) | (ln == e + EXPERTS_PER_GROUP), cblk, 0.0),
                             axis=-1, keepdims=True)
                gt = jnp.dot(hblk, wg_ref[g * EXPERTS_PER_GROUP + e], preferred_element_type=F32)
                up = jnp.dot(hblk, wu_ref[g * EXPERTS_PER_GROUP + e], preferred_element_type=F32)
                a = gt / (1.0 + jnp.exp(-gt)) * up * ce
                parts.append(a.astype(BF16))
            a_all = jnp.concatenate(parts, axis=-1)
            ys[rows, :] = jnp.dot(a_all, wd, preferred_element_type=F32).astype(BF16)
            return carry

        lax.fori_loop(start_blk[g], start_blk[g] + n_blk[g], block, 0)

    def unsorted(r0, n):
        return lax.dot_general(sel_s[r0:r0 + n, :], ys[r0:r0 + n, :], (((0,), (0,)), ((), ())),
                               preferred_element_type=F32)

    def unsort_more(r0, n):
        out_ref[...] += unsorted(r0, n)

    out_ref[...] = (out_ref[...] if has_oproj else x_ref[...]) + unsorted(0, main)
    for r0, n, needed in extra:
        pl.when(needed)(functools.partial(unsort_more, r0, n))
    if has_final:
        out_ref[...] = _rms(out_ref[...], fn_ref[...])


def _moe(x, nf, wr, br, wg, wu, wd, layer, oproj=None, final=None, tm=MOE_ROWS, bm=MOE_BLOCK_ROWS):
    N, D = x.shape
    has_oproj = oproj is not None
    has_final = final is not None
    const2 = lambda i: (0, 0)
    const3 = lambda i: (0, 0, 0)
    cast = layer is not None
    tile = lambda i: (i, 0)
    once = pl.Buffered(1)
    args = [x]
    in_specs = [pl.BlockSpec((tm, D), tile)]
    if has_oproj:
        o, wo, bo = oproj
        args += [o, wo, bo]
        in_specs += [pl.BlockSpec((tm, D), tile), pl.BlockSpec((D, D), const2, pipeline_mode=once),
                     pl.BlockSpec((1, D), const2)]
    wr_hi = wr.astype(BF16)
    wr_lo = (wr - wr_hi.astype(F32)).astype(BF16)
    args += [nf, jnp.concatenate([wr_hi, wr_lo], axis=0), br, wg, wu, wd]
    in_specs += [
        pl.BlockSpec((1, D), const2),
        pl.BlockSpec((2 * ROUTER_ROWS, D), const2, pipeline_mode=once),
        pl.BlockSpec((ROUTER_ROWS, 1), const2),
    ]
    gu_shape, d_shape = (N_EXPERTS, D, D_EXPERT), (N_EXPERTS, D_EXPERT, D)
    if cast:
        in_specs += [pl.BlockSpec(memory_space=pl.ANY)] * 3
    else:
        in_specs += [pl.BlockSpec(gu_shape, const3, pipeline_mode=once), pl.BlockSpec(gu_shape, const3, pipeline_mode=once),
                     pl.BlockSpec(d_shape, const3, pipeline_mode=once)]
    if has_final:
        args.append(final)
        in_specs.append(pl.BlockSpec((1, D), const2))
    cap = (tm // bm + N_EXPERT_GROUPS) * bm
    out_specs = [pl.BlockSpec((tm, D), tile)]
    out_shape = [jax.ShapeDtypeStruct((N, D), F32)]
    scratch = [
        pltpu.VMEM((cap, D), BF16),
        pltpu.VMEM((cap, LANES), F32),
        pltpu.VMEM((cap, D), BF16),
        pltpu.VMEM((cap, tm), BF16),
        pltpu.VMEM((tm, tm), BF16),
    ]
    if cast:
        out_specs += [pl.BlockSpec(memory_space=pl.ANY)] * 3
        out_shape += [jax.ShapeDtypeStruct(gu_shape, BF16)] * 2 + [jax.ShapeDtypeStruct(d_shape, BF16)]
        scratch += [pltpu.VMEM(gu_shape, BF16), pltpu.VMEM(gu_shape, BF16), pltpu.VMEM(d_shape, BF16),
                    pltpu.VMEM((CAST_SLOTS,) + gu_shape[1:], F32), pltpu.VMEM((CAST_SLOTS,) + d_shape[1:], F32),
                    pltpu.SemaphoreType.DMA((CAST_SLOTS,)), pltpu.SemaphoreType.DMA((3,))]
    res = pl.pallas_call(
        functools.partial(_moe_kernel, has_oproj=has_oproj, has_final=has_final, cast_layer=layer, tm=tm, bm=bm),
        grid=(N // tm,),
        in_specs=in_specs,
        out_specs=out_specs,
        out_shape=out_shape,
        scratch_shapes=scratch,
        compiler_params=pltpu.CompilerParams(
            dimension_semantics=("arbitrary",), vmem_limit_bytes=VMEM_LIMIT),
        name="moe",
    )(*args)
    return res if cast else res[0]


def _swap_halves(x, low):
    return jnp.where(low, pltpu.roll(x, LANES - 32, axis=1), pltpu.roll(x, 32, axis=1))


def _qkv_kernel(x_ref, g_ref, w_ref, b_ref, cos_ref, sin_ref, q_ref, kd_ref, vd_ref, k_ref, v_ref, *, keep):
    tm = x_ref.shape[0]
    h = _rms(x_ref[...], g_ref[...]).astype(BF16)
    cos = cos_ref[...]
    sin = sin_ref[...]
    lane = lax.broadcasted_iota(jnp.int32, cos.shape, 1)
    low32 = (lane % HEAD_DIM) < (HEAD_DIM // 2)
    low64 = lane < HEAD_DIM
    nq = N_HEADS * HEAD_DIM
    nk = N_KV_HEADS * HEAD_DIM
    cw = 2 * LANES

    def project(c0):
        y = jnp.dot(h, w_ref[:, c0:c0 + cw], preferred_element_type=F32) + b_ref[:, c0:c0 + cw]
        return [y[:, i * LANES:(i + 1) * LANES] for i in range(cw // LANES)]

    def rope(c):
        return c * cos + _swap_halves(c, low32) * sin

    def dup(c):
        r = pltpu.roll(c, HEAD_DIM, axis=1)
        return jnp.where(low64, c, r), jnp.where(low64, r, c)

    for c0 in range(0, nq, cw):
        for i, c in enumerate(project(c0)):
            lanes = slice(c0 + i * LANES, c0 + (i + 1) * LANES)
            q_ref[:, lanes] = (rope(c) * Q_SCALE).astype(BF16)
    for c0 in range(0, nk, cw):
        for (src, rot, f_ref, d_ref) in ((nq, True, k_ref, kd_ref), (nq + nk, False, v_ref, vd_ref)):
            for i, c in enumerate(project(src + c0)):
                j = c0 // LANES + i
                c = rope(c) if rot else c
                f_ref[:, j * LANES:(j + 1) * LANES] = c[tm - keep:]
                d0, d1 = dup(c)
                d_ref[:, 2 * j * LANES:(2 * j + 1) * LANES] = d0.astype(BF16)
                d_ref[:, (2 * j + 1) * LANES:(2 * j + 2) * LANES] = d1.astype(BF16)


def _qkv(x, g, w, b, cos, sin, tm, n_pos_tiles, keep):
    N, D = x.shape
    nq, nk = N_HEADS * HEAD_DIM, N_KV_HEADS * HEAD_DIM
    const = lambda i: (0, 0)
    tile = lambda i: (i, 0)
    ptile = lambda i: (i % n_pos_tiles, 0)
    seq = lambda i: (i // n_pos_tiles, 0)
    n_seq = N // (tm * n_pos_tiles)
    return pl.pallas_call(
        functools.partial(_qkv_kernel, keep=keep),
        grid=(N // tm,),
        in_specs=[
            pl.BlockSpec((tm, D), tile),
            pl.BlockSpec((1, D), const),
            pl.BlockSpec((D, nq + 2 * nk), const),
            pl.BlockSpec((1, nq + 2 * nk), const),
            pl.BlockSpec((tm, LANES), ptile),
            pl.BlockSpec((tm, LANES), ptile),
        ],
        out_specs=[
            pl.BlockSpec((tm, nq), tile),
            pl.BlockSpec((tm, 2 * nk), tile),
            pl.BlockSpec((tm, 2 * nk), tile),
            pl.BlockSpec((keep, nk), seq),
            pl.BlockSpec((keep, nk), seq),
        ],
        out_shape=[
            jax.ShapeDtypeStruct((N, nq), BF16),
            jax.ShapeDtypeStruct((N, 2 * nk), BF16),
            jax.ShapeDtypeStruct((N, 2 * nk), BF16),
            jax.ShapeDtypeStruct((n_seq * keep, nk), F32),
            jax.ShapeDtypeStruct((n_seq * keep, nk), F32),
        ],
        compiler_params=pltpu.CompilerParams(
            dimension_semantics=("arbitrary",), vmem_limit_bytes=VMEM_LIMIT),
        name="qkv",
    )(x, g, w, b, cos, sin)


def _stack_heads(qc_list):
    lane = lax.broadcasted_iota(jnp.int32, qc_list[0].shape, 1)
    low = lane < HEAD_DIM
    zero = jnp.zeros_like(qc_list[0])
    stack = []
    for qc in qc_list:
        stack.append(jnp.where(low, qc, zero))
        stack.append(jnp.where(low, zero, qc))
    return jnp.concatenate(stack, axis=0)


def _softmax_keys_on_rows(s, sinkrow):
    m = jnp.maximum(jnp.max(s, axis=0, keepdims=True), sinkrow)
    p = jnp.exp2(s - m)
    den = jnp.sum(p, axis=0, keepdims=True) + jnp.exp2(sinkrow - m)
    return (p * (1.0 / den)).astype(BF16)


def _unstack_heads(o, R):
    lowf = lax.broadcasted_iota(jnp.int32, (R, LANES), 1) < HEAD_DIM
    return (jnp.where(lowf, o[0:R], o[R:2 * R]), jnp.where(lowf, o[2 * R:3 * R], o[3 * R:4 * R]))


NT_DIMS = (((1,), (1,)), ((), ()))
TN_DIMS = (((0,), (0,)), ((), ()))
TT_DIMS = (((0,), (1,)), ((), ()))


def _attend(qc_list, kd, vd, bias, sinkrow):
    qs = _stack_heads(qc_list)
    s = lax.dot_general(kd, qs, NT_DIMS, preferred_element_type=F32) + bias
    p = _softmax_keys_on_rows(s, sinkrow)
    o = lax.dot_general(p, vd, TN_DIMS, preferred_element_type=F32)
    return _unstack_heads(o, qc_list[0].shape[0])


def _attn_prompt_kernel(q_ref, kp_ref, kc_ref, vp_ref, vc_ref, sink_ref, wg_in, wu_in, wd_in,
                        o_ref, wg_out, wu_out, wd_out, f32_g, f32_u, f32_d, b16_g, b16_u, b16_d, load_sem, store_sem,
                        *, tq, cast_layer):
    n = pl.program_id(1)
    step = pl.program_id(0) * pl.num_programs(1) + n
    kinds = ((wg_in, f32_g, b16_g, wg_out), (wu_in, f32_u, b16_u, wu_out), (wd_in, f32_d, b16_d, wd_out))

    def load(k, e):
        return pltpu.make_async_copy(kinds[k][0].at[cast_layer, e], kinds[k][1], load_sem.at[k])

    def store(k, e):
        return pltpu.make_async_copy(kinds[k][2], kinds[k][3].at[e], store_sem.at[k])

    @pl.when(step == 0)
    def _():
        for k in range(3):
            load(k, 0).start()

    for k in range(3):
        load(k, step).wait()

    @pl.when(step > 0)
    def _():
        for k in range(3):
            store(k, step - 1).wait()

    for k in range(3):
        kinds[k][2][...] = kinds[k][1][...].astype(BF16)
        store(k, step).start()

    @pl.when(step < N_EXPERTS - 1)
    def _():
        for k in range(3):
            load(k, step + 1).start()

    @pl.when(step == N_EXPERTS - 1)
    def _():
        for k in range(3):
            store(k, step).wait()

    B = WINDOW
    kd = jnp.concatenate([kp_ref[...], kc_ref[...]], axis=0)
    vd = jnp.concatenate([vp_ref[...], vc_ref[...]], axis=0)
    kj = lax.broadcasted_iota(jnp.int32, (2 * B, B), 0)
    qi = lax.broadcasted_iota(jnp.int32, (2 * B, B), 1)
    rel = B + qi - kj
    band = (rel >= 0) & (rel < WINDOW)
    neg = jnp.float32(-jnp.inf)
    band_bias = jnp.concatenate([jnp.where(band, 0.0, neg)] * 4, axis=1)
    first_bias = jnp.concatenate([jnp.where(band & ((kj >= B) | (n > 0)), 0.0, neg)] * 4, axis=1)
    for j in range(tq // B):
        bias = first_bias if j == 0 else band_bias
        for g in range(N_KV_HEADS):
            qc = [q_ref[j * B:(j + 1) * B, (2 * g + i) * LANES:(2 * g + i + 1) * LANES] for i in range(2)]
            kg = kd[j * B:(j + 2) * B, g * LANES:(g + 1) * LANES]
            vg = vd[j * B:(j + 2) * B, g * LANES:(g + 1) * LANES]
            o0, o1 = _attend(qc, kg, vg, bias, sink_ref[g])
            o_ref[j * B:(j + 1) * B, 2 * g * LANES:(2 * g + 1) * LANES] = o0.astype(BF16)
            o_ref[j * B:(j + 1) * B, (2 * g + 1) * LANES:(2 * g + 2) * LANES] = o1.astype(BF16)


def _attn_prompt(q, kd, vd, sinkrow, wg, wu, wd, layer, batch, seq, tq=ATTN_PROMPT_ROWS):
    N = q.shape[0]
    nt = seq // tq
    assert batch * nt == N_EXPERTS
    gu_shape, d_shape = (N_EXPERTS, D_MODEL, D_EXPERT), (N_EXPERTS, D_EXPERT, D_MODEL)
    r = tq // WINDOW
    cur = lambda b, n: (b * nt + n, 0)
    prev = lambda b, n: (jnp.maximum((b * nt + n) * r - 1, 0), 0)
    kvw = 2 * N_KV_HEADS * HEAD_DIM
    return pl.pallas_call(
        functools.partial(_attn_prompt_kernel, tq=tq, cast_layer=layer),
        grid=(batch, nt),
        in_specs=[
            pl.BlockSpec((tq, N_HEADS * HEAD_DIM), cur),
            pl.BlockSpec((WINDOW, kvw), prev),
            pl.BlockSpec((tq, kvw), cur),
            pl.BlockSpec((WINDOW, kvw), prev),
            pl.BlockSpec((tq, kvw), cur),
            pl.BlockSpec((N_KV_HEADS, 1, 4 * WINDOW), lambda b, n: (0, 0, 0)),
        ] + [pl.BlockSpec(memory_space=pl.ANY)] * 3,
        out_specs=[pl.BlockSpec((tq, N_HEADS * HEAD_DIM), cur)] + [pl.BlockSpec(memory_space=pl.ANY)] * 3,
        out_shape=[jax.ShapeDtypeStruct((N, N_HEADS * HEAD_DIM), BF16), jax.ShapeDtypeStruct(gu_shape, BF16),
                   jax.ShapeDtypeStruct(gu_shape, BF16), jax.ShapeDtypeStruct(d_shape, BF16)],
        scratch_shapes=[pltpu.VMEM(gu_shape[1:], F32), pltpu.VMEM(gu_shape[1:], F32), pltpu.VMEM(d_shape[1:], F32),
                        pltpu.VMEM(gu_shape[1:], BF16), pltpu.VMEM(gu_shape[1:], BF16), pltpu.VMEM(d_shape[1:], BF16),
                        pltpu.SemaphoreType.DMA((3,)), pltpu.SemaphoreType.DMA((3,))],
        compiler_params=pltpu.CompilerParams(
            dimension_semantics=("arbitrary", "arbitrary"), vmem_limit_bytes=VMEM_LIMIT),
        name="attn_prompt",
    )(q, kd, kd, vd, vd, sinkrow, wg, wu, wd)


def _attn_sample_kernel(q_ref, kdn_ref, vdn_ref, kn_ref, vn_ref, ck_ref, cv_ref, mask_ref, kval_ref, sink_ref,
                        o_ref, nk_ref, nv_ref, *, bs, t):
    W = WINDOW
    rows = bs * t
    bias = jnp.where((mask_ref[...] > 0) & (kval_ref[0] > 0), 0.0, -jnp.inf).astype(F32)
    lane = lax.broadcasted_iota(jnp.int32, (HEAD_DIM, W), 1)
    fresh = lane >= W - t

    def new_cols(n_ref):
        n = jnp.concatenate([n_ref[...], jnp.zeros((LANES - rows, N_KV_HEADS * HEAD_DIM), F32)], axis=0)
        nt_ = jnp.transpose(n)
        return [pltpu.roll(nt_, (W - t - s * t) % LANES, axis=1) for s in range(bs)]

    k_cols, v_cols = new_cols(kn_ref), new_cols(vn_ref)
    for g in range(N_KV_HEADS):
        hd = slice(g * HEAD_DIM, (g + 1) * HEAD_DIM)
        kt = jnp.concatenate([ck_ref[s, g] for s in range(bs)], axis=1).astype(BF16)
        vt = jnp.concatenate([cv_ref[s, g] for s in range(bs)], axis=1).astype(BF16)
        kt2 = jnp.concatenate([kt, kt], axis=0)
        vt2 = jnp.concatenate([vt, vt], axis=0)
        qs = _stack_heads([q_ref[:, (2 * g + i) * LANES:(2 * g + i + 1) * LANES] for i in range(2)])
        s_old = lax.dot_general(kt2, qs, TT_DIMS, preferred_element_type=F32)
        s_new = lax.dot_general(kdn_ref[:, g * LANES:(g + 1) * LANES], qs, NT_DIMS, preferred_element_type=F32)
        p = _softmax_keys_on_rows(jnp.concatenate([s_old, s_new], axis=0) + bias, sink_ref[g])
        o = (lax.dot_general(p[:bs * W], vt2, TT_DIMS, preferred_element_type=F32)
             + lax.dot_general(p[bs * W:], vdn_ref[:, g * LANES:(g + 1) * LANES], TN_DIMS,
                               preferred_element_type=F32))
        o0, o1 = _unstack_heads(o, rows)
        o_ref[:, 2 * g * LANES:(2 * g + 1) * LANES] = o0.astype(BF16)
        o_ref[:, (2 * g + 1) * LANES:(2 * g + 2) * LANES] = o1.astype(BF16)
        for s in range(bs):
            nk_ref[s, g] = jnp.where(fresh, k_cols[s][hd], pltpu.roll(ck_ref[s, g], W - t, axis=1))
            nv_ref[s, g] = jnp.where(fresh, v_cols[s][hd], pltpu.roll(cv_ref[s, g], W - t, axis=1))


def _attn_sample(q, kdn, vdn, kn, vn, ck, cv, mask, kval, sinkrow, bs, t):
    Bd = ck.shape[0]
    rows = bs * t
    kvw = N_KV_HEADS * HEAD_DIM
    S = bs * WINDOW + rows
    tile = lambda i: (i, 0)
    tile3 = lambda i: (i, 0, 0)
    tile4 = lambda i: (i, 0, 0, 0)
    cblk = (bs, N_KV_HEADS, HEAD_DIM, WINDOW)
    return pl.pallas_call(
        functools.partial(_attn_sample_kernel, bs=bs, t=t),
        grid=(Bd // bs,),
        in_specs=[
            pl.BlockSpec((rows, N_HEADS * HEAD_DIM), tile),
            pl.BlockSpec((rows, 2 * kvw), tile),
            pl.BlockSpec((rows, 2 * kvw), tile),
            pl.BlockSpec((rows, kvw), tile),
            pl.BlockSpec((rows, kvw), tile),
            pl.BlockSpec(cblk, tile4),
            pl.BlockSpec(cblk, tile4),
            pl.BlockSpec((S, 4 * rows), lambda i: (0, 0)),
            pl.BlockSpec((1, S, 1), tile3),
            pl.BlockSpec((N_KV_HEADS, 1, 4 * rows), lambda i: (0, 0, 0)),
        ],
        out_specs=[
            pl.BlockSpec((rows, N_HEADS * HEAD_DIM), tile),
            pl.BlockSpec(cblk, tile4),
            pl.BlockSpec(cblk, tile4),
        ],
        out_shape=[
            jax.ShapeDtypeStruct((Bd * t, N_HEADS * HEAD_DIM), BF16),
            jax.ShapeDtypeStruct((Bd,) + cblk[1:], F32),
            jax.ShapeDtypeStruct((Bd,) + cblk[1:], F32),
        ],
        compiler_params=pltpu.CompilerParams(
            dimension_semantics=("arbitrary",), vmem_limit_bytes=VMEM_LIMIT),
        name="attn_sample",
    )(q, kdn, vdn, kn, vn, ck, cv, mask, kval, sinkrow)


def _rope_tables(pos):
    inv = ROPE_THETA ** (-jnp.arange(0, HEAD_DIM, 2, dtype=F32) / HEAD_DIM)
    reps = LANES // (HEAD_DIM // 2)
    sign = jnp.asarray(np.where((np.arange(LANES) % HEAD_DIM) < HEAD_DIM // 2, -1.0, 1.0), F32)
    ang = pos.astype(F32)[:, None] * inv[None, :]
    cos, sin = lax.optimization_barrier((jnp.cos(ang), jnp.sin(ang)))
    return jnp.tile(cos, (1, reps)), jnp.tile(sin, (1, reps)) * sign[None, :]


def _sink_rows(sinks, rows):
    s = (sinks.astype(F32) * LOG2E).reshape(N_KV_HEADS, 1, N_HEADS // N_KV_HEADS, 1)
    return jnp.broadcast_to(s, (N_KV_HEADS, 1, 4, rows)).reshape(N_KV_HEADS, 1, 4 * rows)


def kernel(x_prompt, x_sample, state_pool, cache_k, cache_v, sample_start, norm_mix, norm_ffn, norm_final,
           w_pool, ls_pool, w_qkv, b_qkv, sinks, w_o, b_o, w_rg, b_rg, w_re, b_re, w_gate, w_up, w_down):
    B, T, D = x_prompt.shape
    Bd, Td, _ = x_sample.shape
    row = lambda v: v.reshape(1, -1).astype(F32)

    wp = w_pool[0].astype(BF16)
    wqkv = w_qkv[0].astype(BF16)
    wo = w_o[0].astype(BF16)
    def router_rows(g_part, e_part):
        z = lambda n: jnp.zeros((n,) + g_part.shape[1:], F32)
        return jnp.concatenate([g_part, z(EXPERT_ROW0 - N_EXPERT_GROUPS), e_part,
                                z(ROUTER_ROWS - EXPERT_ROW0 - N_EXPERTS)], axis=0)

    wr = [router_rows(w_rg[l].T, w_re[l].T) for l in range(2)]
    br = [router_rows(b_rg[l][:, None], b_re[l][:, None]) for l in range(2)]

    start = sample_start.astype(jnp.int32)

    x1p, pool_p16 = _pool_prompt(x_prompt, jnp.zeros((B, HALO, D), F32), row(norm_mix[0]), wp, row(ls_pool[0]))
    pos_s = (start[:, None] + jnp.arange(Td, dtype=jnp.int32)[None, :]).reshape(-1)
    x1s, pool_s_t = _pool_sample(x_sample, jnp.transpose(state_pool[0], (1, 0, 2)), start[:, None],
                                 row(norm_mix[0]), wp, row(ls_pool[0]))
    x1s = x1s.reshape(Bd * Td, D)
    pool_s = jnp.transpose(pool_s_t, (1, 0, 2))
    pool_p = pool_p16[:, HALO - POOL_BUF:]

    moe0 = functools.partial(_moe, nf=row(norm_ffn[0]), wr=wr[0], br=br[0])
    x2p, *w0 = moe0(x1p.reshape(B * T, D), wg=w_gate, wu=w_up, wd=w_down, layer=0)
    x2s = moe0(x1s, wg=w0[0], wu=w0[1], wd=w0[2], layer=None)

    cos_p, sin_p = _rope_tables(jnp.arange(T, dtype=jnp.int32))
    cos_s, sin_s = _rope_tables(pos_s)
    g1 = row(norm_mix[1])
    tmq = QKV_ROWS
    keep = min(WINDOW, T)
    qp, kdp, vdp, kp, vp = _qkv(x2p, g1, wqkv, row(b_qkv[0]), cos_p, sin_p, tmq, T // tmq, keep)
    qs, kds, vds, ks, vs = _qkv(x2s, g1, wqkv, row(b_qkv[0]), cos_s, sin_s, tmq, 1, tmq)

    op, *w1 = _attn_prompt(qp, kdp, vdp, _sink_rows(sinks[0], WINDOW), w_gate, w_up, w_down, 1, B, T)

    bs = ATTN_SAMPLE_SEQS
    rows = bs * Td
    W = cache_k.shape[2]
    qrow = np.arange(rows)
    ccol = np.arange(bs * W)
    ncol = np.arange(rows)
    samp_q, t_q = qrow // Td, qrow % Td
    m_cache = (samp_q[:, None] == (ccol // W)[None, :]) & ((ccol % W)[None, :] > t_q[:, None])
    m_new = (samp_q[:, None] == (ncol // Td)[None, :]) & ((ncol % Td)[None, :] <= t_q[:, None])
    amask = jnp.asarray(np.tile(np.concatenate([m_cache, m_new], axis=1).T, (1, 4)), F32)
    kv_cache = (jnp.arange(W, dtype=jnp.int32)[None, :] >= (W - start)[:, None]).reshape(Bd // bs, bs * W)
    kval = jnp.concatenate([kv_cache, jnp.ones((Bd // bs, rows), bool)], axis=1).astype(F32)
    kval = kval.reshape(Bd // bs, bs * W + rows, 1)
    to_stored = lambda c: jnp.transpose(c[0], (0, 2, 3, 1))
    from_stored = lambda c: jnp.transpose(c, (0, 3, 1, 2))[None]
    osamp, nk_s, nv_s = _attn_sample(
        qs, kds, vds, ks, vs, to_stored(cache_k), to_stored(cache_v),
        amask, kval, _sink_rows(sinks[0], rows), bs, Td)

    moe1 = functools.partial(_moe, nf=row(norm_ffn[1]), wr=wr[1], br=br[1], final=row(norm_final))
    yp = moe1(x2p, wg=w1[0], wu=w1[1], wd=w1[2], layer=None, oproj=(op, wo, row(b_o[0])))
    ys = moe1(x2s, wg=w1[0], wu=w1[1], wd=w1[2], layer=None, oproj=(osamp, wo, row(b_o[0])))

    k_p = kp.reshape(1, B, keep, N_KV_HEADS, HEAD_DIM)
    v_p = vp.reshape(1, B, keep, N_KV_HEADS, HEAD_DIM)
    return (yp.reshape(B, T, D), ys.reshape(Bd, Td, D), pool_p[None], k_p, v_p, pool_s[None],
            from_stored(nk_s), from_stored(nv_s))
```

```python
import functools

import jax
import jax.numpy as jnp
import numpy as np
from jax import lax
from jax.experimental import pallas as pl
from jax.experimental.pallas import tpu as pltpu

F32 = jnp.float32
BF16 = jnp.bfloat16

D_MODEL = 1024
POOL_WINDOWS = (2, 4, 8, 16)
POOL_GROUP_DIM = 256
POOL_BUF = 15
HALO = 16
HEAD_DIM = 64
N_HEADS = 16
N_KV_HEADS = 4
WINDOW = 128
ROPE_THETA = 10000.0
N_EXPERT_GROUPS = 4
EXPERTS_PER_GROUP = 4
N_EXPERTS = 16
D_EXPERT = 256
RMS_EPS = 1e-6
LANES = 128
ROUTER_ROWS = 32
EXPERT_ROW0 = 8
CAST_SLOTS = 6

POOL_PROMPT_ROWS = 1024
POOL_SAMPLE_SEQS = 32
MOE_ROWS = 512
MOE_BLOCK_ROWS = 144
QKV_ROWS = 1024
ATTN_PROMPT_ROWS = 1024
ATTN_SAMPLE_SEQS = 8
VMEM_LIMIT = 56 * 1024 * 1024
LOG2E = 1.4426950408889634
Q_SCALE = LOG2E * HEAD_DIM ** -0.5


def _rms(x, g):
    ms = jnp.mean(x * x, axis=-1, keepdims=True)
    return x * lax.rsqrt(ms + RMS_EPS) * g


def _pool_project(h, wins, pos, wp_ref, ls):
    G = POOL_GROUP_DIM
    outs = []
    for g, w in enumerate(POOL_WINDOWS):
        cnt = jnp.minimum(w, pos + 1).astype(F32)
        d = wins[g] / cnt - h[:, g * G:(g + 1) * G]
        outs.append(jnp.dot(d.astype(BF16), wp_ref[g], preferred_element_type=F32))
    return jnp.concatenate(outs, axis=-1) * ls


POOL_BLOCK = 128


def _pool_band():
    t = np.arange(POOL_BLOCK)[:, None] + HALO
    k = np.arange(POOL_BLOCK + HALO)[None, :]
    return jnp.asarray(np.stack([(k <= t) & (k > t - w) for w in POOL_WINDOWS]), BF16)


def _pool_prompt_kernel(x_ref, buf_ref, g_ref, wp_ref, ls_ref, band_ref, o_ref, nb_ref, hc, *, tq):
    t = pl.program_id(1)
    G = POOL_GROUP_DIM

    @pl.when(t == 0)
    def _():
        hc[pl.ds(0, HALO), :] = buf_ref[0]

    @pl.when(t > 0)
    def _():
        hc[pl.ds(0, HALO), :] = hc[pl.ds(tq, HALO), :]

    x = x_ref[0]
    h = _rms(x, g_ref[...])
    hc[pl.ds(HALO, tq), :] = h
    hb = hc[...].astype(BF16)
    wins = []
    for g in range(len(POOL_WINDOWS)):
        blocks = [jnp.dot(band_ref[g], hb[b * POOL_BLOCK:(b + 1) * POOL_BLOCK + HALO, g * G:(g + 1) * G],
                          preferred_element_type=F32) for b in range(tq // POOL_BLOCK)]
        wins.append(jnp.concatenate(blocks, axis=0))
    pos = t * tq + lax.broadcasted_iota(jnp.int32, (tq, 1), 0)
    o_ref[0] = x + _pool_project(h, wins, pos, wp_ref, ls_ref[...])

    @pl.when(t == pl.num_programs(1) - 1)
    def _():
        nb_ref[0] = hc[pl.ds(tq, HALO), :]


def _pool_prompt(x, buf16, g, wp, ls, tq=POOL_PROMPT_ROWS):
    B, T, D = x.shape
    return pl.pallas_call(
        functools.partial(_pool_prompt_kernel, tq=tq),
        grid=(B, T // tq),
        in_specs=[
            pl.BlockSpec((1, tq, D), lambda b, t: (b, t, 0)),
            pl.BlockSpec((1, HALO, D), lambda b, t: (b, 0, 0)),
            pl.BlockSpec((1, D), lambda b, t: (0, 0)),
            pl.BlockSpec((4, POOL_GROUP_DIM, POOL_GROUP_DIM), lambda b, t: (0, 0, 0)),
            pl.BlockSpec((1, D), lambda b, t: (0, 0)),
            pl.BlockSpec((4, POOL_BLOCK, POOL_BLOCK + HALO), lambda b, t: (0, 0, 0)),
        ],
        out_specs=[
            pl.BlockSpec((1, tq, D), lambda b, t: (b, t, 0)),
            pl.BlockSpec((1, HALO, D), lambda b, t: (b, 0, 0)),
        ],
        out_shape=[jax.ShapeDtypeStruct((B, T, D), F32), jax.ShapeDtypeStruct((B, HALO, D), F32)],
        scratch_shapes=[pltpu.VMEM((HALO + tq, D), F32)],
        compiler_params=pltpu.CompilerParams(
            dimension_semantics=("arbitrary", "arbitrary"), vmem_limit_bytes=VMEM_LIMIT),
        name="pool_prompt",
    )(x, buf16, g, wp, ls, _pool_band())


def _pool_sample_kernel(x_ref, st_ref, start_ref, g_ref, wp_ref, ls_ref, o_ref, ns_ref, *, bs, t):
    G = POOL_GROUP_DIM
    xs = [x_ref[:, i, :] for i in range(t)]
    hn = [_rms(x, g_ref[...]) for x in xs]
    hist = [st_ref[r] for r in range(POOL_BUF)] + hn

    def doubled(prev, lag, lo):
        out = [None] * len(prev)
        for i in range(len(prev)):
            if i >= lag and prev[i] is not None and prev[i - lag] is not None:
                out[i] = prev[i][:, lo:] + prev[i - lag][:, lo:]
        return out

    s2 = doubled(hist, 1, 0)
    s4 = doubled(s2, 2, G)
    s8 = doubled(s4, 4, G)
    s16 = doubled(s8, 8, G)
    wins = (s2, s4, s8, s16)
    start = start_ref[...]
    ds = [[] for _ in POOL_WINDOWS]
    for i in range(t):
        for g, w in enumerate(POOL_WINDOWS):
            cnt = jnp.minimum(w, start + (i + 1)).astype(F32)
            ds[g].append(wins[g][POOL_BUF + i][:, :G] / cnt - hn[i][:, g * G:(g + 1) * G])
    ys = [jnp.dot(jnp.concatenate(ds[g], axis=0).astype(BF16), wp_ref[g], preferred_element_type=F32)
          for g in range(len(POOL_WINDOWS))]
    y = jnp.concatenate(ys, axis=-1) * ls_ref[...]
    for i in range(t):
        o_ref[:, i, :] = xs[i] + y[i * bs:(i + 1) * bs]
    for r in range(POOL_BUF):
        ns_ref[r] = hist[r + t]


def _pool_sample(x, st, start, g, wp, ls, bs=POOL_SAMPLE_SEQS):
    Bd, t, D = x.shape
    return pl.pallas_call(
        functools.partial(_pool_sample_kernel, bs=bs, t=t),
        grid=(Bd // bs,),
        in_specs=[
            pl.BlockSpec((bs, t, D), lambda i: (i, 0, 0)),
            pl.BlockSpec((POOL_BUF, bs, D), lambda i: (0, i, 0)),
            pl.BlockSpec((bs, 1), lambda i: (i, 0)),
            pl.BlockSpec((1, D), lambda i: (0, 0)),
            pl.BlockSpec((4, POOL_GROUP_DIM, POOL_GROUP_DIM), lambda i: (0, 0, 0)),
            pl.BlockSpec((1, D), lambda i: (0, 0)),
        ],
        out_specs=[pl.BlockSpec((bs, t, D), lambda i: (i, 0, 0)),
                   pl.BlockSpec((POOL_BUF, bs, D), lambda i: (0, i, 0))],
        out_shape=[jax.ShapeDtypeStruct((Bd, t, D), F32), jax.ShapeDtypeStruct((POOL_BUF, Bd, D), F32)],
        compiler_params=pltpu.CompilerParams(
            dimension_semantics=("arbitrary",), vmem_limit_bytes=VMEM_LIMIT),
        name="pool_sample",
    )(x, st, start, g, wp, ls)


def _route(lg):
    R = lg.shape[1]
    big = jnp.float32(1 << 20)
    neg = jnp.float32(-jnp.inf)
    r8 = lax.broadcasted_iota(jnp.int32, (8, R), 0).astype(F32)
    r16 = lax.broadcasted_iota(jnp.int32, (N_EXPERTS, R), 0).astype(F32)
    is_g = r8 < N_EXPERT_GROUPS
    gl = jnp.where(is_g, lg[0:8], neg)
    m = jnp.max(gl, axis=0, keepdims=True)
    gidx = jnp.min(jnp.where(gl == m, r8, big), axis=0, keepdims=True)
    z = jnp.sum(jnp.where(is_g, jnp.exp(gl - m), 0.0), axis=0, keepdims=True)
    gw = 1.0 / z
    lo = gidx * EXPERTS_PER_GROUP
    in_grp = (r16 >= lo) & (r16 < lo + EXPERTS_PER_GROUP)
    el = jnp.where(in_grp, lg[EXPERT_ROW0:EXPERT_ROW0 + N_EXPERTS], neg)
    v1 = jnp.max(el, axis=0, keepdims=True)
    i1 = jnp.min(jnp.where(el == v1, r16, big), axis=0, keepdims=True)
    el2 = jnp.where(r16 == i1, neg, el)
    v2 = jnp.max(el2, axis=0, keepdims=True)
    i2 = jnp.min(jnp.where(el2 == v2, r16, big), axis=0, keepdims=True)
    t = jnp.exp(v2 - v1)
    w1 = 1.0 / (1.0 + t)
    w2 = t * w1
    e8 = jnp.where(is_g, r8, r8 - EXPERTS_PER_GROUP)
    cw = gw * (jnp.where(e8 == i1 - lo, w1, 0.0) + jnp.where(e8 == i2 - lo, w2, 0.0))
    return gidx, cw


def _moe_kernel(*refs, has_oproj, has_final, cast_layer, tm, bm):
    it = iter(refs)
    x_ref = next(it)
    if has_oproj:
        oin_ref, wo_ref, bo_ref = next(it), next(it), next(it)
    nf_ref, wr_ref, br_ref, wg_in, wu_in, wd_in = (next(it) for _ in range(6))
    if has_final:
        fn_ref = next(it)
    out_ref = next(it)
    if cast_layer is not None:
        wg_out, wu_out, wd_out = (next(it) for _ in range(3))
    hs, cs, ys, sel_s, earlier = (next(it) for _ in range(5))
    if cast_layer is None:
        wg_ref, wu_ref, wd_ref = wg_in, wu_in, wd_in
    else:
        wg_ref, wu_ref, wd_ref, stage_gu, stage_d, load_sem, store_sem = (next(it) for _ in range(7))
        pairs = ((wg_in, wg_ref, wg_out, stage_gu), (wu_in, wu_ref, wu_out, stage_gu), (wd_in, wd_ref, wd_out, stage_d))

        def store_copy(k):
            return pltpu.make_async_copy(pairs[k][1], pairs[k][2], store_sem.at[k])

        @pl.when(pl.program_id(0) == 0)
        def _():
            ahead = stage_gu.shape[0] - 1
            for src, dst, _, stage in pairs:
                def load(e, src=src, stage=stage):
                    slot = e % (ahead + 1)
                    return pltpu.make_async_copy(src.at[cast_layer, e], stage.at[slot], load_sem.at[slot])

                for e in range(ahead):
                    load(e).start()
                for e in range(N_EXPERTS):
                    if e + ahead < N_EXPERTS:
                        load(e + ahead).start()
                    load(e).wait()
                    dst[e] = stage[e % (ahead + 1)].astype(BF16)
            for k in range(len(pairs)):
                store_copy(k).start()

        @pl.when(pl.program_id(0) == pl.num_programs(0) - 1)
        def _():
            for k in range(len(pairs)):
                store_copy(k).wait()
    min_blk = -(-tm // bm)
    nblk = tm // bm + N_EXPERT_GROUPS
    cap = nblk * bm
    assert bm % 16 == 0
    nt = (((1,), (1,)), ((), ()))

    @pl.when(pl.program_id(0) == 0)
    def _():
        ri = lax.broadcasted_iota(jnp.int32, (tm, tm), 0)
        ci = lax.broadcasted_iota(jnp.int32, (tm, tm), 1)
        earlier[...] = jnp.where(ri < ci, 1.0, 0.0).astype(BF16)

    x = x_ref[...]
    if has_oproj:
        x = x + jnp.dot(oin_ref[...], wo_ref[...], preferred_element_type=F32) + bo_ref[...]
    if has_oproj:
        out_ref[...] = x
    h = _rms(x, nf_ref[...])
    hb = h.astype(BF16)
    h_lo = (h - hb.astype(F32)).astype(BF16)
    wr = wr_ref[...]
    both = lax.dot_general(wr, hb, nt, preferred_element_type=F32)
    lg = (both[0:ROUTER_ROWS] + both[ROUTER_ROWS:]
          + lax.dot_general(wr[0:ROUTER_ROWS], h_lo, nt, preferred_element_type=F32)) + br_ref[...]
    gidx, cw = _route(lg)
    r8 = lax.broadcasted_iota(jnp.int32, (8, tm), 0)
    cw_hi = cw.astype(BF16).astype(F32)
    cwt = jnp.concatenate([jnp.where(r8 < EXPERTS_PER_GROUP, cw_hi, cw - cw_hi),
                           jnp.zeros((LANES - 8, tm), F32)], axis=0).astype(BF16)

    oh = jnp.where(r8.astype(F32) == gidx, 1.0, 0.0)
    cnt_before = jnp.dot(oh.astype(BF16), earlier[...], preferred_element_type=F32)
    pos_row = jnp.sum(oh * cnt_before, axis=0, keepdims=True)
    start_blk, n_blk = [], []
    off = jnp.int32(0)
    for g in range(N_EXPERT_GROUPS):
        in_g = gidx == g
        n_g = jnp.sum(jnp.where(in_g, 1.0, 0.0)).astype(jnp.int32)
        blocks = sum(jnp.where(n_g > k * bm, 1, 0) for k in range(min_blk))
        start_blk.append(off)
        n_blk.append(blocks)
        pos_row = pos_row + jnp.where(in_g, (off * bm).astype(F32), 0.0)
        off = off + blocks

    total_blk = off
    main = min_blk * bm
    extra = ((main, bm, total_blk > min_blk), (main + bm, cap - main - bm, total_blk > min_blk + 1))

    def sort_rows(r0, n):
        sub = (r0 + lax.broadcasted_iota(jnp.int32, (n, tm), 0)).astype(F32)
        sel = jnp.where(pos_row == sub, 1.0, 0.0).astype(BF16)
        sel_s[r0:r0 + n, :] = sel
        hs[r0:r0 + n, :] = jnp.dot(sel, hb, preferred_element_type=F32).astype(BF16)
        cs[r0:r0 + n, :] = lax.dot_general(sel, cwt, nt, preferred_element_type=F32)

    sort_rows(0, main)
    for r0, n, needed in extra:
        pl.when(needed)(functools.partial(sort_rows, r0, n))
    ys[min_blk * bm:, :] = jnp.zeros((cap - min_blk * bm, D_MODEL), BF16)

    for g in range(N_EXPERT_GROUPS):
        wd = wd_ref[g * EXPERTS_PER_GROUP:(g + 1) * EXPERTS_PER_GROUP].reshape(
            EXPERTS_PER_GROUP * D_EXPERT, D_MODEL)

        def block(b, carry, g=g, wd=wd):
            rows = pl.ds(pl.multiple_of(b * bm, 16), bm)
            hblk = hs[rows, :]
            cblk = cs[rows, :]
            ln = lax.broadcasted_iota(jnp.int32, cblk.shape, 1)
            parts = []
            for e in range(EXPERTS_PER_GROUP):
                ce = jnp.sum(jnp.where((ln == e) | (ln == e + EXPERTS_PER_GROUP), cblk, 0.0),
                             axis=-1, keepdims=True)
                gt = jnp.dot(hblk, wg_ref[g * EXPERTS_PER_GROUP + e], preferred_element_type=F32)
                up = jnp.dot(hblk, wu_ref[g * EXPERTS_PER_GROUP + e], preferred_element_type=F32)
                a = gt / (1.0 + jnp.exp(-gt)) * up * ce
                parts.append(a.astype(BF16))
            a_all = jnp.concatenate(parts, axis=-1)
            ys[rows, :] = jnp.dot(a_all, wd, preferred_element_type=F32).astype(BF16)
            return carry

        lax.fori_loop(start_blk[g], start_blk[g] + n_blk[g], block, 0)

    def unsorted(r0, n):
        return lax.dot_general(sel_s[r0:r0 + n, :], ys[r0:r0 + n, :], (((0,), (0,)), ((), ())),
                               preferred_element_type=F32)

    def unsort_more(r0, n):
        out_ref[...] += unsorted(r0, n)

    out_ref[...] = (out_ref[...] if has_oproj else x_ref[...]) + unsorted(0, main)
    for r0, n, needed in extra:
        pl.when(needed)(functools.partial(unsort_more, r0, n))
    if has_final:
        out_ref[...] = _rms(out_ref[...], fn_ref[...])


def _moe(x, nf, wr, br, wg, wu, wd, layer, oproj=None, final=None, tm=MOE_ROWS, bm=MOE_BLOCK_ROWS):
    N, D = x.shape
    has_oproj = oproj is not None
    has_final = final is not None
    const2 = lambda i: (0, 0)
    const3 = lambda i: (0, 0, 0)
    cast = layer is not None
    tile = lambda i: (i, 0)
    once = pl.Buffered(1)
    args = [x]
    in_specs = [pl.BlockSpec((tm, D), tile)]
    if has_oproj:
        o, wo, bo = oproj
        args += [o, wo, bo]
        in_specs += [pl.BlockSpec((tm, D), tile), pl.BlockSpec((D, D), const2, pipeline_mode=once),
                     pl.BlockSpec((1, D), const2)]
    wr_hi = wr.astype(BF16)
    wr_lo = (wr - wr_hi.astype(F32)).astype(BF16)
    args += [nf, jnp.concatenate([wr_hi, wr_lo], axis=0), br, wg, wu, wd]
    in_specs += [
        pl.BlockSpec((1, D), const2),
        pl.BlockSpec((2 * ROUTER_ROWS, D), const2, pipeline_mode=once),
        pl.BlockSpec((ROUTER_ROWS, 1), const2),
    ]
    gu_shape, d_shape = (N_EXPERTS, D, D_EXPERT), (N_EXPERTS, D_EXPERT, D)
    if cast:
        in_specs += [pl.BlockSpec(memory_space=pl.ANY)] * 3
    else:
        in_specs += [pl.BlockSpec(gu_shape, const3, pipeline_mode=once), pl.BlockSpec(gu_shape, const3, pipeline_mode=once),
                     pl.BlockSpec(d_shape, const3, pipeline_mode=once)]
    if has_final:
        args.append(final)
        in_specs.append(pl.BlockSpec((1, D), const2))
    cap = (tm // bm + N_EXPERT_GROUPS) * bm
    out_specs = [pl.BlockSpec((tm, D), tile)]
    out_shape = [jax.ShapeDtypeStruct((N, D), F32)]
    scratch = [
        pltpu.VMEM((cap, D), BF16),
        pltpu.VMEM((cap, LANES), F32),
        pltpu.VMEM((cap, D), BF16),
        pltpu.VMEM((cap, tm), BF16),
        pltpu.VMEM((tm, tm), BF16),
    ]
    if cast:
        out_specs += [pl.BlockSpec(memory_space=pl.ANY)] * 3
        out_shape += [jax.ShapeDtypeStruct(gu_shape, BF16)] * 2 + [jax.ShapeDtypeStruct(d_shape, BF16)]
        scratch += [pltpu.VMEM(gu_shape, BF16), pltpu.VMEM(gu_shape, BF16), pltpu.VMEM(d_shape, BF16),
                    pltpu.VMEM((CAST_SLOTS,) + gu_shape[1:], F32), pltpu.VMEM((CAST_SLOTS,) + d_shape[1:], F32),
                    pltpu.SemaphoreType.DMA((CAST_SLOTS,)), pltpu.SemaphoreType.DMA((3,))]
    res = pl.pallas_call(
        functools.partial(_moe_kernel, has_oproj=has_oproj, has_final=has_final, cast_layer=layer, tm=tm, bm=bm),
        grid=(N // tm,),
        in_specs=in_specs,
        out_specs=out_specs,
        out_shape=out_shape,
        scratch_shapes=scratch,
        compiler_params=pltpu.CompilerParams(
            dimension_semantics=("arbitrary",), vmem_limit_bytes=VMEM_LIMIT),
        name="moe",
    )(*args)
    return res if cast else res[0]


def _swap_halves(x, low):
    return jnp.where(low, pltpu.roll(x, LANES - 32, axis=1), pltpu.roll(x, 32, axis=1))


def _qkv_kernel(x_ref, g_ref, w_ref, b_ref, cos_ref, sin_ref, q_ref, kd_ref, vd_ref, k_ref, v_ref, *, keep):
    tm = x_ref.shape[0]
    h = _rms(x_ref[...], g_ref[...]).astype(BF16)
    cos = cos_ref[...]
    sin = sin_ref[...]
    lane = lax.broadcasted_iota(jnp.int32, cos.shape, 1)
    low32 = (lane % HEAD_DIM) < (HEAD_DIM // 2)
    low64 = lane < HEAD_DIM
    nq = N_HEADS * HEAD_DIM
    nk = N_KV_HEADS * HEAD_DIM
    cw = 2 * LANES

    def project(c0):
        y = jnp.dot(h, w_ref[:, c0:c0 + cw], preferred_element_type=F32) + b_ref[:, c0:c0 + cw]
        return [y[:, i * LANES:(i + 1) * LANES] for i in range(cw // LANES)]

    def rope(c):
        return c * cos + _swap_halves(c, low32) * sin

    def dup(c):
        r = pltpu.roll(c, HEAD_DIM, axis=1)
        return jnp.where(low64, c, r), jnp.where(low64, r, c)

    for c0 in range(0, nq, cw):
        for i, c in enumerate(project(c0)):
            lanes = slice(c0 + i * LANES, c0 + (i + 1) * LANES)
            q_ref[:, lanes] = (rope(c) * Q_SCALE).astype(BF16)
    for c0 in range(0, nk, cw):
        for (src, rot, f_ref, d_ref) in ((nq, True, k_ref, kd_ref), (nq + nk, False, v_ref, vd_ref)):
            for i, c in enumerate(project(src + c0)):
                j = c0 // LANES + i
                c = rope(c) if rot else c
                f_ref[:, j * LANES:(j + 1) * LANES] = c[tm - keep:]
                d0, d1 = dup(c)
                d_ref[:, 2 * j * LANES:(2 * j + 1) * LANES] = d0.astype(BF16)
                d_ref[:, (2 * j + 1) * LANES:(2 * j + 2) * LANES] = d1.astype(BF16)


def _qkv(x, g, w, b, cos, sin, tm, n_pos_tiles, keep):
    N, D = x.shape
    nq, nk = N_HEADS * HEAD_DIM, N_KV_HEADS * HEAD_DIM
    const = lambda i: (0, 0)
    tile = lambda i: (i, 0)
    ptile = lambda i: (i % n_pos_tiles, 0)
    seq = lambda i: (i // n_pos_tiles, 0)
    n_seq = N // (tm * n_pos_tiles)
    return pl.pallas_call(
        functools.partial(_qkv_kernel, keep=keep),
        grid=(N // tm,),
        in_specs=[
            pl.BlockSpec((tm, D), tile),
            pl.BlockSpec((1, D), const),
            pl.BlockSpec((D, nq + 2 * nk), const),
            pl.BlockSpec((1, nq + 2 * nk), const),
            pl.BlockSpec((tm, LANES), ptile),
            pl.BlockSpec((tm, LANES), ptile),
        ],
        out_specs=[
            pl.BlockSpec((tm, nq), tile),
            pl.BlockSpec((tm, 2 * nk), tile),
            pl.BlockSpec((tm, 2 * nk), tile),
            pl.BlockSpec((keep, nk), seq),
            pl.BlockSpec((keep, nk), seq),
        ],
        out_shape=[
            jax.ShapeDtypeStruct((N, nq), BF16),
            jax.ShapeDtypeStruct((N, 2 * nk), BF16),
            jax.ShapeDtypeStruct((N, 2 * nk), BF16),
            jax.ShapeDtypeStruct((n_seq * keep, nk), F32),
            jax.ShapeDtypeStruct((n_seq * keep, nk), F32),
        ],
        compiler_params=pltpu.CompilerParams(
            dimension_semantics=("arbitrary",), vmem_limit_bytes=VMEM_LIMIT),
        name="qkv",
    )(x, g, w, b, cos, sin)


def _stack_heads(qc_list):
    lane = lax.broadcasted_iota(jnp.int32, qc_list[0].shape, 1)
    low = lane < HEAD_DIM
    zero = jnp.zeros_like(qc_list[0])
    stack = []
    for qc in qc_list:
        stack.append(jnp.where(low, qc, zero))
        stack.append(jnp.where(low, zero, qc))
    return jnp.concatenate(stack, axis=0)


def _softmax_keys_on_rows(s, sinkrow):
    m = jnp.maximum(jnp.max(s, axis=0, keepdims=True), sinkrow)
    p = jnp.exp2(s - m)
    den = jnp.sum(p, axis=0, keepdims=True) + jnp.exp2(sinkrow - m)
    return (p * (1.0 / den)).astype(BF16)


def _unstack_heads(o, R):
    lowf = lax.broadcasted_iota(jnp.int32, (R, LANES), 1) < HEAD_DIM
    return (jnp.where(lowf, o[0:R], o[R:2 * R]), jnp.where(lowf, o[2 * R:3 * R], o[3 * R:4 * R]))


NT_DIMS = (((1,), (1,)), ((), ()))
TN_DIMS = (((0,), (0,)), ((), ()))
TT_DIMS = (((0,), (1,)), ((), ()))


def _attend(qc_list, kd, vd, bias, sinkrow):
    qs = _stack_heads(qc_list)
    s = lax.dot_general(kd, qs, NT_DIMS, preferred_element_type=F32) + bias
    p = _softmax_keys_on_rows(s, sinkrow)
    o = lax.dot_general(p, vd, TN_DIMS, preferred_element_type=F32)
    return _unstack_heads(o, qc_list[0].shape[0])


def _attn_prompt_kernel(q_ref, kp_ref, kc_ref, vp_ref, vc_ref, sink_ref, wg_in, wu_in, wd_in,
                        o_ref, wg_out, wu_out, wd_out, f32_g, f32_u, f32_d, b16_g, b16_u, b16_d, load_sem, store_sem,
                        *, tq, cast_layer):
    n = pl.program_id(1)
    step = pl.program_id(0) * pl.num_programs(1) + n
    kinds = ((wg_in, f32_g, b16_g, wg_out), (wu_in, f32_u, b16_u, wu_out), (wd_in, f32_d, b16_d, wd_out))

    def load(k, e):
        return pltpu.make_async_copy(kinds[k][0].at[cast_layer, e], kinds[k][1].at[e % 2], load_sem.at[2 * k + e % 2])

    def store(k, e):
        return pltpu.make_async_copy(kinds[k][2], kinds[k][3].at[e], store_sem.at[k])

    @pl.when(step == 0)
    def _():
        for k in range(3):
            load(k, 0).start()

    @pl.when(step < N_EXPERTS - 1)
    def _():
        for k in range(3):
            load(k, step + 1).start()

    for k in range(3):
        load(k, step).wait()

    @pl.when(step > 0)
    def _():
        for k in range(3):
            store(k, step - 1).wait()

    for k in range(3):
        kinds[k][2][...] = kinds[k][1][step % 2].astype(BF16)
        store(k, step).start()

    B = WINDOW
    kd = jnp.concatenate([kp_ref[...], kc_ref[...]], axis=0)
    vd = jnp.concatenate([vp_ref[...], vc_ref[...]], axis=0)
    kj = lax.broadcasted_iota(jnp.int32, (2 * B, B), 0)
    qi = lax.broadcasted_iota(jnp.int32, (2 * B, B), 1)
    rel = B + qi - kj
    band = (rel >= 0) & (rel < WINDOW)
    neg = jnp.float32(-jnp.inf)
    band_bias = jnp.concatenate([jnp.where(band, 0.0, neg)] * 4, axis=1)
    first_bias = jnp.concatenate([jnp.where(band & ((kj >= B) | (n > 0)), 0.0, neg)] * 4, axis=1)
    for j in range(tq // B):
        bias = first_bias if j == 0 else band_bias
        for g in range(N_KV_HEADS):
            qc = [q_ref[j * B:(j + 1) * B, (2 * g + i) * LANES:(2 * g + i + 1) * LANES] for i in range(2)]
            kg = kd[j * B:(j + 2) * B, g * LANES:(g + 1) * LANES]
            vg = vd[j * B:(j + 2) * B, g * LANES:(g + 1) * LANES]
            o0, o1 = _attend(qc, kg, vg, bias, sink_ref[g])
            o_ref[j * B:(j + 1) * B, 2 * g * LANES:(2 * g + 1) * LANES] = o0.astype(BF16)
            o_ref[j * B:(j + 1) * B, (2 * g + 1) * LANES:(2 * g + 2) * LANES] = o1.astype(BF16)

    @pl.when(step == N_EXPERTS - 1)
    def _():
        for k in range(3):
            store(k, step).wait()


def _attn_prompt(q, kd, vd, sinkrow, wg, wu, wd, layer, batch, seq, tq=ATTN_PROMPT_ROWS):
    N = q.shape[0]
    nt = seq // tq
    assert batch * nt == N_EXPERTS
    gu_shape, d_shape = (N_EXPERTS, D_MODEL, D_EXPERT), (N_EXPERTS, D_EXPERT, D_MODEL)
    r = tq // WINDOW
    cur = lambda b, n: (b * nt + n, 0)
    prev = lambda b, n: (jnp.maximum((b * nt + n) * r - 1, 0), 0)
    kvw = 2 * N_KV_HEADS * HEAD_DIM
    return pl.pallas_call(
        functools.partial(_attn_prompt_kernel, tq=tq, cast_layer=layer),
        grid=(batch, nt),
        in_specs=[
            pl.BlockSpec((tq, N_HEADS * HEAD_DIM), cur),
            pl.BlockSpec((WINDOW, kvw), prev),
            pl.BlockSpec((tq, kvw), cur),
            pl.BlockSpec((WINDOW, kvw), prev),
            pl.BlockSpec((tq, kvw), cur),
            pl.BlockSpec((N_KV_HEADS, 1, 4 * WINDOW), lambda b, n: (0, 0, 0)),
        ] + [pl.BlockSpec(memory_space=pl.ANY)] * 3,
        out_specs=[pl.BlockSpec((tq, N_HEADS * HEAD_DIM), cur)] + [pl.BlockSpec(memory_space=pl.ANY)] * 3,
        out_shape=[jax.ShapeDtypeStruct((N, N_HEADS * HEAD_DIM), BF16), jax.ShapeDtypeStruct(gu_shape, BF16),
                   jax.ShapeDtypeStruct(gu_shape, BF16), jax.ShapeDtypeStruct(d_shape, BF16)],
        scratch_shapes=[pltpu.VMEM((2,) + gu_shape[1:], F32), pltpu.VMEM((2,) + gu_shape[1:], F32),
                        pltpu.VMEM((2,) + d_shape[1:], F32),
                        pltpu.VMEM(gu_shape[1:], BF16), pltpu.VMEM(gu_shape[1:], BF16), pltpu.VMEM(d_shape[1:], BF16),
                        pltpu.SemaphoreType.DMA((6,)), pltpu.SemaphoreType.DMA((3,))],
        compiler_params=pltpu.CompilerParams(
            dimension_semantics=("arbitrary", "arbitrary"), vmem_limit_bytes=VMEM_LIMIT),
        name="attn_prompt",
    )(q, kd, kd, vd, vd, sinkrow, wg, wu, wd)


def _attn_sample_kernel(q_ref, kdn_ref, vdn_ref, kn_ref, vn_ref, ck_ref, cv_ref, mask_ref, kval_ref, sink_ref,
                        o_ref, nk_ref, nv_ref, *, bs, t):
    W = WINDOW
    rows = bs * t
    bias = jnp.where((mask_ref[...] > 0) & (kval_ref[0] > 0), 0.0, -jnp.inf).astype(F32)
    lane = lax.broadcasted_iota(jnp.int32, (HEAD_DIM, W), 1)
    fresh = lane >= W - t

    def new_cols(n_ref):
        n = jnp.concatenate([n_ref[...], jnp.zeros((LANES - rows, N_KV_HEADS * HEAD_DIM), F32)], axis=0)
        nt_ = jnp.transpose(n)
        return [pltpu.roll(nt_, (W - t - s * t) % LANES, axis=1) for s in range(bs)]

    k_cols, v_cols = new_cols(kn_ref), new_cols(vn_ref)
    for g in range(N_KV_HEADS):
        hd = slice(g * HEAD_DIM, (g + 1) * HEAD_DIM)
        kt = jnp.concatenate([ck_ref[s, g] for s in range(bs)], axis=1).astype(BF16)
        vt = jnp.concatenate([cv_ref[s, g] for s in range(bs)], axis=1).astype(BF16)
        kt2 = jnp.concatenate([kt, kt], axis=0)
        vt2 = jnp.concatenate([vt, vt], axis=0)
        qs = _stack_heads([q_ref[:, (2 * g + i) * LANES:(2 * g + i + 1) * LANES] for i in range(2)])
        s_old = lax.dot_general(kt2, qs, TT_DIMS, preferred_element_type=F32)
        s_new = lax.dot_general(kdn_ref[:, g * LANES:(g + 1) * LANES], qs, NT_DIMS, preferred_element_type=F32)
        p = _softmax_keys_on_rows(jnp.concatenate([s_old, s_new], axis=0) + bias, sink_ref[g])
        o = (lax.dot_general(p[:bs * W], vt2, TT_DIMS, preferred_element_type=F32)
             + lax.dot_general(p[bs * W:], vdn_ref[:, g * LANES:(g + 1) * LANES], TN_DIMS,
                               preferred_element_type=F32))
        o0, o1 = _unstack_heads(o, rows)
        o_ref[:, 2 * g * LANES:(2 * g + 1) * LANES] = o0.astype(BF16)
        o_ref[:, (2 * g + 1) * LANES:(2 * g + 2) * LANES] = o1.astype(BF16)
        for s in range(bs):
            nk_ref[s, g] = jnp.where(fresh, k_cols[s][hd], pltpu.roll(ck_ref[s, g], W - t, axis=1))
            nv_ref[s, g] = jnp.where(fresh, v_cols[s][hd], pltpu.roll(cv_ref[s, g], W - t, axis=1))


def _attn_sample(q, kdn, vdn, kn, vn, ck, cv, mask, kval, sinkrow, bs, t):
    Bd = ck.shape[0]
    rows = bs * t
    kvw = N_KV_HEADS * HEAD_DIM
    S = bs * WINDOW + rows
    tile = lambda i: (i, 0)
    tile3 = lambda i: (i, 0, 0)
    tile4 = lambda i: (i, 0, 0, 0)
    cblk = (bs, N_KV_HEADS, HEAD_DIM, WINDOW)
    return pl.pallas_call(
        functools.partial(_attn_sample_kernel, bs=bs, t=t),
        grid=(Bd // bs,),
        in_specs=[
            pl.BlockSpec((rows, N_HEADS * HEAD_DIM), tile),
            pl.BlockSpec((rows, 2 * kvw), tile),
            pl.BlockSpec((rows, 2 * kvw), tile),
            pl.BlockSpec((rows, kvw), tile),
            pl.BlockSpec((rows, kvw), tile),
            pl.BlockSpec(cblk, tile4),
            pl.BlockSpec(cblk, tile4),
            pl.BlockSpec((S, 4 * rows), lambda i: (0, 0)),
            pl.BlockSpec((1, S, 1), tile3),
            pl.BlockSpec((N_KV_HEADS, 1, 4 * rows), lambda i: (0, 0, 0)),
        ],
        out_specs=[
            pl.BlockSpec((rows, N_HEADS * HEAD_DIM), tile),
            pl.BlockSpec(cblk, tile4),
            pl.BlockSpec(cblk, tile4),
        ],
        out_shape=[
            jax.ShapeDtypeStruct((Bd * t, N_HEADS * HEAD_DIM), BF16),
            jax.ShapeDtypeStruct((Bd,) + cblk[1:], F32),
            jax.ShapeDtypeStruct((Bd,) + cblk[1:], F32),
        ],
        compiler_params=pltpu.CompilerParams(
            dimension_semantics=("arbitrary",), vmem_limit_bytes=VMEM_LIMIT),
        name="attn_sample",
    )(q, kdn, vdn, kn, vn, ck, cv, mask, kval, sinkrow)


def _rope_tables(pos):
    inv = ROPE_THETA ** (-jnp.arange(0, HEAD_DIM, 2, dtype=F32) / HEAD_DIM)
    reps = LANES // (HEAD_DIM // 2)
    sign = jnp.asarray(np.where((np.arange(LANES) % HEAD_DIM) < HEAD_DIM // 2, -1.0, 1.0), F32)
    ang = pos.astype(F32)[:, None] * inv[None, :]
    cos, sin = lax.optimization_barrier((jnp.cos(ang), jnp.sin(ang)))
    return jnp.tile(cos, (1, reps)), jnp.tile(sin, (1, reps)) * sign[None, :]


def _sink_rows(sinks, rows):
    s = (sinks.astype(F32) * LOG2E).reshape(N_KV_HEADS, 1, N_HEADS // N_KV_HEADS, 1)
    return jnp.broadcast_to(s, (N_KV_HEADS, 1, 4, rows)).reshape(N_KV_HEADS, 1, 4 * rows)


def kernel(x_prompt, x_sample, state_pool, cache_k, cache_v, sample_start, norm_mix, norm_ffn, norm_final,
           w_pool, ls_pool, w_qkv, b_qkv, sinks, w_o, b_o, w_rg, b_rg, w_re, b_re, w_gate, w_up, w_down):
    B, T, D = x_prompt.shape
    Bd, Td, _ = x_sample.shape
    row = lambda v: v.reshape(1, -1).astype(F32)

    wp = w_pool[0].astype(BF16)
    wqkv = w_qkv[0].astype(BF16)
    wo = w_o[0].astype(BF16)
    def router_rows(g_part, e_part):
        z = lambda n: jnp.zeros((n,) + g_part.shape[1:], F32)
        return jnp.concatenate([g_part, z(EXPERT_ROW0 - N_EXPERT_GROUPS), e_part,
                                z(ROUTER_ROWS - EXPERT_ROW0 - N_EXPERTS)], axis=0)

    wr = [router_rows(w_rg[l].T, w_re[l].T) for l in range(2)]
    br = [router_rows(b_rg[l][:, None], b_re[l][:, None]) for l in range(2)]

    start = sample_start.astype(jnp.int32)

    x1p, pool_p16 = _pool_prompt(x_prompt, jnp.zeros((B, HALO, D), F32), row(norm_mix[0]), wp, row(ls_pool[0]))
    pos_s = (start[:, None] + jnp.arange(Td, dtype=jnp.int32)[None, :]).reshape(-1)
    x1s, pool_s_t = _pool_sample(x_sample, jnp.transpose(state_pool[0], (1, 0, 2)), start[:, None],
                                 row(norm_mix[0]), wp, row(ls_pool[0]))
    x1s = x1s.reshape(Bd * Td, D)
    pool_s = jnp.transpose(pool_s_t, (1, 0, 2))
    pool_p = pool_p16[:, HALO - POOL_BUF:]

    moe0 = functools.partial(_moe, nf=row(norm_ffn[0]), wr=wr[0], br=br[0])
    x2p, *w0 = moe0(x1p.reshape(B * T, D), wg=w_gate, wu=w_up, wd=w_down, layer=0)
    x2s = moe0(x1s, wg=w0[0], wu=w0[1], wd=w0[2], layer=None)

    cos_p, sin_p = _rope_tables(jnp.arange(T, dtype=jnp.int32))
    cos_s, sin_s = _rope_tables(pos_s)
    g1 = row(norm_mix[1])
    tmq = QKV_ROWS
    keep = min(WINDOW, T)
    qp, kdp, vdp, kp, vp = _qkv(x2p, g1, wqkv, row(b_qkv[0]), cos_p, sin_p, tmq, T // tmq, keep)
    qs, kds, vds, ks, vs = _qkv(x2s, g1, wqkv, row(b_qkv[0]), cos_s, sin_s, tmq, 1, tmq)

    op, *w1 = _attn_prompt(qp, kdp, vdp, _sink_rows(sinks[0], WINDOW), w_gate, w_up, w_down, 1, B, T)

    bs = ATTN_SAMPLE_SEQS
    rows = bs * Td
    W = cache_k.shape[2]
    qrow = np.arange(rows)
    ccol = np.arange(bs * W)
    ncol = np.arange(rows)
    samp_q, t_q = qrow // Td, qrow % Td
    m_cache = (samp_q[:, None] == (ccol // W)[None, :]) & ((ccol % W)[None, :] > t_q[:, None])
    m_new = (samp_q[:, None] == (ncol // Td)[None, :]) & ((ncol % Td)[None, :] <= t_q[:, None])
    amask = jnp.asarray(np.tile(np.concatenate([m_cache, m_new], axis=1).T, (1, 4)), F32)
    kv_cache = (jnp.arange(W, dtype=jnp.int32)[None, :] >= (W - start)[:, None]).reshape(Bd // bs, bs * W)
    kval = jnp.concatenate([kv_cache, jnp.ones((Bd // bs, rows), bool)], axis=1).astype(F32)
    kval = kval.reshape(Bd // bs, bs * W + rows, 1)
    to_stored = lambda c: jnp.transpose(c[0], (0, 2, 3, 1))
    from_stored = lambda c: jnp.transpose(c, (0, 3, 1, 2))[None]
    osamp, nk_s, nv_s = _attn_sample(
        qs, kds, vds, ks, vs, to_stored(cache_k), to_stored(cache_v),
        amask, kval, _sink_rows(sinks[0], rows), bs, Td)

    moe1 = functools.partial(_moe, nf=row(norm_ffn[1]), wr=wr[1], br=br[1], final=row(norm_final))
    yp = moe1(x2p, wg=w1[0], wu=w1[1], wd=w1[2], layer=None, oproj=(op, wo, row(b_o[0])))
    ys = moe1(x2s, wg=w1[0], wu=w1[1], wd=w1[2], layer=None, oproj=(osamp, wo, row(b_o[0])))

    k_p = kp.reshape(1, B, keep, N_KV_HEADS, HEAD_DIM)
    v_p = vp.reshape(1, B, keep, N_KV_HEADS, HEAD_DIM)
    return (yp.reshape(B, T, D), ys.reshape(Bd, Td, D), pool_p[None], k_p, v_p, pool_s[None],
            from_stored(nk_s), from_stored(nv_s))
```

```python
import functools

import jax
import jax.numpy as jnp
import numpy as np
from jax import lax
from jax.experimental import pallas as pl
from jax.experimental.pallas import tpu as pltpu

F32 = jnp.float32
BF16 = jnp.bfloat16

D_MODEL = 1024
POOL_WINDOWS = (2, 4, 8, 16)
POOL_GROUP_DIM = 256
POOL_BUF = 15
HALO = 16
HEAD_DIM = 64
N_HEADS = 16
N_KV_HEADS = 4
WINDOW = 128
ROPE_THETA = 10000.0
N_EXPERT_GROUPS = 4
EXPERTS_PER_GROUP = 4
N_EXPERTS = 16
D_EXPERT = 256
RMS_EPS = 1e-6
LANES = 128
ROUTER_ROWS = 32
EXPERT_ROW0 = 8
CAST_SLOTS = 6

POOL_PROMPT_ROWS = 1024
POOL_SAMPLE_SEQS = 32
MOE_ROWS = 512
MOE_BLOCK_ROWS = 144
QKV_ROWS = 1024
ATTN_PROMPT_ROWS = 1024
ATTN_SAMPLE_SEQS = 8
VMEM_LIMIT = 56 * 1024 * 1024
LOG2E = 1.4426950408889634
Q_SCALE = LOG2E * HEAD_DIM ** -0.5


def _rms(x, g):
    ms = jnp.mean(x * x, axis=-1, keepdims=True)
    return x * lax.rsqrt(ms + RMS_EPS) * g


def _pool_project(h, wins, pos, wp_ref, ls):
    G = POOL_GROUP_DIM
    outs = []
    for g, w in enumerate(POOL_WINDOWS):
        cnt = jnp.minimum(w, pos + 1).astype(F32)
        d = wins[g] / cnt - h[:, g * G:(g + 1) * G]
        outs.append(jnp.dot(d.astype(BF16), wp_ref[g], preferred_element_type=F32))
    return jnp.concatenate(outs, axis=-1) * ls


POOL_BLOCK = 128


def _pool_band():
    t = np.arange(POOL_BLOCK)[:, None] + HALO
    k = np.arange(POOL_BLOCK + HALO)[None, :]
    return jnp.asarray(np.stack([(k <= t) & (k > t - w) for w in POOL_WINDOWS]), BF16)


def _pool_prompt_kernel(x_ref, buf_ref, g_ref, wp_ref, ls_ref, band_ref, o_ref, nb_ref, hc, *, tq):
    t = pl.program_id(1)
    G = POOL_GROUP_DIM

    @pl.when(t == 0)
    def _():
        hc[pl.ds(0, HALO), :] = buf_ref[0]

    @pl.when(t > 0)
    def _():
        hc[pl.ds(0, HALO), :] = hc[pl.ds(tq, HALO), :]

    x = x_ref[0]
    h = _rms(x, g_ref[...])
    hc[pl.ds(HALO, tq), :] = h
    hb = hc[...].astype(BF16)
    wins = []
    for g in range(len(POOL_WINDOWS)):
        blocks = [jnp.dot(band_ref[g], hb[b * POOL_BLOCK:(b + 1) * POOL_BLOCK + HALO, g * G:(g + 1) * G],
                          preferred_element_type=F32) for b in range(tq // POOL_BLOCK)]
        wins.append(jnp.concatenate(blocks, axis=0))
    pos = t * tq + lax.broadcasted_iota(jnp.int32, (tq, 1), 0)
    o_ref[0] = x + _pool_project(h, wins, pos, wp_ref, ls_ref[...])

    @pl.when(t == pl.num_programs(1) - 1)
    def _():
        nb_ref[0] = hc[pl.ds(tq, HALO), :]


def _pool_prompt(x, buf16, g, wp, ls, tq=POOL_PROMPT_ROWS):
    B, T, D = x.shape
    return pl.pallas_call(
        functools.partial(_pool_prompt_kernel, tq=tq),
        grid=(B, T // tq),
        in_specs=[
            pl.BlockSpec((1, tq, D), lambda b, t: (b, t, 0)),
            pl.BlockSpec((1, HALO, D), lambda b, t: (b, 0, 0)),
            pl.BlockSpec((1, D), lambda b, t: (0, 0)),
            pl.BlockSpec((4, POOL_GROUP_DIM, POOL_GROUP_DIM), lambda b, t: (0, 0, 0)),
            pl.BlockSpec((1, D), lambda b, t: (0, 0)),
            pl.BlockSpec((4, POOL_BLOCK, POOL_BLOCK + HALO), lambda b, t: (0, 0, 0)),
        ],
        out_specs=[
            pl.BlockSpec((1, tq, D), lambda b, t: (b, t, 0)),
            pl.BlockSpec((1, HALO, D), lambda b, t: (b, 0, 0)),
        ],
        out_shape=[jax.ShapeDtypeStruct((B, T, D), F32), jax.ShapeDtypeStruct((B, HALO, D), F32)],
        scratch_shapes=[pltpu.VMEM((HALO + tq, D), F32)],
        compiler_params=pltpu.CompilerParams(
            dimension_semantics=("arbitrary", "arbitrary"), vmem_limit_bytes=VMEM_LIMIT),
        name="pool_prompt",
    )(x, buf16, g, wp, ls, _pool_band())


def _pool_sample_kernel(x_ref, st_ref, start_ref, g_ref, wp_ref, ls_ref, o_ref, ns_ref, *, bs, t):
    G = POOL_GROUP_DIM
    xs = [x_ref[:, i, :] for i in range(t)]
    hn = [_rms(x, g_ref[...]) for x in xs]
    hist = [st_ref[r] for r in range(POOL_BUF)] + hn

    def doubled(prev, lag, lo):
        out = [None] * len(prev)
        for i in range(len(prev)):
            if i >= lag and prev[i] is not None and prev[i - lag] is not None:
                out[i] = prev[i][:, lo:] + prev[i - lag][:, lo:]
        return out

    s2 = doubled(hist, 1, 0)
    s4 = doubled(s2, 2, G)
    s8 = doubled(s4, 4, G)
    s16 = doubled(s8, 8, G)
    wins = (s2, s4, s8, s16)
    start = start_ref[...]
    ds = [[] for _ in POOL_WINDOWS]
    for i in range(t):
        for g, w in enumerate(POOL_WINDOWS):
            cnt = jnp.minimum(w, start + (i + 1)).astype(F32)
            ds[g].append(wins[g][POOL_BUF + i][:, :G] / cnt - hn[i][:, g * G:(g + 1) * G])
    ys = [jnp.dot(jnp.concatenate(ds[g], axis=0).astype(BF16), wp_ref[g], preferred_element_type=F32)
          for g in range(len(POOL_WINDOWS))]
    y = jnp.concatenate(ys, axis=-1) * ls_ref[...]
    for i in range(t):
        o_ref[:, i, :] = xs[i] + y[i * bs:(i + 1) * bs]
    for r in range(POOL_BUF):
        ns_ref[r] = hist[r + t]


def _pool_sample(x, st, start, g, wp, ls, bs=POOL_SAMPLE_SEQS):
    Bd, t, D = x.shape
    return pl.pallas_call(
        functools.partial(_pool_sample_kernel, bs=bs, t=t),
        grid=(Bd // bs,),
        in_specs=[
            pl.BlockSpec((bs, t, D), lambda i: (i, 0, 0)),
            pl.BlockSpec((POOL_BUF, bs, D), lambda i: (0, i, 0)),
            pl.BlockSpec((bs, 1), lambda i: (i, 0)),
            pl.BlockSpec((1, D), lambda i: (0, 0)),
            pl.BlockSpec((4, POOL_GROUP_DIM, POOL_GROUP_DIM), lambda i: (0, 0, 0)),
            pl.BlockSpec((1, D), lambda i: (0, 0)),
        ],
        out_specs=[pl.BlockSpec((bs, t, D), lambda i: (i, 0, 0)),
                   pl.BlockSpec((POOL_BUF, bs, D), lambda i: (0, i, 0))],
        out_shape=[jax.ShapeDtypeStruct((Bd, t, D), F32), jax.ShapeDtypeStruct((POOL_BUF, Bd, D), F32)],
        compiler_params=pltpu.CompilerParams(
            dimension_semantics=("arbitrary",), vmem_limit_bytes=VMEM_LIMIT),
        name="pool_sample",
    )(x, st, start, g, wp, ls)


def _route(lg):
    R = lg.shape[1]
    big = jnp.float32(1 << 20)
    neg = jnp.float32(-jnp.inf)
    r8 = lax.broadcasted_iota(jnp.int32, (8, R), 0).astype(F32)
    r16 = lax.broadcasted_iota(jnp.int32, (N_EXPERTS, R), 0).astype(F32)
    is_g = r8 < N_EXPERT_GROUPS
    gl = jnp.where(is_g, lg[0:8], neg)
    m = jnp.max(gl, axis=0, keepdims=True)
    gidx = jnp.min(jnp.where(gl == m, r8, big), axis=0, keepdims=True)
    z = jnp.sum(jnp.where(is_g, jnp.exp(gl - m), 0.0), axis=0, keepdims=True)
    gw = 1.0 / z
    lo = gidx * EXPERTS_PER_GROUP
    in_grp = (r16 >= lo) & (r16 < lo + EXPERTS_PER_GROUP)
    el = jnp.where(in_grp, lg[EXPERT_ROW0:EXPERT_ROW0 + N_EXPERTS], neg)
    v1 = jnp.max(el, axis=0, keepdims=True)
    i1 = jnp.min(jnp.where(el == v1, r16, big), axis=0, keepdims=True)
    el2 = jnp.where(r16 == i1, neg, el)
    v2 = jnp.max(el2, axis=0, keepdims=True)
    i2 = jnp.min(jnp.where(el2 == v2, r16, big), axis=0, keepdims=True)
    t = jnp.exp(v2 - v1)
    w1 = 1.0 / (1.0 + t)
    w2 = t * w1
    e8 = jnp.where(is_g, r8, r8 - EXPERTS_PER_GROUP)
    cw = gw * (jnp.where(e8 == i1 - lo, w1, 0.0) + jnp.where(e8 == i2 - lo, w2, 0.0))
    return gidx, cw


def _moe_kernel(*refs, has_oproj, has_final, cast_layer, tm, bm):
    it = iter(refs)
    x_ref = next(it)
    if has_oproj:
        oin_ref, wo_ref, bo_ref = next(it), next(it), next(it)
    nf_ref, wr_ref, br_ref, wg_in, wu_in, wd_in = (next(it) for _ in range(6))
    if has_final:
        fn_ref = next(it)
    out_ref = next(it)
    if cast_layer is not None:
        wg_out, wu_out, wd_out = (next(it) for _ in range(3))
    hs, cs, ys, sel_s, earlier = (next(it) for _ in range(5))
    if cast_layer is None:
        wg_ref, wu_ref, wd_ref = wg_in, wu_in, wd_in
    else:
        wg_ref, wu_ref, wd_ref, stage_gu, stage_d, load_sem, store_sem = (next(it) for _ in range(7))
        pairs = ((wg_in, wg_ref, wg_out, stage_gu), (wu_in, wu_ref, wu_out, stage_gu), (wd_in, wd_ref, wd_out, stage_d))

        def store_copy(k):
            return pltpu.make_async_copy(pairs[k][1], pairs[k][2], store_sem.at[k])

        @pl.when(pl.program_id(0) == 0)
        def _():
            ahead = stage_gu.shape[0] - 1
            for src, dst, _, stage in pairs:
                def load(e, src=src, stage=stage):
                    slot = e % (ahead + 1)
                    return pltpu.make_async_copy(src.at[cast_layer, e], stage.at[slot], load_sem.at[slot])

                for e in range(ahead):
                    load(e).start()
                for e in range(N_EXPERTS):
                    if e + ahead < N_EXPERTS:
                        load(e + ahead).start()
                    load(e).wait()
                    dst[e] = stage[e % (ahead + 1)].astype(BF16)
            for k in range(len(pairs)):
                store_copy(k).start()

        @pl.when(pl.program_id(0) == pl.num_programs(0) - 1)
        def _():
            for k in range(len(pairs)):
                store_copy(k).wait()
    min_blk = -(-tm // bm)
    nblk = tm // bm + N_EXPERT_GROUPS
    cap = nblk * bm
    assert bm % 16 == 0
    nt = (((1,), (1,)), ((), ()))

    @pl.when(pl.program_id(0) == 0)
    def _():
        ri = lax.broadcasted_iota(jnp.int32, (tm, tm), 0)
        ci = lax.broadcasted_iota(jnp.int32, (tm, tm), 1)
        earlier[...] = jnp.where(ri < ci, 1.0, 0.0).astype(BF16)

    x = x_ref[...]
    if has_oproj:
        x = x + jnp.dot(oin_ref[...], wo_ref[...], preferred_element_type=F32) + bo_ref[...]
    if has_oproj:
        out_ref[...] = x
    h = _rms(x, nf_ref[...])
    hb = h.astype(BF16)
    h_lo = (h - hb.astype(F32)).astype(BF16)
    wr = wr_ref[...]
    both = lax.dot_general(wr, hb, nt, preferred_element_type=F32)
    lg = (both[0:ROUTER_ROWS] + both[ROUTER_ROWS:]
          + lax.dot_general(wr[0:ROUTER_ROWS], h_lo, nt, preferred_element_type=F32)) + br_ref[...]
    gidx, cw = _route(lg)
    r8 = lax.broadcasted_iota(jnp.int32, (8, tm), 0)
    cw_hi = cw.astype(BF16).astype(F32)
    cwt = jnp.concatenate([jnp.where(r8 < EXPERTS_PER_GROUP, cw_hi, cw - cw_hi),
                           jnp.zeros((LANES - 8, tm), F32)], axis=0).astype(BF16)

    oh = jnp.where(r8.astype(F32) == gidx, 1.0, 0.0)
    cnt_before = jnp.dot(oh.astype(BF16), earlier[...], preferred_element_type=F32)
    pos_row = jnp.sum(oh * cnt_before, axis=0, keepdims=True)
    start_blk, n_blk = [], []
    off = jnp.int32(0)
    for g in range(N_EXPERT_GROUPS):
        in_g = gidx == g
        n_g = jnp.sum(jnp.where(in_g, 1.0, 0.0)).astype(jnp.int32)
        blocks = sum(jnp.where(n_g > k * bm, 1, 0) for k in range(min_blk))
        start_blk.append(off)
        n_blk.append(blocks)
        pos_row = pos_row + jnp.where(in_g, (off * bm).astype(F32), 0.0)
        off = off + blocks

    total_blk = off
    main = min_blk * bm
    extra = ((main, bm, total_blk > min_blk), (main + bm, cap - main - bm, total_blk > min_blk + 1))

    def sort_rows(r0, n):
        sub = (r0 + lax.broadcasted_iota(jnp.int32, (n, tm), 0)).astype(F32)
        sel = jnp.where(pos_row == sub, 1.0, 0.0).astype(BF16)
        sel_s[r0:r0 + n, :] = sel
        hs[r0:r0 + n, :] = jnp.dot(sel, hb, preferred_element_type=F32).astype(BF16)
        cs[r0:r0 + n, :] = lax.dot_general(sel, cwt, nt, preferred_element_type=F32)

    sort_rows(0, main)
    for r0, n, needed in extra:
        pl.when(needed)(functools.partial(sort_rows, r0, n))
    ys[min_blk * bm:, :] = jnp.zeros((cap - min_blk * bm, D_MODEL), BF16)

    for g in range(N_EXPERT_GROUPS):
        wd = wd_ref[g * EXPERTS_PER_GROUP:(g + 1) * EXPERTS_PER_GROUP].reshape(
            EXPERTS_PER_GROUP * D_EXPERT, D_MODEL)

        def block(b, carry, g=g, wd=wd):
            rows = pl.ds(pl.multiple_of(b * bm, 16), bm)
            hblk = hs[rows, :]
            cblk = cs[rows, :]
            ln = lax.broadcasted_iota(jnp.int32, cblk.shape, 1)
            parts = []
            for e in range(EXPERTS_PER_GROUP):
                ce = jnp.sum(jnp.where((ln == e) | (ln == e + EXPERTS_PER_GROUP), cblk, 0.0),
                             axis=-1, keepdims=True)
                gt = jnp.dot(hblk, wg_ref[g * EXPERTS_PER_GROUP + e], preferred_element_type=F32)
                up = jnp.dot(hblk, wu_ref[g * EXPERTS_PER_GROUP + e], preferred_element_type=F32)
                a = gt / (1.0 + jnp.exp(-gt)) * up * ce
                parts.append(a.astype(BF16))
            a_all = jnp.concatenate(parts, axis=-1)
            ys[rows, :] = jnp.dot(a_all, wd, preferred_element_type=F32).astype(BF16)
            return carry

        lax.fori_loop(start_blk[g], start_blk[g] + n_blk[g], block, 0)

    def unsorted(r0, n):
        return lax.dot_general(sel_s[r0:r0 + n, :], ys[r0:r0 + n, :], (((0,), (0,)), ((), ())),
                               preferred_element_type=F32)

    def unsort_more(r0, n):
        out_ref[...] += unsorted(r0, n)

    out_ref[...] = (out_ref[...] if has_oproj else x_ref[...]) + unsorted(0, main)
    for r0, n, needed in extra:
        pl.when(needed)(functools.partial(unsort_more, r0, n))
    if has_final:
        out_ref[...] = _rms(out_ref[...], fn_ref[...])


def _moe(x, nf, wr, br, wg, wu, wd, layer, oproj=None, final=None, tm=MOE_ROWS, bm=MOE_BLOCK_ROWS):
    N, D = x.shape
    has_oproj = oproj is not None
    has_final = final is not None
    const2 = lambda i: (0, 0)
    const3 = lambda i: (0, 0, 0)
    cast = layer is not None
    tile = lambda i: (i, 0)
    once = pl.Buffered(1)
    args = [x]
    in_specs = [pl.BlockSpec((tm, D), tile)]
    if has_oproj:
        o, wo, bo = oproj
        args += [o, wo, bo]
        in_specs += [pl.BlockSpec((tm, D), tile), pl.BlockSpec((D, D), const2, pipeline_mode=once),
                     pl.BlockSpec((1, D), const2)]
    wr_hi = wr.astype(BF16)
    wr_lo = (wr - wr_hi.astype(F32)).astype(BF16)
    args += [nf, jnp.concatenate([wr_hi, wr_lo], axis=0), br, wg, wu, wd]
    in_specs += [
        pl.BlockSpec((1, D), const2),
        pl.BlockSpec((2 * ROUTER_ROWS, D), const2, pipeline_mode=once),
        pl.BlockSpec((ROUTER_ROWS, 1), const2),
    ]
    gu_shape, d_shape = (N_EXPERTS, D, D_EXPERT), (N_EXPERTS, D_EXPERT, D)
    if cast:
        in_specs += [pl.BlockSpec(memory_space=pl.ANY)] * 3
    else:
        in_specs += [pl.BlockSpec(gu_shape, const3, pipeline_mode=once), pl.BlockSpec(gu_shape, const3, pipeline_mode=once),
                     pl.BlockSpec(d_shape, const3, pipeline_mode=once)]
    if has_final:
        args.append(final)
        in_specs.append(pl.BlockSpec((1, D), const2))
    cap = (tm // bm + N_EXPERT_GROUPS) * bm
    out_specs = [pl.BlockSpec((tm, D), tile)]
    out_shape = [jax.ShapeDtypeStruct((N, D), F32)]
    scratch = [
        pltpu.VMEM((cap, D), BF16),
        pltpu.VMEM((cap, LANES), F32),
        pltpu.VMEM((cap, D), BF16),
        pltpu.VMEM((cap, tm), BF16),
        pltpu.VMEM((tm, tm), BF16),
    ]
    if cast:
        out_specs += [pl.BlockSpec(memory_space=pl.ANY)] * 3
        out_shape += [jax.ShapeDtypeStruct(gu_shape, BF16)] * 2 + [jax.ShapeDtypeStruct(d_shape, BF16)]
        scratch += [pltpu.VMEM(gu_shape, BF16), pltpu.VMEM(gu_shape, BF16), pltpu.VMEM(d_shape, BF16),
                    pltpu.VMEM((CAST_SLOTS,) + gu_shape[1:], F32), pltpu.VMEM((CAST_SLOTS,) + d_shape[1:], F32),
                    pltpu.SemaphoreType.DMA((CAST_SLOTS,)), pltpu.SemaphoreType.DMA((3,))]
    res = pl.pallas_call(
        functools.partial(_moe_kernel, has_oproj=has_oproj, has_final=has_final, cast_layer=layer, tm=tm, bm=bm),
        grid=(N // tm,),
        in_specs=in_specs,
        out_specs=out_specs,
        out_shape=out_shape,
        scratch_shapes=scratch,
        compiler_params=pltpu.CompilerParams(
            dimension_semantics=("arbitrary",), vmem_limit_bytes=VMEM_LIMIT),
        name="moe",
    )(*args)
    return res if cast else res[0]


def _swap_halves(x, low):
    return jnp.where(low, pltpu.roll(x, LANES - 32, axis=1), pltpu.roll(x, 32, axis=1))


def _qkv_kernel(x_ref, g_ref, w_ref, b_ref, cos_ref, sin_ref, q_ref, kd_ref, vd_ref, k_ref, v_ref, *, keep):
    tm = x_ref.shape[0]
    h = _rms(x_ref[...], g_ref[...]).astype(BF16)
    cos = cos_ref[...]
    sin = sin_ref[...]
    lane = lax.broadcasted_iota(jnp.int32, cos.shape, 1)
    low32 = (lane % HEAD_DIM) < (HEAD_DIM // 2)
    low64 = lane < HEAD_DIM
    nq = N_HEADS * HEAD_DIM
    nk = N_KV_HEADS * HEAD_DIM
    cw = 2 * LANES

    def project(c0):
        y = jnp.dot(h, w_ref[:, c0:c0 + cw], preferred_element_type=F32) + b_ref[:, c0:c0 + cw]
        return [y[:, i * LANES:(i + 1) * LANES] for i in range(cw // LANES)]

    def rope(c):
        return c * cos + _swap_halves(c, low32) * sin

    def dup(c):
        r = pltpu.roll(c, HEAD_DIM, axis=1)
        return jnp.where(low64, c, r), jnp.where(low64, r, c)

    for c0 in range(0, nq, cw):
        for i, c in enumerate(project(c0)):
            lanes = slice(c0 + i * LANES, c0 + (i + 1) * LANES)
            q_ref[:, lanes] = (rope(c) * Q_SCALE).astype(BF16)
    for c0 in range(0, nk, cw):
        for (src, rot, f_ref, d_ref) in ((nq, True, k_ref, kd_ref), (nq + nk, False, v_ref, vd_ref)):
            for i, c in enumerate(project(src + c0)):
                j = c0 // LANES + i
                c = rope(c) if rot else c
                f_ref[:, j * LANES:(j + 1) * LANES] = c[tm - keep:]
                d0, d1 = dup(c)
                d_ref[:, 2 * j * LANES:(2 * j + 1) * LANES] = d0.astype(BF16)
                d_ref[:, (2 * j + 1) * LANES:(2 * j + 2) * LANES] = d1.astype(BF16)


def _qkv(x, g, w, b, cos, sin, tm, n_pos_tiles, keep):
    N, D = x.shape
    nq, nk = N_HEADS * HEAD_DIM, N_KV_HEADS * HEAD_DIM
    const = lambda i: (0, 0)
    tile = lambda i: (i, 0)
    ptile = lambda i: (i % n_pos_tiles, 0)
    seq = lambda i: (i // n_pos_tiles, 0)
    n_seq = N // (tm * n_pos_tiles)
    return pl.pallas_call(
        functools.partial(_qkv_kernel, keep=keep),
        grid=(N // tm,),
        in_specs=[
            pl.BlockSpec((tm, D), tile),
            pl.BlockSpec((1, D), const),
            pl.BlockSpec((D, nq + 2 * nk), const),
            pl.BlockSpec((1, nq + 2 * nk), const),
            pl.BlockSpec((tm, LANES), ptile),
            pl.BlockSpec((tm, LANES), ptile),
        ],
        out_specs=[
            pl.BlockSpec((tm, nq), tile),
            pl.BlockSpec((tm, 2 * nk), tile),
            pl.BlockSpec((tm, 2 * nk), tile),
            pl.BlockSpec((keep, nk), seq),
            pl.BlockSpec((keep, nk), seq),
        ],
        out_shape=[
            jax.ShapeDtypeStruct((N, nq), BF16),
            jax.ShapeDtypeStruct((N, 2 * nk), BF16),
            jax.ShapeDtypeStruct((N, 2 * nk), BF16),
            jax.ShapeDtypeStruct((n_seq * keep, nk), F32),
            jax.ShapeDtypeStruct((n_seq * keep, nk), F32),
        ],
        compiler_params=pltpu.CompilerParams(
            dimension_semantics=("arbitrary",), vmem_limit_bytes=VMEM_LIMIT),
        name="qkv",
    )(x, g, w, b, cos, sin)


def _stack_heads(qc_list):
    lane = lax.broadcasted_iota(jnp.int32, qc_list[0].shape, 1)
    low = lane < HEAD_DIM
    zero = jnp.zeros_like(qc_list[0])
    stack = []
    for qc in qc_list:
        stack.append(jnp.where(low, qc, zero))
        stack.append(jnp.where(low, zero, qc))
    return jnp.concatenate(stack, axis=0)


def _softmax_keys_on_rows(s, sinkrow):
    m = jnp.maximum(jnp.max(s, axis=0, keepdims=True), sinkrow)
    p = jnp.exp2(s - m)
    den = jnp.sum(p, axis=0, keepdims=True) + jnp.exp2(sinkrow - m)
    return (p * (1.0 / den)).astype(BF16)


def _unstack_heads(o, R):
    lowf = lax.broadcasted_iota(jnp.int32, (R, LANES), 1) < HEAD_DIM
    return (jnp.where(lowf, o[0:R], o[R:2 * R]), jnp.where(lowf, o[2 * R:3 * R], o[3 * R:4 * R]))


NT_DIMS = (((1,), (1,)), ((), ()))
TN_DIMS = (((0,), (0,)), ((), ()))
TT_DIMS = (((0,), (1,)), ((), ()))


def _attend(qc_list, kd, vd, bias, sinkrow):
    qs = _stack_heads(qc_list)
    s = lax.dot_general(kd, qs, NT_DIMS, preferred_element_type=F32) + bias
    p = _softmax_keys_on_rows(s, sinkrow)
    o = lax.dot_general(p, vd, TN_DIMS, preferred_element_type=F32)
    return _unstack_heads(o, qc_list[0].shape[0])


def _attn_prompt_kernel(q_ref, kp_ref, kc_ref, vp_ref, vc_ref, sink_ref, wg_in, wu_in, wd_in,
                        o_ref, wg_out, wu_out, wd_out, f32_g, f32_u, f32_d, b16_g, b16_u, b16_d, load_sem, store_sem,
                        *, tq, cast_layer):
    n = pl.program_id(1)
    step = pl.program_id(0) * pl.num_programs(1) + n
    kinds = ((wg_in, f32_g, b16_g, wg_out), (wu_in, f32_u, b16_u, wu_out), (wd_in, f32_d, b16_d, wd_out))

    def load(k, e):
        return pltpu.make_async_copy(kinds[k][0].at[cast_layer, e], kinds[k][1], load_sem.at[k])

    def store(k, e):
        return pltpu.make_async_copy(kinds[k][2], kinds[k][3].at[e], store_sem.at[k])

    @pl.when(step == 0)
    def _():
        for k in range(3):
            load(k, 0).start()

    for k in range(3):
        load(k, step).wait()

    @pl.when(step > 0)
    def _():
        for k in range(3):
            store(k, step - 1).wait()

    for k in range(3):
        kinds[k][2][...] = kinds[k][1][...].astype(BF16)
        store(k, step).start()

    @pl.when(step < N_EXPERTS - 1)
    def _():
        for k in range(3):
            load(k, step + 1).start()

    @pl.when(step == N_EXPERTS - 1)
    def _():
        for k in range(3):
            store(k, step).wait()

    B = WINDOW
    kj = lax.broadcasted_iota(jnp.int32, (2 * B, B), 0)
    qi = lax.broadcasted_iota(jnp.int32, (2 * B, B), 1)
    rel = B + qi - kj
    band = (rel >= 0) & (rel < WINDOW)
    neg = jnp.float32(-jnp.inf)
    band_bias = jnp.concatenate([jnp.where(band, 0.0, neg)] * 4, axis=1)
    first_bias = jnp.concatenate([jnp.where(band & ((kj >= B) | (n > 0)), 0.0, neg)] * 4, axis=1)
    for j in range(tq // B):
        bias = first_bias if j == 0 else band_bias
        for g in range(N_KV_HEADS):
            qc = [q_ref[j * B:(j + 1) * B, (2 * g + i) * LANES:(2 * g + i + 1) * LANES] for i in range(2)]
            lanes = slice(g * LANES, (g + 1) * LANES)
            if j == 0:
                kg = jnp.concatenate([kp_ref[:, lanes], kc_ref[0:B, lanes]], axis=0)
                vg = jnp.concatenate([vp_ref[:, lanes], vc_ref[0:B, lanes]], axis=0)
            else:
                kg = kc_ref[(j - 1) * B:(j + 1) * B, lanes]
                vg = vc_ref[(j - 1) * B:(j + 1) * B, lanes]
            o0, o1 = _attend(qc, kg, vg, bias, sink_ref[g])
            o_ref[j * B:(j + 1) * B, 2 * g * LANES:(2 * g + 1) * LANES] = o0.astype(BF16)
            o_ref[j * B:(j + 1) * B, (2 * g + 1) * LANES:(2 * g + 2) * LANES] = o1.astype(BF16)


def _attn_prompt(q, kd, vd, sinkrow, wg, wu, wd, layer, batch, seq, tq=ATTN_PROMPT_ROWS):
    N = q.shape[0]
    nt = seq // tq
    assert batch * nt == N_EXPERTS
    gu_shape, d_shape = (N_EXPERTS, D_MODEL, D_EXPERT), (N_EXPERTS, D_EXPERT, D_MODEL)
    r = tq // WINDOW
    cur = lambda b, n: (b * nt + n, 0)
    prev = lambda b, n: (jnp.maximum((b * nt + n) * r - 1, 0), 0)
    kvw = 2 * N_KV_HEADS * HEAD_DIM
    return pl.pallas_call(
        functools.partial(_attn_prompt_kernel, tq=tq, cast_layer=layer),
        grid=(batch, nt),
        in_specs=[
            pl.BlockSpec((tq, N_HEADS * HEAD_DIM), cur),
            pl.BlockSpec((WINDOW, kvw), prev),
            pl.BlockSpec((tq, kvw), cur),
            pl.BlockSpec((WINDOW, kvw), prev),
            pl.BlockSpec((tq, kvw), cur),
            pl.BlockSpec((N_KV_HEADS, 1, 4 * WINDOW), lambda b, n: (0, 0, 0)),
        ] + [pl.BlockSpec(memory_space=pl.ANY)] * 3,
        out_specs=[pl.BlockSpec((tq, N_HEADS * HEAD_DIM), cur)] + [pl.BlockSpec(memory_space=pl.ANY)] * 3,
        out_shape=[jax.ShapeDtypeStruct((N, N_HEADS * HEAD_DIM), BF16), jax.ShapeDtypeStruct(gu_shape, BF16),
                   jax.ShapeDtypeStruct(gu_shape, BF16), jax.ShapeDtypeStruct(d_shape, BF16)],
        scratch_shapes=[pltpu.VMEM(gu_shape[1:], F32), pltpu.VMEM(gu_shape[1:], F32), pltpu.VMEM(d_shape[1:], F32),
                        pltpu.VMEM(gu_shape[1:], BF16), pltpu.VMEM(gu_shape[1:], BF16), pltpu.VMEM(d_shape[1:], BF16),
                        pltpu.SemaphoreType.DMA((3,)), pltpu.SemaphoreType.DMA((3,))],
        compiler_params=pltpu.CompilerParams(
            dimension_semantics=("arbitrary", "arbitrary"), vmem_limit_bytes=VMEM_LIMIT),
        name="attn_prompt",
    )(q, kd, kd, vd, vd, sinkrow, wg, wu, wd)


def _attn_sample_kernel(q_ref, kdn_ref, vdn_ref, kn_ref, vn_ref, ck_ref, cv_ref, mask_ref, kval_ref, sink_ref,
                        o_ref, nk_ref, nv_ref, *, bs, t):
    W = WINDOW
    rows = bs * t
    bias = jnp.where((mask_ref[...] > 0) & (kval_ref[0] > 0), 0.0, -jnp.inf).astype(F32)
    lane = lax.broadcasted_iota(jnp.int32, (HEAD_DIM, W), 1)
    fresh = lane >= W - t

    def new_cols(n_ref):
        n = jnp.concatenate([n_ref[...], jnp.zeros((LANES - rows, N_KV_HEADS * HEAD_DIM), F32)], axis=0)
        nt_ = jnp.transpose(n)
        return [pltpu.roll(nt_, (W - t - s * t) % LANES, axis=1) for s in range(bs)]

    k_cols, v_cols = new_cols(kn_ref), new_cols(vn_ref)
    for g in range(N_KV_HEADS):
        hd = slice(g * HEAD_DIM, (g + 1) * HEAD_DIM)
        kt = jnp.concatenate([ck_ref[s, g] for s in range(bs)], axis=1).astype(BF16)
        vt = jnp.concatenate([cv_ref[s, g] for s in range(bs)], axis=1).astype(BF16)
        kt2 = jnp.concatenate([kt, kt], axis=0)
        vt2 = jnp.concatenate([vt, vt], axis=0)
        qs = _stack_heads([q_ref[:, (2 * g + i) * LANES:(2 * g + i + 1) * LANES] for i in range(2)])
        s_old = lax.dot_general(kt2, qs, TT_DIMS, preferred_element_type=F32)
        s_new = lax.dot_general(kdn_ref[:, g * LANES:(g + 1) * LANES], qs, NT_DIMS, preferred_element_type=F32)
        p = _softmax_keys_on_rows(jnp.concatenate([s_old, s_new], axis=0) + bias, sink_ref[g])
        o = (lax.dot_general(p[:bs * W], vt2, TT_DIMS, preferred_element_type=F32)
             + lax.dot_general(p[bs * W:], vdn_ref[:, g * LANES:(g + 1) * LANES], TN_DIMS,
                               preferred_element_type=F32))
        o0, o1 = _unstack_heads(o, rows)
        o_ref[:, 2 * g * LANES:(2 * g + 1) * LANES] = o0.astype(BF16)
        o_ref[:, (2 * g + 1) * LANES:(2 * g + 2) * LANES] = o1.astype(BF16)
        for s in range(bs):
            nk_ref[s, g] = jnp.where(fresh, k_cols[s][hd], pltpu.roll(ck_ref[s, g], W - t, axis=1))
            nv_ref[s, g] = jnp.where(fresh, v_cols[s][hd], pltpu.roll(cv_ref[s, g], W - t, axis=1))


def _attn_sample(q, kdn, vdn, kn, vn, ck, cv, mask, kval, sinkrow, bs, t):
    Bd = ck.shape[0]
    rows = bs * t
    kvw = N_KV_HEADS * HEAD_DIM
    S = bs * WINDOW + rows
    tile = lambda i: (i, 0)
    tile3 = lambda i: (i, 0, 0)
    tile4 = lambda i: (i, 0, 0, 0)
    cblk = (bs, N_KV_HEADS, HEAD_DIM, WINDOW)
    return pl.pallas_call(
        functools.partial(_attn_sample_kernel, bs=bs, t=t),
        grid=(Bd // bs,),
        in_specs=[
            pl.BlockSpec((rows, N_HEADS * HEAD_DIM), tile),
            pl.BlockSpec((rows, 2 * kvw), tile),
            pl.BlockSpec((rows, 2 * kvw), tile),
            pl.BlockSpec((rows, kvw), tile),
            pl.BlockSpec((rows, kvw), tile),
            pl.BlockSpec(cblk, tile4),
            pl.BlockSpec(cblk, tile4),
            pl.BlockSpec((S, 4 * rows), lambda i: (0, 0)),
            pl.BlockSpec((1, S, 1), tile3),
            pl.BlockSpec((N_KV_HEADS, 1, 4 * rows), lambda i: (0, 0, 0)),
        ],
        out_specs=[
            pl.BlockSpec((rows, N_HEADS * HEAD_DIM), tile),
            pl.BlockSpec(cblk, tile4),
            pl.BlockSpec(cblk, tile4),
        ],
        out_shape=[
            jax.ShapeDtypeStruct((Bd * t, N_HEADS * HEAD_DIM), BF16),
            jax.ShapeDtypeStruct((Bd,) + cblk[1:], F32),
            jax.ShapeDtypeStruct((Bd,) + cblk[1:], F32),
        ],
        compiler_params=pltpu.CompilerParams(
            dimension_semantics=("arbitrary",), vmem_limit_bytes=VMEM_LIMIT),
        name="attn_sample",
    )(q, kdn, vdn, kn, vn, ck, cv, mask, kval, sinkrow)


def _rope_tables(pos):
    inv = ROPE_THETA ** (-jnp.arange(0, HEAD_DIM, 2, dtype=F32) / HEAD_DIM)
    reps = LANES // (HEAD_DIM // 2)
    sign = jnp.asarray(np.where((np.arange(LANES) % HEAD_DIM) < HEAD_DIM // 2, -1.0, 1.0), F32)
    ang = pos.astype(F32)[:, None] * inv[None, :]
    cos, sin = lax.optimization_barrier((jnp.cos(ang), jnp.sin(ang)))
    return jnp.tile(cos, (1, reps)), jnp.tile(sin, (1, reps)) * sign[None, :]


def _sink_rows(sinks, rows):
    s = (sinks.astype(F32) * LOG2E).reshape(N_KV_HEADS, 1, N_HEADS // N_KV_HEADS, 1)
    return jnp.broadcast_to(s, (N_KV_HEADS, 1, 4, rows)).reshape(N_KV_HEADS, 1, 4 * rows)


def kernel(x_prompt, x_sample, state_pool, cache_k, cache_v, sample_start, norm_mix, norm_ffn, norm_final,
           w_pool, ls_pool, w_qkv, b_qkv, sinks, w_o, b_o, w_rg, b_rg, w_re, b_re, w_gate, w_up, w_down):
    B, T, D = x_prompt.shape
    Bd, Td, _ = x_sample.shape
    row = lambda v: v.reshape(1, -1).astype(F32)

    wp = w_pool[0].astype(BF16)
    wqkv = w_qkv[0].astype(BF16)
    wo = w_o[0].astype(BF16)
    def router_rows(g_part, e_part):
        z = lambda n: jnp.zeros((n,) + g_part.shape[1:], F32)
        return jnp.concatenate([g_part, z(EXPERT_ROW0 - N_EXPERT_GROUPS), e_part,
                                z(ROUTER_ROWS - EXPERT_ROW0 - N_EXPERTS)], axis=0)

    wr = [router_rows(w_rg[l].T, w_re[l].T) for l in range(2)]
    br = [router_rows(b_rg[l][:, None], b_re[l][:, None]) for l in range(2)]

    start = sample_start.astype(jnp.int32)

    x1p, pool_p16 = _pool_prompt(x_prompt, jnp.zeros((B, HALO, D), F32), row(norm_mix[0]), wp, row(ls_pool[0]))
    pos_s = (start[:, None] + jnp.arange(Td, dtype=jnp.int32)[None, :]).reshape(-1)
    x1s, pool_s_t = _pool_sample(x_sample, jnp.transpose(state_pool[0], (1, 0, 2)), start[:, None],
                                 row(norm_mix[0]), wp, row(ls_pool[0]))
    x1s = x1s.reshape(Bd * Td, D)
    pool_s = jnp.transpose(pool_s_t, (1, 0, 2))
    pool_p = pool_p16[:, HALO - POOL_BUF:]

    moe0 = functools.partial(_moe, nf=row(norm_ffn[0]), wr=wr[0], br=br[0])
    x2p, *w0 = moe0(x1p.reshape(B * T, D), wg=w_gate, wu=w_up, wd=w_down, layer=0)
    x2s = moe0(x1s, wg=w0[0], wu=w0[1], wd=w0[2], layer=None)

    cos_p, sin_p = _rope_tables(jnp.arange(T, dtype=jnp.int32))
    cos_s, sin_s = _rope_tables(pos_s)
    g1 = row(norm_mix[1])
    tmq = QKV_ROWS
    keep = min(WINDOW, T)
    qp, kdp, vdp, kp, vp = _qkv(x2p, g1, wqkv, row(b_qkv[0]), cos_p, sin_p, tmq, T // tmq, keep)
    qs, kds, vds, ks, vs = _qkv(x2s, g1, wqkv, row(b_qkv[0]), cos_s, sin_s, tmq, 1, tmq)

    op, *w1 = _attn_prompt(qp, kdp, vdp, _sink_rows(sinks[0], WINDOW), w_gate, w_up, w_down, 1, B, T)

    bs = ATTN_SAMPLE_SEQS
    rows = bs * Td
    W = cache_k.shape[2]
    qrow = np.arange(rows)
    ccol = np.arange(bs * W)
    ncol = np.arange(rows)
    samp_q, t_q = qrow // Td, qrow % Td
    m_cache = (samp_q[:, None] == (ccol // W)[None, :]) & ((ccol % W)[None, :] > t_q[:, None])
    m_new = (samp_q[:, None] == (ncol // Td)[None, :]) & ((ncol % Td)[None, :] <= t_q[:, None])
    amask = jnp.asarray(np.tile(np.concatenate([m_cache, m_new], axis=1).T, (1, 4)), F32)
    kv_cache = (jnp.arange(W, dtype=jnp.int32)[None, :] >= (W - start)[:, None]).reshape(Bd // bs, bs * W)
    kval = jnp.concatenate([kv_cache, jnp.ones((Bd // bs, rows), bool)], axis=1).astype(F32)
    kval = kval.reshape(Bd // bs, bs * W + rows, 1)
    to_stored = lambda c: jnp.transpose(c[0], (0, 2, 3, 1))
    from_stored = lambda c: jnp.transpose(c, (0, 3, 1, 2))[None]
    osamp, nk_s, nv_s = _attn_sample(
        qs, kds, vds, ks, vs, to_stored(cache_k), to_stored(cache_v),
        amask, kval, _sink_rows(sinks[0], rows), bs, Td)

    moe1 = functools.partial(_moe, nf=row(norm_ffn[1]), wr=wr[1], br=br[1], final=row(norm_final))
    yp = moe1(x2p, wg=w1[0], wu=w1[1], wd=w1[2], layer=None, oproj=(op, wo, row(b_o[0])))
    ys = moe1(x2s, wg=w1[0], wu=w1[1], wd=w1[2], layer=None, oproj=(osamp, wo, row(b_o[0])))

    k_p = kp.reshape(1, B, keep, N_KV_HEADS, HEAD_DIM)
    v_p = vp.reshape(1, B, keep, N_KV_HEADS, HEAD_DIM)
    return (yp.reshape(B, T, D), ys.reshape(Bd, Td, D), pool_p[None], k_p, v_p, pool_s[None],
            from_stored(nk_s), from_stored(nv_s))
```
